```python
import jax
import jax.numpy as jnp
from jax import lax
import numpy as np


D_MODEL = 1024
BATCH = 8
SEQ = 4096
DEPTH = 4

GRID_W = 64
CTX_LEN = 256
EPS = 1e-6
NEG = -1e30
RET_HEADS = 4
RET_QK_DIM = 64
RET_V_DIM = 128
RET_CHUNK = 128
RET_QK = RET_HEADS * RET_QK_DIM
RET_V = RET_HEADS * RET_V_DIM
SG_GROUPS = 4
SG_CHUNK = 128
SG_W = D_MODEL // 2
SG_DIM = SG_W // SG_GROUPS
EVEN_SPLITS = (RET_QK, 2 * RET_QK, 2 * RET_QK + RET_V, 2 * RET_QK + 2 * RET_V, 2 * RET_QK + 2 * RET_V + SG_W)
EVEN_IN = 2 * RET_QK + 2 * RET_V + 2 * SG_W
EVEN_OUT = RET_V + SG_W
ATT_HEADS = 8
ATT_KV_HEADS = 2
HEAD_DIM = 128
ATT_GROUP = ATT_HEADS // ATT_KV_HEADS
ATT_BLOCK = 128
WINDOW = 128
ROPE_BASE = 10000.0
Q_COLS = ATT_HEADS * HEAD_DIM
KV_COLS = ATT_KV_HEADS * HEAD_DIM
ODD_IN = Q_COLS + 2 * KV_COLS
N_EXPERTS = 16
N_GROUPS = 4
EXPERTS_PER_GROUP = N_EXPERTS // N_GROUPS
GROUP_SCORE_K = 2
TOP_K = 2
D_EXPERT = 1024
N_EVEN = (DEPTH + 1) // 2
N_ODD = DEPTH // 2

kernel_name = 'hybrid_retention_sgmlp_swa_moe_dit'


def rms_norm(x, g):
    xf = x.astype(jnp.float32)
    y = xf * lax.rsqrt(jnp.mean(xf * xf, axis=-1, keepdims=True) + EPS)
    return (y * g.astype(jnp.float32)).astype(x.dtype)


def layer_norm(x, g, b):
    xf = x.astype(jnp.float32)
    mu = jnp.mean(xf, axis=-1, keepdims=True)
    var = jnp.mean(jnp.square(xf - mu), axis=-1, keepdims=True)
    y = (xf - mu) * lax.rsqrt(var + EPS)
    return (y * g.astype(jnp.float32) + b.astype(jnp.float32)).astype(x.dtype)


def head_norm(o, g):
    mu = jnp.mean(o, axis=-1, keepdims=True)
    var = jnp.mean(jnp.square(o - mu), axis=-1, keepdims=True)
    y = (o - mu) * lax.rsqrt(var + EPS)
    return y.reshape(o.shape[0], o.shape[1], -1) * g.astype(jnp.float32)


def maybe_flip(a, rev):
    return jnp.flip(a, axis=1) if rev else a


def retention_chunkwise(q, k, v, log_g, s0):
    f32 = jnp.float32
    b, L, H, dk = q.shape
    dv = v.shape[-1]
    C = RET_CHUNK
    n = L // C
    qc = q.astype(f32).reshape(b, n, C, H, dk)
    kc = k.astype(f32).reshape(b, n, C, H, dk)
    vc = v.astype(f32).reshape(b, n, C, H, dv)
    pos = jnp.arange(C, dtype=f32)
    diff = pos[:, None] - pos[None, :]
    inner_decay = jnp.where(diff[None] >= 0, jnp.exp(log_g[:, None, None] * jnp.maximum(diff, 0.0)[None]), 0.0)
    xi = jnp.exp(log_g[None, :] * (pos[:, None] + 1.0))
    zeta = jnp.exp(log_g[None, :] * (C - 1.0 - pos[:, None]))
    chunk_decay = jnp.exp(log_g * C)
    scores = jnp.einsum('bnihd,bnjhd->bnhij', qc, kc) * inner_decay
    inner = jnp.einsum('bnhij,bnjhe->bnihe', scores, vc)
    kv = jnp.einsum('bnjhd,jh,bnjhe->nbhde', kc, zeta, vc)

    def step(s, kv_n):
        return s * chunk_decay[None, :, None, None] + kv_n, s

    _, s_prev = lax.scan(step, s0.astype(f32), kv)
    cross = jnp.einsum('bnihd,nbhde->bnihe', qc, s_prev) * xi[None, None, :, :, None]
    return (inner + cross).reshape(b, L, H, dv)


def retention_final_state(k, v, log_g):
    f32 = jnp.float32
    Lc = k.shape[1]
    w = jnp.exp(log_g[None, :] * (Lc - 1.0 - jnp.arange(Lc, dtype=f32))[:, None])
    return jnp.einsum('bjhd,jh,bjhe->bhde', k.astype(f32), w, v.astype(f32))


def spatial_gating(u, s, ln_g, ln_b, w_s, b_s):
    b, L, _ = s.shape
    n = L // SG_CHUNK
    sn = layer_norm(s, ln_g, ln_b).reshape(b, n, SG_CHUNK, SG_GROUPS, SG_DIM)
    mixed = jnp.einsum('gpq,bnqgc->bnpgc', w_s, sn) + b_s.T[None, None, :, :, None]
    return u * mixed.reshape(b, L, SG_W)


def split_even(p):
    b, L, _ = p.shape
    q, k, v, g, u, s = jnp.split(p, EVEN_SPLITS, axis=-1)
    q = q.reshape(b, L, RET_HEADS, RET_QK_DIM)
    k = k.reshape(b, L, RET_HEADS, RET_QK_DIM) * (RET_QK_DIM ** -0.5)
    v = v.reshape(b, L, RET_HEADS, RET_V_DIM)
    return q, k, v, g, jax.nn.gelu(u), jax.nn.gelu(s)


def even_mixer(h_l, h_c, w_in, w_out, log_decay, gn_g, ln_g, ln_b, w_s, b_s, need_ctx):
    f32 = jnp.float32
    b, Lc, _ = h_c.shape
    ql, kl, vl, gl, ul, sl = split_even(h_l @ w_in)
    if need_ctx:
        qc, kc, vc, gc, uc, sc = split_even(h_c @ w_in)
    else:
        pkv = h_c @ w_in[:, RET_QK:2 * RET_QK + RET_V]
        kc = pkv[..., :RET_QK].reshape(b, Lc, RET_HEADS, RET_QK_DIM) * (RET_QK_DIM ** -0.5)
        vc = pkv[..., RET_QK:].reshape(b, Lc, RET_HEADS, RET_V_DIM)
    o_l = jnp.zeros(vl.shape, f32)
    o_c = jnp.zeros(vc.shape, f32)
    for d in range(2):
        rev = d == 1
        lg = -jnp.exp(log_decay[d].astype(f32))
        s_ctx = retention_final_state(maybe_flip(kc, rev), maybe_flip(vc, rev), lg)
        o_l = o_l + maybe_flip(retention_chunkwise(maybe_flip(ql, rev), maybe_flip(kl, rev), maybe_flip(vl, rev), lg, s_ctx), rev)
        if need_ctx:
            o_c = o_c + maybe_flip(retention_chunkwise(maybe_flip(qc, rev), maybe_flip(kc, rev), maybe_flip(vc, rev), lg, jnp.zeros_like(s_ctx)), rev)

    def merge(o, g, u, s):
        ret = (head_norm(o, gn_g) * jax.nn.silu(g.astype(f32))).astype(u.dtype)
        sg = spatial_gating(u, s, ln_g, ln_b, w_s, b_s)
        return jnp.concatenate([ret, sg], axis=-1) @ w_out

    y_l = merge(o_l, gl, ul, sl)
    y_c = merge(o_c, gc, uc, sc) if need_ctx else None
    return y_l, y_c


def axial_rope(x):
    f32 = jnp.float32
    L = x.shape[1]
    rows = L // GRID_W
    row = jnp.repeat(jnp.arange(rows, dtype=f32), GRID_W)
    col = jnp.tile(jnp.arange(GRID_W, dtype=f32), rows)
    half = HEAD_DIM // 2
    quarter = half // 2
    inv_freq = ROPE_BASE ** (-jnp.arange(quarter, dtype=f32) / quarter)

    def rot(xa, pos):
        ang = pos[:, None] * inv_freq[None, :]
        cos = jnp.cos(ang)[None, :, None, :]
        sin = jnp.sin(ang)[None, :, None, :]
        x1 = xa[..., :quarter].astype(f32)
        x2 = xa[..., quarter:].astype(f32)
        return jnp.concatenate([x1 * cos - x2 * sin, x2 * cos + x1 * sin], axis=-1)

    return jnp.concatenate([rot(x[..., :half], row), rot(x[..., half:], col)], axis=-1).astype(x.dtype)


def window_attention(q, k, v, kc, vc, sink):
    f32 = jnp.float32
    b, L, _, _ = q.shape
    nb = L // ATT_BLOCK
    scale = HEAD_DIM ** -0.5
    qb = q.reshape(b, nb, ATT_BLOCK, ATT_KV_HEADS, ATT_GROUP, HEAD_DIM)
    pad = ((0, 0), (ATT_BLOCK, ATT_BLOCK), (0, 0), (0, 0))

    def band(t):
        tp = jnp.pad(t, pad).reshape(b, nb + 2, ATT_BLOCK, ATT_KV_HEADS, HEAD_DIM)
        return jnp.concatenate([tp[:, 0:nb], tp[:, 1:nb + 1], tp[:, 2:nb + 2]], axis=2)

    kw, vw = band(k), band(v)
    qpos = jnp.arange(L).reshape(nb, ATT_BLOCK)
    kpos = jnp.arange(nb)[:, None] * ATT_BLOCK + jnp.arange(-ATT_BLOCK, 2 * ATT_BLOCK)[None, :]
    valid = (kpos[:, None, :] >= 0) & (kpos[:, None, :] < L) & (jnp.abs(qpos[:, :, None] - kpos[:, None, :]) <= WINDOW)
    s_loc = jnp.einsum('bnikgd,bnjkd->bnkgij', qb, kw).astype(f32) * scale
    s_loc = jnp.where(valid[None, :, None, None], s_loc, NEG)
    s_ctx = jnp.einsum('bnikgd,bjkd->bnkgij', qb, kc).astype(f32) * scale
    sink_b = jnp.broadcast_to(sink.astype(f32).reshape(ATT_KV_HEADS, ATT_GROUP)[None, None, :, :, None, None], s_loc.shape[:-1] + (1,))
    p = jax.nn.softmax(jnp.concatenate([sink_b, s_loc, s_ctx], axis=-1), axis=-1)
    nloc = 3 * ATT_BLOCK
    p_loc = p[..., 1:1 + nloc].astype(v.dtype)
    p_ctx = p[..., 1 + nloc:].astype(v.dtype)
    o = jnp.einsum('bnkgij,bnjkd->bnikgd', p_loc, vw) + jnp.einsum('bnkgij,bjkd->bnikgd', p_ctx, vc)
    return o.reshape(b, L, ATT_HEADS * HEAD_DIM)


def context_attention(qc, kc, vc, sink):
    f32 = jnp.float32
    b, Lc, _, _ = qc.shape
    qg = qc.reshape(b, Lc, ATT_KV_HEADS, ATT_GROUP, HEAD_DIM)
    s = jnp.einsum('bikgd,bjkd->bkgij', qg, kc).astype(f32) * (HEAD_DIM ** -0.5)
    sink_b = jnp.broadcast_to(sink.astype(f32).reshape(ATT_KV_HEADS, ATT_GROUP)[None, :, :, None, None], s.shape[:-1] + (1,))
    p = jax.nn.softmax(jnp.concatenate([sink_b, s], axis=-1), axis=-1)[..., 1:].astype(vc.dtype)
    o = jnp.einsum('bkgij,bjkd->bikgd', p, vc)
    return o.reshape(b, Lc, ATT_HEADS * HEAD_DIM)


def odd_mixer(h_l, h_c, w_qkv, w_o, sink, need_ctx):
    b, L, _ = h_l.shape
    Lc = h_c.shape[1]
    p = h_l @ w_qkv
    ql = axial_rope(p[..., :Q_COLS].reshape(b, L, ATT_HEADS, HEAD_DIM))
    kl = axial_rope(p[..., Q_COLS:Q_COLS + KV_COLS].reshape(b, L, ATT_KV_HEADS, HEAD_DIM))
    vl = p[..., Q_COLS + KV_COLS:].reshape(b, L, ATT_KV_HEADS, HEAD_DIM)
    pkv = h_c @ w_qkv[:, Q_COLS:]
    kc = pkv[..., :KV_COLS].reshape(b, Lc, ATT_KV_HEADS, HEAD_DIM)
    vc = pkv[..., KV_COLS:].reshape(b, Lc, ATT_KV_HEADS, HEAD_DIM)
    y_l = window_attention(ql, kl, vl, kc, vc, sink) @ w_o
    y_c = None
    if need_ctx:
        qc = (h_c @ w_qkv[:, :Q_COLS]).reshape(b, Lc, ATT_HEADS, HEAD_DIM)
        y_c = context_attention(qc, kc, vc, sink) @ w_o
    return y_l, y_c


def grouped_moe(h, w_router, b_router, w_gate, w_up, w_down):
    f32 = jnp.float32
    n = h.shape[0]
    probs = jax.nn.softmax((h @ w_router + b_router).astype(f32), axis=-1)
    grp = probs.reshape(n, N_GROUPS, EXPERTS_PER_GROUP)
    group_score = jnp.sum(lax.top_k(grp, GROUP_SCORE_K)[0], axis=-1)
    g_sel = jnp.argmax(group_score, axis=-1)
    in_group = (jnp.arange(N_EXPERTS) // EXPERTS_PER_GROUP)[None, :] == g_sel[:, None]
    top_w, top_i = lax.top_k(jnp.where(in_group, probs, -1.0), TOP_K)
    top_w = top_w / jnp.sum(top_w, axis=-1, keepdims=True)
    combine = jnp.sum(jax.nn.one_hot(top_i, N_EXPERTS, dtype=f32) * top_w[..., None], axis=1).astype(h.dtype)
    out = jnp.zeros_like(h)
    for e in range(N_EXPERTS):
        a = jax.nn.silu(h @ w_gate[e]) * (h @ w_up[e])
        out = out + combine[:, e:e + 1] * (a @ w_down[e])
    return out


def setup_inputs(seed: int = 0) -> dict:
    key = jax.random.key(seed)
    ks = iter(jax.random.split(key, 32))

    def nrm(shape, std):
        return jax.random.normal(next(ks), shape, jnp.float32) * std

    D = D_MODEL
    base_decay = jnp.asarray(np.log(-np.log(1.0 - 2.0 ** (-5.0 - np.arange(RET_HEADS)))), jnp.float32)
    return {
        'x': nrm((BATCH, SEQ, D), 1.0),
        'c': nrm((BATCH, D), 1.0),
        'ctx': nrm((BATCH, CTX_LEN, D), 1.0),
        'c_ctx': nrm((D,), 1.0),
        'w_mod': nrm((DEPTH, D, 6 * D), 0.5 * D ** -0.5),
        'b_mod': nrm((DEPTH, 6 * D), 0.02),
        'norm_g': 1.0 + nrm((DEPTH, 2, D), 0.02),
        'w_in_even': nrm((N_EVEN, D, EVEN_IN), D ** -0.5),
        'w_out_even': nrm((N_EVEN, EVEN_OUT, D), EVEN_OUT ** -0.5),
        'ret_log_decay': base_decay + nrm((N_EVEN, 2, RET_HEADS), 0.1),
        'ret_gn_g': 1.0 + nrm((N_EVEN, RET_V), 0.02),
        'sg_ln_g': 1.0 + nrm((N_EVEN, SG_W), 0.02),
        'sg_ln_b': nrm((N_EVEN, SG_W), 0.02),
        'sg_w': nrm((N_EVEN, SG_GROUPS, SG_CHUNK, SG_CHUNK), SG_CHUNK ** -0.5),
        'sg_b': 1.0 + nrm((N_EVEN, SG_GROUPS, SG_CHUNK), 0.1),
        'w_qkv_odd': nrm((N_ODD, D, ODD_IN), D ** -0.5),
        'w_o_odd': nrm((N_ODD, Q_COLS, D), Q_COLS ** -0.5),
        'attn_sink': nrm((N_ODD, ATT_HEADS), 0.5),
        'w_router': nrm((D, N_EXPERTS), D ** -0.5),
        'b_router': nrm((N_EXPERTS,), 0.01),
        'w_e_gate': nrm((DEPTH, N_EXPERTS, D, D_EXPERT), D ** -0.5),
        'w_e_up': nrm((DEPTH, N_EXPERTS, D, D_EXPERT), D ** -0.5),
        'w_e_down': nrm((DEPTH, N_EXPERTS, D_EXPERT, D), D_EXPERT ** -0.5),
        'final_g': 1.0 + nrm((D,), 0.02),
    }


def reference(x, c, ctx, c_ctx, w_mod, b_mod, norm_g, w_in_even, w_out_even, ret_log_decay, ret_gn_g, sg_ln_g, sg_ln_b, sg_w, sg_b, w_qkv_odd, w_o_odd, attn_sink, w_router, b_router, w_e_gate, w_e_up, w_e_down, final_g):
    b, L, D = x.shape
    for l in range(DEPTH):
        need_ctx = l < DEPTH - 1
        m_l = (jax.nn.silu(c) @ w_mod[l] + b_mod[l])[:, None, :]
        m_c = (jax.nn.silu(c_ctx) @ w_mod[l] + b_mod[l])[None, None, :]
        sh1, sc1, g1, sh2, sc2, g2 = jnp.split(m_l, 6, axis=-1)
        csh1, csc1, cg1, csh2, csc2, cg2 = jnp.split(m_c, 6, axis=-1)
        h_l = rms_norm(x, norm_g[l, 0]) * (1.0 + sc1) + sh1
        h_c = rms_norm(ctx, norm_g[l, 0]) * (1.0 + csc1) + csh1
        i = l // 2
        if l % 2 == 0:
            y_l, y_c = even_mixer(h_l, h_c, w_in_even[i], w_out_even[i], ret_log_decay[i], ret_gn_g[i], sg_ln_g[i], sg_ln_b[i], sg_w[i], sg_b[i], need_ctx)
        else:
            y_l, y_c = odd_mixer(h_l, h_c, w_qkv_odd[i], w_o_odd[i], attn_sink[i], need_ctx)
        x = x + g1 * y_l
        h2_l = rms_norm(x, norm_g[l, 1]) * (1.0 + sc2) + sh2
        if need_ctx:
            ctx = ctx + cg1 * y_c
            h2_c = rms_norm(ctx, norm_g[l, 1]) * (1.0 + csc2) + csh2
            rows = jnp.concatenate([h2_l.reshape(-1, D), h2_c.reshape(-1, D)], axis=0)
            out = grouped_moe(rows, w_router, b_router, w_e_gate[l], w_e_up[l], w_e_down[l])
            x = x + g2 * out[:b * L].reshape(x.shape)
            ctx = ctx + cg2 * out[b * L:].reshape(ctx.shape)
        else:
            x = x + g2 * grouped_moe(h2_l.reshape(-1, D), w_router, b_router, w_e_gate[l], w_e_up[l], w_e_down[l]).reshape(x.shape)
    return rms_norm(x, final_g)
```

```python
import functools

import jax
import jax.numpy as jnp
import numpy as np
from jax import lax
from jax.experimental import pallas as pl
from jax.experimental.pallas import tpu as pltpu

F32 = jnp.float32
BF16 = jnp.bfloat16
I32 = jnp.int32

EPS = 1e-6
NEG = -1e30
GRID_W = 64
ROPE_BASE = 10000.0
RET_HEADS = 4
RET_QK_DIM = 64
RET_V_DIM = 128
SG_GROUPS = 4
SG_CHUNK = 128
RET_BLOCK = 256
ATT_HEADS = 8
ATT_KV_HEADS = 2
ATT_GROUP = ATT_HEADS // ATT_KV_HEADS
HEAD_DIM = 128
ATT_BLOCK = 128
WINDOW = 128
N_EXPERTS = 16
N_GROUPS = 4
EXPERTS_PER_GROUP = 4
N_PAIRS = 6
N_BUCKETS = N_GROUPS * N_PAIRS
BUCKET_ROWS = 32

ROW_TILE = 512
MOE_TILE = 256
VMEM_LIMIT = 56 * 1024 * 1024

_NT = (((1,), (1,)), ((), ()))
_TN = (((0,), (0,)), ((), ()))


def _params(n_axes, vmem=None):
    return pltpu.CompilerParams(dimension_semantics=("arbitrary",) * n_axes, vmem_limit_bytes=vmem)


def _silu(v):
    return v * jax.nn.sigmoid(v)


def _rms_mod(x, g, scale, shift):
    y = x * lax.rsqrt(jnp.mean(x * x, axis=-1, keepdims=True) + EPS) * g
    return y * (1.0 + scale) + shift


def _mod_kernel(c_ref, w_ref, b_ref, o_ref):
    a = _silu(c_ref[...]).astype(BF16)
    o_ref[0] = jnp.dot(a, w_ref[0].astype(BF16), preferred_element_type=F32) + b_ref[0]


def _modulation(c_rows, w_mod, b_mod):
    depth, d, six_d = w_mod.shape
    mr = c_rows.shape[0]
    tn = 1536
    return pl.pallas_call(
        _mod_kernel,
        grid=(depth, six_d // tn),
        in_specs=[
            pl.BlockSpec((mr, d), lambda l, j: (0, 0)),
            pl.BlockSpec((1, d, tn), lambda l, j: (l, 0, j)),
            pl.BlockSpec((1, 1, tn), lambda l, j: (l, 0, j)),
        ],
        out_specs=pl.BlockSpec((1, mr, tn), lambda l, j: (l, 0, j)),
        out_shape=jax.ShapeDtypeStruct((depth, mr, six_d), F32),
        compiler_params=_params(2, VMEM_LIMIT),
        name="modulation",
    )(c_rows, w_mod, b_mod.reshape(depth, 1, six_d))


def _mod_row_map(tm, n_lat, seq, n_batch):
    def index(i):
        row0 = i * tm
        return jnp.where(row0 < n_lat, row0 // seq, n_batch)
    return index


def _in_even_kernel(x_ref, mod_ref, g_ref, w_ref, o_ref):
    d = x_ref.shape[1]
    h = _rms_mod(x_ref[...], g_ref[...], mod_ref[0, :, d:2 * d], mod_ref[0, :, 0:d]).astype(BF16)
    for c in range(5):
        p = jnp.dot(h, w_ref[:, 512 * c:512 * (c + 1)], preferred_element_type=F32)
        if c == 0:
            col = lax.broadcasted_iota(I32, p.shape, 1)
            p = jnp.where(col >= 256, p * (RET_QK_DIM ** -0.5), p)
        elif c >= 3:
            p = jax.nn.gelu(p)
        o_ref[:, 512 * c:512 * (c + 1)] = p.astype(BF16)


def _in_even(xa, mod_l, g, w, n_lat, seq, n_batch):
    n, d = xa.shape
    tm = ROW_TILE
    mrow = _mod_row_map(tm, n_lat, seq, n_batch)
    return pl.pallas_call(
        _in_even_kernel,
        grid=(n // tm,),
        in_specs=[
            pl.BlockSpec((tm, d), lambda i: (i, 0)),
            pl.BlockSpec((1, 1, 6 * d), lambda i: (mrow(i), 0, 0)),
            pl.BlockSpec((1, d), lambda i: (0, 0)),
            pl.BlockSpec(w.shape, lambda i: (0, 0)),
        ],
        out_specs=pl.BlockSpec((tm, w.shape[1]), lambda i: (i, 0)),
        out_shape=jax.ShapeDtypeStruct((n, w.shape[1]), BF16),
        compiler_params=_params(1, VMEM_LIMIT),
        name="in_even",
    )(xa, mod_l, g, w)


def _in_odd_kernel(x_ref, mod_ref, g_ref, w_ref, cos_ref, sin_ref, q_ref, kv_ref):
    d = x_ref.shape[1]
    tm = x_ref.shape[0]
    h = _rms_mod(x_ref[...], g_ref[...], mod_ref[0, :, d:2 * d], mod_ref[0, :, 0:d]).astype(BF16)
    cos = cos_ref[...]
    sin = sin_ref[...]
    lane = lax.broadcasted_iota(I32, (tm, HEAD_DIM), 1)
    first = (lane % 64) < 32

    def rope(p):
        partner = jnp.where(first, pltpu.roll(p, 96, 1), pltpu.roll(p, 32, 1))
        return p * cos + partner * sin

    scale = HEAD_DIM ** -0.5
    for c in range(3):
        p = jnp.dot(h, w_ref[:, 512 * c:512 * (c + 1)], preferred_element_type=F32)
        for j in range(4):
            pj = p[:, 128 * j:128 * (j + 1)]
            if c < 2:
                q_ref[:, 512 * c + 128 * j:512 * c + 128 * (j + 1)] = (rope(pj) * scale).astype(BF16)
            elif j < 2:
                kv_ref[:, 128 * j:128 * (j + 1)] = rope(pj).astype(BF16)
            else:
                kv_ref[:, 128 * j:128 * (j + 1)] = pj.astype(BF16)


def _in_odd(xa, mod_l, g, w, cos_t, sin_t, n_lat, seq, n_batch):
    n, d = xa.shape
    tm = ROW_TILE
    mrow = _mod_row_map(tm, n_lat, seq, n_batch)
    tiles_per_seq = seq // tm

    def rope_row(i):
        return jnp.where(i * tm < n_lat, i % tiles_per_seq, tiles_per_seq)

    return pl.pallas_call(
        _in_odd_kernel,
        grid=(n // tm,),
        in_specs=[
            pl.BlockSpec((tm, d), lambda i: (i, 0)),
            pl.BlockSpec((1, 1, 6 * d), lambda i: (mrow(i), 0, 0)),
            pl.BlockSpec((1, d), lambda i: (0, 0)),
            pl.BlockSpec(w.shape, lambda i: (0, 0)),
            pl.BlockSpec((tm, HEAD_DIM), lambda i: (rope_row(i), 0)),
            pl.BlockSpec((tm, HEAD_DIM), lambda i: (rope_row(i), 0)),
        ],
        out_specs=[
            pl.BlockSpec((tm, ATT_HEADS * HEAD_DIM), lambda i: (i, 0)),
            pl.BlockSpec((tm, 2 * ATT_KV_HEADS * HEAD_DIM), lambda i: (i, 0)),
        ],
        out_shape=[
            jax.ShapeDtypeStruct((n, ATT_HEADS * HEAD_DIM), BF16),
            jax.ShapeDtypeStruct((n, 2 * ATT_KV_HEADS * HEAD_DIM), BF16),
        ],
        compiler_params=_params(1, VMEM_LIMIT),
        name="in_odd",
    )(xa, mod_l, g, w, cos_t, sin_t)


def _rope_tables(seq, tm):
    t = jnp.arange(seq)
    row = (t // GRID_W).astype(F32)
    col = (t % GRID_W).astype(F32)
    quarter = HEAD_DIM // 4
    inv_freq = ROPE_BASE ** (-jnp.arange(quarter, dtype=F32) / quarter)
    ar = row[:, None] * inv_freq[None, :]
    ac = col[:, None] * inv_freq[None, :]
    cos = jnp.concatenate([jnp.cos(ar), jnp.cos(ar), jnp.cos(ac), jnp.cos(ac)], axis=1)
    sin = jnp.concatenate([-jnp.sin(ar), jnp.sin(ar), -jnp.sin(ac), jnp.sin(ac)], axis=1)
    cos = jnp.concatenate([cos, jnp.ones((tm, HEAD_DIM), F32)], axis=0)
    sin = jnp.concatenate([sin, jnp.zeros((tm, HEAD_DIM), F32)], axis=0)
    return cos, sin


def _even_mix_kernel(qk_ref, v_ref, gate_ref, u_ref, s_ref, dec_ref, xi_ref, zf_ref, zb_ref, gdec_ref,
                     gn_ref, lng_ref, lnb_ref, ws_ref, sgb_ref, o_ref, ds_ref, sp_ref):
    sweep = pl.program_id(1)
    n = pl.program_id(2)
    n_slots = ds_ref.shape[0]
    cb = qk_ref.shape[0]
    lane = lax.broadcasted_iota(I32, (1, 128), 1)
    head_mask = [(lane // RET_QK_DIM == hh).astype(BF16) for hh in range(2)]
    upper_rows = lax.broadcasted_iota(I32, (128, 128), 0) < RET_QK_DIM

    @pl.when(sweep == 0)
    def _increments():
        for grp in range(2):
            k2 = qk_ref[:, 256 + 128 * grp:256 + 128 * (grp + 1)].astype(F32)
            v2 = v_ref[:, 256 * grp:256 * (grp + 1)]
            for half, z_ref in enumerate((zf_ref, zb_ref)):
                kz = (k2 * z_ref[grp]).astype(BF16)
                full = lax.dot_general(kz, v2, _TN, preferred_element_type=F32)
                ds_ref[n, grp, :, 128 * half:128 * (half + 1)] = jnp.where(
                    upper_rows, full[:, 0:128], full[:, 128:256])

    @pl.when(jnp.logical_and(sweep == 1, n == 0))
    def _scan():
        sp_ref[0] = jnp.zeros(sp_ref.shape[1:], F32)
        fwd = ds_ref[0, :, :, 0:128]
        for j in range(1, n_slots):
            sp_ref[j, :, :, 0:128] = fwd
            fwd = fwd * gdec_ref[:, :, 0:128] + ds_ref[j, :, :, 0:128]
        bwd = ds_ref[0, :, :, 128:256]
        for j in range(n_slots - 1, 0, -1):
            sp_ref[j, :, :, 128:256] = bwd
            bwd = bwd * gdec_ref[:, :, 128:256] + ds_ref[j, :, :, 128:256]

    @pl.when(sweep == 1)
    def _outputs():
        for grp in range(2):
            q2 = qk_ref[:, 128 * grp:128 * (grp + 1)]
            k2 = qk_ref[:, 256 + 128 * grp:256 + 128 * (grp + 1)]
            state = sp_ref[n, grp].astype(BF16)
            for hh in range(2):
                hd = 2 * grp + hh
                cols = slice(128 * hd, 128 * (hd + 1))
                qm = q2 * head_mask[hh]
                sc = lax.dot_general(qm, k2, _NT, preferred_element_type=F32) * dec_ref[hd]
                inner = jnp.dot(sc.astype(BF16), v_ref[:, cols], preferred_element_type=F32)
                cross = jnp.dot(qm, state, preferred_element_type=F32) * xi_ref[hd]
                o = inner + cross[:, 0:128] + cross[:, 128:256]
                mu = jnp.mean(o, axis=-1, keepdims=True)
                oc = o - mu
                var = jnp.mean(oc * oc, axis=-1, keepdims=True)
                y = oc * lax.rsqrt(var + EPS) * gn_ref[:, cols]
                y = y * _silu(gate_ref[:, cols].astype(F32))
                o_ref[:, cols] = y.astype(BF16)
        for sub in range(cb // SG_CHUNK):
            rows = slice(SG_CHUNK * sub, SG_CHUNK * (sub + 1))
            s = s_ref[rows, :].astype(F32)
            mu = jnp.mean(s, axis=-1, keepdims=True)
            sc_ = s - mu
            var = jnp.mean(sc_ * sc_, axis=-1, keepdims=True)
            sn = (sc_ * lax.rsqrt(var + EPS) * lng_ref[...] + lnb_ref[...]).astype(BF16)
            for g in range(SG_GROUPS):
                cols = slice(128 * g, 128 * (g + 1))
                mixed = jnp.dot(ws_ref[g], sn[:, cols], preferred_element_type=F32) + sgb_ref[:, cols]
                o_ref[rows, 512 + 128 * g:512 + 128 * (g + 1)] = (
                    u_ref[rows, cols].astype(F32) * mixed).astype(BF16)


def _retention_tables(log_decay, cb):
    lg = -jnp.exp(log_decay.astype(F32))
    pos = jnp.arange(cb, dtype=F32)
    diff = pos[:, None] - pos[None, :]
    lower = jnp.where(diff >= 0, jnp.exp(lg[0][:, None, None] * jnp.maximum(diff, 0.0)[None]), 0.0)
    upper = jnp.where(diff <= 0, jnp.exp(lg[1][:, None, None] * jnp.maximum(-diff, 0.0)[None]), 0.0)
    dec = lower + upper
    xi_f = jnp.exp(lg[0][:, None] * (pos[None, :] + 1.0))
    xi_b = jnp.exp(lg[1][:, None] * (cb - pos[None, :]))
    xi = jnp.concatenate([jnp.broadcast_to(xi_f[:, :, None], (RET_HEADS, cb, 128)),
                          jnp.broadcast_to(xi_b[:, :, None], (RET_HEADS, cb, 128))], axis=2)
    zeta_f = jnp.exp(lg[0][:, None] * (cb - 1.0 - pos[None, :]))
    zeta_b = jnp.exp(lg[1][:, None] * pos[None, :])

    def lanes(z):
        return jnp.repeat(z.reshape(2, 2, cb), RET_QK_DIM, axis=1).transpose(0, 2, 1)

    gstep = jnp.exp(lg * cb)

    def rows(gv):
        return jnp.broadcast_to(jnp.repeat(gv.reshape(2, 2), RET_QK_DIM, axis=1)[:, :, None], (2, 128, 128))

    gdec = jnp.concatenate([rows(gstep[0]), rows(gstep[1])], axis=2)
    return dec, xi, lanes(zeta_f), lanes(zeta_b), gdec


def _even_mix(p, tabs, gn_g, ln_g, ln_b, w_s, sg_bias, n_batch, seq, ctx_len):
    n = p.shape[0]
    cb = RET_BLOCK
    assert ctx_len == cb and seq % cb == 0
    nl = seq // cb
    ctx0 = n_batch * seq // cb

    def rb(b, n_):
        return jnp.where(n_ == 0, ctx0 + b, b * nl + n_ - 1)

    def late(col):
        return lambda b, s, n_: (rb(b, n_ * s), col)

    dec, xi, zf, zb, gdec = tabs
    full = lambda a: pl.BlockSpec(a.shape, lambda b, s, n_, _nd=a.ndim: (0,) * _nd)
    return pl.pallas_call(
        _even_mix_kernel,
        grid=(n_batch, 2, nl + 1),
        in_specs=[
            pl.BlockSpec((cb, 512), lambda b, s, n_: (rb(b, n_), 0)),
            pl.BlockSpec((cb, 512), lambda b, s, n_: (rb(b, n_), 1)),
            pl.BlockSpec((cb, 512), late(2)),
            pl.BlockSpec((cb, 512), late(3)),
            pl.BlockSpec((cb, 512), late(4)),
            full(dec), full(xi), full(zf), full(zb), full(gdec),
            full(gn_g), full(ln_g), full(ln_b), full(w_s), full(sg_bias),
        ],
        out_specs=pl.BlockSpec((cb, 1024), late(0)),
        out_shape=jax.ShapeDtypeStruct((n, 1024), BF16),
        scratch_shapes=[pltpu.VMEM((nl + 1, 2, 128, 256), F32), pltpu.VMEM((nl + 1, 2, 128, 256), F32)],
        compiler_params=_params(3, VMEM_LIMIT),
        name="even_mix",
    )(p, p, p, p, p, dec, xi, zf, zb, gdec, gn_g, ln_g, ln_b, w_s, sg_bias)


def _attn_kernel(q_ref, kvp_ref, kvc_ref, kvn_ref, kvx_ref, bias_ref, sink_ref, o_ref):
    nloc = 3 * ATT_BLOCK
    for kvh in range(ATT_KV_HEADS):
        kcols = slice(HEAD_DIM * kvh, HEAD_DIM * (kvh + 1))
        vcols = slice(HEAD_DIM * (ATT_KV_HEADS + kvh), HEAD_DIM * (ATT_KV_HEADS + kvh + 1))
        k_all = jnp.concatenate([kvp_ref[:, kcols], kvc_ref[:, kcols], kvn_ref[:, kcols], kvx_ref[:, kcols]], axis=0)
        v_all = jnp.concatenate([kvp_ref[:, vcols], kvc_ref[:, vcols], kvn_ref[:, vcols], kvx_ref[:, vcols]], axis=0)
        heads = [ATT_GROUP * kvh + j for j in range(ATT_GROUP)]
        q4 = jnp.concatenate([q_ref[:, HEAD_DIM * hd:HEAD_DIM * (hd + 1)] for hd in heads], axis=0)
        s = lax.dot_general(q4, k_all, _NT, preferred_element_type=F32)
        s_loc = s[:, 0:nloc] + bias_ref[0]
        s_ctx = s[:, nloc:]
        sink = jnp.concatenate(
            [jnp.broadcast_to(sink_ref[hd:hd + 1, 0:1], (ATT_BLOCK, 1)) for hd in heads], axis=0)
        m = jnp.maximum(jnp.maximum(jnp.max(s_loc, axis=-1, keepdims=True),
                                    jnp.max(s_ctx, axis=-1, keepdims=True)), sink)
        p_loc = jnp.exp(s_loc - m)
        p_ctx = jnp.exp(s_ctx - m)
        denom = (jnp.sum(p_loc, axis=-1, keepdims=True) + jnp.sum(p_ctx, axis=-1, keepdims=True)
                 + jnp.exp(sink - m))
        o = (jnp.dot(p_loc.astype(BF16), v_all[0:nloc], preferred_element_type=F32)
             + jnp.dot(p_ctx.astype(BF16), v_all[nloc:], preferred_element_type=F32)) / denom
        for j, hd in enumerate(heads):
            o_ref[:, HEAD_DIM * hd:HEAD_DIM * (hd + 1)] = o[ATT_BLOCK * j:ATT_BLOCK * (j + 1)].astype(BF16)


def _attn_bias():
    i = np.arange(ATT_BLOCK)[:, None]
    j = np.arange(3 * ATT_BLOCK)[None, :] - ATT_BLOCK
    band = np.abs(i - j) <= WINDOW
    cases = [band & (j >= 0), band, band & (j < ATT_BLOCK), np.zeros_like(band)]
    bias = np.stack([np.where(c, 0.0, NEG) for c in cases]).astype(np.float32)
    return jnp.asarray(np.tile(bias, (1, ATT_GROUP, 1)))


def _attention(q, kv, bias, sink_t, n_batch, seq, ctx_len, need_ctx):
    blk = ATT_BLOCK
    nb = seq // blk
    ncb = ctx_len // blk
    n_lat = n_batch * seq
    steps = nb + (ncb if need_ctx else 0)
    n_out = n_lat + (n_batch * ctx_len if need_ctx else 0)

    def qrow(b, n_):
        return jnp.where(n_ < nb, b * nb + n_, n_lat // blk + b * ncb + (n_ - nb))

    def krow(off):
        def index(b, n_):
            loc = jnp.clip(jnp.minimum(n_, nb - 1) + off, 0, nb - 1)
            return (b * nb + loc, 0)
        return index

    def case(b, n_):
        c = jnp.where(n_ == 0, 0, jnp.where(n_ == nb - 1, 2, 1))
        return (jnp.where(n_ < nb, c, 3), 0, 0)

    return pl.pallas_call(
        _attn_kernel,
        grid=(n_batch, steps),
        in_specs=[
            pl.BlockSpec((blk, q.shape[1]), lambda b, n_: (qrow(b, n_), 0)),
            pl.BlockSpec((blk, kv.shape[1]), krow(-1)),
            pl.BlockSpec((blk, kv.shape[1]), krow(0)),
            pl.BlockSpec((blk, kv.shape[1]), krow(1)),
            pl.BlockSpec((ctx_len, kv.shape[1]), lambda b, n_: (n_lat // ctx_len + b, 0)),
            pl.BlockSpec((1,) + bias.shape[1:], case),
            pl.BlockSpec(sink_t.shape, lambda b, n_: (0, 0)),
        ],
        out_specs=pl.BlockSpec((blk, q.shape[1]), lambda b, n_: (qrow(b, n_), 0)),
        out_shape=jax.ShapeDtypeStruct((n_out, q.shape[1]), BF16),
        compiler_params=_params(2, VMEM_LIMIT),
        name="window_attention",
    )(q, kv, kv, kv, kv, bias, sink_t)


def _route(logits):
    m = jnp.max(logits, axis=0, keepdims=True)
    e = jnp.exp(logits - m)
    p = e / jnp.sum(e, axis=0, keepdims=True)
    rows = [p[i:i + 1, :] for i in range(N_EXPERTS)]
    best = None
    gsel = None
    for g in range(N_GROUPS):
        a, b, c, d = rows[4 * g:4 * g + 4]
        m1, n1 = jnp.maximum(a, b), jnp.minimum(a, b)
        m2, n2 = jnp.maximum(c, d), jnp.minimum(c, d)
        score = jnp.maximum(m1, m2) + jnp.maximum(jnp.minimum(m1, m2), jnp.maximum(n1, n2))
        if g == 0:
            best, gsel = score, jnp.zeros(score.shape, I32)
        else:
            upd = score > best
            gsel = jnp.where(upd, g, gsel)
            best = jnp.where(upd, score, best)
    v = [jnp.where(gsel == 0, rows[j], jnp.where(gsel == 1, rows[4 + j], jnp.where(gsel == 2, rows[8 + j], rows[12 + j])))
         for j in range(EXPERTS_PER_GROUP)]
    b1, i1 = v[0], jnp.zeros(gsel.shape, I32)
    for j in range(1, 4):
        upd = v[j] > b1
        i1 = jnp.where(upd, j, i1)
        b1 = jnp.where(upd, v[j], b1)
    b2, i2 = jnp.full(b1.shape, -1.0, F32), jnp.zeros(gsel.shape, I32)
    for j in range(4):
        upd = jnp.logical_and(i1 != j, v[j] > b2)
        i2 = jnp.where(upd, j, i2)
        b2 = jnp.where(upd, v[j], b2)
    lo = jnp.minimum(i1, i2)
    hi = jnp.maximum(i1, i2)
    pair = jnp.where(lo == 0, hi - 1, jnp.where(lo == 1, hi + 1, 5))
    return gsel * N_PAIRS + pair


def _post_kernel(mix_ref, x_ref, mod_ref, g2_ref, wo_ref, wrh_ref, wrl_ref, br_ref, xo_ref, bk_ref):
    d = x_ref.shape[1]
    y = jnp.dot(mix_ref[...], wo_ref[...], preferred_element_type=F32)
    x = x_ref[...] + mod_ref[0, :, 2 * d:3 * d] * y
    xo_ref[...] = x
    h2 = _rms_mod(x, g2_ref[...], mod_ref[0, :, 4 * d:5 * d], mod_ref[0, :, 3 * d:4 * d])
    hi = h2.astype(BF16)
    lo = (h2 - hi.astype(F32)).astype(BF16)
    logits = (lax.dot_general(wrh_ref[...], hi, _NT, preferred_element_type=F32)
              + lax.dot_general(wrh_ref[...], lo, _NT, preferred_element_type=F32)
              + lax.dot_general(wrl_ref[...], hi, _NT, preferred_element_type=F32)
              + br_ref[:, 0:1])
    bk_ref[...] = _route(logits)


def _post(mix, xa, mod_l, g2, wo, wr_hi, wr_lo, br_t, n_act, n_lat, seq, n_batch):
    d = xa.shape[1]
    tm = ROW_TILE
    mrow = _mod_row_map(tm, n_lat, seq, n_batch)
    full = lambda a: pl.BlockSpec(a.shape, lambda i, _nd=a.ndim: (0,) * _nd)
    return pl.pallas_call(
        _post_kernel,
        grid=(n_act // tm,),
        in_specs=[
            pl.BlockSpec((tm, d), lambda i: (i, 0)),
            pl.BlockSpec((tm, d), lambda i: (i, 0)),
            pl.BlockSpec((1, 1, 6 * d), lambda i: (mrow(i), 0, 0)),
            full(g2), full(wo), full(wr_hi), full(wr_lo), full(br_t),
        ],
        out_specs=[
            pl.BlockSpec((tm, d), lambda i: (i, 0)),
            pl.BlockSpec((1, tm), lambda i: (0, i)),
        ],
        out_shape=[
            jax.ShapeDtypeStruct((n_act, d), F32),
            jax.ShapeDtypeStruct((1, n_act), I32),
        ],
        compiler_params=_params(1, VMEM_LIMIT),
        name="post_mixer",
    )(mix, xa, mod_l, g2, wo, wr_hi, wr_lo, br_t)


def _rank_kernel(bk_ref, tri_ref, rank_ref, cnt_ref, carry_ref):
    @pl.when(pl.program_id(0) == 0)
    def _():
        carry_ref[...] = jnp.zeros(carry_ref.shape, F32)

    tm = bk_ref.shape[1]
    ids = lax.broadcasted_iota(I32, (BUCKET_ROWS, tm), 0)
    onehot = (ids == bk_ref[...]).astype(F32)
    before = jnp.dot(onehot.astype(BF16), tri_ref[...], preferred_element_type=F32)
    carry = carry_ref[...]
    rank = jnp.sum(onehot * (before + carry[:, 0:1]), axis=0, keepdims=True)
    rank_ref[...] = rank.astype(I32)
    carry = carry + jnp.sum(onehot, axis=1, keepdims=True)
    carry_ref[...] = carry
    cnt_ref[...] = carry.astype(I32)


def _rank(bucket, tri):
    n = bucket.shape[1]
    tm = tri.shape[0]
    return pl.pallas_call(
        _rank_kernel,
        grid=(n // tm,),
        in_specs=[pl.BlockSpec((1, tm), lambda i: (0, i)), pl.BlockSpec(tri.shape, lambda i: (0, 0))],
        out_specs=[pl.BlockSpec((1, tm), lambda i: (0, i)), pl.BlockSpec((BUCKET_ROWS, 128), lambda i: (0, 0))],
        out_shape=[jax.ShapeDtypeStruct((1, n), I32), jax.ShapeDtypeStruct((BUCKET_ROWS, 128), I32)],
        scratch_shapes=[pltpu.VMEM((BUCKET_ROWS, 128), F32)],
        compiler_params=_params(1),
        name="bucket_rank",
    )(bucket, tri)


def _row_copy_wait(src_ref, dst_ref, sem, count):
    def body(r, carry):
        pltpu.make_async_copy(src_ref.at[pl.ds(0, 1), :], dst_ref.at[pl.ds(0, 1), :], sem).wait()
        return carry
    lax.fori_loop(0, count, body, 0)


def _dispatch_kernel(pos_ref, x_ref, mod_ref, g2_ref, hs_in_ref, hs_ref, hbuf_ref, sem):
    del hs_in_ref
    d = x_ref.shape[1]
    tm = x_ref.shape[0]
    hbuf_ref[...] = _rms_mod(x_ref[...], g2_ref[...], mod_ref[0, :, 4 * d:5 * d], mod_ref[0, :, 3 * d:4 * d])

    def body(r, carry):
        dst = pos_ref[0, 0, r]
        pltpu.make_async_copy(hbuf_ref.at[pl.ds(r, 1), :], hs_ref.at[pl.ds(dst, 1), :], sem).start()
        return carry
    lax.fori_loop(0, tm, body, 0)
    _row_copy_wait(hbuf_ref, hs_ref, sem, tm)


def _dispatch(pos3, xa, mod_l, g2, hs, n_lat, seq, n_batch):
    n, d = xa.shape
    tm = ROW_TILE
    mrow = _mod_row_map(tm, n_lat, seq, n_batch)
    return pl.pallas_call(
        _dispatch_kernel,
        grid=(n // tm,),
        in_specs=[
            pl.BlockSpec((1, 1, tm), lambda i: (i, 0, 0), memory_space=pltpu.SMEM),
            pl.BlockSpec((tm, d), lambda i: (i, 0)),
            pl.BlockSpec((1, 1, 6 * d), lambda i: (mrow(i), 0, 0)),
            pl.BlockSpec((1, d), lambda i: (0, 0)),
            pl.BlockSpec(memory_space=pl.ANY),
        ],
        out_specs=pl.BlockSpec(memory_space=pl.ANY),
        out_shape=jax.ShapeDtypeStruct(hs.shape, F32),
        scratch_shapes=[pltpu.VMEM((tm, d), F32), pltpu.SemaphoreType.DMA(())],
        input_output_aliases={4: 0},
        compiler_params=_params(1, VMEM_LIMIT),
        name="dispatch",
    )(pos3, xa, mod_l, g2, hs)


def _moe_kernel(ea_ref, eb_ref, nu_ref, h_ref, wga_ref, wua_ref, wda_ref, wgb_ref, wub_ref, wdb_ref,
                wrt_ref, brt_ref, o_ref):
    t = pl.program_id(0)

    @pl.when(t >= nu_ref[0])
    def _():
        o_ref[...] = jnp.zeros(o_ref.shape, F32)

    @pl.when(t < nu_ref[0])
    def _():
        ea = ea_ref[t]
        eb = eb_ref[t]
        h = h_ref[...]
        hb = h.astype(BF16)
        wdiff = wrt_ref[pl.ds(ea, 1), :] - wrt_ref[pl.ds(eb, 1), :]
        bdiff = brt_ref[pl.ds(ea, 1), 0:1] - brt_ref[pl.ds(eb, 1), 0:1]
        wa = jax.nn.sigmoid(jnp.sum(h * wdiff, axis=-1, keepdims=True) + bdiff)

        def expert(wg_ref, wu_ref, wd_ref):
            f = wg_ref.shape[2]
            acc = None
            for c in range(f // 512):
                cols = slice(512 * c, 512 * (c + 1))
                g = jnp.dot(hb, wg_ref[0, :, cols], preferred_element_type=F32)
                u = jnp.dot(hb, wu_ref[0, :, cols], preferred_element_type=F32)
                part = jnp.dot((_silu(g) * u).astype(BF16), wd_ref[0, cols, :], preferred_element_type=F32)
                acc = part if acc is None else acc + part
            return acc

        o_ref[...] = wa * expert(wga_ref, wua_ref, wda_ref) + (1.0 - wa) * expert(wgb_ref, wub_ref, wdb_ref)


def _moe(ea, eb, nu, hs, wg, wu, wd, wr_t, br_t):
    p_rows, d = hs.shape
    tm = MOE_TILE
    f = wg.shape[2]
    wa_spec = lambda shape: pl.BlockSpec(shape, lambda t, ea_, eb_, nu_: (ea_[t], 0, 0))
    wb_spec = lambda shape: pl.BlockSpec(shape, lambda t, ea_, eb_, nu_: (eb_[t], 0, 0))
    grid_spec = pltpu.PrefetchScalarGridSpec(
        num_scalar_prefetch=3,
        grid=(p_rows // tm,),
        in_specs=[
            pl.BlockSpec((tm, d), lambda t, ea_, eb_, nu_: (t, 0)),
            wa_spec((1, d, f)), wa_spec((1, d, f)), wa_spec((1, f, d)),
            wb_spec((1, d, f)), wb_spec((1, d, f)), wb_spec((1, f, d)),
            pl.BlockSpec(wr_t.shape, lambda t, ea_, eb_, nu_: (0, 0)),
            pl.BlockSpec(br_t.shape, lambda t, ea_, eb_, nu_: (0, 0)),
        ],
        out_specs=pl.BlockSpec((tm, d), lambda t, ea_, eb_, nu_: (t, 0)),
    )
    return pl.pallas_call(
        _moe_kernel,
        grid_spec=grid_spec,
        out_shape=jax.ShapeDtypeStruct((p_rows, d), F32),
        compiler_params=_params(1, VMEM_LIMIT),
        name="experts",
    )(ea, eb, nu, hs, wg, wu, wd, wg, wu, wd, wr_t, br_t)


def _combine_kernel(pos_ref, x_ref, mod_ref, fg_ref, ys_ref, o_ref, gbuf_ref, sem, *, final):
    d = x_ref.shape[1]
    tm = x_ref.shape[0]

    def body(r, carry):
        src = pos_ref[0, 0, r]
        pltpu.make_async_copy(ys_ref.at[pl.ds(src, 1), :], gbuf_ref.at[pl.ds(r, 1), :], sem).start()
        return carry
    lax.fori_loop(0, tm, body, 0)
    _row_copy_wait(ys_ref, gbuf_ref, sem, tm)
    x = x_ref[...] + mod_ref[0, :, 5 * d:6 * d] * gbuf_ref[...]
    if final:
        x = x * lax.rsqrt(jnp.mean(x * x, axis=-1, keepdims=True) + EPS) * fg_ref[...]
    o_ref[...] = x


def _combine(pos3, xa, mod_l, fg, ys, n_lat, seq, n_batch, final):
    n, d = xa.shape
    tm = ROW_TILE
    mrow = _mod_row_map(tm, n_lat, seq, n_batch)
    return pl.pallas_call(
        functools.partial(_combine_kernel, final=final),
        grid=(n // tm,),
        in_specs=[
            pl.BlockSpec((1, 1, tm), lambda i: (i, 0, 0), memory_space=pltpu.SMEM),
            pl.BlockSpec((tm, d), lambda i: (i, 0)),
            pl.BlockSpec((1, 1, 6 * d), lambda i: (mrow(i), 0, 0)),
            pl.BlockSpec((1, d), lambda i: (0, 0)),
            pl.BlockSpec(memory_space=pl.ANY),
        ],
        out_specs=pl.BlockSpec((tm, d), lambda i: (i, 0)),
        out_shape=jax.ShapeDtypeStruct((n, d), F32),
        scratch_shapes=[pltpu.VMEM((tm, d), F32), pltpu.SemaphoreType.DMA(())],
        compiler_params=_params(1, VMEM_LIMIT),
        name="combine",
    )(pos3, xa, mod_l, fg, ys)


def _bucket_plan(bucket, rank, counts, n_tiles, tile):
    cnt = counts[:N_BUCKETS, 0]
    padded = ((cnt + tile - 1) // tile) * tile
    ends = jnp.cumsum(padded)
    offs = ends - padded
    pos = offs[bucket[0]] + rank[0]
    n_used = ends[-1] // tile
    t_eff = jnp.minimum(jnp.arange(n_tiles, dtype=I32), jnp.maximum(n_used - 1, 0))
    tb = jnp.minimum(jnp.searchsorted(ends, t_eff * tile, side="right"), N_BUCKETS - 1).astype(I32)
    pairs = np.array([(a, b) for a in range(4) for b in range(a + 1, 4)], dtype=np.int32)
    ea = (tb // N_PAIRS) * EXPERTS_PER_GROUP + jnp.asarray(pairs[:, 0])[tb % N_PAIRS]
    eb = (tb // N_PAIRS) * EXPERTS_PER_GROUP + jnp.asarray(pairs[:, 1])[tb % N_PAIRS]
    return pos.astype(I32), ea.astype(I32), eb.astype(I32), n_used.astype(I32).reshape(1)


def kernel(x, c, ctx, c_ctx, w_mod, b_mod, norm_g, w_in_even, w_out_even, ret_log_decay, ret_gn_g, sg_ln_g, sg_ln_b, sg_w, sg_b, w_qkv_odd, w_o_odd, attn_sink, w_router, b_router, w_e_gate, w_e_up, w_e_down, final_g):
    n_batch, seq, d = x.shape
    ctx_len = ctx.shape[1]
    depth = w_mod.shape[0]
    n_lat = n_batch * seq
    n_all = n_lat + n_batch * ctx_len
    tm = ROW_TILE
    assert seq % tm == 0 and (n_batch * ctx_len) % tm == 0

    xa = jnp.concatenate([x.reshape(n_lat, d), ctx.reshape(n_batch * ctx_len, d)], axis=0)

    mod_rows = -(-(n_batch + 1) // 16) * 16
    c_rows = jnp.zeros((mod_rows, d), F32).at[:n_batch].set(c).at[n_batch].set(c_ctx)
    mod = _modulation(c_rows, w_mod, b_mod).reshape(depth, mod_rows, 1, 6 * d)

    wr_t = w_router.T.astype(F32)
    wr_hi = wr_t.astype(BF16)
    wr_lo = (wr_t - wr_hi.astype(F32)).astype(BF16)
    br_t = jnp.broadcast_to(b_router.astype(F32)[:, None], (N_EXPERTS, 128))
    tri = jnp.triu(jnp.ones((tm, tm), BF16), k=1)
    cos_t, sin_t = _rope_tables(seq, tm)
    bias = _attn_bias()

    n_tiles = n_all // MOE_TILE + N_BUCKETS
    hs = jnp.zeros((n_tiles * MOE_TILE, d), F32)

    for l in range(depth):
        i = l // 2
        last = l == depth - 1
        mod_l = mod[l]
        g1 = norm_g[l, 0].reshape(1, d)
        g2 = norm_g[l, 1].reshape(1, d)
        if l % 2 == 0:
            p = _in_even(xa, mod_l, g1, w_in_even[i].astype(BF16), n_lat, seq, n_batch)
            tabs = _retention_tables(ret_log_decay[i], RET_BLOCK)
            sg_bias = jnp.repeat(sg_b[i].astype(F32).T, SG_CHUNK, axis=1)
            mix = _even_mix(p, tabs, ret_gn_g[i].reshape(1, -1), sg_ln_g[i].reshape(1, -1), sg_ln_b[i].reshape(1, -1),
                            sg_w[i].astype(BF16), sg_bias, n_batch, seq, ctx_len)
            wo = w_out_even[i].astype(BF16)
        else:
            q, kv = _in_odd(xa, mod_l, g1, w_qkv_odd[i].astype(BF16), cos_t, sin_t, n_lat, seq, n_batch)
            sink_t = jnp.broadcast_to(attn_sink[i].astype(F32)[:, None], (ATT_HEADS, 128))
            mix = _attention(q, kv, bias, sink_t, n_batch, seq, ctx_len, need_ctx=not last)
            wo = w_o_odd[i].astype(BF16)
        n_act = n_lat if last else n_all
        xa, bucket = _post(mix, xa, mod_l, g2, wo, wr_hi, wr_lo, br_t, n_act, n_lat, seq, n_batch)
        rank, counts = _rank(bucket, tri)
        pos, ea, eb, n_used = _bucket_plan(bucket, rank, counts, n_tiles, MOE_TILE)
        pos3 = pos.reshape(n_act // tm, 1, tm)
        hs = _dispatch(pos3, xa, mod_l, g2, hs, n_lat, seq, n_batch)
        ys = _moe(ea, eb, n_used, hs, w_e_gate[l].astype(BF16), w_e_up[l].astype(BF16), w_e_down[l].astype(BF16),
                  wr_t, br_t)
        xa = _combine(pos3, xa, mod_l, final_g.reshape(1, d), ys, n_lat, seq, n_batch, final=last)
    return xa.reshape(n_batch, seq, d)
```

```python
import functools

import jax
import jax.numpy as jnp
import numpy as np
from jax import lax
from jax.experimental import pallas as pl
from jax.experimental.pallas import tpu as pltpu

F32 = jnp.float32
BF16 = jnp.bfloat16
I32 = jnp.int32

EPS = 1e-6
NEG = -1e30
GRID_W = 64
ROPE_BASE = 10000.0
RET_HEADS = 4
RET_QK_DIM = 64
RET_V_DIM = 128
SG_GROUPS = 4
SG_CHUNK = 128
RET_BLOCK = 256
ATT_HEADS = 8
ATT_KV_HEADS = 2
ATT_GROUP = ATT_HEADS // ATT_KV_HEADS
HEAD_DIM = 128
ATT_BLOCK = 128
WINDOW = 128
N_EXPERTS = 16
N_GROUPS = 4
EXPERTS_PER_GROUP = 4
N_PAIRS = 6
N_BUCKETS = N_GROUPS * N_PAIRS
BUCKET_ROWS = 32

ROW_TILE = 512
MOE_TILE = 256
DMA_UNROLL = 8
VMEM_LIMIT = 56 * 1024 * 1024

_NT = (((1,), (1,)), ((), ()))
_TN = (((0,), (0,)), ((), ()))


def _params(n_axes, vmem=None):
    return pltpu.CompilerParams(dimension_semantics=("arbitrary",) * n_axes, vmem_limit_bytes=vmem)


def _silu(v):
    return v * jax.nn.sigmoid(v)


def _rms_mod(x, g, scale, shift):
    y = x * lax.rsqrt(jnp.mean(x * x, axis=-1, keepdims=True) + EPS) * g
    return y * (1.0 + scale) + shift


def _mod_kernel(c_ref, w_ref, b_ref, o_ref):
    a = _silu(c_ref[...]).astype(BF16)
    o_ref[0] = jnp.dot(a, w_ref[0].astype(BF16), preferred_element_type=F32) + b_ref[0]


def _modulation(c_rows, w_mod, b_mod):
    depth, d, six_d = w_mod.shape
    mr = c_rows.shape[0]
    tn = 1536
    return pl.pallas_call(
        _mod_kernel,
        grid=(depth, six_d // tn),
        in_specs=[
            pl.BlockSpec((mr, d), lambda l, j: (0, 0)),
            pl.BlockSpec((1, d, tn), lambda l, j: (l, 0, j)),
            pl.BlockSpec((1, 1, tn), lambda l, j: (l, 0, j)),
        ],
        out_specs=pl.BlockSpec((1, mr, tn), lambda l, j: (l, 0, j)),
        out_shape=jax.ShapeDtypeStruct((depth, mr, six_d), F32),
        compiler_params=_params(2, VMEM_LIMIT),
        name="modulation",
    )(c_rows, w_mod, b_mod.reshape(depth, 1, six_d))


def _mod_row_map(tm, n_lat, seq, n_batch):
    def index(i):
        row0 = i * tm
        return jnp.where(row0 < n_lat, row0 // seq, n_batch)
    return index


def _in_even_kernel(x_ref, mod_ref, g_ref, w_ref, o_ref):
    d = x_ref.shape[1]
    h = _rms_mod(x_ref[...], g_ref[...], mod_ref[0, :, d:2 * d], mod_ref[0, :, 0:d]).astype(BF16)
    for c in range(5):
        p = jnp.dot(h, w_ref[:, 512 * c:512 * (c + 1)], preferred_element_type=F32)
        if c == 0:
            col = lax.broadcasted_iota(I32, p.shape, 1)
            p = jnp.where(col >= 256, p * (RET_QK_DIM ** -0.5), p)
        elif c >= 3:
            p = jax.nn.gelu(p)
        o_ref[:, 512 * c:512 * (c + 1)] = p.astype(BF16)


def _in_even(xa, mod_l, g, w, n_lat, seq, n_batch):
    n, d = xa.shape
    tm = ROW_TILE
    mrow = _mod_row_map(tm, n_lat, seq, n_batch)
    return pl.pallas_call(
        _in_even_kernel,
        grid=(n // tm,),
        in_specs=[
            pl.BlockSpec((tm, d), lambda i: (i, 0)),
            pl.BlockSpec((1, 1, 6 * d), lambda i: (mrow(i), 0, 0)),
            pl.BlockSpec((1, d), lambda i: (0, 0)),
            pl.BlockSpec(w.shape, lambda i: (0, 0)),
        ],
        out_specs=pl.BlockSpec((tm, w.shape[1]), lambda i: (i, 0)),
        out_shape=jax.ShapeDtypeStruct((n, w.shape[1]), BF16),
        compiler_params=_params(1, VMEM_LIMIT),
        name="in_even",
    )(xa, mod_l, g, w)


def _in_odd_kernel(x_ref, mod_ref, g_ref, w_ref, cos_ref, sin_ref, q_ref, kv_ref):
    d = x_ref.shape[1]
    tm = x_ref.shape[0]
    h = _rms_mod(x_ref[...], g_ref[...], mod_ref[0, :, d:2 * d], mod_ref[0, :, 0:d]).astype(BF16)
    cos = cos_ref[...]
    sin = sin_ref[...]
    lane = lax.broadcasted_iota(I32, (tm, HEAD_DIM), 1)
    first = (lane % 64) < 32

    def rope(p):
        partner = jnp.where(first, pltpu.roll(p, 96, 1), pltpu.roll(p, 32, 1))
        return p * cos + partner * sin

    scale = HEAD_DIM ** -0.5
    for c in range(3):
        p = jnp.dot(h, w_ref[:, 512 * c:512 * (c + 1)], preferred_element_type=F32)
        for j in range(4):
            pj = p[:, 128 * j:128 * (j + 1)]
            if c < 2:
                q_ref[:, 512 * c + 128 * j:512 * c + 128 * (j + 1)] = (rope(pj) * scale).astype(BF16)
            elif j < 2:
                kv_ref[:, 128 * j:128 * (j + 1)] = rope(pj).astype(BF16)
            else:
                kv_ref[:, 128 * j:128 * (j + 1)] = pj.astype(BF16)


def _in_odd(xa, mod_l, g, w, cos_t, sin_t, n_lat, seq, n_batch):
    n, d = xa.shape
    tm = ROW_TILE
    mrow = _mod_row_map(tm, n_lat, seq, n_batch)
    tiles_per_seq = seq // tm

    def rope_row(i):
        return jnp.where(i * tm < n_lat, i % tiles_per_seq, tiles_per_seq)

    return pl.pallas_call(
        _in_odd_kernel,
        grid=(n // tm,),
        in_specs=[
            pl.BlockSpec((tm, d), lambda i: (i, 0)),
            pl.BlockSpec((1, 1, 6 * d), lambda i: (mrow(i), 0, 0)),
            pl.BlockSpec((1, d), lambda i: (0, 0)),
            pl.BlockSpec(w.shape, lambda i: (0, 0)),
            pl.BlockSpec((tm, HEAD_DIM), lambda i: (rope_row(i), 0)),
            pl.BlockSpec((tm, HEAD_DIM), lambda i: (rope_row(i), 0)),
        ],
        out_specs=[
            pl.BlockSpec((tm, ATT_HEADS * HEAD_DIM), lambda i: (i, 0)),
            pl.BlockSpec((tm, 2 * ATT_KV_HEADS * HEAD_DIM), lambda i: (i, 0)),
        ],
        out_shape=[
            jax.ShapeDtypeStruct((n, ATT_HEADS * HEAD_DIM), BF16),
            jax.ShapeDtypeStruct((n, 2 * ATT_KV_HEADS * HEAD_DIM), BF16),
        ],
        compiler_params=_params(1, VMEM_LIMIT),
        name="in_odd",
    )(xa, mod_l, g, w, cos_t, sin_t)


def _rope_tables(seq, tm):
    t = jnp.arange(seq)
    row = (t // GRID_W).astype(F32)
    col = (t % GRID_W).astype(F32)
    quarter = HEAD_DIM // 4
    inv_freq = ROPE_BASE ** (-jnp.arange(quarter, dtype=F32) / quarter)
    ar = row[:, None] * inv_freq[None, :]
    ac = col[:, None] * inv_freq[None, :]
    cos = jnp.concatenate([jnp.cos(ar), jnp.cos(ar), jnp.cos(ac), jnp.cos(ac)], axis=1)
    sin = jnp.concatenate([-jnp.sin(ar), jnp.sin(ar), -jnp.sin(ac), jnp.sin(ac)], axis=1)
    cos = jnp.concatenate([cos, jnp.ones((tm, HEAD_DIM), F32)], axis=0)
    sin = jnp.concatenate([sin, jnp.zeros((tm, HEAD_DIM), F32)], axis=0)
    return cos, sin


def _even_mix_kernel(qk_ref, v_ref, gate_ref, u_ref, s_ref, dec_ref, xi_ref, zf_ref, zb_ref, gdec_ref,
                     gn_ref, lng_ref, lnb_ref, ws_ref, sgb_ref, o_ref, ds_ref, sp_ref):
    sweep = pl.program_id(1)
    n = pl.program_id(2)
    n_slots = ds_ref.shape[0]
    cb = qk_ref.shape[0]
    lane = lax.broadcasted_iota(I32, (1, 128), 1)
    head_mask = [(lane // RET_QK_DIM == hh).astype(BF16) for hh in range(2)]
    upper_rows = lax.broadcasted_iota(I32, (128, 128), 0) < RET_QK_DIM

    @pl.when(sweep == 0)
    def _increments():
        for grp in range(2):
            k2 = qk_ref[:, 256 + 128 * grp:256 + 128 * (grp + 1)].astype(F32)
            v2 = v_ref[:, 256 * grp:256 * (grp + 1)]
            for half, z_ref in enumerate((zf_ref, zb_ref)):
                kz = (k2 * z_ref[grp]).astype(BF16)
                full = lax.dot_general(kz, v2, _TN, preferred_element_type=F32)
                ds_ref[n, grp, :, 128 * half:128 * (half + 1)] = jnp.where(
                    upper_rows, full[:, 0:128], full[:, 128:256])

    @pl.when(jnp.logical_and(sweep == 1, n == 0))
    def _scan():
        sp_ref[0] = jnp.zeros(sp_ref.shape[1:], F32)
        fwd = ds_ref[0, :, :, 0:128]
        for j in range(1, n_slots):
            sp_ref[j, :, :, 0:128] = fwd
            fwd = fwd * gdec_ref[:, :, 0:128] + ds_ref[j, :, :, 0:128]
        bwd = ds_ref[0, :, :, 128:256]
        for j in range(n_slots - 1, 0, -1):
            sp_ref[j, :, :, 128:256] = bwd
            bwd = bwd * gdec_ref[:, :, 128:256] + ds_ref[j, :, :, 128:256]

    @pl.when(sweep == 1)
    def _outputs():
        for grp in range(2):
            q2 = qk_ref[:, 128 * grp:128 * (grp + 1)]
            k2 = qk_ref[:, 256 + 128 * grp:256 + 128 * (grp + 1)]
            state = sp_ref[n, grp].astype(BF16)
            for hh in range(2):
                hd = 2 * grp + hh
                cols = slice(128 * hd, 128 * (hd + 1))
                qm = q2 * head_mask[hh]
                sc = lax.dot_general(qm, k2, _NT, preferred_element_type=F32) * dec_ref[hd]
                inner = jnp.dot(sc.astype(BF16), v_ref[:, cols], preferred_element_type=F32)
                cross = jnp.dot(qm, state, preferred_element_type=F32) * xi_ref[hd]
                o = inner + cross[:, 0:128] + cross[:, 128:256]
                mu = jnp.mean(o, axis=-1, keepdims=True)
                oc = o - mu
                var = jnp.mean(oc * oc, axis=-1, keepdims=True)
                y = oc * lax.rsqrt(var + EPS) * gn_ref[:, cols]
                y = y * _silu(gate_ref[:, cols].astype(F32))
                o_ref[:, cols] = y.astype(BF16)
        for sub in range(cb // SG_CHUNK):
            rows = slice(SG_CHUNK * sub, SG_CHUNK * (sub + 1))
            s = s_ref[rows, :].astype(F32)
            mu = jnp.mean(s, axis=-1, keepdims=True)
            sc_ = s - mu
            var = jnp.mean(sc_ * sc_, axis=-1, keepdims=True)
            sn = (sc_ * lax.rsqrt(var + EPS) * lng_ref[...] + lnb_ref[...]).astype(BF16)
            for g in range(SG_GROUPS):
                cols = slice(128 * g, 128 * (g + 1))
                mixed = jnp.dot(ws_ref[g], sn[:, cols], preferred_element_type=F32) + sgb_ref[:, cols]
                o_ref[rows, 512 + 128 * g:512 + 128 * (g + 1)] = (
                    u_ref[rows, cols].astype(F32) * mixed).astype(BF16)


def _retention_tables(log_decay, cb):
    lg = -jnp.exp(log_decay.astype(F32))
    pos = jnp.arange(cb, dtype=F32)
    diff = pos[:, None] - pos[None, :]
    lower = jnp.where(diff >= 0, jnp.exp(lg[0][:, None, None] * jnp.maximum(diff, 0.0)[None]), 0.0)
    upper = jnp.where(diff <= 0, jnp.exp(lg[1][:, None, None] * jnp.maximum(-diff, 0.0)[None]), 0.0)
    dec = lower + upper
    xi_f = jnp.exp(lg[0][:, None] * (pos[None, :] + 1.0))
    xi_b = jnp.exp(lg[1][:, None] * (cb - pos[None, :]))
    xi = jnp.concatenate([jnp.broadcast_to(xi_f[:, :, None], (RET_HEADS, cb, 128)),
                          jnp.broadcast_to(xi_b[:, :, None], (RET_HEADS, cb, 128))], axis=2)
    zeta_f = jnp.exp(lg[0][:, None] * (cb - 1.0 - pos[None, :]))
    zeta_b = jnp.exp(lg[1][:, None] * pos[None, :])

    def lanes(z):
        return jnp.repeat(z.reshape(2, 2, cb), RET_QK_DIM, axis=1).transpose(0, 2, 1)

    gstep = jnp.exp(lg * cb)

    def rows(gv):
        return jnp.broadcast_to(jnp.repeat(gv.reshape(2, 2), RET_QK_DIM, axis=1)[:, :, None], (2, 128, 128))

    gdec = jnp.concatenate([rows(gstep[0]), rows(gstep[1])], axis=2)
    return dec, xi, lanes(zeta_f), lanes(zeta_b), gdec


def _even_mix(p, tabs, gn_g, ln_g, ln_b, w_s, sg_bias, n_batch, seq, ctx_len):
    n = p.shape[0]
    cb = RET_BLOCK
    assert ctx_len == cb and seq % cb == 0
    nl = seq // cb
    ctx0 = n_batch * seq // cb

    def rb(b, n_):
        return jnp.where(n_ == 0, ctx0 + b, b * nl + n_ - 1)

    def late(col):
        return lambda b, s, n_: (rb(b, n_ * s), col)

    dec, xi, zf, zb, gdec = tabs
    full = lambda a: pl.BlockSpec(a.shape, lambda b, s, n_, _nd=a.ndim: (0,) * _nd)
    return pl.pallas_call(
        _even_mix_kernel,
        grid=(n_batch, 2, nl + 1),
        in_specs=[
            pl.BlockSpec((cb, 512), lambda b, s, n_: (rb(b, n_), 0)),
            pl.BlockSpec((cb, 512), lambda b, s, n_: (rb(b, n_), 1)),
            pl.BlockSpec((cb, 512), late(2)),
            pl.BlockSpec((cb, 512), late(3)),
            pl.BlockSpec((cb, 512), late(4)),
            full(dec), full(xi), full(zf), full(zb), full(gdec),
            full(gn_g), full(ln_g), full(ln_b), full(w_s), full(sg_bias),
        ],
        out_specs=pl.BlockSpec((cb, 1024), late(0)),
        out_shape=jax.ShapeDtypeStruct((n, 1024), BF16),
        scratch_shapes=[pltpu.VMEM((nl + 1, 2, 128, 256), F32), pltpu.VMEM((nl + 1, 2, 128, 256), F32)],
        compiler_params=_params(3, VMEM_LIMIT),
        name="even_mix",
    )(p, p, p, p, p, dec, xi, zf, zb, gdec, gn_g, ln_g, ln_b, w_s, sg_bias)


def _attn_kernel(q_ref, kvp_ref, kvc_ref, kvn_ref, kvx_ref, bias_ref, sink_ref, o_ref):
    nloc = 3 * ATT_BLOCK
    for kvh in range(ATT_KV_HEADS):
        kcols = slice(HEAD_DIM * kvh, HEAD_DIM * (kvh + 1))
        vcols = slice(HEAD_DIM * (ATT_KV_HEADS + kvh), HEAD_DIM * (ATT_KV_HEADS + kvh + 1))
        k_all = jnp.concatenate([kvp_ref[:, kcols], kvc_ref[:, kcols], kvn_ref[:, kcols], kvx_ref[:, kcols]], axis=0)
        v_all = jnp.concatenate([kvp_ref[:, vcols], kvc_ref[:, vcols], kvn_ref[:, vcols], kvx_ref[:, vcols]], axis=0)
        heads = [ATT_GROUP * kvh + j for j in range(ATT_GROUP)]
        q4 = jnp.concatenate([q_ref[:, HEAD_DIM * hd:HEAD_DIM * (hd + 1)] for hd in heads], axis=0)
        s = lax.dot_general(q4, k_all, _NT, preferred_element_type=F32)
        s_loc = s[:, 0:nloc] + bias_ref[0]
        s_ctx = s[:, nloc:]
        sink = jnp.concatenate(
            [jnp.broadcast_to(sink_ref[hd:hd + 1, 0:1], (ATT_BLOCK, 1)) for hd in heads], axis=0)
        m = jnp.maximum(jnp.maximum(jnp.max(s_loc, axis=-1, keepdims=True),
                                    jnp.max(s_ctx, axis=-1, keepdims=True)), sink)
        p_loc = jnp.exp(s_loc - m)
        p_ctx = jnp.exp(s_ctx - m)
        denom = (jnp.sum(p_loc, axis=-1, keepdims=True) + jnp.sum(p_ctx, axis=-1, keepdims=True)
                 + jnp.exp(sink - m))
        o = (jnp.dot(p_loc.astype(BF16), v_all[0:nloc], preferred_element_type=F32)
             + jnp.dot(p_ctx.astype(BF16), v_all[nloc:], preferred_element_type=F32)) / denom
        for j, hd in enumerate(heads):
            o_ref[:, HEAD_DIM * hd:HEAD_DIM * (hd + 1)] = o[ATT_BLOCK * j:ATT_BLOCK * (j + 1)].astype(BF16)


def _attn_bias():
    i = np.arange(ATT_BLOCK)[:, None]
    j = np.arange(3 * ATT_BLOCK)[None, :] - ATT_BLOCK
    band = np.abs(i - j) <= WINDOW
    cases = [band & (j >= 0), band, band & (j < ATT_BLOCK), np.zeros_like(band)]
    bias = np.stack([np.where(c, 0.0, NEG) for c in cases]).astype(np.float32)
    return jnp.asarray(np.tile(bias, (1, ATT_GROUP, 1)))


def _attention(q, kv, bias, sink_t, n_batch, seq, ctx_len, need_ctx):
    blk = ATT_BLOCK
    nb = seq // blk
    ncb = ctx_len // blk
    n_lat = n_batch * seq
    steps = nb + (ncb if need_ctx else 0)
    n_out = n_lat + (n_batch * ctx_len if need_ctx else 0)

    def qrow(b, n_):
        return jnp.where(n_ < nb, b * nb + n_, n_lat // blk + b * ncb + (n_ - nb))

    def krow(off):
        def index(b, n_):
            loc = jnp.clip(jnp.minimum(n_, nb - 1) + off, 0, nb - 1)
            return (b * nb + loc, 0)
        return index

    def case(b, n_):
        c = jnp.where(n_ == 0, 0, jnp.where(n_ == nb - 1, 2, 1))
        return (jnp.where(n_ < nb, c, 3), 0, 0)

    return pl.pallas_call(
        _attn_kernel,
        grid=(n_batch, steps),
        in_specs=[
            pl.BlockSpec((blk, q.shape[1]), lambda b, n_: (qrow(b, n_), 0)),
            pl.BlockSpec((blk, kv.shape[1]), krow(-1)),
            pl.BlockSpec((blk, kv.shape[1]), krow(0)),
            pl.BlockSpec((blk, kv.shape[1]), krow(1)),
            pl.BlockSpec((ctx_len, kv.shape[1]), lambda b, n_: (n_lat // ctx_len + b, 0)),
            pl.BlockSpec((1,) + bias.shape[1:], case),
            pl.BlockSpec(sink_t.shape, lambda b, n_: (0, 0)),
        ],
        out_specs=pl.BlockSpec((blk, q.shape[1]), lambda b, n_: (qrow(b, n_), 0)),
        out_shape=jax.ShapeDtypeStruct((n_out, q.shape[1]), BF16),
        compiler_params=_params(2, VMEM_LIMIT),
        name="window_attention",
    )(q, kv, kv, kv, kv, bias, sink_t)


def _route(logits):
    m = jnp.max(logits, axis=0, keepdims=True)
    e = jnp.exp(logits - m)
    p = e / jnp.sum(e, axis=0, keepdims=True)
    rows = [p[i:i + 1, :] for i in range(N_EXPERTS)]
    best = None
    gsel = None
    for g in range(N_GROUPS):
        a, b, c, d = rows[4 * g:4 * g + 4]
        m1, n1 = jnp.maximum(a, b), jnp.minimum(a, b)
        m2, n2 = jnp.maximum(c, d), jnp.minimum(c, d)
        score = jnp.maximum(m1, m2) + jnp.maximum(jnp.minimum(m1, m2), jnp.maximum(n1, n2))
        if g == 0:
            best, gsel = score, jnp.zeros(score.shape, I32)
        else:
            upd = score > best
            gsel = jnp.where(upd, g, gsel)
            best = jnp.where(upd, score, best)
    v = [jnp.where(gsel == 0, rows[j], jnp.where(gsel == 1, rows[4 + j], jnp.where(gsel == 2, rows[8 + j], rows[12 + j])))
         for j in range(EXPERTS_PER_GROUP)]
    b1, i1 = v[0], jnp.zeros(gsel.shape, I32)
    for j in range(1, 4):
        upd = v[j] > b1
        i1 = jnp.where(upd, j, i1)
        b1 = jnp.where(upd, v[j], b1)
    b2, i2 = jnp.full(b1.shape, -1.0, F32), jnp.zeros(gsel.shape, I32)
    for j in range(4):
        upd = jnp.logical_and(i1 != j, v[j] > b2)
        i2 = jnp.where(upd, j, i2)
        b2 = jnp.where(upd, v[j], b2)
    lo = jnp.minimum(i1, i2)
    hi = jnp.maximum(i1, i2)
    pair = jnp.where(lo == 0, hi - 1, jnp.where(lo == 1, hi + 1, 5))
    return gsel * N_PAIRS + pair


def _post_kernel(mix_ref, x_ref, mod_ref, g2_ref, wo_ref, wrh_ref, wrl_ref, br_ref, tri_ref,
                 xo_ref, bk_ref, rank_ref, cnt_ref, carry_ref):
    @pl.when(pl.program_id(0) == 0)
    def _():
        carry_ref[...] = jnp.zeros(carry_ref.shape, F32)

    d = x_ref.shape[1]
    tm = x_ref.shape[0]
    y = jnp.dot(mix_ref[...], wo_ref[...], preferred_element_type=F32)
    x = x_ref[...] + mod_ref[0, :, 2 * d:3 * d] * y
    xo_ref[...] = x
    h2 = _rms_mod(x, g2_ref[...], mod_ref[0, :, 4 * d:5 * d], mod_ref[0, :, 3 * d:4 * d])
    hi = h2.astype(BF16)
    lo = (h2 - hi.astype(F32)).astype(BF16)
    logits = (lax.dot_general(wrh_ref[...], hi, _NT, preferred_element_type=F32)
              + lax.dot_general(wrh_ref[...], lo, _NT, preferred_element_type=F32)
              + lax.dot_general(wrl_ref[...], hi, _NT, preferred_element_type=F32)
              + br_ref[:, 0:1])
    bucket = _route(logits)
    bk_ref[...] = bucket
    ids = lax.broadcasted_iota(I32, (BUCKET_ROWS, tm), 0)
    onehot = (ids == bucket).astype(F32)
    before = jnp.dot(onehot.astype(BF16), tri_ref[...], preferred_element_type=F32)
    carry = carry_ref[...]
    rank_ref[...] = jnp.sum(onehot * (before + carry[:, 0:1]), axis=0, keepdims=True).astype(I32)
    carry = carry + jnp.sum(onehot, axis=1, keepdims=True)
    carry_ref[...] = carry
    cnt_ref[...] = carry.astype(I32)


def _post(mix, xa, mod_l, g2, wo, wr_hi, wr_lo, br_t, tri, n_act, n_lat, seq, n_batch):
    d = xa.shape[1]
    tm = ROW_TILE
    mrow = _mod_row_map(tm, n_lat, seq, n_batch)
    full = lambda a: pl.BlockSpec(a.shape, lambda i, _nd=a.ndim: (0,) * _nd)
    return pl.pallas_call(
        _post_kernel,
        grid=(n_act // tm,),
        in_specs=[
            pl.BlockSpec((tm, d), lambda i: (i, 0)),
            pl.BlockSpec((tm, d), lambda i: (i, 0)),
            pl.BlockSpec((1, 1, 6 * d), lambda i: (mrow(i), 0, 0)),
            full(g2), full(wo), full(wr_hi), full(wr_lo), full(br_t), full(tri),
        ],
        out_specs=[
            pl.BlockSpec((tm, d), lambda i: (i, 0)),
            pl.BlockSpec((1, tm), lambda i: (0, i)),
            pl.BlockSpec((1, tm), lambda i: (0, i)),
            pl.BlockSpec((BUCKET_ROWS, 128), lambda i: (0, 0)),
        ],
        out_shape=[
            jax.ShapeDtypeStruct((n_act, d), F32),
            jax.ShapeDtypeStruct((1, n_act), I32),
            jax.ShapeDtypeStruct((1, n_act), I32),
            jax.ShapeDtypeStruct((BUCKET_ROWS, 128), I32),
        ],
        scratch_shapes=[pltpu.VMEM((BUCKET_ROWS, 128), F32)],
        compiler_params=_params(1, VMEM_LIMIT),
        name="post_mixer",
    )(mix, xa, mod_l, g2, wo, wr_hi, wr_lo, br_t, tri)


def _dispatch_kernel(pos_ref, x_ref, mod_ref, g2_ref, hs_in_ref, hs_ref, hbuf_ref, sem):
    del hs_in_ref
    d = x_ref.shape[1]
    tm = x_ref.shape[0]
    hbuf_ref[...] = _rms_mod(x_ref[...], g2_ref[...], mod_ref[0, :, 4 * d:5 * d], mod_ref[0, :, 3 * d:4 * d])

    def body(j, carry):
        for u in range(DMA_UNROLL):
            r = j * DMA_UNROLL + u
            dst = pos_ref[0, 0, r]
            pltpu.make_async_copy(hbuf_ref.at[pl.ds(r, 1), :], hs_ref.at[pl.ds(dst, 1), :], sem).start(priority=u % 2)
        return carry
    lax.fori_loop(0, tm // DMA_UNROLL, body, 0)
    pltpu.make_async_copy(hbuf_ref, hs_ref.at[pl.ds(0, tm), :], sem).wait()


def _dispatch(pos3, xa, mod_l, g2, hs, n_lat, seq, n_batch):
    n, d = xa.shape
    tm = ROW_TILE
    mrow = _mod_row_map(tm, n_lat, seq, n_batch)
    return pl.pallas_call(
        _dispatch_kernel,
        grid=(n // tm,),
        in_specs=[
            pl.BlockSpec((1, 1, tm), lambda i: (i, 0, 0), memory_space=pltpu.SMEM),
            pl.BlockSpec((tm, d), lambda i: (i, 0)),
            pl.BlockSpec((1, 1, 6 * d), lambda i: (mrow(i), 0, 0)),
            pl.BlockSpec((1, d), lambda i: (0, 0)),
            pl.BlockSpec(memory_space=pl.ANY),
        ],
        out_specs=pl.BlockSpec(memory_space=pl.ANY),
        out_shape=jax.ShapeDtypeStruct(hs.shape, F32),
        scratch_shapes=[pltpu.VMEM((tm, d), F32), pltpu.SemaphoreType.DMA(())],
        input_output_aliases={4: 0},
        compiler_params=_params(1, VMEM_LIMIT),
        name="dispatch",
    )(pos3, xa, mod_l, g2, hs)


def _moe_kernel(ea_ref, eb_ref, nu_ref, h_ref, wga_ref, wua_ref, wda_ref, wgb_ref, wub_ref, wdb_ref,
                wrt_ref, brt_ref, o_ref):
    t = pl.program_id(0)

    @pl.when(t >= nu_ref[0])
    def _():
        o_ref[...] = jnp.zeros(o_ref.shape, F32)

    @pl.when(t < nu_ref[0])
    def _():
        ea = ea_ref[t]
        eb = eb_ref[t]
        h = h_ref[...]
        hb = h.astype(BF16)
        wdiff = wrt_ref[pl.ds(ea, 1), :] - wrt_ref[pl.ds(eb, 1), :]
        bdiff = brt_ref[pl.ds(ea, 1), 0:1] - brt_ref[pl.ds(eb, 1), 0:1]
        wa = jax.nn.sigmoid(jnp.sum(h * wdiff, axis=-1, keepdims=True) + bdiff)

        def expert(wg_ref, wu_ref, wd_ref):
            f = wg_ref.shape[2]
            acc = None
            for c in range(f // 512):
                cols = slice(512 * c, 512 * (c + 1))
                g = jnp.dot(hb, wg_ref[0, :, cols], preferred_element_type=F32)
                u = jnp.dot(hb, wu_ref[0, :, cols], preferred_element_type=F32)
                part = jnp.dot((_silu(g) * u).astype(BF16), wd_ref[0, cols, :], preferred_element_type=F32)
                acc = part if acc is None else acc + part
            return acc

        o_ref[...] = wa * expert(wga_ref, wua_ref, wda_ref) + (1.0 - wa) * expert(wgb_ref, wub_ref, wdb_ref)


def _moe(ea, eb, nu, hs, wg, wu, wd, wr_t, br_t):
    p_rows, d = hs.shape
    tm = MOE_TILE
    f = wg.shape[2]
    wa_spec = lambda shape: pl.BlockSpec(shape, lambda t, ea_, eb_, nu_: (ea_[t], 0, 0))
    wb_spec = lambda shape: pl.BlockSpec(shape, lambda t, ea_, eb_, nu_: (eb_[t], 0, 0))
    grid_spec = pltpu.PrefetchScalarGridSpec(
        num_scalar_prefetch=3,
        grid=(p_rows // tm,),
        in_specs=[
            pl.BlockSpec((tm, d), lambda t, ea_, eb_, nu_: (t, 0)),
            wa_spec((1, d, f)), wa_spec((1, d, f)), wa_spec((1, f, d)),
            wb_spec((1, d, f)), wb_spec((1, d, f)), wb_spec((1, f, d)),
            pl.BlockSpec(wr_t.shape, lambda t, ea_, eb_, nu_: (0, 0)),
            pl.BlockSpec(br_t.shape, lambda t, ea_, eb_, nu_: (0, 0)),
        ],
        out_specs=pl.BlockSpec((tm, d), lambda t, ea_, eb_, nu_: (t, 0)),
    )
    return pl.pallas_call(
        _moe_kernel,
        grid_spec=grid_spec,
        out_shape=jax.ShapeDtypeStruct((p_rows, d), F32),
        compiler_params=_params(1, VMEM_LIMIT),
        name="experts",
    )(ea, eb, nu, hs, wg, wu, wd, wg, wu, wd, wr_t, br_t)


def _combine_kernel(pos_ref, posn_ref, x_ref, mod_ref, fg_ref, ys_ref, o_ref, gbuf_ref, sem, *, final):
    i = pl.program_id(0)
    slot = i % 2
    d = x_ref.shape[1]
    tm = x_ref.shape[0]

    def issue(p_ref, s):
        dst = gbuf_ref.at[s]

        def body(j, carry):
            for u in range(DMA_UNROLL):
                r = j * DMA_UNROLL + u
                src = p_ref[0, 0, r]
                pltpu.make_async_copy(ys_ref.at[pl.ds(src, 1), :], dst.at[pl.ds(r, 1), :], sem.at[s]).start(priority=u % 2)
            return carry
        lax.fori_loop(0, tm // DMA_UNROLL, body, 0)

    @pl.when(i == 0)
    def _():
        issue(pos_ref, 0)

    @pl.when(i + 1 < pl.num_programs(0))
    def _():
        issue(posn_ref, 1 - slot)

    pltpu.make_async_copy(ys_ref.at[pl.ds(0, tm), :], gbuf_ref.at[slot], sem.at[slot]).wait()
    x = x_ref[...] + mod_ref[0, :, 5 * d:6 * d] * gbuf_ref[slot]
    if final:
        x = x * lax.rsqrt(jnp.mean(x * x, axis=-1, keepdims=True) + EPS) * fg_ref[...]
    o_ref[...] = x


def _combine(pos3, xa, mod_l, fg, ys, n_lat, seq, n_batch, final):
    n, d = xa.shape
    tm = ROW_TILE
    steps = n // tm
    mrow = _mod_row_map(tm, n_lat, seq, n_batch)
    return pl.pallas_call(
        functools.partial(_combine_kernel, final=final),
        grid=(steps,),
        in_specs=[
            pl.BlockSpec((1, 1, tm), lambda i: (i, 0, 0), memory_space=pltpu.SMEM),
            pl.BlockSpec((1, 1, tm), lambda i: (jnp.minimum(i + 1, steps - 1), 0, 0), memory_space=pltpu.SMEM),
            pl.BlockSpec((tm, d), lambda i: (i, 0)),
            pl.BlockSpec((1, 1, 6 * d), lambda i: (mrow(i), 0, 0)),
            pl.BlockSpec((1, d), lambda i: (0, 0)),
            pl.BlockSpec(memory_space=pl.ANY),
        ],
        out_specs=pl.BlockSpec((tm, d), lambda i: (i, 0)),
        out_shape=jax.ShapeDtypeStruct((n, d), F32),
        scratch_shapes=[pltpu.VMEM((2, tm, d), F32), pltpu.SemaphoreType.DMA((2,))],
        compiler_params=_params(1, VMEM_LIMIT),
        name="combine",
    )(pos3, pos3, xa, mod_l, fg, ys)


def _bucket_plan(bucket, rank, counts, n_tiles, tile):
    cnt = counts[:N_BUCKETS, 0]
    padded = ((cnt + tile - 1) // tile) * tile
    ends = jnp.cumsum(padded)
    offs = ends - padded
    pos = offs[bucket[0]] + rank[0]
    n_used = ends[-1] // tile
    t_eff = jnp.minimum(jnp.arange(n_tiles, dtype=I32), jnp.maximum(n_used - 1, 0))
    tb = jnp.minimum(jnp.sum((ends[None, :] <= (t_eff * tile)[:, None]).astype(I32), axis=1), N_BUCKETS - 1)
    pairs = np.array([(a, b) for a in range(4) for b in range(a + 1, 4)], dtype=np.int32)
    ea = (tb // N_PAIRS) * EXPERTS_PER_GROUP + jnp.asarray(pairs[:, 0])[tb % N_PAIRS]
    eb = (tb // N_PAIRS) * EXPERTS_PER_GROUP + jnp.asarray(pairs[:, 1])[tb % N_PAIRS]
    return pos.astype(I32), ea.astype(I32), eb.astype(I32), n_used.astype(I32).reshape(1)


def kernel(x, c, ctx, c_ctx, w_mod, b_mod, norm_g, w_in_even, w_out_even, ret_log_decay, ret_gn_g, sg_ln_g, sg_ln_b, sg_w, sg_b, w_qkv_odd, w_o_odd, attn_sink, w_router, b_router, w_e_gate, w_e_up, w_e_down, final_g):
    n_batch, seq, d = x.shape
    ctx_len = ctx.shape[1]
    depth = w_mod.shape[0]
    n_lat = n_batch * seq
    n_all = n_lat + n_batch * ctx_len
    tm = ROW_TILE
    assert seq % tm == 0 and (n_batch * ctx_len) % tm == 0

    xa = jnp.concatenate([x.reshape(n_lat, d), ctx.reshape(n_batch * ctx_len, d)], axis=0)

    mod_rows = -(-(n_batch + 1) // 16) * 16
    c_rows = jnp.zeros((mod_rows, d), F32).at[:n_batch].set(c).at[n_batch].set(c_ctx)
    mod = _modulation(c_rows, w_mod, b_mod).reshape(depth, mod_rows, 1, 6 * d)

    wr_t = w_router.T.astype(F32)
    wr_hi = wr_t.astype(BF16)
    wr_lo = (wr_t - wr_hi.astype(F32)).astype(BF16)
    br_t = jnp.broadcast_to(b_router.astype(F32)[:, None], (N_EXPERTS, 128))
    tri = jnp.triu(jnp.ones((tm, tm), BF16), k=1)
    cos_t, sin_t = _rope_tables(seq, tm)
    bias = _attn_bias()

    n_tiles = n_all // MOE_TILE + N_BUCKETS
    hs = jnp.zeros((n_tiles * MOE_TILE, d), F32)

    for l in range(depth):
        i = l // 2
        last = l == depth - 1
        mod_l = mod[l]
        g1 = norm_g[l, 0].reshape(1, d)
        g2 = norm_g[l, 1].reshape(1, d)
        if l % 2 == 0:
            p = _in_even(xa, mod_l, g1, w_in_even[i].astype(BF16), n_lat, seq, n_batch)
            tabs = _retention_tables(ret_log_decay[i], RET_BLOCK)
            sg_bias = jnp.repeat(sg_b[i].astype(F32).T, SG_CHUNK, axis=1)
            mix = _even_mix(p, tabs, ret_gn_g[i].reshape(1, -1), sg_ln_g[i].reshape(1, -1), sg_ln_b[i].reshape(1, -1),
                            sg_w[i].astype(BF16), sg_bias, n_batch, seq, ctx_len)
            wo = w_out_even[i].astype(BF16)
        else:
            q, kv = _in_odd(xa, mod_l, g1, w_qkv_odd[i].astype(BF16), cos_t, sin_t, n_lat, seq, n_batch)
            sink_t = jnp.broadcast_to(attn_sink[i].astype(F32)[:, None], (ATT_HEADS, 128))
            mix = _attention(q, kv, bias, sink_t, n_batch, seq, ctx_len, need_ctx=not last)
            wo = w_o_odd[i].astype(BF16)
        n_act = n_lat if last else n_all
        xa, bucket, rank, counts = _post(mix, xa, mod_l, g2, wo, wr_hi, wr_lo, br_t, tri, n_act, n_lat, seq, n_batch)
        pos, ea, eb, n_used = _bucket_plan(bucket, rank, counts, n_tiles, MOE_TILE)
        pos3 = pos.reshape(n_act // tm, 1, tm)
        hs = _dispatch(pos3, xa, mod_l, g2, hs, n_lat, seq, n_batch)
        ys = _moe(ea, eb, n_used, hs, w_e_gate[l].astype(BF16), w_e_up[l].astype(BF16), w_e_down[l].astype(BF16),
                  wr_t, br_t)
        xa = _combine(pos3, xa, mod_l, final_g.reshape(1, d), ys, n_lat, seq, n_batch, final=last)
    return xa.reshape(n_batch, seq, d)
```

```python
import functools

import jax
import jax.numpy as jnp
import numpy as np
from jax import lax
from jax.experimental import pallas as pl
from jax.experimental.pallas import tpu as pltpu

F32 = jnp.float32
BF16 = jnp.bfloat16
I32 = jnp.int32

EPS = 1e-6
NEG = -1e30
GRID_W = 64
ROPE_BASE = 10000.0
RET_HEADS = 4
RET_QK_DIM = 64
RET_V_DIM = 128
SG_GROUPS = 4
SG_CHUNK = 128
RET_BLOCK = 256
ATT_HEADS = 8
ATT_KV_HEADS = 2
ATT_GROUP = ATT_HEADS // ATT_KV_HEADS
HEAD_DIM = 128
ATT_BLOCK = 128
WINDOW = 128
N_EXPERTS = 16
N_GROUPS = 4
EXPERTS_PER_GROUP = 4
N_PAIRS = 6
N_BUCKETS = N_GROUPS * N_PAIRS
PAIR_SLOTS = ((0, 1), (0, 2), (0, 3), (1, 3), (1, 2), (3, 2))
BUCKET_ROWS = 32

ROW_TILE = 512
MOE_TILE = 256
DMA_UNROLL = 8
ROW_SUB = 8
VMEM_LIMIT = 56 * 1024 * 1024

_NT = (((1,), (1,)), ((), ()))
_TN = (((0,), (0,)), ((), ()))


def _params(n_axes, vmem=None):
    return pltpu.CompilerParams(dimension_semantics=("arbitrary",) * n_axes, vmem_limit_bytes=vmem)


def _silu(v):
    return v * jax.nn.sigmoid(v)


def _rms_mod(x, g, scale, shift):
    y = x * lax.rsqrt(jnp.mean(x * x, axis=-1, keepdims=True) + EPS) * g
    return y * (1.0 + scale) + shift


def _mod_kernel(c_ref, w_ref, b_ref, o_ref):
    a = _silu(c_ref[...]).astype(BF16)
    o_ref[0] = jnp.dot(a, w_ref[0].astype(BF16), preferred_element_type=F32) + b_ref[0]


def _modulation(c_rows, w_mod, b_mod):
    depth, d, six_d = w_mod.shape
    mr = c_rows.shape[0]
    tn = 1536
    return pl.pallas_call(
        _mod_kernel,
        grid=(depth, six_d // tn),
        in_specs=[
            pl.BlockSpec((mr, d), lambda l, j: (0, 0)),
            pl.BlockSpec((1, d, tn), lambda l, j: (l, 0, j)),
            pl.BlockSpec((1, 1, tn), lambda l, j: (l, 0, j)),
        ],
        out_specs=pl.BlockSpec((1, mr, tn), lambda l, j: (l, 0, j)),
        out_shape=jax.ShapeDtypeStruct((depth, mr, six_d), F32),
        compiler_params=_params(2, VMEM_LIMIT),
        name="modulation",
    )(c_rows, w_mod, b_mod.reshape(depth, 1, six_d))


def _mod_row_map(tm, n_lat, seq, n_batch):
    def index(i):
        row0 = i * tm
        return jnp.where(row0 < n_lat, row0 // seq, n_batch)
    return index


def _in_even_kernel(x_ref, mod_ref, g_ref, w_ref, o_ref):
    d = x_ref.shape[1]
    h = _rms_mod(x_ref[...], g_ref[...], mod_ref[0, :, d:2 * d], mod_ref[0, :, 0:d]).astype(BF16)
    for c in range(5):
        p = jnp.dot(h, w_ref[:, 512 * c:512 * (c + 1)], preferred_element_type=F32)
        if c == 0:
            col = lax.broadcasted_iota(I32, p.shape, 1)
            p = jnp.where(col >= 256, p * (RET_QK_DIM ** -0.5), p)
        elif c >= 3:
            p = jax.nn.gelu(p)
        o_ref[:, 512 * c:512 * (c + 1)] = p.astype(BF16)


def _in_even(xa, mod_l, g, w, n_lat, seq, n_batch):
    n, d = xa.shape
    tm = ROW_TILE
    mrow = _mod_row_map(tm, n_lat, seq, n_batch)
    return pl.pallas_call(
        _in_even_kernel,
        grid=(n // tm,),
        in_specs=[
            pl.BlockSpec((tm, d), lambda i: (i, 0)),
            pl.BlockSpec((1, 1, 6 * d), lambda i: (mrow(i), 0, 0)),
            pl.BlockSpec((1, d), lambda i: (0, 0)),
            pl.BlockSpec(w.shape, lambda i: (0, 0)),
        ],
        out_specs=pl.BlockSpec((tm, w.shape[1]), lambda i: (i, 0)),
        out_shape=jax.ShapeDtypeStruct((n, w.shape[1]), BF16),
        compiler_params=_params(1, VMEM_LIMIT),
        name="in_even",
    )(xa, mod_l, g, w)


def _in_odd_kernel(x_ref, mod_ref, g_ref, w_ref, cos_ref, sin_ref, q_ref, kv_ref):
    d = x_ref.shape[1]
    tm = x_ref.shape[0]
    h = _rms_mod(x_ref[...], g_ref[...], mod_ref[0, :, d:2 * d], mod_ref[0, :, 0:d]).astype(BF16)
    cos = cos_ref[...]
    sin = sin_ref[...]
    lane = lax.broadcasted_iota(I32, (tm, HEAD_DIM), 1)
    first = (lane % 64) < 32

    def rope(p):
        partner = jnp.where(first, pltpu.roll(p, 96, 1), pltpu.roll(p, 32, 1))
        return p * cos + partner * sin

    scale = HEAD_DIM ** -0.5
    for c in range(3):
        p = jnp.dot(h, w_ref[:, 512 * c:512 * (c + 1)], preferred_element_type=F32)
        for j in range(4):
            pj = p[:, 128 * j:128 * (j + 1)]
            if c < 2:
                q_ref[:, 512 * c + 128 * j:512 * c + 128 * (j + 1)] = (rope(pj) * scale).astype(BF16)
            elif j < 2:
                kv_ref[:, 128 * j:128 * (j + 1)] = rope(pj).astype(BF16)
            else:
                kv_ref[:, 128 * j:128 * (j + 1)] = pj.astype(BF16)


def _in_odd(xa, mod_l, g, w, cos_t, sin_t, n_lat, seq, n_batch):
    n, d = xa.shape
    tm = ROW_TILE
    mrow = _mod_row_map(tm, n_lat, seq, n_batch)
    tiles_per_seq = seq // tm

    def rope_row(i):
        return jnp.where(i * tm < n_lat, i % tiles_per_seq, tiles_per_seq)

    return pl.pallas_call(
        _in_odd_kernel,
        grid=(n // tm,),
        in_specs=[
            pl.BlockSpec((tm, d), lambda i: (i, 0)),
            pl.BlockSpec((1, 1, 6 * d), lambda i: (mrow(i), 0, 0)),
            pl.BlockSpec((1, d), lambda i: (0, 0)),
            pl.BlockSpec(w.shape, lambda i: (0, 0)),
            pl.BlockSpec((tm, HEAD_DIM), lambda i: (rope_row(i), 0)),
            pl.BlockSpec((tm, HEAD_DIM), lambda i: (rope_row(i), 0)),
        ],
        out_specs=[
            pl.BlockSpec((tm, ATT_HEADS * HEAD_DIM), lambda i: (i, 0)),
            pl.BlockSpec((tm, 2 * ATT_KV_HEADS * HEAD_DIM), lambda i: (i, 0)),
        ],
        out_shape=[
            jax.ShapeDtypeStruct((n, ATT_HEADS * HEAD_DIM), BF16),
            jax.ShapeDtypeStruct((n, 2 * ATT_KV_HEADS * HEAD_DIM), BF16),
        ],
        compiler_params=_params(1, VMEM_LIMIT),
        name="in_odd",
    )(xa, mod_l, g, w, cos_t, sin_t)


def _rope_tables(seq, tm):
    t = jnp.arange(seq)
    row = (t // GRID_W).astype(F32)
    col = (t % GRID_W).astype(F32)
    quarter = HEAD_DIM // 4
    inv_freq = ROPE_BASE ** (-jnp.arange(quarter, dtype=F32) / quarter)
    ar = row[:, None] * inv_freq[None, :]
    ac = col[:, None] * inv_freq[None, :]
    cos = jnp.concatenate([jnp.cos(ar), jnp.cos(ar), jnp.cos(ac), jnp.cos(ac)], axis=1)
    sin = jnp.concatenate([-jnp.sin(ar), jnp.sin(ar), -jnp.sin(ac), jnp.sin(ac)], axis=1)
    cos = jnp.concatenate([cos, jnp.ones((tm, HEAD_DIM), F32)], axis=0)
    sin = jnp.concatenate([sin, jnp.zeros((tm, HEAD_DIM), F32)], axis=0)
    return cos, sin


def _even_mix_kernel(qk_ref, v_ref, gate_ref, u_ref, s_ref, dec_ref, xi_ref, zf_ref, zb_ref, gdec_ref,
                     gn_ref, lng_ref, lnb_ref, ws_ref, sgb_ref, o_ref, ds_ref, sp_ref):
    sweep = pl.program_id(1)
    n = pl.program_id(2)
    n_slots = ds_ref.shape[0]
    cb = qk_ref.shape[0]
    lane = lax.broadcasted_iota(I32, (1, 128), 1)
    head_mask = [(lane // RET_QK_DIM == hh).astype(BF16) for hh in range(2)]
    upper_rows = lax.broadcasted_iota(I32, (128, 128), 0) < RET_QK_DIM

    @pl.when(sweep == 0)
    def _increments():
        for grp in range(2):
            k2 = qk_ref[:, 256 + 128 * grp:256 + 128 * (grp + 1)].astype(F32)
            v2 = v_ref[:, 256 * grp:256 * (grp + 1)]
            for half, z_ref in enumerate((zf_ref, zb_ref)):
                kz = (k2 * z_ref[grp]).astype(BF16)
                full = lax.dot_general(kz, v2, _TN, preferred_element_type=F32)
                ds_ref[n, grp, :, 128 * half:128 * (half + 1)] = jnp.where(
                    upper_rows, full[:, 0:128], full[:, 128:256])

    @pl.when(jnp.logical_and(sweep == 1, n == 0))
    def _scan():
        sp_ref[0] = jnp.zeros(sp_ref.shape[1:], F32)
        fwd = ds_ref[0, :, :, 0:128]
        for j in range(1, n_slots):
            sp_ref[j, :, :, 0:128] = fwd
            fwd = fwd * gdec_ref[:, :, 0:128] + ds_ref[j, :, :, 0:128]
        bwd = ds_ref[0, :, :, 128:256]
        for j in range(n_slots - 1, 0, -1):
            sp_ref[j, :, :, 128:256] = bwd
            bwd = bwd * gdec_ref[:, :, 128:256] + ds_ref[j, :, :, 128:256]

    @pl.when(sweep == 1)
    def _outputs():
        for grp in range(2):
            q2 = qk_ref[:, 128 * grp:128 * (grp + 1)]
            k2 = qk_ref[:, 256 + 128 * grp:256 + 128 * (grp + 1)]
            state = sp_ref[n, grp].astype(BF16)
            for hh in range(2):
                hd = 2 * grp + hh
                cols = slice(128 * hd, 128 * (hd + 1))
                qm = q2 * head_mask[hh]
                sc = lax.dot_general(qm, k2, _NT, preferred_element_type=F32) * dec_ref[hd]
                inner = jnp.dot(sc.astype(BF16), v_ref[:, cols], preferred_element_type=F32)
                cross = jnp.dot(qm, state, preferred_element_type=F32) * xi_ref[hd]
                o = inner + cross[:, 0:128] + cross[:, 128:256]
                mu = jnp.mean(o, axis=-1, keepdims=True)
                oc = o - mu
                var = jnp.mean(oc * oc, axis=-1, keepdims=True)
                y = oc * lax.rsqrt(var + EPS) * gn_ref[:, cols]
                y = y * _silu(gate_ref[:, cols].astype(F32))
                o_ref[:, cols] = y.astype(BF16)
        for sub in range(cb // SG_CHUNK):
            rows = slice(SG_CHUNK * sub, SG_CHUNK * (sub + 1))
            s = s_ref[rows, :].astype(F32)
            mu = jnp.mean(s, axis=-1, keepdims=True)
            sc_ = s - mu
            var = jnp.mean(sc_ * sc_, axis=-1, keepdims=True)
            sn = (sc_ * lax.rsqrt(var + EPS) * lng_ref[...] + lnb_ref[...]).astype(BF16)
            for g in range(SG_GROUPS):
                cols = slice(128 * g, 128 * (g + 1))
                mixed = jnp.dot(ws_ref[g], sn[:, cols], preferred_element_type=F32) + sgb_ref[:, cols]
                o_ref[rows, 512 + 128 * g:512 + 128 * (g + 1)] = (
                    u_ref[rows, cols].astype(F32) * mixed).astype(BF16)


def _retention_tables(log_decay, cb):
    lg = -jnp.exp(log_decay.astype(F32))
    pos = jnp.arange(cb, dtype=F32)
    diff = pos[:, None] - pos[None, :]
    lower = jnp.where(diff >= 0, jnp.exp(lg[0][:, None, None] * jnp.maximum(diff, 0.0)[None]), 0.0)
    upper = jnp.where(diff <= 0, jnp.exp(lg[1][:, None, None] * jnp.maximum(-diff, 0.0)[None]), 0.0)
    dec = lower + upper
    xi_f = jnp.exp(lg[0][:, None] * (pos[None, :] + 1.0))
    xi_b = jnp.exp(lg[1][:, None] * (cb - pos[None, :]))
    xi = jnp.concatenate([jnp.broadcast_to(xi_f[:, :, None], (RET_HEADS, cb, 128)),
                          jnp.broadcast_to(xi_b[:, :, None], (RET_HEADS, cb, 128))], axis=2)
    zeta_f = jnp.exp(lg[0][:, None] * (cb - 1.0 - pos[None, :]))
    zeta_b = jnp.exp(lg[1][:, None] * pos[None, :])

    def lanes(z):
        return jnp.repeat(z.reshape(2, 2, cb), RET_QK_DIM, axis=1).transpose(0, 2, 1)

    gstep = jnp.exp(lg * cb)

    def rows(gv):
        return jnp.broadcast_to(jnp.repeat(gv.reshape(2, 2), RET_QK_DIM, axis=1)[:, :, None], (2, 128, 128))

    gdec = jnp.concatenate([rows(gstep[0]), rows(gstep[1])], axis=2)
    return dec, xi, lanes(zeta_f), lanes(zeta_b), gdec


def _even_mix(p, tabs, gn_g, ln_g, ln_b, w_s, sg_bias, n_batch, seq, ctx_len):
    n = p.shape[0]
    cb = RET_BLOCK
    assert ctx_len == cb and seq % cb == 0
    nl = seq // cb
    ctx0 = n_batch * seq // cb

    def rb(b, n_):
        return jnp.where(n_ == 0, ctx0 + b, b * nl + n_ - 1)

    def late(col):
        return lambda b, s, n_: (rb(b, n_ * s), col)

    dec, xi, zf, zb, gdec = tabs
    full = lambda a: pl.BlockSpec(a.shape, lambda b, s, n_, _nd=a.ndim: (0,) * _nd)
    return pl.pallas_call(
        _even_mix_kernel,
        grid=(n_batch, 2, nl + 1),
        in_specs=[
            pl.BlockSpec((cb, 512), lambda b, s, n_: (rb(b, n_), 0)),
            pl.BlockSpec((cb, 512), lambda b, s, n_: (rb(b, n_), 1)),
            pl.BlockSpec((cb, 512), late(2)),
            pl.BlockSpec((cb, 512), late(3)),
            pl.BlockSpec((cb, 512), late(4)),
            full(dec), full(xi), full(zf), full(zb), full(gdec),
            full(gn_g), full(ln_g), full(ln_b), full(w_s), full(sg_bias),
        ],
        out_specs=pl.BlockSpec((cb, 1024), late(0)),
        out_shape=jax.ShapeDtypeStruct((n, 1024), BF16),
        scratch_shapes=[pltpu.VMEM((nl + 1, 2, 128, 256), F32), pltpu.VMEM((nl + 1, 2, 128, 256), F32)],
        compiler_params=_params(3, VMEM_LIMIT),
        name="even_mix",
    )(p, p, p, p, p, dec, xi, zf, zb, gdec, gn_g, ln_g, ln_b, w_s, sg_bias)


def _attn_kernel(q_ref, kvp_ref, kvc_ref, kvn_ref, kvx_ref, bias_ref, sink_ref, o_ref):
    nloc = 3 * ATT_BLOCK
    for kvh in range(ATT_KV_HEADS):
        kcols = slice(HEAD_DIM * kvh, HEAD_DIM * (kvh + 1))
        vcols = slice(HEAD_DIM * (ATT_KV_HEADS + kvh), HEAD_DIM * (ATT_KV_HEADS + kvh + 1))
        k_all = jnp.concatenate([kvp_ref[:, kcols], kvc_ref[:, kcols], kvn_ref[:, kcols], kvx_ref[:, kcols]], axis=0)
        v_all = jnp.concatenate([kvp_ref[:, vcols], kvc_ref[:, vcols], kvn_ref[:, vcols], kvx_ref[:, vcols]], axis=0)
        heads = [ATT_GROUP * kvh + j for j in range(ATT_GROUP)]
        q4 = jnp.concatenate([q_ref[:, HEAD_DIM * hd:HEAD_DIM * (hd + 1)] for hd in heads], axis=0)
        s = lax.dot_general(q4, k_all, _NT, preferred_element_type=F32)
        s_loc = s[:, 0:nloc] + bias_ref[0]
        s_ctx = s[:, nloc:]
        sink = jnp.concatenate(
            [jnp.broadcast_to(sink_ref[hd:hd + 1, 0:1], (ATT_BLOCK, 1)) for hd in heads], axis=0)
        m = jnp.maximum(jnp.maximum(jnp.max(s_loc, axis=-1, keepdims=True),
                                    jnp.max(s_ctx, axis=-1, keepdims=True)), sink)
        p_loc = jnp.exp(s_loc - m)
        p_ctx = jnp.exp(s_ctx - m)
        denom = (jnp.sum(p_loc, axis=-1, keepdims=True) + jnp.sum(p_ctx, axis=-1, keepdims=True)
                 + jnp.exp(sink - m))
        o = (jnp.dot(p_loc.astype(BF16), v_all[0:nloc], preferred_element_type=F32)
             + jnp.dot(p_ctx.astype(BF16), v_all[nloc:], preferred_element_type=F32)) / denom
        for j, hd in enumerate(heads):
            o_ref[:, HEAD_DIM * hd:HEAD_DIM * (hd + 1)] = o[ATT_BLOCK * j:ATT_BLOCK * (j + 1)].astype(BF16)


def _attn_bias():
    i = np.arange(ATT_BLOCK)[:, None]
    j = np.arange(3 * ATT_BLOCK)[None, :] - ATT_BLOCK
    band = np.abs(i - j) <= WINDOW
    cases = [band & (j >= 0), band, band & (j < ATT_BLOCK), np.zeros_like(band)]
    bias = np.stack([np.where(c, 0.0, NEG) for c in cases]).astype(np.float32)
    return jnp.asarray(np.tile(bias, (1, ATT_GROUP, 1)))


def _attention(q, kv, bias, sink_t, n_batch, seq, ctx_len, need_ctx):
    blk = ATT_BLOCK
    nb = seq // blk
    ncb = ctx_len // blk
    n_lat = n_batch * seq
    steps = nb + (ncb if need_ctx else 0)
    n_out = n_lat + (n_batch * ctx_len if need_ctx else 0)

    def qrow(b, n_):
        return jnp.where(n_ < nb, b * nb + n_, n_lat // blk + b * ncb + (n_ - nb))

    def krow(off):
        def index(b, n_):
            loc = jnp.clip(jnp.minimum(n_, nb - 1) + off, 0, nb - 1)
            return (b * nb + loc, 0)
        return index

    def case(b, n_):
        c = jnp.where(n_ == 0, 0, jnp.where(n_ == nb - 1, 2, 1))
        return (jnp.where(n_ < nb, c, 3), 0, 0)

    return pl.pallas_call(
        _attn_kernel,
        grid=(n_batch, steps),
        in_specs=[
            pl.BlockSpec((blk, q.shape[1]), lambda b, n_: (qrow(b, n_), 0)),
            pl.BlockSpec((blk, kv.shape[1]), krow(-1)),
            pl.BlockSpec((blk, kv.shape[1]), krow(0)),
            pl.BlockSpec((blk, kv.shape[1]), krow(1)),
            pl.BlockSpec((ctx_len, kv.shape[1]), lambda b, n_: (n_lat // ctx_len + b, 0)),
            pl.BlockSpec((1,) + bias.shape[1:], case),
            pl.BlockSpec(sink_t.shape, lambda b, n_: (0, 0)),
        ],
        out_specs=pl.BlockSpec((blk, q.shape[1]), lambda b, n_: (qrow(b, n_), 0)),
        out_shape=jax.ShapeDtypeStruct((n_out, q.shape[1]), BF16),
        compiler_params=_params(2, VMEM_LIMIT),
        name="window_attention",
    )(q, kv, kv, kv, kv, bias, sink_t)


def _route(logits):
    m = jnp.max(logits, axis=0, keepdims=True)
    e = jnp.exp(logits - m)
    p = e / jnp.sum(e, axis=0, keepdims=True)
    rows = [p[i:i + 1, :] for i in range(N_EXPERTS)]
    best = None
    gsel = None
    for g in range(N_GROUPS):
        a, b, c, d = rows[4 * g:4 * g + 4]
        m1, n1 = jnp.maximum(a, b), jnp.minimum(a, b)
        m2, n2 = jnp.maximum(c, d), jnp.minimum(c, d)
        score = jnp.maximum(m1, m2) + jnp.maximum(jnp.minimum(m1, m2), jnp.maximum(n1, n2))
        if g == 0:
            best, gsel = score, jnp.zeros(score.shape, I32)
        else:
            upd = score > best
            gsel = jnp.where(upd, g, gsel)
            best = jnp.where(upd, score, best)
    v = [jnp.where(gsel == 0, rows[j], jnp.where(gsel == 1, rows[4 + j], jnp.where(gsel == 2, rows[8 + j], rows[12 + j])))
         for j in range(EXPERTS_PER_GROUP)]
    b1, i1 = v[0], jnp.zeros(gsel.shape, I32)
    for j in range(1, 4):
        upd = v[j] > b1
        i1 = jnp.where(upd, j, i1)
        b1 = jnp.where(upd, v[j], b1)
    b2, i2 = jnp.full(b1.shape, -1.0, F32), jnp.zeros(gsel.shape, I32)
    for j in range(4):
        upd = jnp.logical_and(i1 != j, v[j] > b2)
        i2 = jnp.where(upd, j, i2)
        b2 = jnp.where(upd, v[j], b2)
    lo = jnp.minimum(i1, i2)
    hi = jnp.maximum(i1, i2)
    pair = jnp.where(lo == 0, hi - 1, jnp.where(lo == 1, 6 - hi, 5))
    return gsel * N_PAIRS + pair


def _post_kernel(mix_ref, x_ref, mod_ref, g2_ref, wo_ref, wrh_ref, wrl_ref, br_ref, tri_ref,
                 xo_ref, bk_ref, rank_ref, cnt_ref, carry_ref):
    @pl.when(pl.program_id(0) == 0)
    def _():
        carry_ref[...] = jnp.zeros(carry_ref.shape, F32)

    d = x_ref.shape[1]
    tm = x_ref.shape[0]
    y = jnp.dot(mix_ref[...], wo_ref[...], preferred_element_type=F32)
    x = x_ref[...] + mod_ref[0, :, 2 * d:3 * d] * y
    xo_ref[...] = x
    h2 = _rms_mod(x, g2_ref[...], mod_ref[0, :, 4 * d:5 * d], mod_ref[0, :, 3 * d:4 * d])
    hi = h2.astype(BF16)
    lo = (h2 - hi.astype(F32)).astype(BF16)
    logits = (lax.dot_general(wrh_ref[...], hi, _NT, preferred_element_type=F32)
              + lax.dot_general(wrh_ref[...], lo, _NT, preferred_element_type=F32)
              + lax.dot_general(wrl_ref[...], hi, _NT, preferred_element_type=F32)
              + br_ref[:, 0:1])
    bucket = _route(logits)
    bk_ref[...] = bucket
    ids = lax.broadcasted_iota(I32, (BUCKET_ROWS, tm), 0)
    onehot = (ids == bucket).astype(F32)
    before = jnp.dot(onehot.astype(BF16), tri_ref[...], preferred_element_type=F32)
    carry = carry_ref[...]
    rank_ref[...] = jnp.sum(onehot * (before + carry[:, 0:1]), axis=0, keepdims=True).astype(I32)
    carry = carry + jnp.sum(onehot, axis=1, keepdims=True)
    carry_ref[...] = carry
    cnt_ref[...] = carry.astype(I32)


def _post(mix, xa, mod_l, g2, wo, wr_hi, wr_lo, br_t, tri, n_act, n_lat, seq, n_batch):
    d = xa.shape[1]
    tm = ROW_TILE
    mrow = _mod_row_map(tm, n_lat, seq, n_batch)
    full = lambda a: pl.BlockSpec(a.shape, lambda i, _nd=a.ndim: (0,) * _nd)
    return pl.pallas_call(
        _post_kernel,
        grid=(n_act // tm,),
        in_specs=[
            pl.BlockSpec((tm, d), lambda i: (i, 0)),
            pl.BlockSpec((tm, d), lambda i: (i, 0)),
            pl.BlockSpec((1, 1, 6 * d), lambda i: (mrow(i), 0, 0)),
            full(g2), full(wo), full(wr_hi), full(wr_lo), full(br_t), full(tri),
        ],
        out_specs=[
            pl.BlockSpec((tm, d), lambda i: (i, 0)),
            pl.BlockSpec((1, tm), lambda i: (0, i)),
            pl.BlockSpec((1, tm), lambda i: (0, i)),
            pl.BlockSpec((BUCKET_ROWS, 128), lambda i: (0, 0)),
        ],
        out_shape=[
            jax.ShapeDtypeStruct((n_act, d), F32),
            jax.ShapeDtypeStruct((1, n_act), I32),
            jax.ShapeDtypeStruct((1, n_act), I32),
            jax.ShapeDtypeStruct((BUCKET_ROWS, 128), I32),
        ],
        scratch_shapes=[pltpu.VMEM((BUCKET_ROWS, 128), F32)],
        compiler_params=_params(1, VMEM_LIMIT),
        name="post_mixer",
    )(mix, xa, mod_l, g2, wo, wr_hi, wr_lo, br_t, tri)


def _dispatch_kernel(pos_ref, x_ref, mod_ref, g2_ref, hs_in_ref, hs_ref, hbuf_ref, sem):
    del hs_in_ref
    d = x_ref.shape[1]
    tm = x_ref.shape[0]
    h2 = _rms_mod(x_ref[...], g2_ref[...], mod_ref[0, :, 4 * d:5 * d], mod_ref[0, :, 3 * d:4 * d])
    hbuf_ref[...] = h2.reshape(hbuf_ref.shape)

    def body(j, carry):
        for u in range(DMA_UNROLL):
            r = j * DMA_UNROLL + u
            pltpu.make_async_copy(hbuf_ref.at[r], hs_ref.at[pos_ref[0, 0, r]], sem).start(priority=u % 2)
        return carry
    lax.fori_loop(0, tm // DMA_UNROLL, body, 0)
    pltpu.make_async_copy(hbuf_ref, hs_ref.at[pl.ds(0, tm)], sem).wait()


def _dispatch(pos3, xa, mod_l, g2, hs, n_lat, seq, n_batch):
    n, d = xa.shape
    tm = ROW_TILE
    mrow = _mod_row_map(tm, n_lat, seq, n_batch)
    return pl.pallas_call(
        _dispatch_kernel,
        grid=(n // tm,),
        in_specs=[
            pl.BlockSpec((1, 1, tm), lambda i: (i, 0, 0), memory_space=pltpu.SMEM),
            pl.BlockSpec((tm, d), lambda i: (i, 0)),
            pl.BlockSpec((1, 1, 6 * d), lambda i: (mrow(i), 0, 0)),
            pl.BlockSpec((1, d), lambda i: (0, 0)),
            pl.BlockSpec(memory_space=pl.ANY),
        ],
        out_specs=pl.BlockSpec(memory_space=pl.ANY),
        out_shape=jax.ShapeDtypeStruct(hs.shape, F32),
        scratch_shapes=[pltpu.VMEM((tm,) + hs.shape[1:], F32), pltpu.SemaphoreType.DMA(())],
        input_output_aliases={4: 0},
        compiler_params=_params(1, VMEM_LIMIT),
        name="dispatch",
    )(pos3, xa, mod_l, g2, hs)


def _moe_kernel(ea_ref, eb_ref, nu_ref, h_ref, wga_ref, wua_ref, wda_ref, wgb_ref, wub_ref, wdb_ref,
                wrt_ref, brt_ref, o_ref):
    t = pl.program_id(0)

    @pl.when(t >= nu_ref[0])
    def _():
        o_ref[...] = jnp.zeros(o_ref.shape, F32)

    @pl.when(t < nu_ref[0])
    def _():
        ea = ea_ref[t]
        eb = eb_ref[t]
        tm = h_ref.shape[0]
        h = h_ref[...].reshape(tm, h_ref.shape[1] * h_ref.shape[2])
        hb = h.astype(BF16)
        wdiff = wrt_ref[pl.ds(ea, 1), :] - wrt_ref[pl.ds(eb, 1), :]
        bdiff = brt_ref[pl.ds(ea, 1), 0:1] - brt_ref[pl.ds(eb, 1), 0:1]
        wa = jax.nn.sigmoid(jnp.sum(h * wdiff, axis=-1, keepdims=True) + bdiff)

        def expert(wg_ref, wu_ref, wd_ref):
            f = wg_ref.shape[3]
            acc = None
            for c in range(f // 512):
                cols = slice(512 * c, 512 * (c + 1))
                g = jnp.dot(hb, wg_ref[0, 0, :, cols], preferred_element_type=F32)
                u = jnp.dot(hb, wu_ref[0, 0, :, cols], preferred_element_type=F32)
                part = jnp.dot((_silu(g) * u).astype(BF16), wd_ref[0, 0, cols, :], preferred_element_type=F32)
                acc = part if acc is None else acc + part
            return acc

        y = wa * expert(wga_ref, wua_ref, wda_ref) + (1.0 - wa) * expert(wgb_ref, wub_ref, wdb_ref)
        o_ref[...] = y.reshape(o_ref.shape)


def _moe(layer, ea, eb, nu, hs, wg, wu, wd, wr_t, br_t):
    p_rows = hs.shape[0]
    row = hs.shape[1:]
    tm = MOE_TILE
    d, f = wg.shape[2:]
    wa_spec = lambda shape: pl.BlockSpec(shape, lambda t, ea_, eb_, nu_: (layer, ea_[t], 0, 0))
    wb_spec = lambda shape: pl.BlockSpec(shape, lambda t, ea_, eb_, nu_: (layer, eb_[t], 0, 0))
    grid_spec = pltpu.PrefetchScalarGridSpec(
        num_scalar_prefetch=3,
        grid=(p_rows // tm,),
        in_specs=[
            pl.BlockSpec((tm,) + row, lambda t, ea_, eb_, nu_: (t, 0, 0)),
            wa_spec((1, 1, d, f)), wa_spec((1, 1, d, f)), wa_spec((1, 1, f, d)),
            wb_spec((1, 1, d, f)), wb_spec((1, 1, d, f)), wb_spec((1, 1, f, d)),
            pl.BlockSpec(wr_t.shape, lambda t, ea_, eb_, nu_: (0, 0)),
            pl.BlockSpec(br_t.shape, lambda t, ea_, eb_, nu_: (0, 0)),
        ],
        out_specs=pl.BlockSpec((tm,) + row, lambda t, ea_, eb_, nu_: (t, 0, 0)),
    )
    return pl.pallas_call(
        _moe_kernel,
        grid_spec=grid_spec,
        out_shape=jax.ShapeDtypeStruct(hs.shape, F32),
        compiler_params=_params(1, VMEM_LIMIT),
        name="experts",
    )(ea, eb, nu, hs, wg, wu, wd, wg, wu, wd, wr_t, br_t)


def _combine_kernel(pos_ref, posn_ref, x_ref, mod_ref, fg_ref, ys_ref, o_ref, gbuf_ref, sem, *, final):
    i = pl.program_id(0)
    slot = i % 2
    d = x_ref.shape[1]
    tm = x_ref.shape[0]

    def issue(p_ref, s):
        def body(j, carry):
            for u in range(DMA_UNROLL):
                r = j * DMA_UNROLL + u
                pltpu.make_async_copy(ys_ref.at[p_ref[0, 0, r]], gbuf_ref.at[s, r], sem.at[s]).start(priority=u % 2)
            return carry
        lax.fori_loop(0, tm // DMA_UNROLL, body, 0)

    @pl.when(i == 0)
    def _():
        issue(pos_ref, 0)

    @pl.when(i + 1 < pl.num_programs(0))
    def _():
        issue(posn_ref, 1 - slot)

    pltpu.make_async_copy(ys_ref.at[pl.ds(0, tm)], gbuf_ref.at[slot], sem.at[slot]).wait()
    x = x_ref[...] + mod_ref[0, :, 5 * d:6 * d] * gbuf_ref[slot].reshape(tm, d)
    if final:
        x = x * lax.rsqrt(jnp.mean(x * x, axis=-1, keepdims=True) + EPS) * fg_ref[...]
    o_ref[...] = x


def _combine(pos3, xa, mod_l, fg, ys, n_lat, seq, n_batch, final):
    n, d = xa.shape
    tm = ROW_TILE
    steps = n // tm
    mrow = _mod_row_map(tm, n_lat, seq, n_batch)
    return pl.pallas_call(
        functools.partial(_combine_kernel, final=final),
        grid=(steps,),
        in_specs=[
            pl.BlockSpec((1, 1, tm), lambda i: (i, 0, 0), memory_space=pltpu.SMEM),
            pl.BlockSpec((1, 1, tm), lambda i: (jnp.minimum(i + 1, steps - 1), 0, 0), memory_space=pltpu.SMEM),
            pl.BlockSpec((tm, d), lambda i: (i, 0)),
            pl.BlockSpec((1, 1, 6 * d), lambda i: (mrow(i), 0, 0)),
            pl.BlockSpec((1, d), lambda i: (0, 0)),
            pl.BlockSpec(memory_space=pl.ANY),
        ],
        out_specs=pl.BlockSpec((tm, d), lambda i: (i, 0)),
        out_shape=jax.ShapeDtypeStruct((n, d), F32),
        scratch_shapes=[pltpu.VMEM((2, tm) + ys.shape[1:], F32), pltpu.SemaphoreType.DMA((2,))],
        compiler_params=_params(1, VMEM_LIMIT),
        name="combine",
    )(pos3, pos3, xa, mod_l, fg, ys)


def _bucket_plan(bucket, rank, counts, n_tiles, tile):
    cnt = counts[:N_BUCKETS, 0]
    padded = ((cnt + tile - 1) // tile) * tile
    ends = jnp.cumsum(padded)
    offs = ends - padded
    pos = offs[bucket[0]] + rank[0]
    n_used = ends[-1] // tile
    t_eff = jnp.minimum(jnp.arange(n_tiles, dtype=I32), jnp.maximum(n_used - 1, 0))
    tb = jnp.minimum(jnp.sum((ends[None, :] <= (t_eff * tile)[:, None]).astype(I32), axis=1), N_BUCKETS - 1)
    pairs = np.array(PAIR_SLOTS, dtype=np.int32)
    ea = (tb // N_PAIRS) * EXPERTS_PER_GROUP + jnp.asarray(pairs[:, 0])[tb % N_PAIRS]
    eb = (tb // N_PAIRS) * EXPERTS_PER_GROUP + jnp.asarray(pairs[:, 1])[tb % N_PAIRS]
    return pos.astype(I32), ea.astype(I32), eb.astype(I32), n_used.astype(I32).reshape(1)


def kernel(x, c, ctx, c_ctx, w_mod, b_mod, norm_g, w_in_even, w_out_even, ret_log_decay, ret_gn_g, sg_ln_g, sg_ln_b, sg_w, sg_b, w_qkv_odd, w_o_odd, attn_sink, w_router, b_router, w_e_gate, w_e_up, w_e_down, final_g):
    n_batch, seq, d = x.shape
    ctx_len = ctx.shape[1]
    depth = w_mod.shape[0]
    n_lat = n_batch * seq
    n_all = n_lat + n_batch * ctx_len
    tm = ROW_TILE
    assert seq % tm == 0 and (n_batch * ctx_len) % tm == 0

    xa = jnp.concatenate([x.reshape(n_lat, d), ctx.reshape(n_batch * ctx_len, d)], axis=0)

    mod_rows = -(-(n_batch + 1) // 16) * 16
    c_rows = jnp.zeros((mod_rows, d), F32).at[:n_batch].set(c).at[n_batch].set(c_ctx)
    mod = _modulation(c_rows, w_mod, b_mod).reshape(depth, mod_rows, 1, 6 * d)

    wr_t = w_router.T.astype(F32)
    wr_hi = wr_t.astype(BF16)
    wr_lo = (wr_t - wr_hi.astype(F32)).astype(BF16)
    br_t = jnp.broadcast_to(b_router.astype(F32)[:, None], (N_EXPERTS, 128))
    tri = jnp.triu(jnp.ones((tm, tm), BF16), k=1)
    cos_t, sin_t = _rope_tables(seq, tm)
    bias = _attn_bias()

    wg_all, wu_all, wd_all = w_e_gate.astype(BF16), w_e_up.astype(BF16), w_e_down.astype(BF16)

    n_tiles = n_all // MOE_TILE + N_BUCKETS
    p_rows = n_tiles * MOE_TILE
    hs = jnp.zeros((p_rows, ROW_SUB, d // ROW_SUB), F32)

    for l in range(depth):
        i = l // 2
        last = l == depth - 1
        mod_l = mod[l]
        g1 = norm_g[l, 0].reshape(1, d)
        g2 = norm_g[l, 1].reshape(1, d)
        if l % 2 == 0:
            p = _in_even(xa, mod_l, g1, w_in_even[i].astype(BF16), n_lat, seq, n_batch)
            tabs = _retention_tables(ret_log_decay[i], RET_BLOCK)
            sg_bias = jnp.repeat(sg_b[i].astype(F32).T, SG_CHUNK, axis=1)
            mix = _even_mix(p, tabs, ret_gn_g[i].reshape(1, -1), sg_ln_g[i].reshape(1, -1), sg_ln_b[i].reshape(1, -1),
                            sg_w[i].astype(BF16), sg_bias, n_batch, seq, ctx_len)
            wo = w_out_even[i].astype(BF16)
        else:
            q, kv = _in_odd(xa, mod_l, g1, w_qkv_odd[i].astype(BF16), cos_t, sin_t, n_lat, seq, n_batch)
            sink_t = jnp.broadcast_to(attn_sink[i].astype(F32)[:, None], (ATT_HEADS, 128))
            mix = _attention(q, kv, bias, sink_t, n_batch, seq, ctx_len, need_ctx=not last)
            wo = w_o_odd[i].astype(BF16)
        n_act = n_lat if last else n_all
        xa, bucket, rank, counts = _post(mix, xa, mod_l, g2, wo, wr_hi, wr_lo, br_t, tri, n_act, n_lat, seq, n_batch)
        pos, ea, eb, n_used = _bucket_plan(bucket, rank, counts, n_tiles, MOE_TILE)
        pos3 = pos.reshape(n_act // tm, 1, tm)
        hs = _dispatch(pos3, xa, mod_l, g2, hs, n_lat, seq, n_batch)
        ys = _moe(l, ea, eb, n_used, hs, wg_all, wu_all, wd_all, wr_t, br_t)
        xa = _combine(pos3, xa, mod_l, final_g.reshape(1, d), ys, n_lat, seq, n_batch, final=last)
    return xa.reshape(n_batch, seq, d)
```

```python
import functools

import jax
import jax.numpy as jnp
import numpy as np
from jax import lax
from jax.experimental import pallas as pl
from jax.experimental.pallas import tpu as pltpu

F32 = jnp.float32
BF16 = jnp.bfloat16
I32 = jnp.int32

EPS = 1e-6
NEG = -1e30
GRID_W = 64
ROPE_BASE = 10000.0
RET_HEADS = 4
RET_QK_DIM = 64
RET_V_DIM = 128
SG_GROUPS = 4
SG_CHUNK = 128
RET_BLOCK = 256
ATT_HEADS = 8
ATT_KV_HEADS = 2
ATT_GROUP = ATT_HEADS // ATT_KV_HEADS
HEAD_DIM = 128
ATT_BLOCK = 128
WINDOW = 128
ATT_SUB = 2
N_EXPERTS = 16
N_GROUPS = 4
EXPERTS_PER_GROUP = 4
N_PAIRS = 6
N_BUCKETS = N_GROUPS * N_PAIRS
PAIR_SLOTS = ((0, 1), (0, 2), (0, 3), (1, 3), (1, 2), (3, 2))
BUCKET_ROWS = 32

ROW_TILE = 512
MOE_TILE = 256
DMA_UNROLL = 8
ROW_SUB = 8
VMEM_LIMIT = 56 * 1024 * 1024

_NT = (((1,), (1,)), ((), ()))
_TN = (((0,), (0,)), ((), ()))


def _params(n_axes, vmem=None):
    return pltpu.CompilerParams(dimension_semantics=("arbitrary",) * n_axes, vmem_limit_bytes=vmem)


def _silu(v):
    return v * jax.nn.sigmoid(v)


def _rms_mod(x, g, scale, shift):
    y = x * lax.rsqrt(jnp.mean(x * x, axis=-1, keepdims=True) + EPS) * g
    return y * (1.0 + scale) + shift


def _mod_kernel(c_ref, w_ref, b_ref, o_ref):
    a = _silu(c_ref[...]).astype(BF16)
    o_ref[0] = jnp.dot(a, w_ref[0].astype(BF16), preferred_element_type=F32) + b_ref[0]


def _modulation(c_rows, w_mod, b_mod):
    depth, d, six_d = w_mod.shape
    mr = c_rows.shape[0]
    tn = 1536
    return pl.pallas_call(
        _mod_kernel,
        grid=(depth, six_d // tn),
        in_specs=[
            pl.BlockSpec((mr, d), lambda l, j: (0, 0)),
            pl.BlockSpec((1, d, tn), lambda l, j: (l, 0, j)),
            pl.BlockSpec((1, 1, tn), lambda l, j: (l, 0, j)),
        ],
        out_specs=pl.BlockSpec((1, mr, tn), lambda l, j: (l, 0, j)),
        out_shape=jax.ShapeDtypeStruct((depth, mr, six_d), F32),
        compiler_params=_params(2, VMEM_LIMIT),
        name="modulation",
    )(c_rows, w_mod, b_mod.reshape(depth, 1, six_d))


def _mod_row_map(tm, n_lat, seq, n_batch):
    def index(i):
        row0 = i * tm
        return jnp.where(row0 < n_lat, row0 // seq, n_batch)
    return index


def _gather_rows(pos_ref, posn_ref, ys_ref, gbuf_ref, sem):
    i = pl.program_id(0)
    slot = i % 2
    tm = gbuf_ref.shape[1]

    def issue(p_ref, s):
        def body(j, carry):
            for u in range(DMA_UNROLL):
                r = j * DMA_UNROLL + u
                pltpu.make_async_copy(ys_ref.at[p_ref[0, 0, r]], gbuf_ref.at[s, r], sem.at[s]).start(priority=u % 2)
            return carry
        lax.fori_loop(0, tm // DMA_UNROLL, body, 0)

    @pl.when(i == 0)
    def _():
        issue(pos_ref, 0)

    @pl.when(i + 1 < pl.num_programs(0))
    def _():
        issue(posn_ref, 1 - slot)

    pltpu.make_async_copy(ys_ref.at[pl.ds(0, tm)], gbuf_ref.at[slot], sem.at[slot]).wait()
    return gbuf_ref[slot].reshape(tm, gbuf_ref.shape[2] * gbuf_ref.shape[3])


def _project_even(h, w_ref, o_ref):
    for c in range(5):
        p = jnp.dot(h, w_ref[:, 512 * c:512 * (c + 1)], preferred_element_type=F32)
        if c == 0:
            col = lax.broadcasted_iota(I32, p.shape, 1)
            p = jnp.where(col >= 256, p * (RET_QK_DIM ** -0.5), p)
        elif c >= 3:
            p = jax.nn.gelu(p)
        o_ref[:, 512 * c:512 * (c + 1)] = p.astype(BF16)


def _project_odd(h, w_ref, cos_ref, sin_ref, q_ref, kv_ref):
    tm = h.shape[0]
    cos = cos_ref[...]
    sin = sin_ref[...]
    lane = lax.broadcasted_iota(I32, (tm, HEAD_DIM), 1)
    first = (lane % 64) < 32

    def rope(p):
        partner = jnp.where(first, pltpu.roll(p, 96, 1), pltpu.roll(p, 32, 1))
        return p * cos + partner * sin

    scale = HEAD_DIM ** -0.5
    for c in range(3):
        p = jnp.dot(h, w_ref[:, 512 * c:512 * (c + 1)], preferred_element_type=F32)
        for j in range(4):
            pj = p[:, 128 * j:128 * (j + 1)]
            if c < 2:
                q_ref[:, 512 * c + 128 * j:512 * c + 128 * (j + 1)] = (rope(pj) * scale).astype(BF16)
            elif j < 2:
                kv_ref[:, 128 * j:128 * (j + 1)] = rope(pj).astype(BF16)
            else:
                kv_ref[:, 128 * j:128 * (j + 1)] = pj.astype(BF16)


def _in_kernel(*refs, first, odd, lat_tiles):
    if first:
        (xl_ref, xc_ref), refs = refs[:2], refs[2:]
        x = jnp.where(pl.program_id(0) < lat_tiles, xl_ref[...], xc_ref[...])
    else:
        (pos_ref, posn_ref, x_ref, modp_ref, ys_ref), refs = refs[:5], refs[5:]
        gbuf_ref, sem = refs[-2:]
        refs = refs[:-2]
        d = x_ref.shape[1]
        x = x_ref[...] + modp_ref[0, :, 5 * d:6 * d] * _gather_rows(pos_ref, posn_ref, ys_ref, gbuf_ref, sem)
    mod_ref, g_ref, w_ref = refs[:3]
    d = x.shape[1]
    h = _rms_mod(x, g_ref[...], mod_ref[0, :, d:2 * d], mod_ref[0, :, 0:d]).astype(BF16)
    if odd:
        cos_ref, sin_ref, xo_ref, q_ref, kv_ref = refs[3:]
        xo_ref[...] = x
        _project_odd(h, w_ref, cos_ref, sin_ref, q_ref, kv_ref)
    else:
        xo_ref, o_ref = refs[3:]
        xo_ref[...] = x
        _project_even(h, w_ref, o_ref)


def _in_stage(source, mod_l, g, w, rope, n_lat, seq, n_batch):
    first = len(source) == 2
    odd = rope is not None
    tm = ROW_TILE
    if first:
        xl, xc = source
        d = xl.shape[1]
        n = xl.shape[0] + xc.shape[0]
        lat_tiles = xl.shape[0] // tm
        src_specs = [pl.BlockSpec((tm, d), lambda i: (jnp.minimum(i, lat_tiles - 1), 0)),
                     pl.BlockSpec((tm, d), lambda i: (jnp.maximum(i - lat_tiles, 0), 0))]
        scratch = []
    else:
        pos3, xm, mod_prev, ys = source
        n, d = xm.shape
        lat_tiles = n_lat // tm
        steps = n // tm
        mrow_p = _mod_row_map(tm, n_lat, seq, n_batch)
        src_specs = [pl.BlockSpec((1, 1, tm), lambda i: (i, 0, 0), memory_space=pltpu.SMEM),
                     pl.BlockSpec((1, 1, tm), lambda i: (jnp.minimum(i + 1, steps - 1), 0, 0), memory_space=pltpu.SMEM),
                     pl.BlockSpec((tm, d), lambda i: (i, 0)),
                     pl.BlockSpec((1, 1, 6 * d), lambda i: (mrow_p(i), 0, 0)),
                     pl.BlockSpec(memory_space=pl.ANY)]
        source = (pos3, pos3, xm, mod_prev, ys)
        scratch = [pltpu.VMEM((2, tm) + ys.shape[1:], F32), pltpu.SemaphoreType.DMA((2,))]
    mrow = _mod_row_map(tm, n_lat, seq, n_batch)
    specs = src_specs + [pl.BlockSpec((1, 1, 6 * d), lambda i: (mrow(i), 0, 0)),
                         pl.BlockSpec((1, d), lambda i: (0, 0)),
                         pl.BlockSpec(w.shape, lambda i: (0, 0))]
    args = tuple(source) + (mod_l, g, w)
    out_specs = [pl.BlockSpec((tm, d), lambda i: (i, 0))]
    out_shape = [jax.ShapeDtypeStruct((n, d), F32)]
    if odd:
        tiles_per_seq = seq // tm

        def rope_row(i):
            return jnp.where(i < lat_tiles, i % tiles_per_seq, tiles_per_seq)

        specs += [pl.BlockSpec((tm, HEAD_DIM), lambda i: (rope_row(i), 0))] * 2
        args += tuple(rope)
        widths = (ATT_HEADS * HEAD_DIM, 2 * ATT_KV_HEADS * HEAD_DIM)
    else:
        widths = (w.shape[1],)
    out_specs += [pl.BlockSpec((tm, wd), lambda i: (i, 0)) for wd in widths]
    out_shape += [jax.ShapeDtypeStruct((n, wd), BF16) for wd in widths]
    return pl.pallas_call(
        functools.partial(_in_kernel, first=first, odd=odd, lat_tiles=lat_tiles),
        grid=(n // tm,),
        in_specs=specs,
        out_specs=out_specs,
        out_shape=out_shape,
        scratch_shapes=scratch,
        compiler_params=_params(1, VMEM_LIMIT),
        name="in_odd" if odd else "in_even",
    )(*args)


def _rope_tables(seq, tm):
    t = jnp.arange(seq)
    row = (t // GRID_W).astype(F32)
    col = (t % GRID_W).astype(F32)
    quarter = HEAD_DIM // 4
    inv_freq = ROPE_BASE ** (-jnp.arange(quarter, dtype=F32) / quarter)
    ar = row[:, None] * inv_freq[None, :]
    ac = col[:, None] * inv_freq[None, :]
    cos = jnp.concatenate([jnp.cos(ar), jnp.cos(ar), jnp.cos(ac), jnp.cos(ac)], axis=1)
    sin = jnp.concatenate([-jnp.sin(ar), jnp.sin(ar), -jnp.sin(ac), jnp.sin(ac)], axis=1)
    cos = jnp.concatenate([cos, jnp.ones((tm, HEAD_DIM), F32)], axis=0)
    sin = jnp.concatenate([sin, jnp.zeros((tm, HEAD_DIM), F32)], axis=0)
    return cos, sin


def _even_mix_kernel(qk_ref, v_ref, gate_ref, u_ref, s_ref, dec_ref, xi_ref, zf_ref, zb_ref, gdec_ref,
                     gn_ref, lng_ref, lnb_ref, ws_ref, sgb_ref, o_ref, ds_ref, sp_ref):
    sweep = pl.program_id(1)
    n = pl.program_id(2)
    n_slots = ds_ref.shape[0]
    cb = qk_ref.shape[0]
    lane = lax.broadcasted_iota(I32, (1, 128), 1)
    head_mask = [(lane // RET_QK_DIM == hh).astype(BF16) for hh in range(2)]
    upper_rows = lax.broadcasted_iota(I32, (128, 128), 0) < RET_QK_DIM

    @pl.when(sweep == 0)
    def _increments():
        for grp in range(2):
            k2 = qk_ref[:, 256 + 128 * grp:256 + 128 * (grp + 1)].astype(F32)
            v2 = v_ref[:, 256 * grp:256 * (grp + 1)]
            for half, z_ref in enumerate((zf_ref, zb_ref)):
                kz = (k2 * z_ref[grp]).astype(BF16)
                full = lax.dot_general(kz, v2, _TN, preferred_element_type=F32)
                ds_ref[n, grp, :, 128 * half:128 * (half + 1)] = jnp.where(
                    upper_rows, full[:, 0:128], full[:, 128:256])

    @pl.when(jnp.logical_and(sweep == 1, n == 0))
    def _scan():
        sp_ref[0] = jnp.zeros(sp_ref.shape[1:], F32)
        fwd = ds_ref[0, :, :, 0:128]
        for j in range(1, n_slots):
            sp_ref[j, :, :, 0:128] = fwd
            fwd = fwd * gdec_ref[:, :, 0:128] + ds_ref[j, :, :, 0:128]
        bwd = ds_ref[0, :, :, 128:256]
        for j in range(n_slots - 1, 0, -1):
            sp_ref[j, :, :, 128:256] = bwd
            bwd = bwd * gdec_ref[:, :, 128:256] + ds_ref[j, :, :, 128:256]

    @pl.when(sweep == 1)
    def _outputs():
        for grp in range(2):
            q2 = qk_ref[:, 128 * grp:128 * (grp + 1)]
            k2 = qk_ref[:, 256 + 128 * grp:256 + 128 * (grp + 1)]
            state = sp_ref[n, grp].astype(BF16)
            for hh in range(2):
                hd = 2 * grp + hh
                cols = slice(128 * hd, 128 * (hd + 1))
                qm = q2 * head_mask[hh]
                sc = lax.dot_general(qm, k2, _NT, preferred_element_type=F32) * dec_ref[hd]
                inner = jnp.dot(sc.astype(BF16), v_ref[:, cols], preferred_element_type=F32)
                cross = jnp.dot(qm, state, preferred_element_type=F32) * xi_ref[hd]
                o = inner + cross[:, 0:128] + cross[:, 128:256]
                mu = jnp.mean(o, axis=-1, keepdims=True)
                oc = o - mu
                var = jnp.mean(oc * oc, axis=-1, keepdims=True)
                y = oc * lax.rsqrt(var + EPS) * gn_ref[:, cols]
                y = y * _silu(gate_ref[:, cols].astype(F32))
                o_ref[:, cols] = y.astype(BF16)
        for sub in range(cb // SG_CHUNK):
            rows = slice(SG_CHUNK * sub, SG_CHUNK * (sub + 1))
            s = s_ref[rows, :].astype(F32)
            mu = jnp.mean(s, axis=-1, keepdims=True)
            sc_ = s - mu
            var = jnp.mean(sc_ * sc_, axis=-1, keepdims=True)
            sn = (sc_ * lax.rsqrt(var + EPS) * lng_ref[...] + lnb_ref[...]).astype(BF16)
            for g in range(SG_GROUPS):
                cols = slice(128 * g, 128 * (g + 1))
                mixed = jnp.dot(ws_ref[g], sn[:, cols], preferred_element_type=F32) + sgb_ref[:, cols]
                o_ref[rows, 512 + 128 * g:512 + 128 * (g + 1)] = (
                    u_ref[rows, cols].astype(F32) * mixed).astype(BF16)


def _retention_tables(log_decay, cb):
    lg = -jnp.exp(log_decay.astype(F32))
    pos = jnp.arange(cb, dtype=F32)
    diff = pos[:, None] - pos[None, :]
    lower = jnp.where(diff >= 0, jnp.exp(lg[0][:, None, None] * jnp.maximum(diff, 0.0)[None]), 0.0)
    upper = jnp.where(diff <= 0, jnp.exp(lg[1][:, None, None] * jnp.maximum(-diff, 0.0)[None]), 0.0)
    dec = lower + upper
    xi_f = jnp.exp(lg[0][:, None] * (pos[None, :] + 1.0))
    xi_b = jnp.exp(lg[1][:, None] * (cb - pos[None, :]))
    xi = jnp.concatenate([jnp.broadcast_to(xi_f[:, :, None], (RET_HEADS, cb, 128)),
                          jnp.broadcast_to(xi_b[:, :, None], (RET_HEADS, cb, 128))], axis=2)
    zeta_f = jnp.exp(lg[0][:, None] * (cb - 1.0 - pos[None, :]))
    zeta_b = jnp.exp(lg[1][:, None] * pos[None, :])

    def lanes(z):
        return jnp.repeat(z.reshape(2, 2, cb), RET_QK_DIM, axis=1).transpose(0, 2, 1)

    gstep = jnp.exp(lg * cb)

    def rows(gv):
        return jnp.broadcast_to(jnp.repeat(gv.reshape(2, 2), RET_QK_DIM, axis=1)[:, :, None], (2, 128, 128))

    gdec = jnp.concatenate([rows(gstep[0]), rows(gstep[1])], axis=2)
    return dec, xi, lanes(zeta_f), lanes(zeta_b), gdec


def _even_mix(p, tabs, gn_g, ln_g, ln_b, w_s, sg_bias, n_batch, seq, ctx_len):
    n = p.shape[0]
    cb = RET_BLOCK
    assert ctx_len == cb and seq % cb == 0
    nl = seq // cb
    ctx0 = n_batch * seq // cb

    def rb(b, n_):
        return jnp.where(n_ == 0, ctx0 + b, b * nl + n_ - 1)

    def late(col):
        return lambda b, s, n_: (rb(b, n_ * s), col)

    dec, xi, zf, zb, gdec = tabs
    full = lambda a: pl.BlockSpec(a.shape, lambda b, s, n_, _nd=a.ndim: (0,) * _nd)
    return pl.pallas_call(
        _even_mix_kernel,
        grid=(n_batch, 2, nl + 1),
        in_specs=[
            pl.BlockSpec((cb, 512), lambda b, s, n_: (rb(b, n_), 0)),
            pl.BlockSpec((cb, 512), lambda b, s, n_: (rb(b, n_), 1)),
            pl.BlockSpec((cb, 512), late(2)),
            pl.BlockSpec((cb, 512), late(3)),
            pl.BlockSpec((cb, 512), late(4)),
            full(dec), full(xi), full(zf), full(zb), full(gdec),
            full(gn_g), full(ln_g), full(ln_b), full(w_s), full(sg_bias),
        ],
        out_specs=pl.BlockSpec((cb, 1024), late(0)),
        out_shape=jax.ShapeDtypeStruct((n, 1024), BF16),
        scratch_shapes=[pltpu.VMEM((nl + 1, 2, 128, 256), F32), pltpu.VMEM((nl + 1, 2, 128, 256), F32)],
        compiler_params=_params(3, VMEM_LIMIT),
        name="even_mix",
    )(p, p, p, p, p, dec, xi, zf, zb, gdec, gn_g, ln_g, ln_b, w_s, sg_bias)


def _attn_kernel(q_ref, kvp_ref, kvm_ref, kvn_ref, kvx_ref, bias0_ref, bias1_ref, sink_ref, o_ref):
    blk = ATT_BLOCK
    nloc = 3 * blk
    bias_refs = (bias0_ref, bias1_ref)
    chains = [(kvh, sub) for kvh in range(ATT_KV_HEADS) for sub in range(ATT_SUB)]
    v_all, s_loc, s_ctx, sinks = {}, {}, {}, {}
    for kvh, sub in chains:
        kcols = slice(HEAD_DIM * kvh, HEAD_DIM * (kvh + 1))
        vcols = slice(HEAD_DIM * (ATT_KV_HEADS + kvh), HEAD_DIM * (ATT_KV_HEADS + kvh + 1))
        k_parts = [kvp_ref[:, kcols], kvm_ref[0:blk, kcols], kvm_ref[blk:2 * blk, kcols], kvn_ref[:, kcols]]
        v_parts = [kvp_ref[:, vcols], kvm_ref[0:blk, vcols], kvm_ref[blk:2 * blk, vcols], kvn_ref[:, vcols]]
        heads = [ATT_GROUP * kvh + j for j in range(ATT_GROUP)]
        rows = slice(blk * sub, blk * (sub + 1))
        k_all = jnp.concatenate(k_parts[sub:sub + 3] + [kvx_ref[:, kcols]], axis=0)
        v_all[kvh, sub] = jnp.concatenate(v_parts[sub:sub + 3] + [kvx_ref[:, vcols]], axis=0)
        q4 = jnp.concatenate([q_ref[rows, HEAD_DIM * hd:HEAD_DIM * (hd + 1)] for hd in heads], axis=0)
        s = lax.dot_general(q4, k_all, _NT, preferred_element_type=F32)
        s_loc[kvh, sub] = s[:, 0:nloc] + bias_refs[sub][0]
        s_ctx[kvh, sub] = s[:, nloc:]
        sinks[kvh, sub] = jnp.concatenate(
            [jnp.broadcast_to(sink_ref[hd:hd + 1, 0:1], (blk, 1)) for hd in heads], axis=0)
    p_loc, p_ctx, denom = {}, {}, {}
    for c in chains:
        m = jnp.maximum(jnp.maximum(jnp.max(s_loc[c], axis=-1, keepdims=True),
                                    jnp.max(s_ctx[c], axis=-1, keepdims=True)), sinks[c])
        pl_ = jnp.exp(s_loc[c] - m)
        pc_ = jnp.exp(s_ctx[c] - m)
        denom[c] = (jnp.sum(pl_, axis=-1, keepdims=True) + jnp.sum(pc_, axis=-1, keepdims=True)
                    + jnp.exp(sinks[c] - m))
        p_loc[c] = pl_.astype(BF16)
        p_ctx[c] = pc_.astype(BF16)
    for kvh, sub in chains:
        c = (kvh, sub)
        rows = slice(blk * sub, blk * (sub + 1))
        o = (jnp.dot(p_loc[c], v_all[c][0:nloc], preferred_element_type=F32)
             + jnp.dot(p_ctx[c], v_all[c][nloc:], preferred_element_type=F32)) / denom[c]
        for j in range(ATT_GROUP):
            hd = ATT_GROUP * kvh + j
            o_ref[rows, HEAD_DIM * hd:HEAD_DIM * (hd + 1)] = o[blk * j:blk * (j + 1)].astype(BF16)


def _attn_bias():
    i = np.arange(ATT_BLOCK)[:, None]
    j = np.arange(3 * ATT_BLOCK)[None, :] - ATT_BLOCK
    band = np.abs(i - j) <= WINDOW
    cases = [band & (j >= 0), band, band & (j < ATT_BLOCK), np.zeros_like(band)]
    bias = np.stack([np.where(c, 0.0, NEG) for c in cases]).astype(np.float32)
    return jnp.asarray(np.tile(bias, (1, ATT_GROUP, 1)))


def _attention(q, kv, bias, sink_t, n_batch, seq, ctx_len, need_ctx):
    blk = ATT_BLOCK
    big = ATT_SUB * blk
    assert ATT_SUB == 2 and seq % big == 0 and ctx_len == big
    nb = seq // blk
    ns = seq // big
    n_lat = n_batch * seq
    steps = ns + (1 if need_ctx else 0)
    n_out = n_lat + (n_batch * ctx_len if need_ctx else 0)

    def qrow(b, s):
        return jnp.where(s < ns, b * ns + s, n_lat // big + b)

    def small(off):
        def index(b, s):
            loc = jnp.clip(2 * jnp.minimum(s, ns - 1) + off, 0, nb - 1)
            return (b * nb + loc, 0)
        return index

    def case0(b, s):
        return (jnp.where(s < ns, jnp.where(s == 0, 0, 1), 3), 0, 0)

    def case1(b, s):
        return (jnp.where(s < ns, jnp.where(s == ns - 1, 2, 1), 3), 0, 0)

    return pl.pallas_call(
        _attn_kernel,
        grid=(n_batch, steps),
        in_specs=[
            pl.BlockSpec((big, q.shape[1]), lambda b, s: (qrow(b, s), 0)),
            pl.BlockSpec((blk, kv.shape[1]), small(-1)),
            pl.BlockSpec((big, kv.shape[1]), lambda b, s: (b * ns + jnp.minimum(s, ns - 1), 0)),
            pl.BlockSpec((blk, kv.shape[1]), small(2)),
            pl.BlockSpec((ctx_len, kv.shape[1]), lambda b, s: (n_lat // ctx_len + b, 0)),
            pl.BlockSpec((1,) + bias.shape[1:], case0),
            pl.BlockSpec((1,) + bias.shape[1:], case1),
            pl.BlockSpec(sink_t.shape, lambda b, s: (0, 0)),
        ],
        out_specs=pl.BlockSpec((big, q.shape[1]), lambda b, s: (qrow(b, s), 0)),
        out_shape=jax.ShapeDtypeStruct((n_out, q.shape[1]), BF16),
        compiler_params=_params(2, VMEM_LIMIT),
        name="window_attention",
    )(q, kv, kv, kv, kv, bias, bias, sink_t)


def _route(logits):
    m = jnp.max(logits, axis=0, keepdims=True)
    e = jnp.exp(logits - m)
    p = e / jnp.sum(e, axis=0, keepdims=True)
    rows = [p[i:i + 1, :] for i in range(N_EXPERTS)]
    best = None
    gsel = None
    for g in range(N_GROUPS):
        a, b, c, d = rows[4 * g:4 * g + 4]
        m1, n1 = jnp.maximum(a, b), jnp.minimum(a, b)
        m2, n2 = jnp.maximum(c, d), jnp.minimum(c, d)
        score = jnp.maximum(m1, m2) + jnp.maximum(jnp.minimum(m1, m2), jnp.maximum(n1, n2))
        if g == 0:
            best, gsel = score, jnp.zeros(score.shape, I32)
        else:
            upd = score > best
            gsel = jnp.where(upd, g, gsel)
            best = jnp.where(upd, score, best)
    v = [jnp.where(gsel == 0, rows[j], jnp.where(gsel == 1, rows[4 + j], jnp.where(gsel == 2, rows[8 + j], rows[12 + j])))
         for j in range(EXPERTS_PER_GROUP)]
    b1, i1 = v[0], jnp.zeros(gsel.shape, I32)
    for j in range(1, 4):
        upd = v[j] > b1
        i1 = jnp.where(upd, j, i1)
        b1 = jnp.where(upd, v[j], b1)
    b2, i2 = jnp.full(b1.shape, -1.0, F32), jnp.zeros(gsel.shape, I32)
    for j in range(4):
        upd = jnp.logical_and(i1 != j, v[j] > b2)
        i2 = jnp.where(upd, j, i2)
        b2 = jnp.where(upd, v[j], b2)
    lo = jnp.minimum(i1, i2)
    hi = jnp.maximum(i1, i2)
    pair = jnp.where(lo == 0, hi - 1, jnp.where(lo == 1, 6 - hi, 5))
    return gsel * N_PAIRS + pair


def _post_kernel(mix_ref, x_ref, mod_ref, g2_ref, wo_ref, wrh_ref, wrl_ref, br_ref, tri_ref,
                 xo_ref, bk_ref, rank_ref, cnt_ref, carry_ref):
    @pl.when(pl.program_id(0) == 0)
    def _():
        carry_ref[...] = jnp.zeros(carry_ref.shape, F32)

    d = x_ref.shape[1]
    tm = x_ref.shape[0]
    y = jnp.dot(mix_ref[...], wo_ref[...], preferred_element_type=F32)
    x = x_ref[...] + mod_ref[0, :, 2 * d:3 * d] * y
    xo_ref[...] = x
    h2 = _rms_mod(x, g2_ref[...], mod_ref[0, :, 4 * d:5 * d], mod_ref[0, :, 3 * d:4 * d])
    hi = h2.astype(BF16)
    lo = (h2 - hi.astype(F32)).astype(BF16)
    logits = (lax.dot_general(wrh_ref[...], hi, _NT, preferred_element_type=F32)
              + lax.dot_general(wrh_ref[...], lo, _NT, preferred_element_type=F32)
              + lax.dot_general(wrl_ref[...], hi, _NT, preferred_element_type=F32)
              + br_ref[:, 0:1])
    bucket = _route(logits)
    bk_ref[...] = bucket
    ids = lax.broadcasted_iota(I32, (BUCKET_ROWS, tm), 0)
    onehot = (ids == bucket).astype(F32)
    before = jnp.dot(onehot.astype(BF16), tri_ref[...], preferred_element_type=F32)
    carry = carry_ref[...]
    rank_ref[...] = jnp.sum(onehot * (before + carry[:, 0:1]), axis=0, keepdims=True).astype(I32)
    carry = carry + jnp.sum(onehot, axis=1, keepdims=True)
    carry_ref[...] = carry
    cnt_ref[...] = carry.astype(I32)


def _post(mix, xa, mod_l, g2, wo, wr_hi, wr_lo, br_t, tri, n_act, n_lat, seq, n_batch):
    d = xa.shape[1]
    tm = ROW_TILE
    mrow = _mod_row_map(tm, n_lat, seq, n_batch)
    full = lambda a: pl.BlockSpec(a.shape, lambda i, _nd=a.ndim: (0,) * _nd)
    return pl.pallas_call(
        _post_kernel,
        grid=(n_act // tm,),
        in_specs=[
            pl.BlockSpec((tm, d), lambda i: (i, 0)),
            pl.BlockSpec((tm, d), lambda i: (i, 0)),
            pl.BlockSpec((1, 1, 6 * d), lambda i: (mrow(i), 0, 0)),
            full(g2), full(wo), full(wr_hi), full(wr_lo), full(br_t), full(tri),
        ],
        out_specs=[
            pl.BlockSpec((tm, d), lambda i: (i, 0)),
            pl.BlockSpec((1, tm), lambda i: (0, i)),
            pl.BlockSpec((1, tm), lambda i: (0, i)),
            pl.BlockSpec((BUCKET_ROWS, 128), lambda i: (0, 0)),
        ],
        out_shape=[
            jax.ShapeDtypeStruct((n_act, d), F32),
            jax.ShapeDtypeStruct((1, n_act), I32),
            jax.ShapeDtypeStruct((1, n_act), I32),
            jax.ShapeDtypeStruct((BUCKET_ROWS, 128), I32),
        ],
        scratch_shapes=[pltpu.VMEM((BUCKET_ROWS, 128), F32)],
        compiler_params=_params(1, VMEM_LIMIT),
        name="post_mixer",
    )(mix, xa, mod_l, g2, wo, wr_hi, wr_lo, br_t, tri)


def _dispatch_kernel(pos_ref, x_ref, mod_ref, g2_ref, hs_in_ref, hs_ref, hbuf_ref, sem):
    del hs_in_ref
    d = x_ref.shape[1]
    tm = x_ref.shape[0]
    h2 = _rms_mod(x_ref[...], g2_ref[...], mod_ref[0, :, 4 * d:5 * d], mod_ref[0, :, 3 * d:4 * d])
    hbuf_ref[...] = h2.reshape(hbuf_ref.shape)

    def body(j, carry):
        for u in range(DMA_UNROLL):
            r = j * DMA_UNROLL + u
            pltpu.make_async_copy(hbuf_ref.at[r], hs_ref.at[pos_ref[0, 0, r]], sem).start(priority=u % 2)
        return carry
    lax.fori_loop(0, tm // DMA_UNROLL, body, 0)
    pltpu.make_async_copy(hbuf_ref, hs_ref.at[pl.ds(0, tm)], sem).wait()


def _dispatch(pos3, xa, mod_l, g2, hs, n_lat, seq, n_batch):
    n, d = xa.shape
    tm = ROW_TILE
    mrow = _mod_row_map(tm, n_lat, seq, n_batch)
    return pl.pallas_call(
        _dispatch_kernel,
        grid=(n // tm,),
        in_specs=[
            pl.BlockSpec((1, 1, tm), lambda i: (i, 0, 0), memory_space=pltpu.SMEM),
            pl.BlockSpec((tm, d), lambda i: (i, 0)),
            pl.BlockSpec((1, 1, 6 * d), lambda i: (mrow(i), 0, 0)),
            pl.BlockSpec((1, d), lambda i: (0, 0)),
            pl.BlockSpec(memory_space=pl.ANY),
        ],
        out_specs=pl.BlockSpec(memory_space=pl.ANY),
        out_shape=jax.ShapeDtypeStruct(hs.shape, F32),
        scratch_shapes=[pltpu.VMEM((tm,) + hs.shape[1:], F32), pltpu.SemaphoreType.DMA(())],
        input_output_aliases={4: 0},
        compiler_params=_params(1, VMEM_LIMIT),
        name="dispatch",
    )(pos3, xa, mod_l, g2, hs)


def _moe_kernel(ea_ref, eb_ref, nu_ref, h_ref, wga_ref, wua_ref, wda_ref, wgb_ref, wub_ref, wdb_ref,
                wrt_ref, brt_ref, o_ref):
    t = pl.program_id(0)

    @pl.when(t >= nu_ref[0])
    def _():
        o_ref[...] = jnp.zeros(o_ref.shape, F32)

    @pl.when(t < nu_ref[0])
    def _():
        ea = ea_ref[t]
        eb = eb_ref[t]
        tm = h_ref.shape[0]
        h = h_ref[...].reshape(tm, h_ref.shape[1] * h_ref.shape[2])
        hb = h.astype(BF16)
        wdiff = wrt_ref[pl.ds(ea, 1), :] - wrt_ref[pl.ds(eb, 1), :]
        bdiff = brt_ref[pl.ds(ea, 1), 0:1] - brt_ref[pl.ds(eb, 1), 0:1]
        wa = jax.nn.sigmoid(jnp.sum(h * wdiff, axis=-1, keepdims=True) + bdiff)

        def expert(wg_ref, wu_ref, wd_ref):
            f = wg_ref.shape[3]
            acc = None
            for c in range(f // 512):
                cols = slice(512 * c, 512 * (c + 1))
                g = jnp.dot(hb, wg_ref[0, 0, :, cols], preferred_element_type=F32)
                u = jnp.dot(hb, wu_ref[0, 0, :, cols], preferred_element_type=F32)
                part = jnp.dot((_silu(g) * u).astype(BF16), wd_ref[0, 0, cols, :], preferred_element_type=F32)
                acc = part if acc is None else acc + part
            return acc

        y = wa * expert(wga_ref, wua_ref, wda_ref) + (1.0 - wa) * expert(wgb_ref, wub_ref, wdb_ref)
        o_ref[...] = y.reshape(o_ref.shape)


def _moe(layer, ea, eb, nu, hs, wg, wu, wd, wr_t, br_t):
    p_rows = hs.shape[0]
    row = hs.shape[1:]
    tm = MOE_TILE
    d, f = wg.shape[2:]
    wa_spec = lambda shape: pl.BlockSpec(shape, lambda t, ea_, eb_, nu_: (layer, ea_[t], 0, 0))
    wb_spec = lambda shape: pl.BlockSpec(shape, lambda t, ea_, eb_, nu_: (layer, eb_[t], 0, 0))
    grid_spec = pltpu.PrefetchScalarGridSpec(
        num_scalar_prefetch=3,
        grid=(p_rows // tm,),
        in_specs=[
            pl.BlockSpec((tm,) + row, lambda t, ea_, eb_, nu_: (t, 0, 0)),
            wa_spec((1, 1, d, f)), wa_spec((1, 1, d, f)), wa_spec((1, 1, f, d)),
            wb_spec((1, 1, d, f)), wb_spec((1, 1, d, f)), wb_spec((1, 1, f, d)),
            pl.BlockSpec(wr_t.shape, lambda t, ea_, eb_, nu_: (0, 0)),
            pl.BlockSpec(br_t.shape, lambda t, ea_, eb_, nu_: (0, 0)),
        ],
        out_specs=pl.BlockSpec((tm,) + row, lambda t, ea_, eb_, nu_: (t, 0, 0)),
    )
    return pl.pallas_call(
        _moe_kernel,
        grid_spec=grid_spec,
        out_shape=jax.ShapeDtypeStruct(hs.shape, F32),
        compiler_params=_params(1, VMEM_LIMIT),
        name="experts",
    )(ea, eb, nu, hs, wg, wu, wd, wg, wu, wd, wr_t, br_t)


def _combine_kernel(pos_ref, posn_ref, x_ref, mod_ref, fg_ref, ys_ref, o_ref, gbuf_ref, sem):
    d = x_ref.shape[1]
    x = x_ref[...] + mod_ref[0, :, 5 * d:6 * d] * _gather_rows(pos_ref, posn_ref, ys_ref, gbuf_ref, sem)
    o_ref[...] = x * lax.rsqrt(jnp.mean(x * x, axis=-1, keepdims=True) + EPS) * fg_ref[...]


def _combine(pos3, xa, mod_l, fg, ys, n_lat, seq, n_batch):
    n, d = xa.shape
    tm = ROW_TILE
    steps = n // tm
    mrow = _mod_row_map(tm, n_lat, seq, n_batch)
    return pl.pallas_call(
        _combine_kernel,
        grid=(steps,),
        in_specs=[
            pl.BlockSpec((1, 1, tm), lambda i: (i, 0, 0), memory_space=pltpu.SMEM),
            pl.BlockSpec((1, 1, tm), lambda i: (jnp.minimum(i + 1, steps - 1), 0, 0), memory_space=pltpu.SMEM),
            pl.BlockSpec((tm, d), lambda i: (i, 0)),
            pl.BlockSpec((1, 1, 6 * d), lambda i: (mrow(i), 0, 0)),
            pl.BlockSpec((1, d), lambda i: (0, 0)),
            pl.BlockSpec(memory_space=pl.ANY),
        ],
        out_specs=pl.BlockSpec((tm, d), lambda i: (i, 0)),
        out_shape=jax.ShapeDtypeStruct((n, d), F32),
        scratch_shapes=[pltpu.VMEM((2, tm) + ys.shape[1:], F32), pltpu.SemaphoreType.DMA((2,))],
        compiler_params=_params(1, VMEM_LIMIT),
        name="combine",
    )(pos3, pos3, xa, mod_l, fg, ys)


def _bucket_plan(bucket, rank, counts, n_tiles, tile):
    cnt = counts[:N_BUCKETS, 0]
    padded = ((cnt + tile - 1) // tile) * tile
    ends = jnp.cumsum(padded)
    offs = ends - padded
    pos = offs[bucket[0]] + rank[0]
    n_used = ends[-1] // tile
    t_eff = jnp.minimum(jnp.arange(n_tiles, dtype=I32), jnp.maximum(n_used - 1, 0))
    tb = jnp.minimum(jnp.sum((ends[None, :] <= (t_eff * tile)[:, None]).astype(I32), axis=1), N_BUCKETS - 1)
    pairs = np.array(PAIR_SLOTS, dtype=np.int32)
    ea = (tb // N_PAIRS) * EXPERTS_PER_GROUP + jnp.asarray(pairs[:, 0])[tb % N_PAIRS]
    eb = (tb // N_PAIRS) * EXPERTS_PER_GROUP + jnp.asarray(pairs[:, 1])[tb % N_PAIRS]
    return pos.astype(I32), ea.astype(I32), eb.astype(I32), n_used.astype(I32).reshape(1)


def kernel(x, c, ctx, c_ctx, w_mod, b_mod, norm_g, w_in_even, w_out_even, ret_log_decay, ret_gn_g, sg_ln_g, sg_ln_b, sg_w, sg_b, w_qkv_odd, w_o_odd, attn_sink, w_router, b_router, w_e_gate, w_e_up, w_e_down, final_g):
    n_batch, seq, d = x.shape
    ctx_len = ctx.shape[1]
    depth = w_mod.shape[0]
    n_lat = n_batch * seq
    n_all = n_lat + n_batch * ctx_len
    tm = ROW_TILE
    assert seq % tm == 0 and (n_batch * ctx_len) % tm == 0

    mod_rows = -(-(n_batch + 1) // 16) * 16
    c_rows = jnp.zeros((mod_rows, d), F32).at[:n_batch].set(c).at[n_batch].set(c_ctx)
    mod = _modulation(c_rows, w_mod, b_mod).reshape(depth, mod_rows, 1, 6 * d)

    wr_t = w_router.T.astype(F32)
    wr_hi = wr_t.astype(BF16)
    wr_lo = (wr_t - wr_hi.astype(F32)).astype(BF16)
    br_t = jnp.broadcast_to(b_router.astype(F32)[:, None], (N_EXPERTS, 128))
    tri = jnp.triu(jnp.ones((tm, tm), BF16), k=1)
    cos_t, sin_t = _rope_tables(seq, tm)
    bias = _attn_bias()

    wg_all, wu_all, wd_all = w_e_gate.astype(BF16), w_e_up.astype(BF16), w_e_down.astype(BF16)

    n_tiles = n_all // MOE_TILE + N_BUCKETS
    p_rows = n_tiles * MOE_TILE
    hs = jnp.zeros((p_rows, ROW_SUB, d // ROW_SUB), F32)

    source = (x.reshape(n_lat, d), ctx.reshape(n_batch * ctx_len, d))
    for l in range(depth):
        i = l // 2
        last = l == depth - 1
        mod_l = mod[l]
        g1 = norm_g[l, 0].reshape(1, d)
        g2 = norm_g[l, 1].reshape(1, d)
        if l % 2 == 0:
            xa, p = _in_stage(source, mod_l, g1, w_in_even[i].astype(BF16), None, n_lat, seq, n_batch)
            tabs = _retention_tables(ret_log_decay[i], RET_BLOCK)
            sg_bias = jnp.repeat(sg_b[i].astype(F32).T, SG_CHUNK, axis=1)
            mix = _even_mix(p, tabs, ret_gn_g[i].reshape(1, -1), sg_ln_g[i].reshape(1, -1), sg_ln_b[i].reshape(1, -1),
                            sg_w[i].astype(BF16), sg_bias, n_batch, seq, ctx_len)
            wo = w_out_even[i].astype(BF16)
        else:
            xa, q, kv = _in_stage(source, mod_l, g1, w_qkv_odd[i].astype(BF16), (cos_t, sin_t), n_lat, seq, n_batch)
            sink_t = jnp.broadcast_to(attn_sink[i].astype(F32)[:, None], (ATT_HEADS, 128))
            mix = _attention(q, kv, bias, sink_t, n_batch, seq, ctx_len, need_ctx=not last)
            wo = w_o_odd[i].astype(BF16)
        n_act = n_lat if last else n_all
        xm, bucket, rank, counts = _post(mix, xa, mod_l, g2, wo, wr_hi, wr_lo, br_t, tri, n_act, n_lat, seq, n_batch)
        pos, ea, eb, n_used = _bucket_plan(bucket, rank, counts, n_tiles, MOE_TILE)
        pos3 = pos.reshape(n_act // tm, 1, tm)
        hs = _dispatch(pos3, xm, mod_l, g2, hs, n_lat, seq, n_batch)
        ys = _moe(l, ea, eb, n_used, hs, wg_all, wu_all, wd_all, wr_t, br_t)
        source = (pos3, xm, mod_l, ys)
    out = _combine(pos3, xm, mod_l, final_g.reshape(1, d), ys, n_lat, seq, n_batch)
    return out.reshape(n_batch, seq, d)
```

```python
import functools

import jax
import jax.numpy as jnp
import numpy as np
from jax import lax
from jax.experimental import pallas as pl
from jax.experimental.pallas import tpu as pltpu

F32 = jnp.float32
BF16 = jnp.bfloat16
I32 = jnp.int32

EPS = 1e-6
NEG = -1e30
GRID_W = 64
ROPE_BASE = 10000.0
RET_HEADS = 4
RET_QK_DIM = 64
RET_V_DIM = 128
SG_GROUPS = 4
SG_CHUNK = 128
RET_BLOCK = 256
ATT_HEADS = 8
ATT_KV_HEADS = 2
ATT_GROUP = ATT_HEADS // ATT_KV_HEADS
HEAD_DIM = 128
ATT_BLOCK = 128
WINDOW = 128
ATT_SUB = 2
ATT_SLAB = 32
N_EXPERTS = 16
N_GROUPS = 4
EXPERTS_PER_GROUP = 4
N_PAIRS = 6
N_BUCKETS = N_GROUPS * N_PAIRS
PAIR_SLOTS = ((0, 1), (0, 2), (0, 3), (1, 3), (1, 2), (3, 2))
BUCKET_ROWS = 32

ROW_TILE = 512
MOE_TILE = 256
ROW_SUB = 8
VMEM_LIMIT = 56 * 1024 * 1024

_NT = (((1,), (1,)), ((), ()))
_TN = (((0,), (0,)), ((), ()))


def _params(n_axes, vmem=None):
    return pltpu.CompilerParams(dimension_semantics=("arbitrary",) * n_axes, vmem_limit_bytes=vmem)


def _silu(v):
    return v * jax.nn.sigmoid(v)


def _rms_mod(x, g, scale, shift):
    y = x * lax.rsqrt(jnp.mean(x * x, axis=-1, keepdims=True) + EPS) * g
    return y * (1.0 + scale) + shift


def _mod_kernel(c_ref, w_ref, b_ref, o_ref):
    a = _silu(c_ref[...]).astype(BF16)
    o_ref[0] = jnp.dot(a, w_ref[0].astype(BF16), preferred_element_type=F32) + b_ref[0]


def _modulation(c_rows, w_mod, b_mod):
    depth, d, six_d = w_mod.shape
    mr = c_rows.shape[0]
    tn = 1536
    return pl.pallas_call(
        _mod_kernel,
        grid=(depth, six_d // tn),
        in_specs=[
            pl.BlockSpec((mr, d), lambda l, j: (0, 0)),
            pl.BlockSpec((1, d, tn), lambda l, j: (l, 0, j)),
            pl.BlockSpec((1, 1, tn), lambda l, j: (l, 0, j)),
        ],
        out_specs=pl.BlockSpec((1, mr, tn), lambda l, j: (l, 0, j)),
        out_shape=jax.ShapeDtypeStruct((depth, mr, six_d), F32),
        compiler_params=_params(2, VMEM_LIMIT),
        name="modulation",
    )(c_rows, w_mod, b_mod.reshape(depth, 1, six_d))


def _mod_row_map(tm, n_lat, seq, n_batch):
    def index(i):
        row0 = i * tm
        return jnp.where(row0 < n_lat, row0 // seq, n_batch)
    return index


def _gather_rows(pos_ref, posn_ref, ys_ref, gbuf_ref, sem):
    i = pl.program_id(0)
    slot = i % 2
    tm = gbuf_ref.shape[1]

    def issue(p_ref, s, lo, hi):
        for r in range(lo, hi):
            pltpu.make_async_copy(ys_ref.at[p_ref[0, 0, r]], gbuf_ref.at[s, r], sem.at[s]).start(priority=r % 2)

    def wait(s):
        pltpu.make_async_copy(ys_ref.at[pl.ds(0, tm)], gbuf_ref.at[s], sem.at[s]).wait()

    @pl.when(i == 0)
    def _():
        issue(pos_ref, 0, 0, tm)

    wait(slot)
    rows = gbuf_ref[slot].reshape(tm, gbuf_ref.shape[2] * gbuf_ref.shape[3])

    def issue_next(k, parts):
        issue(posn_ref, 1 - slot, tm * k // parts, tm * (k + 1) // parts)

    def finish():
        @pl.when(i == pl.num_programs(0) - 1)
        def _():
            wait(1 - slot)

    return rows, issue_next, finish


def _project_even(h, w_ref, o_ref, between):
    for c in range(5):
        between(c, 5)
        p = jnp.dot(h, w_ref[:, 512 * c:512 * (c + 1)], preferred_element_type=F32)
        if c == 0:
            col = lax.broadcasted_iota(I32, p.shape, 1)
            p = jnp.where(col >= 256, p * (RET_QK_DIM ** -0.5), p)
        elif c >= 3:
            p = jax.nn.gelu(p)
        o_ref[:, 512 * c:512 * (c + 1)] = p.astype(BF16)


def _project_odd(h, w_ref, cos_ref, sin_ref, q_ref, kv_ref, between):
    tm = h.shape[0]
    cos = cos_ref[...]
    sin = sin_ref[...]
    lane = lax.broadcasted_iota(I32, (tm, HEAD_DIM), 1)
    first = (lane % 64) < 32

    def rope(p):
        partner = jnp.where(first, pltpu.roll(p, 96, 1), pltpu.roll(p, 32, 1))
        return p * cos + partner * sin

    scale = HEAD_DIM ** -0.5
    for c in range(3):
        between(c, 3)
        p = jnp.dot(h, w_ref[:, 512 * c:512 * (c + 1)], preferred_element_type=F32)
        for j in range(4):
            pj = p[:, 128 * j:128 * (j + 1)]
            if c < 2:
                q_ref[:, 512 * c + 128 * j:512 * c + 128 * (j + 1)] = (rope(pj) * scale).astype(BF16)
            elif j < 2:
                kv_ref[:, 128 * j:128 * (j + 1)] = rope(pj).astype(BF16)
            else:
                kv_ref[:, 128 * j:128 * (j + 1)] = pj.astype(BF16)


def _in_kernel(*refs, first, odd, lat_tiles):
    if first:
        (xl_ref, xc_ref), refs = refs[:2], refs[2:]
        x = jnp.where(pl.program_id(0) < lat_tiles, xl_ref[...], xc_ref[...])
        between, finish = (lambda k, parts: None), (lambda: None)
    else:
        (pos_ref, posn_ref, x_ref, modp_ref, ys_ref), refs = refs[:5], refs[5:]
        gbuf_ref, sem = refs[-2:]
        refs = refs[:-2]
        d = x_ref.shape[1]
        rows, between, finish = _gather_rows(pos_ref, posn_ref, ys_ref, gbuf_ref, sem)
        x = x_ref[...] + modp_ref[0, :, 5 * d:6 * d] * rows
    mod_ref, g_ref, w_ref = refs[:3]
    d = x.shape[1]
    h = _rms_mod(x, g_ref[...], mod_ref[0, :, d:2 * d], mod_ref[0, :, 0:d]).astype(BF16)
    if odd:
        cos_ref, sin_ref, xo_ref, q_ref, kv_ref = refs[3:]
        xo_ref[...] = x
        _project_odd(h, w_ref, cos_ref, sin_ref, q_ref, kv_ref, between)
    else:
        xo_ref, o_ref = refs[3:]
        xo_ref[...] = x
        _project_even(h, w_ref, o_ref, between)
    finish()


def _in_stage(source, mod_l, g, w, rope, n_lat, seq, n_batch):
    first = len(source) == 2
    odd = rope is not None
    tm = ROW_TILE
    if first:
        xl, xc = source
        d = xl.shape[1]
        n = xl.shape[0] + xc.shape[0]
        lat_tiles = xl.shape[0] // tm
        src_specs = [pl.BlockSpec((tm, d), lambda i: (jnp.minimum(i, lat_tiles - 1), 0)),
                     pl.BlockSpec((tm, d), lambda i: (jnp.maximum(i - lat_tiles, 0), 0))]
        scratch = []
    else:
        pos3, xm, mod_prev, ys = source
        n, d = xm.shape
        lat_tiles = n_lat // tm
        steps = n // tm
        mrow_p = _mod_row_map(tm, n_lat, seq, n_batch)
        src_specs = [pl.BlockSpec((1, 1, tm), lambda i: (i, 0, 0), memory_space=pltpu.SMEM),
                     pl.BlockSpec((1, 1, tm), lambda i: (jnp.minimum(i + 1, steps - 1), 0, 0), memory_space=pltpu.SMEM),
                     pl.BlockSpec((tm, d), lambda i: (i, 0)),
                     pl.BlockSpec((1, 1, 6 * d), lambda i: (mrow_p(i), 0, 0)),
                     pl.BlockSpec(memory_space=pl.ANY)]
        source = (pos3, pos3, xm, mod_prev, ys)
        scratch = [pltpu.VMEM((2, tm) + ys.shape[1:], F32), pltpu.SemaphoreType.DMA((2,))]
    mrow = _mod_row_map(tm, n_lat, seq, n_batch)
    specs = src_specs + [pl.BlockSpec((1, 1, 6 * d), lambda i: (mrow(i), 0, 0)),
                         pl.BlockSpec((1, d), lambda i: (0, 0)),
                         pl.BlockSpec(w.shape, lambda i: (0, 0))]
    args = tuple(source) + (mod_l, g, w)
    out_specs = [pl.BlockSpec((tm, d), lambda i: (i, 0))]
    out_shape = [jax.ShapeDtypeStruct((n, d), F32)]
    if odd:
        tiles_per_seq = seq // tm

        def rope_row(i):
            return jnp.where(i < lat_tiles, i % tiles_per_seq, tiles_per_seq)

        specs += [pl.BlockSpec((tm, HEAD_DIM), lambda i: (rope_row(i), 0))] * 2
        args += tuple(rope)
        widths = (ATT_HEADS * HEAD_DIM, 2 * ATT_KV_HEADS * HEAD_DIM)
    else:
        widths = (w.shape[1],)
    out_specs += [pl.BlockSpec((tm, wd), lambda i: (i, 0)) for wd in widths]
    out_shape += [jax.ShapeDtypeStruct((n, wd), BF16) for wd in widths]
    return pl.pallas_call(
        functools.partial(_in_kernel, first=first, odd=odd, lat_tiles=lat_tiles),
        grid=(n // tm,),
        in_specs=specs,
        out_specs=out_specs,
        out_shape=out_shape,
        scratch_shapes=scratch,
        compiler_params=_params(1, VMEM_LIMIT),
        name="in_odd" if odd else "in_even",
    )(*args)


def _rope_tables(seq, tm):
    t = jnp.arange(seq)
    row = (t // GRID_W).astype(F32)
    col = (t % GRID_W).astype(F32)
    quarter = HEAD_DIM // 4
    inv_freq = ROPE_BASE ** (-jnp.arange(quarter, dtype=F32) / quarter)
    ar = row[:, None] * inv_freq[None, :]
    ac = col[:, None] * inv_freq[None, :]
    cos = jnp.concatenate([jnp.cos(ar), jnp.cos(ar), jnp.cos(ac), jnp.cos(ac)], axis=1)
    sin = jnp.concatenate([-jnp.sin(ar), jnp.sin(ar), -jnp.sin(ac), jnp.sin(ac)], axis=1)
    cos = jnp.concatenate([cos, jnp.ones((tm, HEAD_DIM), F32)], axis=0)
    sin = jnp.concatenate([sin, jnp.zeros((tm, HEAD_DIM), F32)], axis=0)
    return cos, sin


def _even_mix_kernel(qk_ref, v_ref, gate_ref, u_ref, s_ref, dec_ref, xi_ref, zf_ref, zb_ref, gdec_ref,
                     gn_ref, lng_ref, lnb_ref, ws_ref, sgb_ref, o_ref, ds_ref, sp_ref):
    sweep = pl.program_id(1)
    n = pl.program_id(2)
    n_slots = ds_ref.shape[0]
    cb = qk_ref.shape[0]
    lane = lax.broadcasted_iota(I32, (1, 128), 1)
    head_mask = [(lane // RET_QK_DIM == hh).astype(BF16) for hh in range(2)]
    upper_rows = lax.broadcasted_iota(I32, (128, 128), 0) < RET_QK_DIM

    @pl.when(sweep == 0)
    def _increments():
        for grp in range(2):
            k2 = qk_ref[:, 256 + 128 * grp:256 + 128 * (grp + 1)].astype(F32)
            v2 = v_ref[:, 256 * grp:256 * (grp + 1)]
            for half, z_ref in enumerate((zf_ref, zb_ref)):
                kz = (k2 * z_ref[grp]).astype(BF16)
                full = lax.dot_general(kz, v2, _TN, preferred_element_type=F32)
                ds_ref[n, grp, :, 128 * half:128 * (half + 1)] = jnp.where(
                    upper_rows, full[:, 0:128], full[:, 128:256])

    @pl.when(jnp.logical_and(sweep == 1, n == 0))
    def _scan():
        sp_ref[0] = jnp.zeros(sp_ref.shape[1:], F32)
        fwd = ds_ref[0, :, :, 0:128]
        for j in range(1, n_slots):
            sp_ref[j, :, :, 0:128] = fwd
            fwd = fwd * gdec_ref[:, :, 0:128] + ds_ref[j, :, :, 0:128]
        bwd = ds_ref[0, :, :, 128:256]
        for j in range(n_slots - 1, 0, -1):
            sp_ref[j, :, :, 128:256] = bwd
            bwd = bwd * gdec_ref[:, :, 128:256] + ds_ref[j, :, :, 128:256]

    @pl.when(sweep == 1)
    def _outputs():
        for grp in range(2):
            q2 = qk_ref[:, 128 * grp:128 * (grp + 1)]
            k2 = qk_ref[:, 256 + 128 * grp:256 + 128 * (grp + 1)]
            state = sp_ref[n, grp].astype(BF16)
            for hh in range(2):
                hd = 2 * grp + hh
                cols = slice(128 * hd, 128 * (hd + 1))
                qm = q2 * head_mask[hh]
                sc = lax.dot_general(qm, k2, _NT, preferred_element_type=F32) * dec_ref[hd]
                inner = jnp.dot(sc.astype(BF16), v_ref[:, cols], preferred_element_type=F32)
                cross = jnp.dot(qm, state, preferred_element_type=F32) * xi_ref[hd]
                o = inner + cross[:, 0:128] + cross[:, 128:256]
                mu = jnp.mean(o, axis=-1, keepdims=True)
                oc = o - mu
                var = jnp.mean(oc * oc, axis=-1, keepdims=True)
                y = oc * lax.rsqrt(var + EPS) * gn_ref[:, cols]
                y = y * _silu(gate_ref[:, cols].astype(F32))
                o_ref[:, cols] = y.astype(BF16)
        for sub in range(cb // SG_CHUNK):
            rows = slice(SG_CHUNK * sub, SG_CHUNK * (sub + 1))
            s = s_ref[rows, :].astype(F32)
            mu = jnp.mean(s, axis=-1, keepdims=True)
            sc_ = s - mu
            var = jnp.mean(sc_ * sc_, axis=-1, keepdims=True)
            sn = (sc_ * lax.rsqrt(var + EPS) * lng_ref[...] + lnb_ref[...]).astype(BF16)
            for g in range(SG_GROUPS):
                cols = slice(128 * g, 128 * (g + 1))
                mixed = jnp.dot(ws_ref[g], sn[:, cols], preferred_element_type=F32) + sgb_ref[:, cols]
                o_ref[rows, 512 + 128 * g:512 + 128 * (g + 1)] = (
                    u_ref[rows, cols].astype(F32) * mixed).astype(BF16)


def _retention_tables(log_decay, cb):
    lg = -jnp.exp(log_decay.astype(F32))
    pos = jnp.arange(cb, dtype=F32)
    diff = pos[:, None] - pos[None, :]
    lower = jnp.where(diff >= 0, jnp.exp(lg[0][:, None, None] * jnp.maximum(diff, 0.0)[None]), 0.0)
    upper = jnp.where(diff <= 0, jnp.exp(lg[1][:, None, None] * jnp.maximum(-diff, 0.0)[None]), 0.0)
    dec = lower + upper
    xi_f = jnp.exp(lg[0][:, None] * (pos[None, :] + 1.0))
    xi_b = jnp.exp(lg[1][:, None] * (cb - pos[None, :]))
    xi = jnp.concatenate([jnp.broadcast_to(xi_f[:, :, None], (RET_HEADS, cb, 128)),
                          jnp.broadcast_to(xi_b[:, :, None], (RET_HEADS, cb, 128))], axis=2)
    zeta_f = jnp.exp(lg[0][:, None] * (cb - 1.0 - pos[None, :]))
    zeta_b = jnp.exp(lg[1][:, None] * pos[None, :])

    def lanes(z):
        return jnp.repeat(z.reshape(2, 2, cb), RET_QK_DIM, axis=1).transpose(0, 2, 1)

    gstep = jnp.exp(lg * cb)

    def rows(gv):
        return jnp.broadcast_to(jnp.repeat(gv.reshape(2, 2), RET_QK_DIM, axis=1)[:, :, None], (2, 128, 128))

    gdec = jnp.concatenate([rows(gstep[0]), rows(gstep[1])], axis=2)
    return dec, xi, lanes(zeta_f), lanes(zeta_b), gdec


def _even_mix(p, tabs, gn_g, ln_g, ln_b, w_s, sg_bias, n_batch, seq, ctx_len):
    n = p.shape[0]
    cb = RET_BLOCK
    assert ctx_len == cb and seq % cb == 0
    nl = seq // cb
    ctx0 = n_batch * seq // cb

    def rb(b, n_):
        return jnp.where(n_ == 0, ctx0 + b, b * nl + n_ - 1)

    def late(col):
        return lambda b, s, n_: (rb(b, n_ * s), col)

    dec, xi, zf, zb, gdec = tabs
    full = lambda a: pl.BlockSpec(a.shape, lambda b, s, n_, _nd=a.ndim: (0,) * _nd)
    return pl.pallas_call(
        _even_mix_kernel,
        grid=(n_batch, 2, nl + 1),
        in_specs=[
            pl.BlockSpec((cb, 512), lambda b, s, n_: (rb(b, n_), 0)),
            pl.BlockSpec((cb, 512), lambda b, s, n_: (rb(b, n_), 1)),
            pl.BlockSpec((cb, 512), late(2)),
            pl.BlockSpec((cb, 512), late(3)),
            pl.BlockSpec((cb, 512), late(4)),
            full(dec), full(xi), full(zf), full(zb), full(gdec),
            full(gn_g), full(ln_g), full(ln_b), full(w_s), full(sg_bias),
        ],
        out_specs=pl.BlockSpec((cb, 1024), late(0)),
        out_shape=jax.ShapeDtypeStruct((n, 1024), BF16),
        scratch_shapes=[pltpu.VMEM((nl + 1, 2, 128, 256), F32), pltpu.VMEM((nl + 1, 2, 128, 256), F32)],
        compiler_params=_params(3, VMEM_LIMIT),
        name="even_mix",
    )(p, p, p, p, p, dec, xi, zf, zb, gdec, gn_g, ln_g, ln_b, w_s, sg_bias)


def _attn_kernel(q_ref, kvp_ref, kvm_ref, kvn_ref, kvx_ref, bias0_ref, bias1_ref, sink_ref, o_ref, s_scr, p_scr, r_scr):
    blk = ATT_BLOCK
    nloc = 3 * blk
    rows_all = ATT_GROUP * blk
    bias_refs = (bias0_ref, bias1_ref)
    chains = [(kvh, sub) for kvh in range(ATT_KV_HEADS) for sub in range(ATT_SUB)]
    v_all, sinks = {}, {}
    for ci, (kvh, sub) in enumerate(chains):
        kcols = slice(HEAD_DIM * kvh, HEAD_DIM * (kvh + 1))
        vcols = slice(HEAD_DIM * (ATT_KV_HEADS + kvh), HEAD_DIM * (ATT_KV_HEADS + kvh + 1))
        k_parts = [kvp_ref[:, kcols], kvm_ref[0:blk, kcols], kvm_ref[blk:2 * blk, kcols], kvn_ref[:, kcols]]
        v_parts = [kvp_ref[:, vcols], kvm_ref[0:blk, vcols], kvm_ref[blk:2 * blk, vcols], kvn_ref[:, vcols]]
        heads = [ATT_GROUP * kvh + j for j in range(ATT_GROUP)]
        rows = slice(blk * sub, blk * (sub + 1))
        k_all = jnp.concatenate(k_parts[sub:sub + 3] + [kvx_ref[:, kcols]], axis=0)
        v_all[ci] = jnp.concatenate(v_parts[sub:sub + 3] + [kvx_ref[:, vcols]], axis=0)
        q4 = jnp.concatenate([q_ref[rows, HEAD_DIM * hd:HEAD_DIM * (hd + 1)] for hd in heads], axis=0)
        s_scr[ci] = lax.dot_general(q4, k_all, _NT, preferred_element_type=F32)
        sinks[ci] = jnp.concatenate(
            [jnp.broadcast_to(sink_ref[hd:hd + 1, 0:1], (blk, 1)) for hd in heads], axis=0)
    nchunk_loc = nloc // 128
    nchunk = s_scr.shape[2] // 128

    def shifted(ci, sub, rs, k):
        t = s_scr[ci, rs, 128 * k:128 * (k + 1)]
        return t + bias_refs[sub][0, rs, 128 * k:128 * (k + 1)] if k < nchunk_loc else t

    mrow = {}
    for ci, (kvh, sub) in enumerate(chains):
        for r0 in range(0, rows_all, ATT_SLAB):
            rs = slice(r0, r0 + ATT_SLAB)
            part = shifted(ci, sub, rs, 0)
            for k in range(1, nchunk):
                part = jnp.maximum(part, shifted(ci, sub, rs, k))
            r_scr[ci, rs, :] = part
    for ci in range(len(chains)):
        m = jnp.maximum(jnp.max(r_scr[ci], axis=-1, keepdims=True), sinks[ci])
        mrow[ci] = m
        r_scr[ci] = jnp.broadcast_to(m, r_scr.shape[1:])
    for ci, (kvh, sub) in enumerate(chains):
        for r0 in range(0, rows_all, ATT_SLAB):
            rs = slice(r0, r0 + ATT_SLAB)
            mb = r_scr[ci, rs, :]
            part = None
            for k in range(nchunk):
                p = jnp.exp(shifted(ci, sub, rs, k) - mb)
                part = p if part is None else part + p
                p_scr[ci, rs, 128 * k:128 * (k + 1)] = p.astype(BF16)
            r_scr[ci, rs, :] = part
    inv = {}
    for ci in range(len(chains)):
        inv[ci] = 1.0 / (jnp.sum(r_scr[ci], axis=-1, keepdims=True) + jnp.exp(sinks[ci] - mrow[ci]))
    for ci, (kvh, sub) in enumerate(chains):
        rows = slice(blk * sub, blk * (sub + 1))
        o = (jnp.dot(p_scr[ci, :, 0:nloc], v_all[ci][0:nloc], preferred_element_type=F32)
             + jnp.dot(p_scr[ci, :, nloc:], v_all[ci][nloc:], preferred_element_type=F32)) * inv[ci]
        for j in range(ATT_GROUP):
            hd = ATT_GROUP * kvh + j
            o_ref[rows, HEAD_DIM * hd:HEAD_DIM * (hd + 1)] = o[blk * j:blk * (j + 1)].astype(BF16)


def _attn_bias():
    i = np.arange(ATT_BLOCK)[:, None]
    j = np.arange(3 * ATT_BLOCK)[None, :] - ATT_BLOCK
    band = np.abs(i - j) <= WINDOW
    cases = [band & (j >= 0), band, band & (j < ATT_BLOCK), np.zeros_like(band)]
    bias = np.stack([np.where(c, 0.0, NEG) for c in cases]).astype(np.float32)
    return jnp.asarray(np.tile(bias, (1, ATT_GROUP, 1)))


def _attention(q, kv, bias, sink_t, n_batch, seq, ctx_len, need_ctx):
    blk = ATT_BLOCK
    big = ATT_SUB * blk
    assert ATT_SUB == 2 and seq % big == 0 and ctx_len == big
    nb = seq // blk
    ns = seq // big
    n_lat = n_batch * seq
    steps = ns + (1 if need_ctx else 0)
    n_out = n_lat + (n_batch * ctx_len if need_ctx else 0)

    def qrow(b, s):
        return jnp.where(s < ns, b * ns + s, n_lat // big + b)

    def small(off):
        def index(b, s):
            loc = jnp.clip(2 * jnp.minimum(s, ns - 1) + off, 0, nb - 1)
            return (b * nb + loc, 0)
        return index

    def case0(b, s):
        return (jnp.where(s < ns, jnp.where(s == 0, 0, 1), 3), 0, 0)

    def case1(b, s):
        return (jnp.where(s < ns, jnp.where(s == ns - 1, 2, 1), 3), 0, 0)

    return pl.pallas_call(
        _attn_kernel,
        grid=(n_batch, steps),
        in_specs=[
            pl.BlockSpec((big, q.shape[1]), lambda b, s: (qrow(b, s), 0)),
            pl.BlockSpec((blk, kv.shape[1]), small(-1)),
            pl.BlockSpec((big, kv.shape[1]), lambda b, s: (b * ns + jnp.minimum(s, ns - 1), 0)),
            pl.BlockSpec((blk, kv.shape[1]), small(2)),
            pl.BlockSpec((ctx_len, kv.shape[1]), lambda b, s: (n_lat // ctx_len + b, 0)),
            pl.BlockSpec((1,) + bias.shape[1:], case0),
            pl.BlockSpec((1,) + bias.shape[1:], case1),
            pl.BlockSpec(sink_t.shape, lambda b, s: (0, 0)),
        ],
        out_specs=pl.BlockSpec((big, q.shape[1]), lambda b, s: (qrow(b, s), 0)),
        out_shape=jax.ShapeDtypeStruct((n_out, q.shape[1]), BF16),
        scratch_shapes=[pltpu.VMEM((ATT_KV_HEADS * ATT_SUB, ATT_GROUP * blk, 3 * blk + ctx_len), F32),
                        pltpu.VMEM((ATT_KV_HEADS * ATT_SUB, ATT_GROUP * blk, 3 * blk + ctx_len), BF16),
                        pltpu.VMEM((ATT_KV_HEADS * ATT_SUB, ATT_GROUP * blk, 128), F32)],
        compiler_params=_params(2, VMEM_LIMIT),
        name="window_attention",
    )(q, kv, kv, kv, kv, bias, bias, sink_t)


def _route(logits):
    m = jnp.max(logits, axis=0, keepdims=True)
    e = jnp.exp(logits - m)
    p = e / jnp.sum(e, axis=0, keepdims=True)
    rows = [p[i:i + 1, :] for i in range(N_EXPERTS)]
    best = None
    gsel = None
    for g in range(N_GROUPS):
        a, b, c, d = rows[4 * g:4 * g + 4]
        m1, n1 = jnp.maximum(a, b), jnp.minimum(a, b)
        m2, n2 = jnp.maximum(c, d), jnp.minimum(c, d)
        score = jnp.maximum(m1, m2) + jnp.maximum(jnp.minimum(m1, m2), jnp.maximum(n1, n2))
        if g == 0:
            best, gsel = score, jnp.zeros(score.shape, I32)
        else:
            upd = score > best
            gsel = jnp.where(upd, g, gsel)
            best = jnp.where(upd, score, best)
    v = [jnp.where(gsel == 0, rows[j], jnp.where(gsel == 1, rows[4 + j], jnp.where(gsel == 2, rows[8 + j], rows[12 + j])))
         for j in range(EXPERTS_PER_GROUP)]
    b1, i1 = v[0], jnp.zeros(gsel.shape, I32)
    for j in range(1, 4):
        upd = v[j] > b1
        i1 = jnp.where(upd, j, i1)
        b1 = jnp.where(upd, v[j], b1)
    b2, i2 = jnp.full(b1.shape, -1.0, F32), jnp.zeros(gsel.shape, I32)
    for j in range(4):
        upd = jnp.logical_and(i1 != j, v[j] > b2)
        i2 = jnp.where(upd, j, i2)
        b2 = jnp.where(upd, v[j], b2)
    lo = jnp.minimum(i1, i2)
    hi = jnp.maximum(i1, i2)
    pair = jnp.where(lo == 0, hi - 1, jnp.where(lo == 1, 6 - hi, 5))
    return gsel * N_PAIRS + pair


def _post_kernel(mix_ref, x_ref, mod_ref, g2_ref, wo_ref, wr_ref, br_ref, tri_ref,
                 xo_ref, bk_ref, rank_ref, cnt_ref, carry_ref):
    @pl.when(pl.program_id(0) == 0)
    def _():
        carry_ref[...] = jnp.zeros(carry_ref.shape, F32)

    d = x_ref.shape[1]
    tm = x_ref.shape[0]
    y = jnp.dot(mix_ref[...], wo_ref[...], preferred_element_type=F32)
    x = x_ref[...] + mod_ref[0, :, 2 * d:3 * d] * y
    xo_ref[...] = x
    h2 = _rms_mod(x, g2_ref[...], mod_ref[0, :, 4 * d:5 * d], mod_ref[0, :, 3 * d:4 * d])
    hi = h2.astype(BF16)
    lo = (h2 - hi.astype(F32)).astype(BF16)
    logits = (lax.dot_general(wr_ref[...], hi, _NT, preferred_element_type=F32)
              + lax.dot_general(wr_ref[...], lo, _NT, preferred_element_type=F32)
              + br_ref[:, 0:1])
    bucket = _route(logits)
    bk_ref[...] = bucket
    ids = lax.broadcasted_iota(I32, (BUCKET_ROWS, tm), 0)
    onehot = (ids == bucket).astype(F32)
    before = jnp.dot(onehot.astype(BF16), tri_ref[...], preferred_element_type=F32)
    carry = carry_ref[...]
    rank_ref[...] = jnp.sum(onehot * (before + carry[:, 0:1]), axis=0, keepdims=True).astype(I32)
    carry = carry + jnp.sum(onehot, axis=1, keepdims=True)
    carry_ref[...] = carry
    cnt_ref[...] = carry.astype(I32)


def _post(mix, xa, mod_l, g2, wo, wr_bf, br_t, tri, n_act, n_lat, seq, n_batch):
    d = xa.shape[1]
    tm = ROW_TILE
    mrow = _mod_row_map(tm, n_lat, seq, n_batch)
    full = lambda a: pl.BlockSpec(a.shape, lambda i, _nd=a.ndim: (0,) * _nd)
    return pl.pallas_call(
        _post_kernel,
        grid=(n_act // tm,),
        in_specs=[
            pl.BlockSpec((tm, d), lambda i: (i, 0)),
            pl.BlockSpec((tm, d), lambda i: (i, 0)),
            pl.BlockSpec((1, 1, 6 * d), lambda i: (mrow(i), 0, 0)),
            full(g2), full(wo), full(wr_bf), full(br_t), full(tri),
        ],
        out_specs=[
            pl.BlockSpec((tm, d), lambda i: (i, 0)),
            pl.BlockSpec((1, tm), lambda i: (0, i)),
            pl.BlockSpec((1, tm), lambda i: (0, i)),
            pl.BlockSpec((BUCKET_ROWS, 128), lambda i: (0, 0)),
        ],
        out_shape=[
            jax.ShapeDtypeStruct((n_act, d), F32),
            jax.ShapeDtypeStruct((1, n_act), I32),
            jax.ShapeDtypeStruct((1, n_act), I32),
            jax.ShapeDtypeStruct((BUCKET_ROWS, 128), I32),
        ],
        scratch_shapes=[pltpu.VMEM((BUCKET_ROWS, 128), F32)],
        compiler_params=_params(1, VMEM_LIMIT),
        name="post_mixer",
    )(mix, xa, mod_l, g2, wo, wr_bf, br_t, tri)


def _dispatch_kernel(pos_ref, x_ref, mod_ref, g2_ref, hs_in_ref, hs_ref, hbuf_ref, sem):
    del hs_in_ref
    d = x_ref.shape[1]
    tm = x_ref.shape[0]
    h2 = _rms_mod(x_ref[...], g2_ref[...], mod_ref[0, :, 4 * d:5 * d], mod_ref[0, :, 3 * d:4 * d])
    hbuf_ref[...] = h2.reshape(hbuf_ref.shape)

    for r in range(tm):
        pltpu.make_async_copy(hbuf_ref.at[r], hs_ref.at[pos_ref[0, 0, r]], sem).start(priority=r % 2)
    pltpu.make_async_copy(hbuf_ref, hs_ref.at[pl.ds(0, tm)], sem).wait()


def _dispatch(pos3, xa, mod_l, g2, hs, n_lat, seq, n_batch):
    n, d = xa.shape
    tm = ROW_TILE
    mrow = _mod_row_map(tm, n_lat, seq, n_batch)
    return pl.pallas_call(
        _dispatch_kernel,
        grid=(n // tm,),
        in_specs=[
            pl.BlockSpec((1, 1, tm), lambda i: (i, 0, 0), memory_space=pltpu.SMEM),
            pl.BlockSpec((tm, d), lambda i: (i, 0)),
            pl.BlockSpec((1, 1, 6 * d), lambda i: (mrow(i), 0, 0)),
            pl.BlockSpec((1, d), lambda i: (0, 0)),
            pl.BlockSpec(memory_space=pl.ANY),
        ],
        out_specs=pl.BlockSpec(memory_space=pl.ANY),
        out_shape=jax.ShapeDtypeStruct(hs.shape, F32),
        scratch_shapes=[pltpu.VMEM((tm,) + hs.shape[1:], F32), pltpu.SemaphoreType.DMA(())],
        input_output_aliases={4: 0},
        compiler_params=_params(1, VMEM_LIMIT),
        name="dispatch",
    )(pos3, xa, mod_l, g2, hs)


def _moe_kernel(ea_ref, eb_ref, nu_ref, h_ref, wga_ref, wua_ref, wda_ref, wgb_ref, wub_ref, wdb_ref,
                wrt_ref, brt_ref, o_ref):
    t = pl.program_id(0)

    @pl.when(t >= nu_ref[0])
    def _():
        o_ref[...] = jnp.zeros(o_ref.shape, F32)

    @pl.when(t < nu_ref[0])
    def _():
        ea = ea_ref[t]
        eb = eb_ref[t]
        tm = h_ref.shape[0]
        h = h_ref[...].reshape(tm, h_ref.shape[1] * h_ref.shape[2])
        hb = h.astype(BF16)
        wdiff = wrt_ref[pl.ds(ea, 1), :] - wrt_ref[pl.ds(eb, 1), :]
        bdiff = brt_ref[pl.ds(ea, 1), 0:1] - brt_ref[pl.ds(eb, 1), 0:1]
        wa = jax.nn.sigmoid(jnp.sum(h * wdiff, axis=-1, keepdims=True) + bdiff)

        def expert(wg_ref, wu_ref, wd_ref):
            f = wg_ref.shape[3]
            acc = None
            for c in range(f // 512):
                cols = slice(512 * c, 512 * (c + 1))
                g = jnp.dot(hb, wg_ref[0, 0, :, cols], preferred_element_type=F32)
                u = jnp.dot(hb, wu_ref[0, 0, :, cols], preferred_element_type=F32)
                part = jnp.dot((_silu(g) * u).astype(BF16), wd_ref[0, 0, cols, :], preferred_element_type=F32)
                acc = part if acc is None else acc + part
            return acc

        y = wa * expert(wga_ref, wua_ref, wda_ref) + (1.0 - wa) * expert(wgb_ref, wub_ref, wdb_ref)
        o_ref[...] = y.reshape(o_ref.shape)


def _moe(layer, ea, eb, nu, hs, wg, wu, wd, wr_t, br_t):
    p_rows = hs.shape[0]
    row = hs.shape[1:]
    tm = MOE_TILE
    d, f = wg.shape[2:]
    wa_spec = lambda shape: pl.BlockSpec(shape, lambda t, ea_, eb_, nu_: (layer, ea_[t], 0, 0))
    wb_spec = lambda shape: pl.BlockSpec(shape, lambda t, ea_, eb_, nu_: (layer, eb_[t], 0, 0))
    grid_spec = pltpu.PrefetchScalarGridSpec(
        num_scalar_prefetch=3,
        grid=(p_rows // tm,),
        in_specs=[
            pl.BlockSpec((tm,) + row, lambda t, ea_, eb_, nu_: (t, 0, 0)),
            wa_spec((1, 1, d, f)), wa_spec((1, 1, d, f)), wa_spec((1, 1, f, d)),
            wb_spec((1, 1, d, f)), wb_spec((1, 1, d, f)), wb_spec((1, 1, f, d)),
            pl.BlockSpec(wr_t.shape, lambda t, ea_, eb_, nu_: (0, 0)),
            pl.BlockSpec(br_t.shape, lambda t, ea_, eb_, nu_: (0, 0)),
        ],
        out_specs=pl.BlockSpec((tm,) + row, lambda t, ea_, eb_, nu_: (t, 0, 0)),
    )
    return pl.pallas_call(
        _moe_kernel,
        grid_spec=grid_spec,
        out_shape=jax.ShapeDtypeStruct(hs.shape, F32),
        compiler_params=_params(1, VMEM_LIMIT),
        name="experts",
    )(ea, eb, nu, hs, wg, wu, wd, wg, wu, wd, wr_t, br_t)


def _combine_kernel(pos_ref, posn_ref, x_ref, mod_ref, fg_ref, ys_ref, o_ref, gbuf_ref, sem):
    d = x_ref.shape[1]
    rows, issue_next, finish = _gather_rows(pos_ref, posn_ref, ys_ref, gbuf_ref, sem)
    issue_next(0, 1)
    x = x_ref[...] + mod_ref[0, :, 5 * d:6 * d] * rows
    o_ref[...] = x * lax.rsqrt(jnp.mean(x * x, axis=-1, keepdims=True) + EPS) * fg_ref[...]
    finish()


def _combine(pos3, xa, mod_l, fg, ys, n_lat, seq, n_batch):
    n, d = xa.shape
    tm = ROW_TILE
    steps = n // tm
    mrow = _mod_row_map(tm, n_lat, seq, n_batch)
    return pl.pallas_call(
        _combine_kernel,
        grid=(steps,),
        in_specs=[
            pl.BlockSpec((1, 1, tm), lambda i: (i, 0, 0), memory_space=pltpu.SMEM),
            pl.BlockSpec((1, 1, tm), lambda i: (jnp.minimum(i + 1, steps - 1), 0, 0), memory_space=pltpu.SMEM),
            pl.BlockSpec((tm, d), lambda i: (i, 0)),
            pl.BlockSpec((1, 1, 6 * d), lambda i: (mrow(i), 0, 0)),
            pl.BlockSpec((1, d), lambda i: (0, 0)),
            pl.BlockSpec(memory_space=pl.ANY),
        ],
        out_specs=pl.BlockSpec((tm, d), lambda i: (i, 0)),
        out_shape=jax.ShapeDtypeStruct((n, d), F32),
        scratch_shapes=[pltpu.VMEM((2, tm) + ys.shape[1:], F32), pltpu.SemaphoreType.DMA((2,))],
        compiler_params=_params(1, VMEM_LIMIT),
        name="combine",
    )(pos3, pos3, xa, mod_l, fg, ys)


def _bucket_plan(bucket, rank, counts, n_tiles, tile):
    cnt = counts[:N_BUCKETS, 0]
    padded = ((cnt + tile - 1) // tile) * tile
    ends = jnp.cumsum(padded)
    offs = ends - padded
    pos = offs[bucket[0]] + rank[0]
    n_used = ends[-1] // tile
    t_eff = jnp.minimum(jnp.arange(n_tiles, dtype=I32), jnp.maximum(n_used - 1, 0))
    tb = jnp.minimum(jnp.sum((ends[None, :] <= (t_eff * tile)[:, None]).astype(I32), axis=1), N_BUCKETS - 1)
    pairs = np.array(PAIR_SLOTS, dtype=np.int32)
    ea = (tb // N_PAIRS) * EXPERTS_PER_GROUP + jnp.asarray(pairs[:, 0])[tb % N_PAIRS]
    eb = (tb // N_PAIRS) * EXPERTS_PER_GROUP + jnp.asarray(pairs[:, 1])[tb % N_PAIRS]
    return pos.astype(I32), ea.astype(I32), eb.astype(I32), n_used.astype(I32).reshape(1)


def kernel(x, c, ctx, c_ctx, w_mod, b_mod, norm_g, w_in_even, w_out_even, ret_log_decay, ret_gn_g, sg_ln_g, sg_ln_b, sg_w, sg_b, w_qkv_odd, w_o_odd, attn_sink, w_router, b_router, w_e_gate, w_e_up, w_e_down, final_g):
    n_batch, seq, d = x.shape
    ctx_len = ctx.shape[1]
    depth = w_mod.shape[0]
    n_lat = n_batch * seq
    n_all = n_lat + n_batch * ctx_len
    tm = ROW_TILE
    assert seq % tm == 0 and (n_batch * ctx_len) % tm == 0

    mod_rows = -(-(n_batch + 1) // 16) * 16
    c_rows = jnp.zeros((mod_rows, d), F32).at[:n_batch].set(c).at[n_batch].set(c_ctx)
    mod = _modulation(c_rows, w_mod, b_mod).reshape(depth, mod_rows, 1, 6 * d)

    wr_t = w_router.T.astype(F32)
    wr_bf = wr_t.astype(BF16)
    br_t = jnp.broadcast_to(b_router.astype(F32)[:, None], (N_EXPERTS, 128))
    tri = jnp.triu(jnp.ones((tm, tm), BF16), k=1)
    cos_t, sin_t = _rope_tables(seq, tm)
    bias = _attn_bias()

    wg_all, wu_all, wd_all = w_e_gate.astype(BF16), w_e_up.astype(BF16), w_e_down.astype(BF16)

    n_tiles = n_all // MOE_TILE + N_BUCKETS
    p_rows = n_tiles * MOE_TILE
    hs = jnp.zeros((p_rows, ROW_SUB, d // ROW_SUB), F32)

    source = (x.reshape(n_lat, d), ctx.reshape(n_batch * ctx_len, d))
    for l in range(depth):
        i = l // 2
        last = l == depth - 1
        mod_l = mod[l]
        g1 = norm_g[l, 0].reshape(1, d)
        g2 = norm_g[l, 1].reshape(1, d)
        if l % 2 == 0:
            xa, p = _in_stage(source, mod_l, g1, w_in_even[i].astype(BF16), None, n_lat, seq, n_batch)
            tabs = _retention_tables(ret_log_decay[i], RET_BLOCK)
            sg_bias = jnp.repeat(sg_b[i].astype(F32).T, SG_CHUNK, axis=1)
            mix = _even_mix(p, tabs, ret_gn_g[i].reshape(1, -1), sg_ln_g[i].reshape(1, -1), sg_ln_b[i].reshape(1, -1),
                            sg_w[i].astype(BF16), sg_bias, n_batch, seq, ctx_len)
            wo = w_out_even[i].astype(BF16)
        else:
            xa, q, kv = _in_stage(source, mod_l, g1, w_qkv_odd[i].astype(BF16), (cos_t, sin_t), n_lat, seq, n_batch)
            sink_t = jnp.broadcast_to(attn_sink[i].astype(F32)[:, None], (ATT_HEADS, 128))
            mix = _attention(q, kv, bias, sink_t, n_batch, seq, ctx_len, need_ctx=not last)
            wo = w_o_odd[i].astype(BF16)
        n_act = n_lat if last else n_all
        xm, bucket, rank, counts = _post(mix, xa, mod_l, g2, wo, wr_bf, br_t, tri, n_act, n_lat, seq, n_batch)
        pos, ea, eb, n_used = _bucket_plan(bucket, rank, counts, n_tiles, MOE_TILE)
        pos3 = pos.reshape(n_act // tm, 1, tm)
        hs = _dispatch(pos3, xm, mod_l, g2, hs, n_lat, seq, n_batch)
        ys = _moe(l, ea, eb, n_used, hs, wg_all, wu_all, wd_all, wr_t, br_t)
        source = (pos3, xm, mod_l, ys)
    out = _combine(pos3, xm, mod_l, final_g.reshape(1, d), ys, n_lat, seq, n_batch)
    return out.reshape(n_batch, seq, d)
```

```python
import functools

import jax
import jax.numpy as jnp
import numpy as np
from jax import lax
from jax.experimental import pallas as pl
from jax.experimental.pallas import tpu as pltpu

F32 = jnp.float32
BF16 = jnp.bfloat16
I32 = jnp.int32

EPS = 1e-6
NEG = -1e30
GRID_W = 64
ROPE_BASE = 10000.0
RET_HEADS = 4
RET_QK_DIM = 64
RET_V_DIM = 128
SG_GROUPS = 4
SG_CHUNK = 128
RET_BLOCK = 256
ATT_HEADS = 8
ATT_KV_HEADS = 2
ATT_GROUP = ATT_HEADS // ATT_KV_HEADS
HEAD_DIM = 128
ATT_BLOCK = 128
WINDOW = 128
ATT_SUB = 2
ATT_SLAB = 32
N_EXPERTS = 16
N_GROUPS = 4
EXPERTS_PER_GROUP = 4
N_PAIRS = 6
N_BUCKETS = N_GROUPS * N_PAIRS
PAIR_SLOTS = ((0, 1), (0, 2), (0, 3), (1, 3), (1, 2), (3, 2))
BUCKET_ROWS = 32

ROW_TILE = 512
MOE_TILE = 256
ROW_SUB = 8
VMEM_LIMIT = 56 * 1024 * 1024

_NT = (((1,), (1,)), ((), ()))
_TN = (((0,), (0,)), ((), ()))


def _params(n_axes, vmem=None):
    return pltpu.CompilerParams(dimension_semantics=("arbitrary",) * n_axes, vmem_limit_bytes=vmem)


def _silu(v):
    return v * jax.nn.sigmoid(v)


def _rms_mod(x, g, scale, shift):
    y = x * lax.rsqrt(jnp.mean(x * x, axis=-1, keepdims=True) + EPS) * g
    return y * (1.0 + scale) + shift


def _mod_kernel(c_ref, w_ref, b_ref, o_ref):
    a = _silu(c_ref[...]).astype(BF16)
    o_ref[0] = jnp.dot(a, w_ref[0].astype(BF16), preferred_element_type=F32) + b_ref[0]


def _modulation(c_rows, w_mod, b_mod):
    depth, d, six_d = w_mod.shape
    mr = c_rows.shape[0]
    tn = 1536
    return pl.pallas_call(
        _mod_kernel,
        grid=(depth, six_d // tn),
        in_specs=[
            pl.BlockSpec((mr, d), lambda l, j: (0, 0)),
            pl.BlockSpec((1, d, tn), lambda l, j: (l, 0, j)),
            pl.BlockSpec((1, 1, tn), lambda l, j: (l, 0, j)),
        ],
        out_specs=pl.BlockSpec((1, mr, tn), lambda l, j: (l, 0, j)),
        out_shape=jax.ShapeDtypeStruct((depth, mr, six_d), F32),
        compiler_params=_params(2, VMEM_LIMIT),
        name="modulation",
    )(c_rows, w_mod, b_mod.reshape(depth, 1, six_d))


def _mod_row_map(tm, n_lat, seq, n_batch):
    def index(i):
        row0 = i * tm
        return jnp.where(row0 < n_lat, row0 // seq, n_batch)
    return index


def _gather_rows(pos_ref, posn_ref, ys_ref, gbuf_ref, sem):
    i = pl.program_id(0)
    slot = i % 2
    tm = gbuf_ref.shape[1]

    def issue(p_ref, s, lo, hi):
        for r in range(lo, hi):
            pltpu.make_async_copy(ys_ref.at[p_ref[0, 0, r]], gbuf_ref.at[s, r], sem.at[s]).start(priority=r % 2)

    def wait(s):
        pltpu.make_async_copy(ys_ref.at[pl.ds(0, tm)], gbuf_ref.at[s], sem.at[s]).wait()

    @pl.when(i == 0)
    def _():
        issue(pos_ref, 0, 0, tm)

    wait(slot)
    rows = gbuf_ref[slot].reshape(tm, gbuf_ref.shape[2] * gbuf_ref.shape[3])

    def issue_next(k, parts):
        issue(posn_ref, 1 - slot, tm * k // parts, tm * (k + 1) // parts)

    def finish():
        @pl.when(i == pl.num_programs(0) - 1)
        def _():
            wait(1 - slot)

    return rows, issue_next, finish


def _project_even(h, w_ref, o_ref, between):
    for c in range(5):
        between(c, 5)
        p = jnp.dot(h, w_ref[:, 512 * c:512 * (c + 1)], preferred_element_type=F32)
        if c == 0:
            col = lax.broadcasted_iota(I32, p.shape, 1)
            p = jnp.where(col >= 256, p * (RET_QK_DIM ** -0.5), p)
        elif c >= 3:
            p = jax.nn.gelu(p)
        o_ref[:, 512 * c:512 * (c + 1)] = p.astype(BF16)


def _project_odd(h, w_ref, cos_ref, sin_ref, q_ref, kv_ref, between):
    tm = h.shape[0]
    cos = cos_ref[...]
    sin = sin_ref[...]
    lane = lax.broadcasted_iota(I32, (tm, HEAD_DIM), 1)
    first = (lane % 64) < 32

    def rope(p):
        partner = jnp.where(first, pltpu.roll(p, 96, 1), pltpu.roll(p, 32, 1))
        return p * cos + partner * sin

    scale = HEAD_DIM ** -0.5
    for c in range(3):
        between(c, 3)
        p = jnp.dot(h, w_ref[:, 512 * c:512 * (c + 1)], preferred_element_type=F32)
        for j in range(4):
            pj = p[:, 128 * j:128 * (j + 1)]
            if c < 2:
                q_ref[:, 512 * c + 128 * j:512 * c + 128 * (j + 1)] = (rope(pj) * scale).astype(BF16)
            elif j < 2:
                kv_ref[:, 128 * j:128 * (j + 1)] = rope(pj).astype(BF16)
            else:
                kv_ref[:, 128 * j:128 * (j + 1)] = pj.astype(BF16)


def _in_kernel(*refs, first, odd, lat_tiles):
    if first:
        (xl_ref, xc_ref), refs = refs[:2], refs[2:]
        x = jnp.where(pl.program_id(0) < lat_tiles, xl_ref[...], xc_ref[...])
        between, finish = (lambda k, parts: None), (lambda: None)
    else:
        (pos_ref, posn_ref, x_ref, modp_ref, ys_ref), refs = refs[:5], refs[5:]
        gbuf_ref, sem = refs[-2:]
        refs = refs[:-2]
        d = x_ref.shape[1]
        rows, between, finish = _gather_rows(pos_ref, posn_ref, ys_ref, gbuf_ref, sem)
        x = x_ref[...] + modp_ref[0, :, 5 * d:6 * d] * rows
    mod_ref, g_ref, w_ref = refs[:3]
    d = x.shape[1]
    h = _rms_mod(x, g_ref[...], mod_ref[0, :, d:2 * d], mod_ref[0, :, 0:d]).astype(BF16)
    if odd:
        cos_ref, sin_ref, xo_ref, q_ref, kv_ref = refs[3:]
        xo_ref[...] = x
        _project_odd(h, w_ref, cos_ref, sin_ref, q_ref, kv_ref, between)
    else:
        xo_ref, o_ref = refs[3:]
        xo_ref[...] = x
        _project_even(h, w_ref, o_ref, between)
    finish()


def _in_stage(source, mod_l, g, w, rope, n_lat, seq, n_batch):
    first = len(source) == 2
    odd = rope is not None
    tm = ROW_TILE
    if first:
        xl, xc = source
        d = xl.shape[1]
        n = xl.shape[0] + xc.shape[0]
        lat_tiles = xl.shape[0] // tm
        src_specs = [pl.BlockSpec((tm, d), lambda i: (jnp.minimum(i, lat_tiles - 1), 0)),
                     pl.BlockSpec((tm, d), lambda i: (jnp.maximum(i - lat_tiles, 0), 0))]
        scratch = []
    else:
        pos3, xm, mod_prev, ys = source
        n, d = xm.shape
        lat_tiles = n_lat // tm
        steps = n // tm
        mrow_p = _mod_row_map(tm, n_lat, seq, n_batch)
        src_specs = [pl.BlockSpec((1, 1, tm), lambda i: (i, 0, 0), memory_space=pltpu.SMEM),
                     pl.BlockSpec((1, 1, tm), lambda i: (jnp.minimum(i + 1, steps - 1), 0, 0), memory_space=pltpu.SMEM),
                     pl.BlockSpec((tm, d), lambda i: (i, 0)),
                     pl.BlockSpec((1, 1, 6 * d), lambda i: (mrow_p(i), 0, 0)),
                     pl.BlockSpec(memory_space=pl.ANY)]
        source = (pos3, pos3, xm, mod_prev, ys)
        scratch = [pltpu.VMEM((2, tm) + ys.shape[1:], F32), pltpu.SemaphoreType.DMA((2,))]
    mrow = _mod_row_map(tm, n_lat, seq, n_batch)
    specs = src_specs + [pl.BlockSpec((1, 1, 6 * d), lambda i: (mrow(i), 0, 0)),
                         pl.BlockSpec((1, d), lambda i: (0, 0)),
                         pl.BlockSpec(w.shape, lambda i: (0, 0))]
    args = tuple(source) + (mod_l, g, w)
    out_specs = [pl.BlockSpec((tm, d), lambda i: (i, 0))]
    out_shape = [jax.ShapeDtypeStruct((n, d), F32)]
    if odd:
        tiles_per_seq = seq // tm

        def rope_row(i):
            return jnp.where(i < lat_tiles, i % tiles_per_seq, tiles_per_seq)

        specs += [pl.BlockSpec((tm, HEAD_DIM), lambda i: (rope_row(i), 0))] * 2
        args += tuple(rope)
        widths = (ATT_HEADS * HEAD_DIM, 2 * ATT_KV_HEADS * HEAD_DIM)
    else:
        widths = (w.shape[1],)
    out_specs += [pl.BlockSpec((tm, wd), lambda i: (i, 0)) for wd in widths]
    out_shape += [jax.ShapeDtypeStruct((n, wd), BF16) for wd in widths]
    return pl.pallas_call(
        functools.partial(_in_kernel, first=first, odd=odd, lat_tiles=lat_tiles),
        grid=(n // tm,),
        in_specs=specs,
        out_specs=out_specs,
        out_shape=out_shape,
        scratch_shapes=scratch,
        compiler_params=_params(1, VMEM_LIMIT),
        name="in_odd" if odd else "in_even",
    )(*args)


def _rope_tables(seq, tm):
    t = jnp.arange(seq)
    row = (t // GRID_W).astype(F32)
    col = (t % GRID_W).astype(F32)
    quarter = HEAD_DIM // 4
    inv_freq = ROPE_BASE ** (-jnp.arange(quarter, dtype=F32) / quarter)
    ar = row[:, None] * inv_freq[None, :]
    ac = col[:, None] * inv_freq[None, :]
    cos = jnp.concatenate([jnp.cos(ar), jnp.cos(ar), jnp.cos(ac), jnp.cos(ac)], axis=1)
    sin = jnp.concatenate([-jnp.sin(ar), jnp.sin(ar), -jnp.sin(ac), jnp.sin(ac)], axis=1)
    cos = jnp.concatenate([cos, jnp.ones((tm, HEAD_DIM), F32)], axis=0)
    sin = jnp.concatenate([sin, jnp.zeros((tm, HEAD_DIM), F32)], axis=0)
    return cos, sin


def _ret_increments(qk_ref, v_ref, rows, zf_ref, zb_ref):
    upper_rows = lax.broadcasted_iota(I32, (128, 128), 0) < RET_QK_DIM
    out = []
    for grp in range(2):
        k2 = qk_ref[rows, 256 + 128 * grp:256 + 128 * (grp + 1)].astype(F32)
        v2 = v_ref[rows, 256 * grp:256 * (grp + 1)]
        halves = []
        for z_ref in (zf_ref, zb_ref):
            kz = (k2 * z_ref[grp]).astype(BF16)
            full = lax.dot_general(kz, v2, _TN, preferred_element_type=F32)
            halves.append(jnp.where(upper_rows, full[:, 0:128], full[:, 128:256]))
        out.append(jnp.concatenate(halves, axis=1))
    return out


def _mix_outputs(qk_ref, v_ref, gate_ref, u_ref, s_ref, rows, states, dec_ref, xi_ref, gn_ref, lng_ref, lnb_ref,
                 ws_ref, sgb_ref, o_ref):
    lane = lax.broadcasted_iota(I32, (1, 128), 1)
    head_mask = [(lane // RET_QK_DIM == hh).astype(BF16) for hh in range(2)]
    for grp in range(2):
        q2 = qk_ref[rows, 128 * grp:128 * (grp + 1)]
        k2 = qk_ref[rows, 256 + 128 * grp:256 + 128 * (grp + 1)]
        for hh in range(2):
            hd = 2 * grp + hh
            cols = slice(128 * hd, 128 * (hd + 1))
            qm = q2 * head_mask[hh]
            sc = lax.dot_general(qm, k2, _NT, preferred_element_type=F32) * dec_ref[hd]
            o = jnp.dot(sc.astype(BF16), v_ref[rows, cols], preferred_element_type=F32)
            if states is not None:
                cross = jnp.dot(qm, states[grp], preferred_element_type=F32) * xi_ref[hd]
                o = o + cross[:, 0:128] + cross[:, 128:256]
            mu = jnp.mean(o, axis=-1, keepdims=True)
            oc = o - mu
            var = jnp.mean(oc * oc, axis=-1, keepdims=True)
            y = oc * lax.rsqrt(var + EPS) * gn_ref[:, cols]
            y = y * _silu(gate_ref[rows, cols].astype(F32))
            o_ref[rows, cols] = y.astype(BF16)
    for sub in range((rows.stop - rows.start) // SG_CHUNK):
        srows = slice(rows.start + SG_CHUNK * sub, rows.start + SG_CHUNK * (sub + 1))
        s = s_ref[srows, :].astype(F32)
        mu = jnp.mean(s, axis=-1, keepdims=True)
        sc_ = s - mu
        var = jnp.mean(sc_ * sc_, axis=-1, keepdims=True)
        sn = (sc_ * lax.rsqrt(var + EPS) * lng_ref[...] + lnb_ref[...]).astype(BF16)
        for g in range(SG_GROUPS):
            cols = slice(128 * g, 128 * (g + 1))
            mixed = jnp.dot(ws_ref[g], sn[:, cols], preferred_element_type=F32) + sgb_ref[:, cols]
            o_ref[srows, 512 + 128 * g:512 + 128 * (g + 1)] = (u_ref[srows, cols].astype(F32) * mixed).astype(BF16)


def _even_lat_kernel(bqk_ref, bv_ref, cqk_ref, cv_ref, qk_ref, v_ref, gate_ref, u_ref, s_ref, dec_ref, xi_ref,
                     zf_ref, zb_ref, gdec_ref, gn_ref, lng_ref, lnb_ref, ws_ref, sgb_ref, o_ref, ds_ref, sp_ref,
                     *, inc_steps):
    j = pl.program_id(1)
    cb = RET_BLOCK
    n_slots = ds_ref.shape[0]
    inc_chunks = bqk_ref.shape[0] // cb
    out_chunks = qk_ref.shape[0] // cb

    @pl.when(j == 0)
    def _():
        inc = _ret_increments(cqk_ref, cv_ref, slice(0, cb), zf_ref, zb_ref)
        for grp in range(2):
            ds_ref[0, grp] = inc[grp]

    @pl.when(j < inc_steps)
    def _():
        for c in range(inc_chunks):
            inc = _ret_increments(bqk_ref, bv_ref, slice(cb * c, cb * (c + 1)), zf_ref, zb_ref)
            for grp in range(2):
                ds_ref[1 + inc_chunks * j + c, grp] = inc[grp]

    @pl.when(j == inc_steps)
    def _scan():
        fwd = ds_ref[0, :, :, 0:128]
        for n in range(1, n_slots):
            sp_ref[n, :, :, 0:128] = fwd
            fwd = fwd * gdec_ref[:, :, 0:128] + ds_ref[n, :, :, 0:128]
        bwd = ds_ref[0, :, :, 128:256]
        for n in range(n_slots - 1, 0, -1):
            sp_ref[n, :, :, 128:256] = bwd
            bwd = bwd * gdec_ref[:, :, 128:256] + ds_ref[n, :, :, 128:256]

    @pl.when(j >= inc_steps)
    def _():
        for c in range(out_chunks):
            n = 1 + out_chunks * (j - inc_steps) + c
            states = [sp_ref[n, grp].astype(BF16) for grp in range(2)]
            _mix_outputs(qk_ref, v_ref, gate_ref, u_ref, s_ref, slice(cb * c, cb * (c + 1)), states, dec_ref, xi_ref,
                         gn_ref, lng_ref, lnb_ref, ws_ref, sgb_ref, o_ref)


def _even_ctx_kernel(qk_ref, v_ref, gate_ref, u_ref, s_ref, dec_ref, xi_ref, gn_ref, lng_ref, lnb_ref, ws_ref, sgb_ref,
                     o_ref):
    _mix_outputs(qk_ref, v_ref, gate_ref, u_ref, s_ref, slice(0, RET_BLOCK), None, dec_ref, xi_ref,
                 gn_ref, lng_ref, lnb_ref, ws_ref, sgb_ref, o_ref)


def _retention_tables(log_decay, cb):
    lg = -jnp.exp(log_decay.astype(F32))
    pos = jnp.arange(cb, dtype=F32)
    diff = pos[:, None] - pos[None, :]
    lower = jnp.where(diff >= 0, jnp.exp(lg[0][:, None, None] * jnp.maximum(diff, 0.0)[None]), 0.0)
    upper = jnp.where(diff <= 0, jnp.exp(lg[1][:, None, None] * jnp.maximum(-diff, 0.0)[None]), 0.0)
    dec = lower + upper
    xi_f = jnp.exp(lg[0][:, None] * (pos[None, :] + 1.0))
    xi_b = jnp.exp(lg[1][:, None] * (cb - pos[None, :]))
    xi = jnp.concatenate([jnp.broadcast_to(xi_f[:, :, None], (RET_HEADS, cb, 128)),
                          jnp.broadcast_to(xi_b[:, :, None], (RET_HEADS, cb, 128))], axis=2)
    zeta_f = jnp.exp(lg[0][:, None] * (cb - 1.0 - pos[None, :]))
    zeta_b = jnp.exp(lg[1][:, None] * pos[None, :])

    def lanes(z):
        return jnp.repeat(z.reshape(2, 2, cb), RET_QK_DIM, axis=1).transpose(0, 2, 1)

    gstep = jnp.exp(lg * cb)

    def rows(gv):
        return jnp.broadcast_to(jnp.repeat(gv.reshape(2, 2), RET_QK_DIM, axis=1)[:, :, None], (2, 128, 128))

    gdec = jnp.concatenate([rows(gstep[0]), rows(gstep[1])], axis=2)
    return dec, xi, lanes(zeta_f), lanes(zeta_b), gdec


def _even_mix(p, tabs, gn_g, ln_g, ln_b, w_s, sg_bias, n_batch, seq, ctx_len):
    cb = RET_BLOCK
    big, blk = 4 * cb, 2 * cb
    assert ctx_len == cb and seq % big == 0
    n_lat = n_batch * seq
    inc_steps, out_steps = seq // big, seq // blk
    dec, xi, zf, zb, gdec = tabs
    consts = (dec, xi, zf, zb, gdec, gn_g, ln_g, ln_b, w_s, sg_bias)
    full2 = lambda a: pl.BlockSpec(a.shape, lambda b, j, _nd=a.ndim: (0,) * _nd)
    inc_row = lambda b, j: b * inc_steps + jnp.minimum(j, inc_steps - 1)
    out_row = lambda b, j: b * out_steps + jnp.maximum(j - inc_steps, 0)
    ctx_row = lambda b, j: n_lat // cb + b
    lat = pl.pallas_call(
        functools.partial(_even_lat_kernel, inc_steps=inc_steps),
        grid=(n_batch, inc_steps + out_steps),
        in_specs=[pl.BlockSpec((big, 512), lambda b, j: (inc_row(b, j), 0)),
                  pl.BlockSpec((big, 512), lambda b, j: (inc_row(b, j), 1)),
                  pl.BlockSpec((cb, 512), lambda b, j: (ctx_row(b, j), 0)),
                  pl.BlockSpec((cb, 512), lambda b, j: (ctx_row(b, j), 1))]
                 + [pl.BlockSpec((blk, 512), lambda b, j, _c=c: (out_row(b, j), _c)) for c in range(5)]
                 + [full2(a) for a in consts],
        out_specs=pl.BlockSpec((blk, 1024), lambda b, j: (out_row(b, j), 0)),
        out_shape=jax.ShapeDtypeStruct((n_lat, 1024), BF16),
        scratch_shapes=[pltpu.VMEM((seq // cb + 1, 2, 128, 256), F32), pltpu.VMEM((seq // cb + 1, 2, 128, 256), F32)],
        compiler_params=_params(2, VMEM_LIMIT),
        name="even_mix",
    )(p, p, p, p, p, p, p, p, p, *consts)
    ctx_consts = (dec, xi, gn_g, ln_g, ln_b, w_s, sg_bias)
    full1 = lambda a: pl.BlockSpec(a.shape, lambda b, _nd=a.ndim: (0,) * _nd)
    ctx = pl.pallas_call(
        _even_ctx_kernel,
        grid=(n_batch,),
        in_specs=[pl.BlockSpec((cb, 512), lambda b, _c=c: (n_lat // cb + b, _c)) for c in range(5)]
                 + [full1(a) for a in ctx_consts],
        out_specs=pl.BlockSpec((cb, 1024), lambda b: (b, 0)),
        out_shape=jax.ShapeDtypeStruct((n_batch * ctx_len, 1024), BF16),
        compiler_params=_params(1, VMEM_LIMIT),
        name="even_mix_ctx",
    )(p, p, p, p, p, *ctx_consts)
    return lat, ctx


def _attn_kernel(q_ref, kvp_ref, kvm_ref, kvn_ref, kvx_ref, bias0_ref, bias1_ref, sink_ref, o_ref, s_scr, p_scr, r_scr):
    blk = ATT_BLOCK
    nloc = 3 * blk
    rows_all = ATT_GROUP * blk
    bias_refs = (bias0_ref, bias1_ref)
    chains = [(kvh, sub) for kvh in range(ATT_KV_HEADS) for sub in range(ATT_SUB)]
    v_all, sinks = {}, {}
    for ci, (kvh, sub) in enumerate(chains):
        kcols = slice(HEAD_DIM * kvh, HEAD_DIM * (kvh + 1))
        vcols = slice(HEAD_DIM * (ATT_KV_HEADS + kvh), HEAD_DIM * (ATT_KV_HEADS + kvh + 1))
        k_parts = [kvp_ref[:, kcols], kvm_ref[0:blk, kcols], kvm_ref[blk:2 * blk, kcols], kvn_ref[:, kcols]]
        v_parts = [kvp_ref[:, vcols], kvm_ref[0:blk, vcols], kvm_ref[blk:2 * blk, vcols], kvn_ref[:, vcols]]
        heads = [ATT_GROUP * kvh + j for j in range(ATT_GROUP)]
        rows = slice(blk * sub, blk * (sub + 1))
        k_all = jnp.concatenate(k_parts[sub:sub + 3] + [kvx_ref[:, kcols]], axis=0)
        v_all[ci] = jnp.concatenate(v_parts[sub:sub + 3] + [kvx_ref[:, vcols]], axis=0)
        q4 = jnp.concatenate([q_ref[rows, HEAD_DIM * hd:HEAD_DIM * (hd + 1)] for hd in heads], axis=0)
        s_scr[ci] = lax.dot_general(q4, k_all, _NT, preferred_element_type=F32)
        sinks[ci] = jnp.concatenate(
            [jnp.broadcast_to(sink_ref[hd:hd + 1, 0:1], (blk, 1)) for hd in heads], axis=0)
    nchunk_loc = nloc // 128
    nchunk = s_scr.shape[2] // 128

    def shifted(ci, sub, rs, k):
        t = s_scr[ci, rs, 128 * k:128 * (k + 1)]
        return t + bias_refs[sub][0, rs, 128 * k:128 * (k + 1)] if k < nchunk_loc else t

    mrow = {}
    for ci, (kvh, sub) in enumerate(chains):
        for r0 in range(0, rows_all, ATT_SLAB):
            rs = slice(r0, r0 + ATT_SLAB)
            part = shifted(ci, sub, rs, 0)
            for k in range(1, nchunk):
                part = jnp.maximum(part, shifted(ci, sub, rs, k))
            r_scr[ci, rs, :] = part
    for ci in range(len(chains)):
        m = jnp.maximum(jnp.max(r_scr[ci], axis=-1, keepdims=True), sinks[ci])
        mrow[ci] = m
        r_scr[ci] = jnp.broadcast_to(m, r_scr.shape[1:])
    for ci, (kvh, sub) in enumerate(chains):
        for r0 in range(0, rows_all, ATT_SLAB):
            rs = slice(r0, r0 + ATT_SLAB)
            mb = r_scr[ci, rs, :]
            part = None
            for k in range(nchunk):
                p = jnp.exp(shifted(ci, sub, rs, k) - mb)
                part = p if part is None else part + p
                p_scr[ci, rs, 128 * k:128 * (k + 1)] = p.astype(BF16)
            r_scr[ci, rs, :] = part
    inv = {}
    for ci in range(len(chains)):
        inv[ci] = 1.0 / (jnp.sum(r_scr[ci], axis=-1, keepdims=True) + jnp.exp(sinks[ci] - mrow[ci]))
    for ci, (kvh, sub) in enumerate(chains):
        rows = slice(blk * sub, blk * (sub + 1))
        o = (jnp.dot(p_scr[ci, :, 0:nloc], v_all[ci][0:nloc], preferred_element_type=F32)
             + jnp.dot(p_scr[ci, :, nloc:], v_all[ci][nloc:], preferred_element_type=F32)) * inv[ci]
        for j in range(ATT_GROUP):
            hd = ATT_GROUP * kvh + j
            o_ref[rows, HEAD_DIM * hd:HEAD_DIM * (hd + 1)] = o[blk * j:blk * (j + 1)].astype(BF16)


def _attn_bias():
    i = np.arange(ATT_BLOCK)[:, None]
    j = np.arange(3 * ATT_BLOCK)[None, :] - ATT_BLOCK
    band = np.abs(i - j) <= WINDOW
    cases = [band & (j >= 0), band, band & (j < ATT_BLOCK), np.zeros_like(band)]
    bias = np.stack([np.where(c, 0.0, NEG) for c in cases]).astype(np.float32)
    return jnp.asarray(np.tile(bias, (1, ATT_GROUP, 1)))


def _attention(q, kv, bias, sink_t, n_batch, seq, ctx_len, need_ctx):
    blk = ATT_BLOCK
    big = ATT_SUB * blk
    assert ATT_SUB == 2 and seq % big == 0 and ctx_len == big
    nb = seq // blk
    ns = seq // big
    n_lat = n_batch * seq
    steps = ns + (1 if need_ctx else 0)
    n_out = n_lat + (n_batch * ctx_len if need_ctx else 0)

    def qrow(b, s):
        return jnp.where(s < ns, b * ns + s, n_lat // big + b)

    def small(off):
        def index(b, s):
            loc = jnp.clip(2 * jnp.minimum(s, ns - 1) + off, 0, nb - 1)
            return (b * nb + loc, 0)
        return index

    def case0(b, s):
        return (jnp.where(s < ns, jnp.where(s == 0, 0, 1), 3), 0, 0)

    def case1(b, s):
        return (jnp.where(s < ns, jnp.where(s == ns - 1, 2, 1), 3), 0, 0)

    return pl.pallas_call(
        _attn_kernel,
        grid=(n_batch, steps),
        in_specs=[
            pl.BlockSpec((big, q.shape[1]), lambda b, s: (qrow(b, s), 0)),
            pl.BlockSpec((blk, kv.shape[1]), small(-1)),
            pl.BlockSpec((big, kv.shape[1]), lambda b, s: (b * ns + jnp.minimum(s, ns - 1), 0)),
            pl.BlockSpec((blk, kv.shape[1]), small(2)),
            pl.BlockSpec((ctx_len, kv.shape[1]), lambda b, s: (n_lat // ctx_len + b, 0)),
            pl.BlockSpec((1,) + bias.shape[1:], case0),
            pl.BlockSpec((1,) + bias.shape[1:], case1),
            pl.BlockSpec(sink_t.shape, lambda b, s: (0, 0)),
        ],
        out_specs=pl.BlockSpec((big, q.shape[1]), lambda b, s: (qrow(b, s), 0)),
        out_shape=jax.ShapeDtypeStruct((n_out, q.shape[1]), BF16),
        scratch_shapes=[pltpu.VMEM((ATT_KV_HEADS * ATT_SUB, ATT_GROUP * blk, 3 * blk + ctx_len), F32),
                        pltpu.VMEM((ATT_KV_HEADS * ATT_SUB, ATT_GROUP * blk, 3 * blk + ctx_len), BF16),
                        pltpu.VMEM((ATT_KV_HEADS * ATT_SUB, ATT_GROUP * blk, 128), F32)],
        compiler_params=_params(2, VMEM_LIMIT),
        name="window_attention",
    )(q, kv, kv, kv, kv, bias, bias, sink_t)


def _route(logits):
    m = jnp.max(logits, axis=0, keepdims=True)
    e = jnp.exp(logits - m)
    p = e / jnp.sum(e, axis=0, keepdims=True)
    rows = [p[i:i + 1, :] for i in range(N_EXPERTS)]
    best = None
    gsel = None
    for g in range(N_GROUPS):
        a, b, c, d = rows[4 * g:4 * g + 4]
        m1, n1 = jnp.maximum(a, b), jnp.minimum(a, b)
        m2, n2 = jnp.maximum(c, d), jnp.minimum(c, d)
        score = jnp.maximum(m1, m2) + jnp.maximum(jnp.minimum(m1, m2), jnp.maximum(n1, n2))
        if g == 0:
            best, gsel = score, jnp.zeros(score.shape, I32)
        else:
            upd = score > best
            gsel = jnp.where(upd, g, gsel)
            best = jnp.where(upd, score, best)
    v = [jnp.where(gsel == 0, rows[j], jnp.where(gsel == 1, rows[4 + j], jnp.where(gsel == 2, rows[8 + j], rows[12 + j])))
         for j in range(EXPERTS_PER_GROUP)]
    b1, i1 = v[0], jnp.zeros(gsel.shape, I32)
    for j in range(1, 4):
        upd = v[j] > b1
        i1 = jnp.where(upd, j, i1)
        b1 = jnp.where(upd, v[j], b1)
    b2, i2 = jnp.full(b1.shape, -1.0, F32), jnp.zeros(gsel.shape, I32)
    for j in range(4):
        upd = jnp.logical_and(i1 != j, v[j] > b2)
        i2 = jnp.where(upd, j, i2)
        b2 = jnp.where(upd, v[j], b2)
    lo = jnp.minimum(i1, i2)
    hi = jnp.maximum(i1, i2)
    pair = jnp.where(lo == 0, hi - 1, jnp.where(lo == 1, 6 - hi, 5))
    return gsel * N_PAIRS + pair


def _post_kernel(*refs, lat_tiles):
    if lat_tiles is None:
        mix_ref, refs = refs[0], refs[1:]
        mix = mix_ref[...]
    else:
        (ml_ref, mc_ref), refs = refs[:2], refs[2:]
        mix = jnp.where(pl.program_id(0) < lat_tiles, ml_ref[...], mc_ref[...])
    x_ref, mod_ref, g2_ref, wo_ref, wr_ref, br_ref, tri_ref, xo_ref, bk_ref, rank_ref, cnt_ref, carry_ref = refs

    @pl.when(pl.program_id(0) == 0)
    def _():
        carry_ref[...] = jnp.zeros(carry_ref.shape, F32)

    d = x_ref.shape[1]
    tm = x_ref.shape[0]
    y = jnp.dot(mix, wo_ref[...], preferred_element_type=F32)
    x = x_ref[...] + mod_ref[0, :, 2 * d:3 * d] * y
    xo_ref[...] = x
    h2 = _rms_mod(x, g2_ref[...], mod_ref[0, :, 4 * d:5 * d], mod_ref[0, :, 3 * d:4 * d])
    hi = h2.astype(BF16)
    lo = (h2 - hi.astype(F32)).astype(BF16)
    logits = (lax.dot_general(wr_ref[...], hi, _NT, preferred_element_type=F32)
              + lax.dot_general(wr_ref[...], lo, _NT, preferred_element_type=F32)
              + br_ref[:, 0:1])
    bucket = _route(logits)
    bk_ref[...] = bucket
    ids = lax.broadcasted_iota(I32, (BUCKET_ROWS, tm), 0)
    onehot = (ids == bucket).astype(F32)
    before = jnp.dot(onehot.astype(BF16), tri_ref[...], preferred_element_type=F32)
    carry = carry_ref[...]
    rank_ref[...] = jnp.sum(onehot * (before + carry[:, 0:1]), axis=0, keepdims=True).astype(I32)
    carry = carry + jnp.sum(onehot, axis=1, keepdims=True)
    carry_ref[...] = carry
    cnt_ref[...] = carry.astype(I32)


def _post(mix, xa, mod_l, g2, wo, wr_bf, br_t, tri, n_act, n_lat, seq, n_batch):
    d = xa.shape[1]
    tm = ROW_TILE
    mrow = _mod_row_map(tm, n_lat, seq, n_batch)
    full = lambda a: pl.BlockSpec(a.shape, lambda i, _nd=a.ndim: (0,) * _nd)
    if len(mix) == 1:
        lat_tiles = None
        mix_specs = [pl.BlockSpec((tm, d), lambda i: (i, 0))]
    else:
        lat_tiles = n_lat // tm
        mix_specs = [pl.BlockSpec((tm, d), lambda i: (jnp.minimum(i, lat_tiles - 1), 0)),
                     pl.BlockSpec((tm, d), lambda i: (jnp.maximum(i - lat_tiles, 0), 0))]
    return pl.pallas_call(
        functools.partial(_post_kernel, lat_tiles=lat_tiles),
        grid=(n_act // tm,),
        in_specs=mix_specs + [
            pl.BlockSpec((tm, d), lambda i: (i, 0)),
            pl.BlockSpec((1, 1, 6 * d), lambda i: (mrow(i), 0, 0)),
            full(g2), full(wo), full(wr_bf), full(br_t), full(tri),
        ],
        out_specs=[
            pl.BlockSpec((tm, d), lambda i: (i, 0)),
            pl.BlockSpec((1, tm), lambda i: (0, i)),
            pl.BlockSpec((1, tm), lambda i: (0, i)),
            pl.BlockSpec((BUCKET_ROWS, 128), lambda i: (0, 0)),
        ],
        out_shape=[
            jax.ShapeDtypeStruct((n_act, d), F32),
            jax.ShapeDtypeStruct((1, n_act), I32),
            jax.ShapeDtypeStruct((1, n_act), I32),
            jax.ShapeDtypeStruct((BUCKET_ROWS, 128), I32),
        ],
        scratch_shapes=[pltpu.VMEM((BUCKET_ROWS, 128), F32)],
        compiler_params=_params(1, VMEM_LIMIT),
        name="post_mixer",
    )(*mix, xa, mod_l, g2, wo, wr_bf, br_t, tri)


def _dispatch_kernel(pos_ref, x_ref, mod_ref, g2_ref, hs_in_ref, wg_ref, wu_ref, wd_ref,
                     hs_ref, wgo_ref, wuo_ref, wdo_ref, hbuf_ref, sem):
    del hs_in_ref
    d = x_ref.shape[1]
    tm = x_ref.shape[0]
    h2 = _rms_mod(x_ref[...], g2_ref[...], mod_ref[0, :, 4 * d:5 * d], mod_ref[0, :, 3 * d:4 * d])
    hbuf_ref[...] = h2.reshape(hbuf_ref.shape)
    casts = ((wg_ref, wgo_ref), (wu_ref, wuo_ref), (wd_ref, wdo_ref))
    for k, (w_ref, wo_ref) in enumerate(casts):
        for r in range(tm * k // len(casts), tm * (k + 1) // len(casts)):
            pltpu.make_async_copy(hbuf_ref.at[r], hs_ref.at[pos_ref[0, 0, r]], sem).start(priority=r % 2)
        wo_ref[...] = w_ref[0].astype(BF16)
    pltpu.make_async_copy(hbuf_ref, hs_ref.at[pl.ds(0, tm)], sem).wait()


def _dispatch(layer, pos3, xa, mod_l, g2, hs, w_gate, w_up, w_down, n_lat, seq, n_batch):
    n, d = xa.shape
    tm = ROW_TILE
    mrow = _mod_row_map(tm, n_lat, seq, n_batch)
    w_rows, w_cols = w_gate.shape[1:]
    cast_steps = 1
    while 2 * cast_steps <= n // tm and w_rows % (2 * cast_steps) == 0:
        cast_steps *= 2
    cast_rows = w_rows // cast_steps
    assert w_up.shape == w_gate.shape == w_down.shape
    cast_row = lambda i: jnp.minimum(i, cast_steps - 1)
    w_spec = pl.BlockSpec((1, cast_rows, w_cols), lambda i: (layer, cast_row(i), 0))
    wo_spec = pl.BlockSpec((cast_rows, w_cols), lambda i: (cast_row(i), 0))
    wo_shape = jax.ShapeDtypeStruct((w_rows, w_cols), BF16)
    return pl.pallas_call(
        _dispatch_kernel,
        grid=(n // tm,),
        in_specs=[
            pl.BlockSpec((1, 1, tm), lambda i: (i, 0, 0), memory_space=pltpu.SMEM),
            pl.BlockSpec((tm, d), lambda i: (i, 0)),
            pl.BlockSpec((1, 1, 6 * d), lambda i: (mrow(i), 0, 0)),
            pl.BlockSpec((1, d), lambda i: (0, 0)),
            pl.BlockSpec(memory_space=pl.ANY),
            w_spec, w_spec, w_spec,
        ],
        out_specs=[pl.BlockSpec(memory_space=pl.ANY), wo_spec, wo_spec, wo_spec],
        out_shape=[jax.ShapeDtypeStruct(hs.shape, F32), wo_shape, wo_shape, wo_shape],
        scratch_shapes=[pltpu.VMEM((tm,) + hs.shape[1:], F32), pltpu.SemaphoreType.DMA(())],
        input_output_aliases={4: 0},
        compiler_params=_params(1, VMEM_LIMIT),
        name="dispatch",
    )(pos3, xa, mod_l, g2, hs, w_gate, w_up, w_down)


def _moe_kernel(ea_ref, eb_ref, nu_ref, h_ref, wga_ref, wua_ref, wda_ref, wgb_ref, wub_ref, wdb_ref,
                wrt_ref, brt_ref, o_ref):
    t = pl.program_id(0)

    @pl.when(t >= nu_ref[0])
    def _():
        o_ref[...] = jnp.zeros(o_ref.shape, F32)

    @pl.when(t < nu_ref[0])
    def _():
        ea = ea_ref[t]
        eb = eb_ref[t]
        tm = h_ref.shape[0]
        h = h_ref[...].reshape(tm, h_ref.shape[1] * h_ref.shape[2])
        hb = h.astype(BF16)
        wdiff = wrt_ref[pl.ds(ea, 1), :] - wrt_ref[pl.ds(eb, 1), :]
        bdiff = brt_ref[pl.ds(ea, 1), 0:1] - brt_ref[pl.ds(eb, 1), 0:1]
        wa = jax.nn.sigmoid(jnp.sum(h * wdiff, axis=-1, keepdims=True) + bdiff)

        slots = ((wga_ref, wua_ref, wda_ref), (wgb_ref, wub_ref, wdb_ref))
        f = wga_ref.shape[2]
        chains = [(e, c) for e in range(2) for c in range(f // 512)]
        gu = {}
        for e, c in chains:
            cols = slice(512 * c, 512 * (c + 1))
            gu[e, c] = (jnp.dot(hb, slots[e][0][0, :, cols], preferred_element_type=F32),
                        jnp.dot(hb, slots[e][1][0, :, cols], preferred_element_type=F32))
        act = {k: (_silu(g) * u).astype(BF16) for k, (g, u) in gu.items()}
        outs = [None, None]
        for e, c in chains:
            part = jnp.dot(act[e, c], slots[e][2][0, 512 * c:512 * (c + 1), :], preferred_element_type=F32)
            outs[e] = part if outs[e] is None else outs[e] + part
        y = wa * outs[0] + (1.0 - wa) * outs[1]
        o_ref[...] = y.reshape(o_ref.shape)


def _moe(ea, eb, nu, hs, wg, wu, wd, wr_t, br_t):
    p_rows = hs.shape[0]
    row = hs.shape[1:]
    tm = MOE_TILE
    d, f = wg.shape[1:]
    wa_spec = lambda shape: pl.BlockSpec(shape, lambda t, ea_, eb_, nu_: (ea_[t], 0, 0))
    wb_spec = lambda shape: pl.BlockSpec(shape, lambda t, ea_, eb_, nu_: (eb_[t], 0, 0))
    grid_spec = pltpu.PrefetchScalarGridSpec(
        num_scalar_prefetch=3,
        grid=(p_rows // tm,),
        in_specs=[
            pl.BlockSpec((tm,) + row, lambda t, ea_, eb_, nu_: (t, 0, 0)),
            wa_spec((1, d, f)), wa_spec((1, d, f)), wa_spec((1, f, d)),
            wb_spec((1, d, f)), wb_spec((1, d, f)), wb_spec((1, f, d)),
            pl.BlockSpec(wr_t.shape, lambda t, ea_, eb_, nu_: (0, 0)),
            pl.BlockSpec(br_t.shape, lambda t, ea_, eb_, nu_: (0, 0)),
        ],
        out_specs=pl.BlockSpec((tm,) + row, lambda t, ea_, eb_, nu_: (t, 0, 0)),
    )
    return pl.pallas_call(
        _moe_kernel,
        grid_spec=grid_spec,
        out_shape=jax.ShapeDtypeStruct(hs.shape, F32),
        compiler_params=_params(1, VMEM_LIMIT),
        name="experts",
    )(ea, eb, nu, hs, wg, wu, wd, wg, wu, wd, wr_t, br_t)


def _combine_kernel(pos_ref, posn_ref, x_ref, mod_ref, fg_ref, ys_ref, o_ref, gbuf_ref, sem):
    d = x_ref.shape[1]
    rows, issue_next, finish = _gather_rows(pos_ref, posn_ref, ys_ref, gbuf_ref, sem)
    issue_next(0, 1)
    x = x_ref[...] + mod_ref[0, :, 5 * d:6 * d] * rows
    o_ref[...] = x * lax.rsqrt(jnp.mean(x * x, axis=-1, keepdims=True) + EPS) * fg_ref[...]
    finish()


def _combine(pos3, xa, mod_l, fg, ys, n_lat, seq, n_batch):
    n, d = xa.shape
    tm = ROW_TILE
    steps = n // tm
    mrow = _mod_row_map(tm, n_lat, seq, n_batch)
    return pl.pallas_call(
        _combine_kernel,
        grid=(steps,),
        in_specs=[
            pl.BlockSpec((1, 1, tm), lambda i: (i, 0, 0), memory_space=pltpu.SMEM),
            pl.BlockSpec((1, 1, tm), lambda i: (jnp.minimum(i + 1, steps - 1), 0, 0), memory_space=pltpu.SMEM),
            pl.BlockSpec((tm, d), lambda i: (i, 0)),
            pl.BlockSpec((1, 1, 6 * d), lambda i: (mrow(i), 0, 0)),
            pl.BlockSpec((1, d), lambda i: (0, 0)),
            pl.BlockSpec(memory_space=pl.ANY),
        ],
        out_specs=pl.BlockSpec((tm, d), lambda i: (i, 0)),
        out_shape=jax.ShapeDtypeStruct((n, d), F32),
        scratch_shapes=[pltpu.VMEM((2, tm) + ys.shape[1:], F32), pltpu.SemaphoreType.DMA((2,))],
        compiler_params=_params(1, VMEM_LIMIT),
        name="combine",
    )(pos3, pos3, xa, mod_l, fg, ys)


def _bucket_plan(bucket, rank, counts, n_tiles, tile):
    cnt = counts[:N_BUCKETS, 0]
    padded = ((cnt + tile - 1) // tile) * tile
    ends = jnp.cumsum(padded)
    offs = ends - padded
    pos = offs[bucket[0]] + rank[0]
    n_used = ends[-1] // tile
    t_eff = jnp.minimum(jnp.arange(n_tiles, dtype=I32), jnp.maximum(n_used - 1, 0))
    tb = jnp.minimum(jnp.sum((ends[None, :] <= (t_eff * tile)[:, None]).astype(I32), axis=1), N_BUCKETS - 1)
    pairs = np.array(PAIR_SLOTS, dtype=np.int32)
    ea = (tb // N_PAIRS) * EXPERTS_PER_GROUP + jnp.asarray(pairs[:, 0])[tb % N_PAIRS]
    eb = (tb // N_PAIRS) * EXPERTS_PER_GROUP + jnp.asarray(pairs[:, 1])[tb % N_PAIRS]
    return pos.astype(I32), ea.astype(I32), eb.astype(I32), n_used.astype(I32).reshape(1)


def kernel(x, c, ctx, c_ctx, w_mod, b_mod, norm_g, w_in_even, w_out_even, ret_log_decay, ret_gn_g, sg_ln_g, sg_ln_b, sg_w, sg_b, w_qkv_odd, w_o_odd, attn_sink, w_router, b_router, w_e_gate, w_e_up, w_e_down, final_g):
    n_batch, seq, d = x.shape
    ctx_len = ctx.shape[1]
    depth = w_mod.shape[0]
    n_lat = n_batch * seq
    n_all = n_lat + n_batch * ctx_len
    tm = ROW_TILE
    assert seq % tm == 0 and (n_batch * ctx_len) % tm == 0

    mod_rows = -(-(n_batch + 1) // 16) * 16
    c_rows = jnp.zeros((mod_rows, d), F32).at[:n_batch].set(c).at[n_batch].set(c_ctx)
    mod = _modulation(c_rows, w_mod, b_mod).reshape(depth, mod_rows, 1, 6 * d)

    wr_t = w_router.T.astype(F32)
    wr_bf = wr_t.astype(BF16)
    br_t = jnp.broadcast_to(b_router.astype(F32)[:, None], (N_EXPERTS, 128))
    tri = jnp.triu(jnp.ones((tm, tm), BF16), k=1)
    cos_t, sin_t = _rope_tables(seq, tm)
    bias = _attn_bias()

    n_exp, d_exp = w_e_gate.shape[1], w_e_gate.shape[3]
    assert d_exp == d
    w_views = [w.reshape(depth, n_exp * d, d) for w in (w_e_gate, w_e_up, w_e_down)]

    n_tiles = n_all // MOE_TILE + N_BUCKETS
    p_rows = n_tiles * MOE_TILE
    hs = jnp.zeros((p_rows, ROW_SUB, d // ROW_SUB), F32)

    source = (x.reshape(n_lat, d), ctx.reshape(n_batch * ctx_len, d))
    for l in range(depth):
        i = l // 2
        last = l == depth - 1
        mod_l = mod[l]
        g1 = norm_g[l, 0].reshape(1, d)
        g2 = norm_g[l, 1].reshape(1, d)
        if l % 2 == 0:
            xa, p = _in_stage(source, mod_l, g1, w_in_even[i].astype(BF16), None, n_lat, seq, n_batch)
            tabs = _retention_tables(ret_log_decay[i], RET_BLOCK)
            sg_bias = jnp.repeat(sg_b[i].astype(F32).T, SG_CHUNK, axis=1)
            mix = _even_mix(p, tabs, ret_gn_g[i].reshape(1, -1), sg_ln_g[i].reshape(1, -1), sg_ln_b[i].reshape(1, -1),
                            sg_w[i].astype(BF16), sg_bias, n_batch, seq, ctx_len)
            wo = w_out_even[i].astype(BF16)
        else:
            xa, q, kv = _in_stage(source, mod_l, g1, w_qkv_odd[i].astype(BF16), (cos_t, sin_t), n_lat, seq, n_batch)
            sink_t = jnp.broadcast_to(attn_sink[i].astype(F32)[:, None], (ATT_HEADS, 128))
            mix = (_attention(q, kv, bias, sink_t, n_batch, seq, ctx_len, need_ctx=not last),)
            wo = w_o_odd[i].astype(BF16)
        n_act = n_lat if last else n_all
        xm, bucket, rank, counts = _post(mix, xa, mod_l, g2, wo, wr_bf, br_t, tri, n_act, n_lat, seq, n_batch)
        pos, ea, eb, n_used = _bucket_plan(bucket, rank, counts, n_tiles, MOE_TILE)
        pos3 = pos.reshape(n_act // tm, 1, tm)
        hs, wg, wu, wd = _dispatch(l, pos3, xm, mod_l, g2, hs, *w_views, n_lat, seq, n_batch)
        ys = _moe(ea, eb, n_used, hs, *(w.reshape(n_exp, d, d) for w in (wg, wu, wd)), wr_t, br_t)
        source = (pos3, xm, mod_l, ys)
    out = _combine(pos3, xm, mod_l, final_g.reshape(1, d), ys, n_lat, seq, n_batch)
    return out.reshape(n_batch, seq, d)
```

```python
import functools

import jax
import jax.numpy as jnp
import numpy as np
from jax import lax
from jax.experimental import pallas as pl
from jax.experimental.pallas import tpu as pltpu

F32 = jnp.float32
BF16 = jnp.bfloat16
I32 = jnp.int32

EPS = 1e-6
NEG = -1e30
GRID_W = 64
ROPE_BASE = 10000.0
RET_HEADS = 4
RET_QK_DIM = 64
RET_V_DIM = 128
SG_GROUPS = 4
SG_CHUNK = 128
RET_BLOCK = 256
ATT_HEADS = 8
ATT_KV_HEADS = 2
ATT_GROUP = ATT_HEADS // ATT_KV_HEADS
HEAD_DIM = 128
ATT_BLOCK = 128
WINDOW = 128
ATT_SUB = 2
ATT_SLAB = 32
N_EXPERTS = 16
N_GROUPS = 4
EXPERTS_PER_GROUP = 4
N_PAIRS = 6
N_BUCKETS = N_GROUPS * N_PAIRS
PAIR_SLOTS = ((0, 1), (0, 2), (0, 3), (1, 3), (1, 2), (3, 2))
BUCKET_ROWS = 32

ROW_TILE = 512
MOE_TILE = 256
ROW_SUB = 8
VMEM_LIMIT = 56 * 1024 * 1024

_NT = (((1,), (1,)), ((), ()))
_TN = (((0,), (0,)), ((), ()))


def _params(n_axes, vmem=None):
    return pltpu.CompilerParams(dimension_semantics=("arbitrary",) * n_axes, vmem_limit_bytes=vmem)


def _silu(v):
    return v * jax.nn.sigmoid(v)


def _rms_mod(x, g, scale, shift):
    y = x * lax.rsqrt(jnp.mean(x * x, axis=-1, keepdims=True) + EPS) * g
    return y * (1.0 + scale) + shift


def _mod_kernel(c_ref, w_ref, b_ref, o_ref):
    a = _silu(c_ref[...]).astype(BF16)
    o_ref[0] = jnp.dot(a, w_ref[0].astype(BF16), preferred_element_type=F32) + b_ref[0]


def _modulation(c_rows, w_mod, b_mod):
    depth, d, six_d = w_mod.shape
    mr = c_rows.shape[0]
    tn = 1536
    return pl.pallas_call(
        _mod_kernel,
        grid=(depth, six_d // tn),
        in_specs=[
            pl.BlockSpec((mr, d), lambda l, j: (0, 0)),
            pl.BlockSpec((1, d, tn), lambda l, j: (l, 0, j)),
            pl.BlockSpec((1, 1, tn), lambda l, j: (l, 0, j)),
        ],
        out_specs=pl.BlockSpec((1, mr, tn), lambda l, j: (l, 0, j)),
        out_shape=jax.ShapeDtypeStruct((depth, mr, six_d), F32),
        compiler_params=_params(2, VMEM_LIMIT),
        name="modulation",
    )(c_rows, w_mod, b_mod.reshape(depth, 1, six_d))


def _mod_row_map(tm, n_lat, seq, n_batch):
    def index(i):
        row0 = i * tm
        return jnp.where(row0 < n_lat, row0 // seq, n_batch)
    return index


def _gather_rows(pos_ref, posn_ref, ys_ref, gbuf_ref, sem):
    i = pl.program_id(0)
    slot = i % 2
    tm = gbuf_ref.shape[1]

    def issue(p_ref, s, lo, hi):
        for r in range(lo, hi):
            pltpu.make_async_copy(ys_ref.at[p_ref[0, 0, r]], gbuf_ref.at[s, r], sem.at[s]).start(priority=r % 2)

    def wait(s):
        pltpu.make_async_copy(ys_ref.at[pl.ds(0, tm)], gbuf_ref.at[s], sem.at[s]).wait()

    @pl.when(i == 0)
    def _():
        issue(pos_ref, 0, 0, tm)

    wait(slot)
    rows = gbuf_ref[slot].reshape(tm, gbuf_ref.shape[2] * gbuf_ref.shape[3])

    def issue_next(k, parts):
        issue(posn_ref, 1 - slot, tm * k // parts, tm * (k + 1) // parts)

    def finish():
        @pl.when(i == pl.num_programs(0) - 1)
        def _():
            wait(1 - slot)

    return rows, issue_next, finish


def _project_even(h, w_ref, o_ref, between):
    for c in range(5):
        between(c, 5)
        p = jnp.dot(h, w_ref[:, 512 * c:512 * (c + 1)], preferred_element_type=F32)
        if c == 0:
            col = lax.broadcasted_iota(I32, p.shape, 1)
            p = jnp.where(col >= 256, p * (RET_QK_DIM ** -0.5), p)
        elif c >= 3:
            p = jax.nn.gelu(p)
        o_ref[:, 512 * c:512 * (c + 1)] = p.astype(BF16)


def _project_odd(h, w_ref, cos_ref, sin_ref, q_ref, kv_ref, between):
    tm = h.shape[0]
    cos = cos_ref[...]
    sin = sin_ref[...]
    lane = lax.broadcasted_iota(I32, (tm, HEAD_DIM), 1)
    first = (lane % 64) < 32

    def rope(p):
        partner = jnp.where(first, pltpu.roll(p, 96, 1), pltpu.roll(p, 32, 1))
        return p * cos + partner * sin

    scale = HEAD_DIM ** -0.5
    for c in range(3):
        if c < 2:
            between(c, 2)
        p = jnp.dot(h, w_ref[:, 512 * c:512 * (c + 1)], preferred_element_type=F32)
        for j in range(4):
            pj = p[:, 128 * j:128 * (j + 1)]
            if c < 2:
                q_ref[:, 512 * c + 128 * j:512 * c + 128 * (j + 1)] = (rope(pj) * scale).astype(BF16)
            elif j < 2:
                kv_ref[:, 128 * j:128 * (j + 1)] = rope(pj).astype(BF16)
            else:
                kv_ref[:, 128 * j:128 * (j + 1)] = pj.astype(BF16)


def _in_kernel(*refs, first, odd, lat_tiles):
    if first:
        (xl_ref, xc_ref), refs = refs[:2], refs[2:]
        x = jnp.where(pl.program_id(0) < lat_tiles, xl_ref[...], xc_ref[...])
        between, finish = (lambda k, parts: None), (lambda: None)
    else:
        (pos_ref, posn_ref, x_ref, modp_ref, ys_ref), refs = refs[:5], refs[5:]
        gbuf_ref, sem = refs[-2:]
        refs = refs[:-2]
        d = x_ref.shape[1]
        rows, between, finish = _gather_rows(pos_ref, posn_ref, ys_ref, gbuf_ref, sem)
        x = x_ref[...] + modp_ref[0, :, 5 * d:6 * d] * rows
    mod_ref, g_ref, w_ref = refs[:3]
    d = x.shape[1]
    h = _rms_mod(x, g_ref[...], mod_ref[0, :, d:2 * d], mod_ref[0, :, 0:d]).astype(BF16)
    if odd:
        cos_ref, sin_ref, xo_ref, q_ref, kv_ref = refs[3:]
        xo_ref[...] = x
        _project_odd(h, w_ref, cos_ref, sin_ref, q_ref, kv_ref, between)
    else:
        xo_ref, o_ref = refs[3:]
        xo_ref[...] = x
        _project_even(h, w_ref, o_ref, between)
    finish()


def _in_stage(source, mod_l, g, w, rope, n_lat, seq, n_batch):
    first = len(source) == 2
    odd = rope is not None
    tm = ROW_TILE
    if first:
        xl, xc = source
        d = xl.shape[1]
        n = xl.shape[0] + xc.shape[0]
        lat_tiles = xl.shape[0] // tm
        src_specs = [pl.BlockSpec((tm, d), lambda i: (jnp.minimum(i, lat_tiles - 1), 0)),
                     pl.BlockSpec((tm, d), lambda i: (jnp.maximum(i - lat_tiles, 0), 0))]
        scratch = []
    else:
        pos3, xm, mod_prev, ys = source
        n, d = xm.shape
        lat_tiles = n_lat // tm
        steps = n // tm
        mrow_p = _mod_row_map(tm, n_lat, seq, n_batch)
        src_specs = [pl.BlockSpec((1, 1, tm), lambda i: (i, 0, 0), memory_space=pltpu.SMEM),
                     pl.BlockSpec((1, 1, tm), lambda i: (jnp.minimum(i + 1, steps - 1), 0, 0), memory_space=pltpu.SMEM),
                     pl.BlockSpec((tm, d), lambda i: (i, 0)),
                     pl.BlockSpec((1, 1, 6 * d), lambda i: (mrow_p(i), 0, 0)),
                     pl.BlockSpec(memory_space=pl.ANY)]
        source = (pos3, pos3, xm, mod_prev, ys)
        scratch = [pltpu.VMEM((2, tm) + ys.shape[1:], F32), pltpu.SemaphoreType.DMA((2,))]
    mrow = _mod_row_map(tm, n_lat, seq, n_batch)
    specs = src_specs + [pl.BlockSpec((1, 1, 6 * d), lambda i: (mrow(i), 0, 0)),
                         pl.BlockSpec((1, d), lambda i: (0, 0)),
                         pl.BlockSpec(w.shape, lambda i: (0, 0))]
    args = tuple(source) + (mod_l, g, w)
    out_specs = [pl.BlockSpec((tm, d), lambda i: (i, 0))]
    out_shape = [jax.ShapeDtypeStruct((n, d), F32)]
    if odd:
        tiles_per_seq = seq // tm

        def rope_row(i):
            return jnp.where(i < lat_tiles, i % tiles_per_seq, tiles_per_seq)

        specs += [pl.BlockSpec((tm, HEAD_DIM), lambda i: (rope_row(i), 0))] * 2
        args += tuple(rope)
        widths = (ATT_HEADS * HEAD_DIM, 2 * ATT_KV_HEADS * HEAD_DIM)
    else:
        widths = (w.shape[1],)
    out_specs += [pl.BlockSpec((tm, wd), lambda i: (i, 0)) for wd in widths]
    out_shape += [jax.ShapeDtypeStruct((n, wd), BF16) for wd in widths]
    return pl.pallas_call(
        functools.partial(_in_kernel, first=first, odd=odd, lat_tiles=lat_tiles),
        grid=(n // tm,),
        in_specs=specs,
        out_specs=out_specs,
        out_shape=out_shape,
        scratch_shapes=scratch,
        compiler_params=_params(1, VMEM_LIMIT),
        name="in_odd" if odd else "in_even",
    )(*args)


def _rope_tables(seq, tm):
    t = jnp.arange(seq)
    row = (t // GRID_W).astype(F32)
    col = (t % GRID_W).astype(F32)
    quarter = HEAD_DIM // 4
    inv_freq = ROPE_BASE ** (-jnp.arange(quarter, dtype=F32) / quarter)
    ar = row[:, None] * inv_freq[None, :]
    ac = col[:, None] * inv_freq[None, :]
    cos = jnp.concatenate([jnp.cos(ar), jnp.cos(ar), jnp.cos(ac), jnp.cos(ac)], axis=1)
    sin = jnp.concatenate([-jnp.sin(ar), jnp.sin(ar), -jnp.sin(ac), jnp.sin(ac)], axis=1)
    cos = jnp.concatenate([cos, jnp.ones((tm, HEAD_DIM), F32)], axis=0)
    sin = jnp.concatenate([sin, jnp.zeros((tm, HEAD_DIM), F32)], axis=0)
    return cos, sin


def _ret_increments(qk_ref, v_ref, rows, zf_ref, zb_ref):
    upper_rows = lax.broadcasted_iota(I32, (128, 128), 0) < RET_QK_DIM
    out = []
    for grp in range(2):
        k2 = qk_ref[rows, 256 + 128 * grp:256 + 128 * (grp + 1)].astype(F32)
        v2 = v_ref[rows, 256 * grp:256 * (grp + 1)]
        halves = []
        for z_ref in (zf_ref, zb_ref):
            kz = (k2 * z_ref[grp]).astype(BF16)
            full = lax.dot_general(kz, v2, _TN, preferred_element_type=F32)
            halves.append(jnp.where(upper_rows, full[:, 0:128], full[:, 128:256]))
        out.append(jnp.concatenate(halves, axis=1))
    return out


def _mix_outputs(qk_ref, v_ref, gate_ref, u_ref, s_ref, rows, states, dec_ref, xi_ref, gn_ref, lng_ref, lnb_ref,
                 ws_ref, sgb_ref, o_ref):
    lane = lax.broadcasted_iota(I32, (1, 128), 1)
    head_mask = [(lane // RET_QK_DIM == hh).astype(BF16) for hh in range(2)]
    for grp in range(2):
        q2 = qk_ref[rows, 128 * grp:128 * (grp + 1)]
        k2 = qk_ref[rows, 256 + 128 * grp:256 + 128 * (grp + 1)]
        for hh in range(2):
            hd = 2 * grp + hh
            cols = slice(128 * hd, 128 * (hd + 1))
            qm = q2 * head_mask[hh]
            sc = lax.dot_general(qm, k2, _NT, preferred_element_type=F32) * dec_ref[hd]
            o = jnp.dot(sc.astype(BF16), v_ref[rows, cols], preferred_element_type=F32)
            if states is not None:
                cross = jnp.dot(qm, states[grp], preferred_element_type=F32) * xi_ref[hd]
                o = o + cross[:, 0:128] + cross[:, 128:256]
            mu = jnp.mean(o, axis=-1, keepdims=True)
            oc = o - mu
            var = jnp.mean(oc * oc, axis=-1, keepdims=True)
            y = oc * lax.rsqrt(var + EPS) * gn_ref[:, cols]
            y = y * _silu(gate_ref[rows, cols].astype(F32))
            o_ref[rows, cols] = y.astype(BF16)
    for sub in range((rows.stop - rows.start) // SG_CHUNK):
        srows = slice(rows.start + SG_CHUNK * sub, rows.start + SG_CHUNK * (sub + 1))
        s = s_ref[srows, :].astype(F32)
        mu = jnp.mean(s, axis=-1, keepdims=True)
        sc_ = s - mu
        var = jnp.mean(sc_ * sc_, axis=-1, keepdims=True)
        sn = (sc_ * lax.rsqrt(var + EPS) * lng_ref[...] + lnb_ref[...]).astype(BF16)
        for g in range(SG_GROUPS):
            cols = slice(128 * g, 128 * (g + 1))
            mixed = jnp.dot(ws_ref[g], sn[:, cols], preferred_element_type=F32) + sgb_ref[:, cols]
            o_ref[srows, 512 + 128 * g:512 + 128 * (g + 1)] = (u_ref[srows, cols].astype(F32) * mixed).astype(BF16)


def _even_lat_kernel(bqk_ref, bv_ref, cqk_ref, cv_ref, qk_ref, v_ref, gate_ref, u_ref, s_ref, dec_ref, xi_ref,
                     zf_ref, zb_ref, gdec_ref, gn_ref, lng_ref, lnb_ref, ws_ref, sgb_ref, o_ref, ds_ref, sp_ref,
                     *, inc_steps):
    j = pl.program_id(1)
    cb = RET_BLOCK
    n_slots = ds_ref.shape[0]
    inc_chunks = bqk_ref.shape[0] // cb
    out_chunks = qk_ref.shape[0] // cb

    @pl.when(j == 0)
    def _():
        inc = _ret_increments(cqk_ref, cv_ref, slice(0, cb), zf_ref, zb_ref)
        for grp in range(2):
            ds_ref[0, grp] = inc[grp]

    @pl.when(j < inc_steps)
    def _():
        for c in range(inc_chunks):
            inc = _ret_increments(bqk_ref, bv_ref, slice(cb * c, cb * (c + 1)), zf_ref, zb_ref)
            for grp in range(2):
                ds_ref[1 + inc_chunks * j + c, grp] = inc[grp]

    @pl.when(j == inc_steps)
    def _scan():
        fwd = ds_ref[0, :, :, 0:128]
        for n in range(1, n_slots):
            sp_ref[n, :, :, 0:128] = fwd
            fwd = fwd * gdec_ref[:, :, 0:128] + ds_ref[n, :, :, 0:128]
        bwd = ds_ref[0, :, :, 128:256]
        for n in range(n_slots - 1, 0, -1):
            sp_ref[n, :, :, 128:256] = bwd
            bwd = bwd * gdec_ref[:, :, 128:256] + ds_ref[n, :, :, 128:256]

    @pl.when(j >= inc_steps)
    def _():
        for c in range(out_chunks):
            n = 1 + out_chunks * (j - inc_steps) + c
            states = [sp_ref[n, grp].astype(BF16) for grp in range(2)]
            _mix_outputs(qk_ref, v_ref, gate_ref, u_ref, s_ref, slice(cb * c, cb * (c + 1)), states, dec_ref, xi_ref,
                         gn_ref, lng_ref, lnb_ref, ws_ref, sgb_ref, o_ref)


def _even_ctx_kernel(qk_ref, v_ref, gate_ref, u_ref, s_ref, dec_ref, xi_ref, gn_ref, lng_ref, lnb_ref, ws_ref, sgb_ref,
                     o_ref):
    _mix_outputs(qk_ref, v_ref, gate_ref, u_ref, s_ref, slice(0, RET_BLOCK), None, dec_ref, xi_ref,
                 gn_ref, lng_ref, lnb_ref, ws_ref, sgb_ref, o_ref)


def _retention_tables(log_decay, cb):
    lg = -jnp.exp(log_decay.astype(F32))
    pos = jnp.arange(cb, dtype=F32)
    diff = pos[:, None] - pos[None, :]
    lower = jnp.where(diff >= 0, jnp.exp(lg[0][:, None, None] * jnp.maximum(diff, 0.0)[None]), 0.0)
    upper = jnp.where(diff <= 0, jnp.exp(lg[1][:, None, None] * jnp.maximum(-diff, 0.0)[None]), 0.0)
    dec = lower + upper
    xi_f = jnp.exp(lg[0][:, None] * (pos[None, :] + 1.0))
    xi_b = jnp.exp(lg[1][:, None] * (cb - pos[None, :]))
    xi = jnp.concatenate([jnp.broadcast_to(xi_f[:, :, None], (RET_HEADS, cb, 128)),
                          jnp.broadcast_to(xi_b[:, :, None], (RET_HEADS, cb, 128))], axis=2)
    zeta_f = jnp.exp(lg[0][:, None] * (cb - 1.0 - pos[None, :]))
    zeta_b = jnp.exp(lg[1][:, None] * pos[None, :])

    def lanes(z):
        return jnp.repeat(z.reshape(2, 2, cb), RET_QK_DIM, axis=1).transpose(0, 2, 1)

    gstep = jnp.exp(lg * cb)

    def rows(gv):
        return jnp.broadcast_to(jnp.repeat(gv.reshape(2, 2), RET_QK_DIM, axis=1)[:, :, None], (2, 128, 128))

    gdec = jnp.concatenate([rows(gstep[0]), rows(gstep[1])], axis=2)
    return dec, xi, lanes(zeta_f), lanes(zeta_b), gdec


def _even_mix(p, tabs, gn_g, ln_g, ln_b, w_s, sg_bias, n_batch, seq, ctx_len):
    cb = RET_BLOCK
    big, blk = 4 * cb, 2 * cb
    assert ctx_len == cb and seq % big == 0
    n_lat = n_batch * seq
    inc_steps, out_steps = seq // big, seq // blk
    dec, xi, zf, zb, gdec = tabs
    consts = (dec, xi, zf, zb, gdec, gn_g, ln_g, ln_b, w_s, sg_bias)
    full2 = lambda a: pl.BlockSpec(a.shape, lambda b, j, _nd=a.ndim: (0,) * _nd)
    inc_row = lambda b, j: b * inc_steps + jnp.minimum(j, inc_steps - 1)
    out_row = lambda b, j: b * out_steps + jnp.maximum(j - inc_steps, 0)
    ctx_row = lambda b, j: n_lat // cb + b
    lat = pl.pallas_call(
        functools.partial(_even_lat_kernel, inc_steps=inc_steps),
        grid=(n_batch, inc_steps + out_steps),
        in_specs=[pl.BlockSpec((big, 512), lambda b, j: (inc_row(b, j), 0)),
                  pl.BlockSpec((big, 512), lambda b, j: (inc_row(b, j), 1)),
                  pl.BlockSpec((cb, 512), lambda b, j: (ctx_row(b, j), 0)),
                  pl.BlockSpec((cb, 512), lambda b, j: (ctx_row(b, j), 1))]
                 + [pl.BlockSpec((blk, 512), lambda b, j, _c=c: (out_row(b, j), _c)) for c in range(5)]
                 + [full2(a) for a in consts],
        out_specs=pl.BlockSpec((blk, 1024), lambda b, j: (out_row(b, j), 0)),
        out_shape=jax.ShapeDtypeStruct((n_lat, 1024), BF16),
        scratch_shapes=[pltpu.VMEM((seq // cb + 1, 2, 128, 256), F32), pltpu.VMEM((seq // cb + 1, 2, 128, 256), F32)],
        compiler_params=_params(2, VMEM_LIMIT),
        name="even_mix",
    )(p, p, p, p, p, p, p, p, p, *consts)
    ctx_consts = (dec, xi, gn_g, ln_g, ln_b, w_s, sg_bias)
    full1 = lambda a: pl.BlockSpec(a.shape, lambda b, _nd=a.ndim: (0,) * _nd)
    ctx = pl.pallas_call(
        _even_ctx_kernel,
        grid=(n_batch,),
        in_specs=[pl.BlockSpec((cb, 512), lambda b, _c=c: (n_lat // cb + b, _c)) for c in range(5)]
                 + [full1(a) for a in ctx_consts],
        out_specs=pl.BlockSpec((cb, 1024), lambda b: (b, 0)),
        out_shape=jax.ShapeDtypeStruct((n_batch * ctx_len, 1024), BF16),
        compiler_params=_params(1, VMEM_LIMIT),
        name="even_mix_ctx",
    )(p, p, p, p, p, *ctx_consts)
    return lat, ctx


def _attn_kernel(q_ref, kvp_ref, kvm_ref, kvn_ref, kvx_ref, bias0_ref, bias1_ref, sink_ref, o_ref, s_scr, p_scr, r_scr):
    blk = ATT_BLOCK
    nloc = 3 * blk
    rows_all = ATT_GROUP * blk
    bias_refs = (bias0_ref, bias1_ref)
    chains = [(kvh, sub) for kvh in range(ATT_KV_HEADS) for sub in range(ATT_SUB)]
    v_all, sinks = {}, {}
    for ci, (kvh, sub) in enumerate(chains):
        kcols = slice(HEAD_DIM * kvh, HEAD_DIM * (kvh + 1))
        vcols = slice(HEAD_DIM * (ATT_KV_HEADS + kvh), HEAD_DIM * (ATT_KV_HEADS + kvh + 1))
        k_parts = [kvp_ref[:, kcols], kvm_ref[0:blk, kcols], kvm_ref[blk:2 * blk, kcols], kvn_ref[:, kcols]]
        v_parts = [kvp_ref[:, vcols], kvm_ref[0:blk, vcols], kvm_ref[blk:2 * blk, vcols], kvn_ref[:, vcols]]
        heads = [ATT_GROUP * kvh + j for j in range(ATT_GROUP)]
        rows = slice(blk * sub, blk * (sub + 1))
        k_all = jnp.concatenate(k_parts[sub:sub + 3] + [kvx_ref[:, kcols]], axis=0)
        v_all[ci] = jnp.concatenate(v_parts[sub:sub + 3] + [kvx_ref[:, vcols]], axis=0)
        q4 = jnp.concatenate([q_ref[rows, HEAD_DIM * hd:HEAD_DIM * (hd + 1)] for hd in heads], axis=0)
        s_scr[ci] = lax.dot_general(q4, k_all, _NT, preferred_element_type=F32)
        sinks[ci] = jnp.concatenate(
            [jnp.broadcast_to(sink_ref[hd:hd + 1, 0:1], (blk, 1)) for hd in heads], axis=0)
    nchunk_loc = nloc // 128
    nchunk = s_scr.shape[2] // 128

    def shifted(ci, sub, rs, k):
        t = s_scr[ci, rs, 128 * k:128 * (k + 1)]
        return t + bias_refs[sub][0, rs, 128 * k:128 * (k + 1)] if k < nchunk_loc else t

    mrow = {}
    for ci, (kvh, sub) in enumerate(chains):
        for r0 in range(0, rows_all, ATT_SLAB):
            rs = slice(r0, r0 + ATT_SLAB)
            part = shifted(ci, sub, rs, 0)
            for k in range(1, nchunk):
                part = jnp.maximum(part, shifted(ci, sub, rs, k))
            r_scr[ci, rs, :] = part
    for ci in range(len(chains)):
        m = jnp.maximum(jnp.max(r_scr[ci], axis=-1, keepdims=True), sinks[ci])
        mrow[ci] = m
        r_scr[ci] = jnp.broadcast_to(m, r_scr.shape[1:])
    for ci, (kvh, sub) in enumerate(chains):
        for r0 in range(0, rows_all, ATT_SLAB):
            rs = slice(r0, r0 + ATT_SLAB)
            mb = r_scr[ci, rs, :]
            part = None
            for k in range(nchunk):
                p = jnp.exp(shifted(ci, sub, rs, k) - mb)
                part = p if part is None else part + p
                p_scr[ci, rs, 128 * k:128 * (k + 1)] = p.astype(BF16)
            r_scr[ci, rs, :] = part
    inv = {}
    for ci in range(len(chains)):
        inv[ci] = 1.0 / (jnp.sum(r_scr[ci], axis=-1, keepdims=True) + jnp.exp(sinks[ci] - mrow[ci]))
    for ci, (kvh, sub) in enumerate(chains):
        rows = slice(blk * sub, blk * (sub + 1))
        o = (jnp.dot(p_scr[ci, :, 0:nloc], v_all[ci][0:nloc], preferred_element_type=F32)
             + jnp.dot(p_scr[ci, :, nloc:], v_all[ci][nloc:], preferred_element_type=F32)) * inv[ci]
        for j in range(ATT_GROUP):
            hd = ATT_GROUP * kvh + j
            o_ref[rows, HEAD_DIM * hd:HEAD_DIM * (hd + 1)] = o[blk * j:blk * (j + 1)].astype(BF16)


def _attn_bias():
    i = np.arange(ATT_BLOCK)[:, None]
    j = np.arange(3 * ATT_BLOCK)[None, :] - ATT_BLOCK
    band = np.abs(i - j) <= WINDOW
    cases = [band & (j >= 0), band, band & (j < ATT_BLOCK), np.zeros_like(band)]
    bias = np.stack([np.where(c, 0.0, NEG) for c in cases]).astype(np.float32)
    return jnp.asarray(np.tile(bias, (1, ATT_GROUP, 1)))


def _attention(q, kv, bias, sink_t, n_batch, seq, ctx_len, need_ctx):
    blk = ATT_BLOCK
    big = ATT_SUB * blk
    assert ATT_SUB == 2 and seq % big == 0 and ctx_len == big
    nb = seq // blk
    ns = seq // big
    n_lat = n_batch * seq
    steps = ns + (1 if need_ctx else 0)
    n_out = n_lat + (n_batch * ctx_len if need_ctx else 0)

    def qrow(b, s):
        return jnp.where(s < ns, b * ns + s, n_lat // big + b)

    def small(off):
        def index(b, s):
            loc = jnp.clip(2 * jnp.minimum(s, ns - 1) + off, 0, nb - 1)
            return (b * nb + loc, 0)
        return index

    def case0(b, s):
        return (jnp.where(s < ns, jnp.where(s == 0, 0, 1), 3), 0, 0)

    def case1(b, s):
        return (jnp.where(s < ns, jnp.where(s == ns - 1, 2, 1), 3), 0, 0)

    return pl.pallas_call(
        _attn_kernel,
        grid=(n_batch, steps),
        in_specs=[
            pl.BlockSpec((big, q.shape[1]), lambda b, s: (qrow(b, s), 0)),
            pl.BlockSpec((blk, kv.shape[1]), small(-1)),
            pl.BlockSpec((big, kv.shape[1]), lambda b, s: (b * ns + jnp.minimum(s, ns - 1), 0)),
            pl.BlockSpec((blk, kv.shape[1]), small(2)),
            pl.BlockSpec((ctx_len, kv.shape[1]), lambda b, s: (n_lat // ctx_len + b, 0)),
            pl.BlockSpec((1,) + bias.shape[1:], case0),
            pl.BlockSpec((1,) + bias.shape[1:], case1),
            pl.BlockSpec(sink_t.shape, lambda b, s: (0, 0)),
        ],
        out_specs=pl.BlockSpec((big, q.shape[1]), lambda b, s: (qrow(b, s), 0)),
        out_shape=jax.ShapeDtypeStruct((n_out, q.shape[1]), BF16),
        scratch_shapes=[pltpu.VMEM((ATT_KV_HEADS * ATT_SUB, ATT_GROUP * blk, 3 * blk + ctx_len), F32),
                        pltpu.VMEM((ATT_KV_HEADS * ATT_SUB, ATT_GROUP * blk, 3 * blk + ctx_len), BF16),
                        pltpu.VMEM((ATT_KV_HEADS * ATT_SUB, ATT_GROUP * blk, 128), F32)],
        compiler_params=_params(2, VMEM_LIMIT),
        name="window_attention",
    )(q, kv, kv, kv, kv, bias, bias, sink_t)


def _route(logits):
    m = jnp.max(logits, axis=0, keepdims=True)
    e = jnp.exp(logits - m)
    p = e / jnp.sum(e, axis=0, keepdims=True)
    rows = [p[i:i + 1, :] for i in range(N_EXPERTS)]
    best = None
    gsel = None
    for g in range(N_GROUPS):
        a, b, c, d = rows[4 * g:4 * g + 4]
        m1, n1 = jnp.maximum(a, b), jnp.minimum(a, b)
        m2, n2 = jnp.maximum(c, d), jnp.minimum(c, d)
        score = jnp.maximum(m1, m2) + jnp.maximum(jnp.minimum(m1, m2), jnp.maximum(n1, n2))
        if g == 0:
            best, gsel = score, jnp.zeros(score.shape, I32)
        else:
            upd = score > best
            gsel = jnp.where(upd, g, gsel)
            best = jnp.where(upd, score, best)
    v = [jnp.where(gsel == 0, rows[j], jnp.where(gsel == 1, rows[4 + j], jnp.where(gsel == 2, rows[8 + j], rows[12 + j])))
         for j in range(EXPERTS_PER_GROUP)]
    b1, i1 = v[0], jnp.zeros(gsel.shape, I32)
    for j in range(1, 4):
        upd = v[j] > b1
        i1 = jnp.where(upd, j, i1)
        b1 = jnp.where(upd, v[j], b1)
    b2, i2 = jnp.full(b1.shape, -1.0, F32), jnp.zeros(gsel.shape, I32)
    for j in range(4):
        upd = jnp.logical_and(i1 != j, v[j] > b2)
        i2 = jnp.where(upd, j, i2)
        b2 = jnp.where(upd, v[j], b2)
    lo = jnp.minimum(i1, i2)
    hi = jnp.maximum(i1, i2)
    pair = jnp.where(lo == 0, hi - 1, jnp.where(lo == 1, 6 - hi, 5))
    return gsel * N_PAIRS + pair


def _post_kernel(*refs, lat_tiles, first_trash):
    if lat_tiles is None:
        mix_ref, refs = refs[0], refs[1:]
        mix = mix_ref[...]
    else:
        (ml_ref, mc_ref), refs = refs[:2], refs[2:]
        mix = jnp.where(pl.program_id(0) < lat_tiles, ml_ref[...], mc_ref[...])
    (x_ref, mod_ref, g2_ref, wo_ref, wr_ref, br_ref, tri_ref, low_ref, hs_in_ref, wg_ref, wu_ref, wd_ref,
     xo_ref, pos_ref, tab_ref, cnt_ref, hs_ref, wgo_ref, wuo_ref, wdo_ref,
     carry_ref, cur_ref, nfree_ref, hbuf_ref, posv_ref, poss_ref, sem_s, sem_p) = refs
    del hs_in_ref
    i = pl.program_id(0)
    slot = i % 2
    tm, d = x_ref.shape

    @pl.when(i == 0)
    def _():
        carry_ref[...] = jnp.zeros(carry_ref.shape, F32)
        cur_ref[...] = jnp.zeros(cur_ref.shape, F32)
        nfree_ref[...] = jnp.zeros(nfree_ref.shape, F32)
        tab_ref[...] = jnp.zeros(tab_ref.shape, F32)
        hbuf_ref[1] = jnp.zeros(hbuf_ref.shape[1:], F32)
        posv_ref[1] = first_trash + lax.broadcasted_iota(I32, (1, tm), 1)
        pltpu.make_async_copy(posv_ref.at[1], poss_ref.at[1], sem_p.at[1]).start()

    pltpu.make_async_copy(posv_ref.at[1 - slot], poss_ref.at[1 - slot], sem_p.at[1 - slot]).wait()

    def scatter(s, lo, hi):
        for r in range(lo, hi):
            pltpu.make_async_copy(hbuf_ref.at[s, r], hs_ref.at[poss_ref[s, 0, r]], sem_s).start(priority=r % 2)

    def scatter_wait(s):
        pltpu.make_async_copy(hbuf_ref.at[s], hs_ref.at[pl.ds(0, tm)], sem_s).wait()

    nchunk = 4
    wc = d // nchunk
    casts = ((wg_ref, wgo_ref), (wu_ref, wuo_ref), (wd_ref, wdo_ref))
    xs = []
    for c in range(nchunk):
        scatter(1 - slot, tm * c // nchunk, tm * (c + 1) // nchunk)
        cols = slice(wc * c, wc * (c + 1))
        y = jnp.dot(mix, wo_ref[:, cols], preferred_element_type=F32)
        xc = x_ref[:, cols] + mod_ref[0, :, 2 * d + wc * c:2 * d + wc * (c + 1)] * y
        xo_ref[:, cols] = xc
        xs.append(xc)
        if c < len(casts):
            casts[c][1][...] = casts[c][0][0].astype(BF16)
    x = jnp.concatenate(xs, axis=1)
    h2 = _rms_mod(x, g2_ref[...], mod_ref[0, :, 4 * d:5 * d], mod_ref[0, :, 3 * d:4 * d])
    hbuf_ref[slot] = h2.reshape(hbuf_ref.shape[1:])
    hi = h2.astype(BF16)
    lo = (h2 - hi.astype(F32)).astype(BF16)
    logits = (lax.dot_general(wr_ref[...], hi, _NT, preferred_element_type=F32)
              + lax.dot_general(wr_ref[...], lo, _NT, preferred_element_type=F32)
              + br_ref[:, 0:1])
    bucket = _route(logits)
    ids = lax.broadcasted_iota(I32, (BUCKET_ROWS, tm), 0)
    onehot = (ids == bucket).astype(F32)
    before = jnp.dot(onehot.astype(BF16), tri_ref[...], preferred_element_type=F32)
    inv = 1.0 / MOE_TILE
    c0 = carry_ref[:, 0:1]
    c1 = c0 + jnp.sum(onehot, axis=1, keepdims=True)
    a0 = jnp.floor((c0 + (MOE_TILE - 1)) * inv)
    new = jnp.floor((c1 + (MOE_TILE - 1)) * inv) - a0
    nfree = nfree_ref[0:1, 0:1]
    new_b = jnp.broadcast_to(new, (BUCKET_ROWS, 128)).astype(BF16)
    id1 = nfree + jnp.dot(low_ref[...], new_b, preferred_element_type=F32)[:, 0:1]
    id2 = id1 + 1.0
    cur = cur_ref[:, 0:1]
    rank = before + c0
    k = jnp.floor(rank * inv)
    chunk = jnp.where(k < a0, cur, jnp.where(k == a0, id1, id2))
    pos = jnp.sum(onehot * (chunk * MOE_TILE + (rank - k * MOE_TILE)), axis=0, keepdims=True).astype(I32)
    pos_ref[...] = pos
    posv_ref[slot] = pos
    pltpu.make_async_copy(posv_ref.at[slot], poss_ref.at[slot], sem_p.at[slot]).start()
    carry_ref[...] = jnp.broadcast_to(c1, carry_ref.shape)
    cnt_ref[...] = jnp.broadcast_to(c1, cnt_ref.shape).astype(I32)
    cur_ref[...] = jnp.broadcast_to(jnp.where(new == 0.0, cur, jnp.where(new == 1.0, id1, id2)), cur_ref.shape)
    nfree_ref[...] = jnp.broadcast_to(nfree + jnp.sum(new, axis=0, keepdims=True), nfree_ref.shape)
    kcol = lax.broadcasted_iota(I32, tab_ref.shape, 1).astype(F32)
    tab = tab_ref[...]
    tab = jnp.where(jnp.logical_and(kcol == a0, new >= 1.0), id1, tab)
    tab_ref[...] = jnp.where(jnp.logical_and(kcol == a0 + 1.0, new >= 2.0), id2, tab)
    scatter_wait(1 - slot)

    @pl.when(i == pl.num_programs(0) - 1)
    def _():
        pltpu.make_async_copy(posv_ref.at[slot], poss_ref.at[slot], sem_p.at[slot]).wait()
        scatter(slot, 0, tm)
        scatter_wait(slot)


def _post(layer, mix, xa, mod_l, g2, wo, wr_bf, br_t, tri, low, hs, w_gate, w_up, w_down, n_act, n_lat, seq, n_batch):
    d = xa.shape[1]
    tm = ROW_TILE
    steps = n_act // tm
    mrow = _mod_row_map(tm, n_lat, seq, n_batch)
    full = lambda a: pl.BlockSpec(a.shape, lambda i, _nd=a.ndim: (0,) * _nd)
    if len(mix) == 1:
        lat_tiles = None
        mix_specs = [pl.BlockSpec((tm, d), lambda i: (i, 0))]
    else:
        lat_tiles = n_lat // tm
        mix_specs = [pl.BlockSpec((tm, d), lambda i: (jnp.minimum(i, lat_tiles - 1), 0)),
                     pl.BlockSpec((tm, d), lambda i: (jnp.maximum(i - lat_tiles, 0), 0))]
    w_rows, w_cols = w_gate.shape[1:]
    cast_steps = 1
    while 2 * cast_steps <= steps and w_rows % (2 * cast_steps) == 0:
        cast_steps *= 2
    cast_rows = w_rows // cast_steps
    assert w_up.shape == w_gate.shape == w_down.shape
    cast_row = lambda i: jnp.minimum(i, cast_steps - 1)
    w_spec = pl.BlockSpec((1, cast_rows, w_cols), lambda i: (layer, cast_row(i), 0))
    wo_spec = pl.BlockSpec((cast_rows, w_cols), lambda i: (cast_row(i), 0))
    wo_shape = jax.ShapeDtypeStruct((w_rows, w_cols), BF16)
    n_in = len(mix_specs)
    return pl.pallas_call(
        functools.partial(_post_kernel, lat_tiles=lat_tiles, first_trash=hs.shape[0] - tm),
        grid=(steps,),
        in_specs=mix_specs + [
            pl.BlockSpec((tm, d), lambda i: (i, 0)),
            pl.BlockSpec((1, 1, 6 * d), lambda i: (mrow(i), 0, 0)),
            full(g2), full(wo), full(wr_bf), full(br_t), full(tri), full(low),
            pl.BlockSpec(memory_space=pl.ANY),
            w_spec, w_spec, w_spec,
        ],
        out_specs=[
            pl.BlockSpec((tm, d), lambda i: (i, 0)),
            pl.BlockSpec((1, tm), lambda i: (0, i)),
            pl.BlockSpec((BUCKET_ROWS, 256), lambda i: (0, 0)),
            pl.BlockSpec((BUCKET_ROWS, 128), lambda i: (0, 0)),
            pl.BlockSpec(memory_space=pl.ANY),
            wo_spec, wo_spec, wo_spec,
        ],
        out_shape=[
            jax.ShapeDtypeStruct((n_act, d), F32),
            jax.ShapeDtypeStruct((1, n_act), I32),
            jax.ShapeDtypeStruct((BUCKET_ROWS, 256), F32),
            jax.ShapeDtypeStruct((BUCKET_ROWS, 128), I32),
            jax.ShapeDtypeStruct(hs.shape, F32),
            wo_shape, wo_shape, wo_shape,
        ],
        scratch_shapes=[pltpu.VMEM((BUCKET_ROWS, 128), F32), pltpu.VMEM((BUCKET_ROWS, 128), F32), pltpu.VMEM((8, 128), F32),
                        pltpu.VMEM((2, tm) + hs.shape[1:], F32), pltpu.VMEM((2, 1, tm), I32), pltpu.SMEM((2, 1, tm), I32),
                        pltpu.SemaphoreType.DMA(()), pltpu.SemaphoreType.DMA((2,))],
        input_output_aliases={n_in + 8: 4},
        compiler_params=_params(1, VMEM_LIMIT),
        name="post_mixer",
    )(*mix, xa, mod_l, g2, wo, wr_bf, br_t, tri, low, hs, w_gate, w_up, w_down)


def _moe_kernel(blk_ref, ea_ref, eb_ref, nu_ref, h_ref, wga_ref, wua_ref, wda_ref, wgb_ref, wub_ref, wdb_ref,
                wrt_ref, brt_ref, o_ref):
    t = pl.program_id(0)

    @pl.when(t >= nu_ref[0])
    def _():
        o_ref[...] = jnp.zeros(o_ref.shape, F32)

    @pl.when(t < nu_ref[0])
    def _():
        ea = ea_ref[t]
        eb = eb_ref[t]
        tm = h_ref.shape[0]
        h = h_ref[...].reshape(tm, h_ref.shape[1] * h_ref.shape[2])
        hb = h.astype(BF16)
        wdiff = wrt_ref[pl.ds(ea, 1), :] - wrt_ref[pl.ds(eb, 1), :]
        bdiff = brt_ref[pl.ds(ea, 1), 0:1] - brt_ref[pl.ds(eb, 1), 0:1]
        wa = jax.nn.sigmoid(jnp.sum(h * wdiff, axis=-1, keepdims=True) + bdiff)

        slots = ((wga_ref, wua_ref, wda_ref), (wgb_ref, wub_ref, wdb_ref))
        f = wga_ref.shape[2]
        chains = [(e, c) for e in range(2) for c in range(f // 512)]
        gu = {}
        for e, c in chains:
            cols = slice(512 * c, 512 * (c + 1))
            gu[e, c] = (jnp.dot(hb, slots[e][0][0, :, cols], preferred_element_type=F32),
                        jnp.dot(hb, slots[e][1][0, :, cols], preferred_element_type=F32))
        act = {k: (_silu(g) * u).astype(BF16) for k, (g, u) in gu.items()}
        outs = [None, None]
        for e, c in chains:
            part = jnp.dot(act[e, c], slots[e][2][0, 512 * c:512 * (c + 1), :], preferred_element_type=F32)
            outs[e] = part if outs[e] is None else outs[e] + part
        y = wa * outs[0] + (1.0 - wa) * outs[1]
        o_ref[...] = y.reshape(o_ref.shape)


def _moe(blk, ea, eb, nu, hs, n_tiles, wg, wu, wd, wr_t, br_t):
    row = hs.shape[1:]
    tm = MOE_TILE
    d, f = wg.shape[1:]
    wa_spec = lambda shape: pl.BlockSpec(shape, lambda t, blk_, ea_, eb_, nu_: (ea_[t], 0, 0))
    wb_spec = lambda shape: pl.BlockSpec(shape, lambda t, blk_, ea_, eb_, nu_: (eb_[t], 0, 0))
    grid_spec = pltpu.PrefetchScalarGridSpec(
        num_scalar_prefetch=4,
        grid=(n_tiles,),
        in_specs=[
            pl.BlockSpec((tm,) + row, lambda t, blk_, ea_, eb_, nu_: (blk_[t], 0, 0)),
            wa_spec((1, d, f)), wa_spec((1, d, f)), wa_spec((1, f, d)),
            wb_spec((1, d, f)), wb_spec((1, d, f)), wb_spec((1, f, d)),
            pl.BlockSpec(wr_t.shape, lambda t, blk_, ea_, eb_, nu_: (0, 0)),
            pl.BlockSpec(br_t.shape, lambda t, blk_, ea_, eb_, nu_: (0, 0)),
        ],
        out_specs=pl.BlockSpec((tm,) + row, lambda t, blk_, ea_, eb_, nu_: (blk_[t], 0, 0)),
    )
    return pl.pallas_call(
        _moe_kernel,
        grid_spec=grid_spec,
        out_shape=jax.ShapeDtypeStruct((n_tiles * tm,) + row, F32),
        compiler_params=_params(1, VMEM_LIMIT),
        name="experts",
    )(blk, ea, eb, nu, hs, wg, wu, wd, wg, wu, wd, wr_t, br_t)


def _combine_kernel(pos_ref, posn_ref, x_ref, mod_ref, fg_ref, ys_ref, o_ref, gbuf_ref, sem):
    d = x_ref.shape[1]
    rows, issue_next, finish = _gather_rows(pos_ref, posn_ref, ys_ref, gbuf_ref, sem)
    issue_next(0, 1)
    x = x_ref[...] + mod_ref[0, :, 5 * d:6 * d] * rows
    o_ref[...] = x * lax.rsqrt(jnp.mean(x * x, axis=-1, keepdims=True) + EPS) * fg_ref[...]
    finish()


def _combine(pos3, xa, mod_l, fg, ys, n_lat, seq, n_batch):
    n, d = xa.shape
    tm = ROW_TILE
    steps = n // tm
    mrow = _mod_row_map(tm, n_lat, seq, n_batch)
    return pl.pallas_call(
        _combine_kernel,
        grid=(steps,),
        in_specs=[
            pl.BlockSpec((1, 1, tm), lambda i: (i, 0, 0), memory_space=pltpu.SMEM),
            pl.BlockSpec((1, 1, tm), lambda i: (jnp.minimum(i + 1, steps - 1), 0, 0), memory_space=pltpu.SMEM),
            pl.BlockSpec((tm, d), lambda i: (i, 0)),
            pl.BlockSpec((1, 1, 6 * d), lambda i: (mrow(i), 0, 0)),
            pl.BlockSpec((1, d), lambda i: (0, 0)),
            pl.BlockSpec(memory_space=pl.ANY),
        ],
        out_specs=pl.BlockSpec((tm, d), lambda i: (i, 0)),
        out_shape=jax.ShapeDtypeStruct((n, d), F32),
        scratch_shapes=[pltpu.VMEM((2, tm) + ys.shape[1:], F32), pltpu.SemaphoreType.DMA((2,))],
        compiler_params=_params(1, VMEM_LIMIT),
        name="combine",
    )(pos3, pos3, xa, mod_l, fg, ys)


def _tile_plan(tab, counts, n_tiles, tile):
    cnt = counts[:N_BUCKETS, 0]
    n_chunks = (cnt + tile - 1) // tile
    ends = jnp.cumsum(n_chunks)
    starts = ends - n_chunks
    n_used = ends[-1]
    t = jnp.arange(n_tiles, dtype=I32)
    t_eff = jnp.minimum(t, jnp.maximum(n_used - 1, 0))
    tb = jnp.minimum(jnp.sum((ends[None, :] <= t_eff[:, None]).astype(I32), axis=1), N_BUCKETS - 1)
    owned = tab.astype(I32)[tb, t_eff - starts[tb]]
    blk = jnp.where(t < n_used, owned, t)
    pairs = np.array(PAIR_SLOTS, dtype=np.int32)
    ea = (tb // N_PAIRS) * EXPERTS_PER_GROUP + jnp.asarray(pairs[:, 0])[tb % N_PAIRS]
    eb = (tb // N_PAIRS) * EXPERTS_PER_GROUP + jnp.asarray(pairs[:, 1])[tb % N_PAIRS]
    return blk.astype(I32), ea.astype(I32), eb.astype(I32), n_used.astype(I32).reshape(1)


def kernel(x, c, ctx, c_ctx, w_mod, b_mod, norm_g, w_in_even, w_out_even, ret_log_decay, ret_gn_g, sg_ln_g, sg_ln_b, sg_w, sg_b, w_qkv_odd, w_o_odd, attn_sink, w_router, b_router, w_e_gate, w_e_up, w_e_down, final_g):
    n_batch, seq, d = x.shape
    ctx_len = ctx.shape[1]
    depth = w_mod.shape[0]
    n_lat = n_batch * seq
    n_all = n_lat + n_batch * ctx_len
    tm = ROW_TILE
    assert seq % tm == 0 and (n_batch * ctx_len) % tm == 0

    mod_rows = -(-(n_batch + 1) // 16) * 16
    c_rows = jnp.zeros((mod_rows, d), F32).at[:n_batch].set(c).at[n_batch].set(c_ctx)
    mod = _modulation(c_rows, w_mod, b_mod).reshape(depth, mod_rows, 1, 6 * d)

    wr_t = w_router.T.astype(F32)
    wr_bf = wr_t.astype(BF16)
    br_t = jnp.broadcast_to(b_router.astype(F32)[:, None], (N_EXPERTS, 128))
    tri = jnp.triu(jnp.ones((tm, tm), BF16), k=1)
    low = jnp.tril(jnp.ones((BUCKET_ROWS, BUCKET_ROWS), BF16), k=-1)
    cos_t, sin_t = _rope_tables(seq, tm)
    bias = _attn_bias()

    n_exp, d_exp = w_e_gate.shape[1], w_e_gate.shape[3]
    assert d_exp == d
    w_views = [w.reshape(depth, n_exp * d, d) for w in (w_e_gate, w_e_up, w_e_down)]

    n_tiles = n_all // MOE_TILE + N_BUCKETS
    p_rows = n_tiles * MOE_TILE
    hs = jnp.zeros((p_rows + tm, ROW_SUB, d // ROW_SUB), F32)

    source = (x.reshape(n_lat, d), ctx.reshape(n_batch * ctx_len, d))
    for l in range(depth):
        i = l // 2
        last = l == depth - 1
        mod_l = mod[l]
        g1 = norm_g[l, 0].reshape(1, d)
        g2 = norm_g[l, 1].reshape(1, d)
        if l % 2 == 0:
            xa, p = _in_stage(source, mod_l, g1, w_in_even[i].astype(BF16), None, n_lat, seq, n_batch)
            tabs = _retention_tables(ret_log_decay[i], RET_BLOCK)
            sg_bias = jnp.repeat(sg_b[i].astype(F32).T, SG_CHUNK, axis=1)
            mix = _even_mix(p, tabs, ret_gn_g[i].reshape(1, -1), sg_ln_g[i].reshape(1, -1), sg_ln_b[i].reshape(1, -1),
                            sg_w[i].astype(BF16), sg_bias, n_batch, seq, ctx_len)
            wo = w_out_even[i].astype(BF16)
        else:
            xa, q, kv = _in_stage(source, mod_l, g1, w_qkv_odd[i].astype(BF16), (cos_t, sin_t), n_lat, seq, n_batch)
            sink_t = jnp.broadcast_to(attn_sink[i].astype(F32)[:, None], (ATT_HEADS, 128))
            mix = (_attention(q, kv, bias, sink_t, n_batch, seq, ctx_len, need_ctx=not last),)
            wo = w_o_odd[i].astype(BF16)
        n_act = n_lat if last else n_all
        xm, pos, tab, counts, hs, wg, wu, wd = _post(l, mix, xa, mod_l, g2, wo, wr_bf, br_t, tri, low, hs, *w_views,
                                                     n_act, n_lat, seq, n_batch)
        blk, ea, eb, n_used = _tile_plan(tab, counts, n_tiles, MOE_TILE)
        pos3 = pos.reshape(n_act // tm, 1, tm)
        ys = _moe(blk, ea, eb, n_used, hs, n_tiles, *(w.reshape(n_exp, d, d) for w in (wg, wu, wd)), wr_t, br_t)
        source = (pos3, xm, mod_l, ys)
    out = _combine(pos3, xm, mod_l, final_g.reshape(1, d), ys, n_lat, seq, n_batch)
    return out.reshape(n_batch, seq, d)
```

```python
import functools

import jax
import jax.numpy as jnp
import numpy as np
from jax import lax
from jax.experimental import pallas as pl
from jax.experimental.pallas import tpu as pltpu

F32 = jnp.float32
BF16 = jnp.bfloat16
I32 = jnp.int32

EPS = 1e-6
NEG = -1e30
LOG2E = 1.4426950408889634
GRID_W = 64
ROPE_BASE = 10000.0
RET_HEADS = 4
RET_QK_DIM = 64
RET_V_DIM = 128
SG_GROUPS = 4
SG_CHUNK = 128
RET_BLOCK = 256
ATT_HEADS = 8
ATT_KV_HEADS = 2
ATT_GROUP = ATT_HEADS // ATT_KV_HEADS
HEAD_DIM = 128
ATT_BLOCK = 128
WINDOW = 128
ATT_SUB = 2
ATT_SLAB = 32
N_EXPERTS = 16
N_GROUPS = 4
EXPERTS_PER_GROUP = 4
N_PAIRS = 6
N_BUCKETS = N_GROUPS * N_PAIRS
PAIR_SLOTS = ((0, 1), (0, 2), (0, 3), (1, 3), (1, 2), (3, 2))
BUCKET_ROWS = 32

ROW_TILE = 512
MOE_TILE = 256
ROW_SUB = 8
VMEM_LIMIT = 56 * 1024 * 1024

_NT = (((1,), (1,)), ((), ()))
_TN = (((0,), (0,)), ((), ()))


def _params(n_axes, vmem=None):
    return pltpu.CompilerParams(dimension_semantics=("arbitrary",) * n_axes, vmem_limit_bytes=vmem)


def _silu(v):
    return v * jax.nn.sigmoid(v)


def _rms_mod(x, g, scale, shift):
    y = x * lax.rsqrt(jnp.mean(x * x, axis=-1, keepdims=True) + EPS) * g
    return y * (1.0 + scale) + shift


def _mod_kernel(c_ref, w_ref, b_ref, o_ref):
    a = _silu(c_ref[...]).astype(BF16)
    o_ref[0] = jnp.dot(a, w_ref[0].astype(BF16), preferred_element_type=F32) + b_ref[0]


def _modulation(c_rows, w_mod, b_mod):
    depth, d, six_d = w_mod.shape
    mr = c_rows.shape[0]
    tn = 1536
    return pl.pallas_call(
        _mod_kernel,
        grid=(depth, six_d // tn),
        in_specs=[
            pl.BlockSpec((mr, d), lambda l, j: (0, 0)),
            pl.BlockSpec((1, d, tn), lambda l, j: (l, 0, j)),
            pl.BlockSpec((1, 1, tn), lambda l, j: (l, 0, j)),
        ],
        out_specs=pl.BlockSpec((1, mr, tn), lambda l, j: (l, 0, j)),
        out_shape=jax.ShapeDtypeStruct((depth, mr, six_d), F32),
        compiler_params=_params(2, VMEM_LIMIT),
        name="modulation",
    )(c_rows, w_mod, b_mod.reshape(depth, 1, six_d))


def _mod_row_map(tm, n_lat, seq, n_batch):
    def index(i):
        row0 = i * tm
        return jnp.where(row0 < n_lat, row0 // seq, n_batch)
    return index


def _gather_rows(pos_ref, posn_ref, ys_ref, gbuf_ref, sem):
    i = pl.program_id(0)
    slot = i % 2
    tm = gbuf_ref.shape[1]

    def issue(p_ref, s, lo, hi):
        for r in range(lo, hi):
            pltpu.make_async_copy(ys_ref.at[p_ref[0, 0, r]], gbuf_ref.at[s, r], sem.at[s]).start(priority=r % 2)

    def wait(s):
        pltpu.make_async_copy(ys_ref.at[pl.ds(0, tm)], gbuf_ref.at[s], sem.at[s]).wait()

    @pl.when(i == 0)
    def _():
        issue(pos_ref, 0, 0, tm)

    wait(slot)
    rows = gbuf_ref[slot].reshape(tm, gbuf_ref.shape[2] * gbuf_ref.shape[3])

    def issue_next(k, parts):
        issue(posn_ref, 1 - slot, tm * k // parts, tm * (k + 1) // parts)

    def finish():
        @pl.when(i == pl.num_programs(0) - 1)
        def _():
            wait(1 - slot)

    return rows, issue_next, finish


def _project_even(h, w_ref, o_ref, between, cast):
    for c in range(5):
        between(c, 5)
        cast(c)
        p = jnp.dot(h, w_ref[:, 512 * c:512 * (c + 1)], preferred_element_type=F32)
        if c == 0:
            col = lax.broadcasted_iota(I32, p.shape, 1)
            p = jnp.where(col >= 256, p * (RET_QK_DIM ** -0.5), p)
        elif c >= 3:
            p = jax.nn.gelu(p)
        o_ref[:, 512 * c:512 * (c + 1)] = p.astype(BF16)


def _project_odd(h, w_ref, cos_ref, sin_ref, q_ref, kv_ref, between, cast):
    tm = h.shape[0]
    cos = cos_ref[...]
    sin = sin_ref[...]
    lane = lax.broadcasted_iota(I32, (tm, HEAD_DIM), 1)
    first = (lane % 64) < 32

    def rope(p):
        partner = jnp.where(first, pltpu.roll(p, 96, 1), pltpu.roll(p, 32, 1))
        return p * cos + partner * sin

    scale = HEAD_DIM ** -0.5 * LOG2E
    for c in range(3):
        if c < 2:
            between(c, 2)
        cast(c)
        p = jnp.dot(h, w_ref[:, 512 * c:512 * (c + 1)], preferred_element_type=F32)
        for j in range(4):
            pj = p[:, 128 * j:128 * (j + 1)]
            if c < 2:
                q_ref[:, 512 * c + 128 * j:512 * c + 128 * (j + 1)] = (rope(pj) * scale).astype(BF16)
            elif j < 2:
                kv_ref[:, 128 * j:128 * (j + 1)] = rope(pj).astype(BF16)
            else:
                kv_ref[:, 128 * j:128 * (j + 1)] = pj.astype(BF16)


def _in_kernel(*refs, first, odd, lat_tiles):
    if first:
        (xl_ref, xc_ref), refs = refs[:2], refs[2:]
        x = jnp.where(pl.program_id(0) < lat_tiles, xl_ref[...], xc_ref[...])
        between, finish = (lambda k, parts: None), (lambda: None)
    else:
        (pos_ref, posn_ref, x_ref, modp_ref, ys_ref), refs = refs[:5], refs[5:]
        gbuf_ref, sem = refs[-2:]
        refs = refs[:-2]
        d = x_ref.shape[1]
        rows, between, finish = _gather_rows(pos_ref, posn_ref, ys_ref, gbuf_ref, sem)
        x = x_ref[...] + modp_ref[0, :, 5 * d:6 * d] * rows
    mod_ref, g_ref, w_ref = refs[:3]
    d = x.shape[1]
    h = _rms_mod(x, g_ref[...], mod_ref[0, :, d:2 * d], mod_ref[0, :, 0:d]).astype(BF16)
    n_tail = 6 if odd else 5
    cast_in, tail = refs[-n_tail - 3:-n_tail], refs[-n_tail:]
    cast_out, tail = tail[-3:], tail[:-3]

    def cast(k):
        if k < 3:
            cast_out[k][...] = cast_in[k][0].astype(BF16)

    if odd:
        xo_ref, q_ref, kv_ref = tail
        cos_ref, sin_ref = refs[3:5]
        xo_ref[...] = x
        _project_odd(h, w_ref, cos_ref, sin_ref, q_ref, kv_ref, between, cast)
    else:
        xo_ref, o_ref = tail
        xo_ref[...] = x
        _project_even(h, w_ref, o_ref, between, cast)
    finish()


def _in_stage(layer, source, mod_l, g, w, rope, w_exp, n_lat, seq, n_batch):
    first = len(source) == 2
    odd = rope is not None
    tm = ROW_TILE
    if first:
        xl, xc = source
        d = xl.shape[1]
        n = xl.shape[0] + xc.shape[0]
        lat_tiles = xl.shape[0] // tm
        src_specs = [pl.BlockSpec((tm, d), lambda i: (jnp.minimum(i, lat_tiles - 1), 0)),
                     pl.BlockSpec((tm, d), lambda i: (jnp.maximum(i - lat_tiles, 0), 0))]
        scratch = []
    else:
        pos3, xm, mod_prev, ys = source
        n, d = xm.shape
        lat_tiles = n_lat // tm
        steps = n // tm
        mrow_p = _mod_row_map(tm, n_lat, seq, n_batch)
        src_specs = [pl.BlockSpec((1, 1, tm), lambda i: (i, 0, 0), memory_space=pltpu.SMEM),
                     pl.BlockSpec((1, 1, tm), lambda i: (jnp.minimum(i + 1, steps - 1), 0, 0), memory_space=pltpu.SMEM),
                     pl.BlockSpec((tm, d), lambda i: (i, 0)),
                     pl.BlockSpec((1, 1, 6 * d), lambda i: (mrow_p(i), 0, 0)),
                     pl.BlockSpec(memory_space=pl.ANY)]
        source = (pos3, pos3, xm, mod_prev, ys)
        scratch = [pltpu.VMEM((2, tm) + ys.shape[1:], F32), pltpu.SemaphoreType.DMA((2,))]
    mrow = _mod_row_map(tm, n_lat, seq, n_batch)
    specs = src_specs + [pl.BlockSpec((1, 1, 6 * d), lambda i: (mrow(i), 0, 0)),
                         pl.BlockSpec((1, d), lambda i: (0, 0)),
                         pl.BlockSpec(w.shape, lambda i: (0, 0))]
    args = tuple(source) + (mod_l, g, w)
    out_specs = [pl.BlockSpec((tm, d), lambda i: (i, 0))]
    out_shape = [jax.ShapeDtypeStruct((n, d), F32)]
    if odd:
        tiles_per_seq = seq // tm

        def rope_row(i):
            return jnp.where(i < lat_tiles, i % tiles_per_seq, tiles_per_seq)

        specs += [pl.BlockSpec((tm, HEAD_DIM), lambda i: (rope_row(i), 0))] * 2
        args += tuple(rope)
        widths = (ATT_HEADS * HEAD_DIM, 2 * ATT_KV_HEADS * HEAD_DIM)
    else:
        widths = (w.shape[1],)
    out_specs += [pl.BlockSpec((tm, wd), lambda i: (i, 0)) for wd in widths]
    out_shape += [jax.ShapeDtypeStruct((n, wd), BF16) for wd in widths]
    w_rows, w_cols = w_exp[0].shape[1:]
    cast_steps = 1
    while 2 * cast_steps <= n // tm and w_rows % (2 * cast_steps) == 0:
        cast_steps *= 2
    cast_rows = w_rows // cast_steps
    cast_row = lambda i: jnp.minimum(i, cast_steps - 1)
    specs += [pl.BlockSpec((1, cast_rows, w_cols), lambda i: (layer, cast_row(i), 0))] * 3
    args += tuple(w_exp)
    out_specs += [pl.BlockSpec((cast_rows, w_cols), lambda i: (cast_row(i), 0))] * 3
    out_shape += [jax.ShapeDtypeStruct((w_rows, w_cols), BF16)] * 3
    return pl.pallas_call(
        functools.partial(_in_kernel, first=first, odd=odd, lat_tiles=lat_tiles),
        grid=(n // tm,),
        in_specs=specs,
        out_specs=out_specs,
        out_shape=out_shape,
        scratch_shapes=scratch,
        compiler_params=_params(1, VMEM_LIMIT),
        name="in_odd" if odd else "in_even",
    )(*args)


def _rope_tables(seq, tm):
    t = np.arange(seq)
    quarter = HEAD_DIM // 4
    inv_freq = ROPE_BASE ** (-np.arange(quarter, dtype=np.float64) / quarter)
    ar = (t // GRID_W)[:, None] * inv_freq[None, :]
    ac = (t % GRID_W)[:, None] * inv_freq[None, :]
    cos = np.concatenate([np.cos(ar), np.cos(ar), np.cos(ac), np.cos(ac)], axis=1)
    sin = np.concatenate([-np.sin(ar), np.sin(ar), -np.sin(ac), np.sin(ac)], axis=1)
    cos = np.concatenate([cos, np.ones((tm, HEAD_DIM))], axis=0)
    sin = np.concatenate([sin, np.zeros((tm, HEAD_DIM))], axis=0)
    return jnp.asarray(cos, F32), jnp.asarray(sin, F32)


def _ret_increments(qk_ref, v_ref, rows, zf_ref, zb_ref):
    upper_rows = lax.broadcasted_iota(I32, (128, 128), 0) < RET_QK_DIM
    out = []
    for grp in range(2):
        k2 = qk_ref[rows, 256 + 128 * grp:256 + 128 * (grp + 1)].astype(F32)
        v2 = v_ref[rows, 256 * grp:256 * (grp + 1)]
        halves = []
        for z_ref in (zf_ref, zb_ref):
            kz = (k2 * z_ref[grp]).astype(BF16)
            full = lax.dot_general(kz, v2, _TN, preferred_element_type=F32)
            halves.append(jnp.where(upper_rows, full[:, 0:128], full[:, 128:256]))
        out.append(jnp.concatenate(halves, axis=1))
    return out


def _mix_outputs(qk_ref, v_ref, gate_ref, u_ref, s_ref, rows, states, dec_ref, xi_ref, gn_ref, lng_ref, lnb_ref,
                 ws_ref, sgb_ref, o_ref):
    lane = lax.broadcasted_iota(I32, (1, 128), 1)
    head_mask = [(lane // RET_QK_DIM == hh).astype(BF16) for hh in range(2)]
    for grp in range(2):
        q2 = qk_ref[rows, 128 * grp:128 * (grp + 1)]
        k2 = qk_ref[rows, 256 + 128 * grp:256 + 128 * (grp + 1)]
        for hh in range(2):
            hd = 2 * grp + hh
            cols = slice(128 * hd, 128 * (hd + 1))
            qm = q2 * head_mask[hh]
            sc = lax.dot_general(qm, k2, _NT, preferred_element_type=F32) * dec_ref[hd]
            o = jnp.dot(sc.astype(BF16), v_ref[rows, cols], preferred_element_type=F32)
            if states is not None:
                cross = jnp.dot(qm, states[grp], preferred_element_type=F32) * xi_ref[hd]
                o = o + cross[:, 0:128] + cross[:, 128:256]
            mu = jnp.mean(o, axis=-1, keepdims=True)
            oc = o - mu
            var = jnp.mean(oc * oc, axis=-1, keepdims=True)
            y = oc * lax.rsqrt(var + EPS) * gn_ref[:, cols]
            y = y * _silu(gate_ref[rows, cols].astype(F32))
            o_ref[rows, cols] = y.astype(BF16)
    for sub in range((rows.stop - rows.start) // SG_CHUNK):
        srows = slice(rows.start + SG_CHUNK * sub, rows.start + SG_CHUNK * (sub + 1))
        s = s_ref[srows, :].astype(F32)
        mu = jnp.mean(s, axis=-1, keepdims=True)
        sc_ = s - mu
        var = jnp.mean(sc_ * sc_, axis=-1, keepdims=True)
        sn = (sc_ * lax.rsqrt(var + EPS) * lng_ref[...] + lnb_ref[...]).astype(BF16)
        for g in range(SG_GROUPS):
            cols = slice(128 * g, 128 * (g + 1))
            mixed = jnp.dot(ws_ref[g], sn[:, cols], preferred_element_type=F32) + sgb_ref[:, cols]
            o_ref[srows, 512 + 128 * g:512 + 128 * (g + 1)] = (u_ref[srows, cols].astype(F32) * mixed).astype(BF16)


def _even_lat_kernel(bqk_ref, bv_ref, cqk_ref, cv_ref, qk_ref, v_ref, gate_ref, u_ref, s_ref, dec_ref, xi_ref,
                     zf_ref, zb_ref, gdec_ref, gn_ref, lng_ref, lnb_ref, ws_ref, sgb_ref, o_ref, ds_ref, sp_ref,
                     *, inc_steps):
    j = pl.program_id(1)
    cb = RET_BLOCK
    n_slots = ds_ref.shape[0]
    inc_chunks = bqk_ref.shape[0] // cb
    out_chunks = qk_ref.shape[0] // cb

    @pl.when(j == 0)
    def _():
        inc = _ret_increments(cqk_ref, cv_ref, slice(0, cb), zf_ref, zb_ref)
        for grp in range(2):
            ds_ref[0, grp] = inc[grp]

    @pl.when(j < inc_steps)
    def _():
        for c in range(inc_chunks):
            inc = _ret_increments(bqk_ref, bv_ref, slice(cb * c, cb * (c + 1)), zf_ref, zb_ref)
            for grp in range(2):
                ds_ref[1 + inc_chunks * j + c, grp] = inc[grp]

    @pl.when(j == inc_steps)
    def _scan():
        fwd = ds_ref[0, :, :, 0:128]
        for n in range(1, n_slots):
            sp_ref[n, :, :, 0:128] = fwd
            fwd = fwd * gdec_ref[:, :, 0:128] + ds_ref[n, :, :, 0:128]
        bwd = ds_ref[0, :, :, 128:256]
        for n in range(n_slots - 1, 0, -1):
            sp_ref[n, :, :, 128:256] = bwd
            bwd = bwd * gdec_ref[:, :, 128:256] + ds_ref[n, :, :, 128:256]

    @pl.when(j >= inc_steps)
    def _():
        for c in range(out_chunks):
            n = 1 + out_chunks * (j - inc_steps) + c
            states = [sp_ref[n, grp].astype(BF16) for grp in range(2)]
            _mix_outputs(qk_ref, v_ref, gate_ref, u_ref, s_ref, slice(cb * c, cb * (c + 1)), states, dec_ref, xi_ref,
                         gn_ref, lng_ref, lnb_ref, ws_ref, sgb_ref, o_ref)


def _even_ctx_kernel(qk_ref, v_ref, gate_ref, u_ref, s_ref, dec_ref, xi_ref, gn_ref, lng_ref, lnb_ref, ws_ref, sgb_ref,
                     o_ref):
    _mix_outputs(qk_ref, v_ref, gate_ref, u_ref, s_ref, slice(0, RET_BLOCK), None, dec_ref, xi_ref,
                 gn_ref, lng_ref, lnb_ref, ws_ref, sgb_ref, o_ref)


def _retention_tables(log_decay, cb):
    lg = -jnp.exp(log_decay.astype(F32))
    pos = jnp.arange(cb, dtype=F32)
    diff = pos[:, None] - pos[None, :]
    lower = jnp.where(diff >= 0, jnp.exp(lg[0][:, None, None] * jnp.maximum(diff, 0.0)[None]), 0.0)
    upper = jnp.where(diff <= 0, jnp.exp(lg[1][:, None, None] * jnp.maximum(-diff, 0.0)[None]), 0.0)
    dec = lower + upper
    xi_f = jnp.exp(lg[0][:, None] * (pos[None, :] + 1.0))
    xi_b = jnp.exp(lg[1][:, None] * (cb - pos[None, :]))
    xi = jnp.concatenate([jnp.broadcast_to(xi_f[:, :, None], (RET_HEADS, cb, 128)),
                          jnp.broadcast_to(xi_b[:, :, None], (RET_HEADS, cb, 128))], axis=2)
    zeta_f = jnp.exp(lg[0][:, None] * (cb - 1.0 - pos[None, :]))
    zeta_b = jnp.exp(lg[1][:, None] * pos[None, :])

    def lanes(z):
        return jnp.repeat(z.reshape(2, 2, cb), RET_QK_DIM, axis=1).transpose(0, 2, 1)

    gstep = jnp.exp(lg * cb)

    def rows(gv):
        return jnp.broadcast_to(jnp.repeat(gv.reshape(2, 2), RET_QK_DIM, axis=1)[:, :, None], (2, 128, 128))

    gdec = jnp.concatenate([rows(gstep[0]), rows(gstep[1])], axis=2)
    return dec, xi, lanes(zeta_f), lanes(zeta_b), gdec


def _even_mix(p, tabs, gn_g, ln_g, ln_b, w_s, sg_bias, n_batch, seq, ctx_len):
    cb = RET_BLOCK
    big, blk = 4 * cb, 2 * cb
    assert ctx_len == cb and seq % big == 0
    n_lat = n_batch * seq
    inc_steps, out_steps = seq // big, seq // blk
    dec, xi, zf, zb, gdec = tabs
    consts = (dec, xi, zf, zb, gdec, gn_g, ln_g, ln_b, w_s, sg_bias)
    full2 = lambda a: pl.BlockSpec(a.shape, lambda b, j, _nd=a.ndim: (0,) * _nd)
    inc_row = lambda b, j: b * inc_steps + jnp.minimum(j, inc_steps - 1)
    out_row = lambda b, j: b * out_steps + jnp.maximum(j - inc_steps, 0)
    ctx_row = lambda b, j: n_lat // cb + b
    lat = pl.pallas_call(
        functools.partial(_even_lat_kernel, inc_steps=inc_steps),
        grid=(n_batch, inc_steps + out_steps),
        in_specs=[pl.BlockSpec((big, 512), lambda b, j: (inc_row(b, j), 0)),
                  pl.BlockSpec((big, 512), lambda b, j: (inc_row(b, j), 1)),
                  pl.BlockSpec((cb, 512), lambda b, j: (ctx_row(b, j), 0)),
                  pl.BlockSpec((cb, 512), lambda b, j: (ctx_row(b, j), 1))]
                 + [pl.BlockSpec((blk, 512), lambda b, j, _c=c: (out_row(b, j), _c)) for c in range(5)]
                 + [full2(a) for a in consts],
        out_specs=pl.BlockSpec((blk, 1024), lambda b, j: (out_row(b, j), 0)),
        out_shape=jax.ShapeDtypeStruct((n_lat, 1024), BF16),
        scratch_shapes=[pltpu.VMEM((seq // cb + 1, 2, 128, 256), F32), pltpu.VMEM((seq // cb + 1, 2, 128, 256), F32)],
        compiler_params=_params(2, VMEM_LIMIT),
        name="even_mix",
    )(p, p, p, p, p, p, p, p, p, *consts)
    ctx_consts = (dec, xi, gn_g, ln_g, ln_b, w_s, sg_bias)
    full1 = lambda a: pl.BlockSpec(a.shape, lambda b, _nd=a.ndim: (0,) * _nd)
    ctx = pl.pallas_call(
        _even_ctx_kernel,
        grid=(n_batch,),
        in_specs=[pl.BlockSpec((cb, 512), lambda b, _c=c: (n_lat // cb + b, _c)) for c in range(5)]
                 + [full1(a) for a in ctx_consts],
        out_specs=pl.BlockSpec((cb, 1024), lambda b: (b, 0)),
        out_shape=jax.ShapeDtypeStruct((n_batch * ctx_len, 1024), BF16),
        compiler_params=_params(1, VMEM_LIMIT),
        name="even_mix_ctx",
    )(p, p, p, p, p, *ctx_consts)
    return lat, ctx


def _attn_kernel(q_ref, kvp_ref, kvm_ref, kvn_ref, kvx_ref, bias0_ref, bias1_ref, sink_ref, o_ref, s_scr, p_scr, r_scr):
    blk = ATT_BLOCK
    nloc = 3 * blk
    rows_all = ATT_GROUP * blk
    bias_refs = (bias0_ref, bias1_ref)
    chains = [(kvh, sub) for kvh in range(ATT_KV_HEADS) for sub in range(ATT_SUB)]
    v_all, sinks = {}, {}
    for ci, (kvh, sub) in enumerate(chains):
        kcols = slice(HEAD_DIM * kvh, HEAD_DIM * (kvh + 1))
        vcols = slice(HEAD_DIM * (ATT_KV_HEADS + kvh), HEAD_DIM * (ATT_KV_HEADS + kvh + 1))
        k_parts = [kvp_ref[:, kcols], kvm_ref[0:blk, kcols], kvm_ref[blk:2 * blk, kcols], kvn_ref[:, kcols]]
        v_parts = [kvp_ref[:, vcols], kvm_ref[0:blk, vcols], kvm_ref[blk:2 * blk, vcols], kvn_ref[:, vcols]]
        heads = [ATT_GROUP * kvh + j for j in range(ATT_GROUP)]
        rows = slice(blk * sub, blk * (sub + 1))
        k_all = jnp.concatenate(k_parts[sub:sub + 3] + [kvx_ref[:, kcols]], axis=0)
        v_all[ci] = jnp.concatenate(v_parts[sub:sub + 3] + [kvx_ref[:, vcols]], axis=0)
        q4 = jnp.concatenate([q_ref[rows, HEAD_DIM * hd:HEAD_DIM * (hd + 1)] for hd in heads], axis=0)
        s_scr[ci] = lax.dot_general(q4, k_all, _NT, preferred_element_type=F32)
        sinks[ci] = LOG2E * jnp.concatenate(
            [jnp.broadcast_to(sink_ref[hd:hd + 1, 0:1], (blk, 1)) for hd in heads], axis=0)
    nchunk_loc = nloc // 128
    nchunk = s_scr.shape[2] // 128

    def shifted(ci, sub, rs, k):
        t = s_scr[ci, rs, 128 * k:128 * (k + 1)]
        return t + bias_refs[sub][0, rs, 128 * k:128 * (k + 1)] if k < nchunk_loc else t

    mrow = {}
    for ci, (kvh, sub) in enumerate(chains):
        for r0 in range(0, rows_all, ATT_SLAB):
            rs = slice(r0, r0 + ATT_SLAB)
            part = shifted(ci, sub, rs, 0)
            for k in range(1, nchunk):
                part = jnp.maximum(part, shifted(ci, sub, rs, k))
            r_scr[ci, rs, :] = part
    for ci in range(len(chains)):
        m = jnp.maximum(jnp.max(r_scr[ci], axis=-1, keepdims=True), sinks[ci])
        mrow[ci] = m
        r_scr[ci] = jnp.broadcast_to(m, r_scr.shape[1:])
    for ci, (kvh, sub) in enumerate(chains):
        for r0 in range(0, rows_all, ATT_SLAB):
            rs = slice(r0, r0 + ATT_SLAB)
            mb = r_scr[ci, rs, :]
            part = None
            for k in range(nchunk):
                p = jnp.exp2(shifted(ci, sub, rs, k) - mb)
                part = p if part is None else part + p
                p_scr[ci, rs, 128 * k:128 * (k + 1)] = p.astype(BF16)
            r_scr[ci, rs, :] = part
    inv = {}
    for ci in range(len(chains)):
        inv[ci] = 1.0 / (jnp.sum(r_scr[ci], axis=-1, keepdims=True) + jnp.exp2(sinks[ci] - mrow[ci]))
    for ci, (kvh, sub) in enumerate(chains):
        rows = slice(blk * sub, blk * (sub + 1))
        o = (jnp.dot(p_scr[ci, :, 0:nloc], v_all[ci][0:nloc], preferred_element_type=F32)
             + jnp.dot(p_scr[ci, :, nloc:], v_all[ci][nloc:], preferred_element_type=F32)) * inv[ci]
        for j in range(ATT_GROUP):
            hd = ATT_GROUP * kvh + j
            o_ref[rows, HEAD_DIM * hd:HEAD_DIM * (hd + 1)] = o[blk * j:blk * (j + 1)].astype(BF16)


def _attn_bias():
    i = np.arange(ATT_BLOCK)[:, None]
    j = np.arange(3 * ATT_BLOCK)[None, :] - ATT_BLOCK
    band = np.abs(i - j) <= WINDOW
    cases = [band & (j >= 0), band, band & (j < ATT_BLOCK), np.zeros_like(band)]
    bias = np.stack([np.where(c, 0.0, NEG) for c in cases]).astype(np.float32)
    return jnp.asarray(np.tile(bias, (1, ATT_GROUP, 1)))


def _attention(q, kv, bias, sink_t, n_batch, seq, ctx_len, need_ctx):
    blk = ATT_BLOCK
    big = ATT_SUB * blk
    assert ATT_SUB == 2 and seq % big == 0 and ctx_len == big
    nb = seq // blk
    ns = seq // big
    n_lat = n_batch * seq
    steps = ns + (1 if need_ctx else 0)
    n_out = n_lat + (n_batch * ctx_len if need_ctx else 0)

    def qrow(b, s):
        return jnp.where(s < ns, b * ns + s, n_lat // big + b)

    def small(off):
        def index(b, s):
            loc = jnp.clip(2 * jnp.minimum(s, ns - 1) + off, 0, nb - 1)
            return (b * nb + loc, 0)
        return index

    def case0(b, s):
        return (jnp.where(s < ns, jnp.where(s == 0, 0, 1), 3), 0, 0)

    def case1(b, s):
        return (jnp.where(s < ns, jnp.where(s == ns - 1, 2, 1), 3), 0, 0)

    return pl.pallas_call(
        _attn_kernel,
        grid=(n_batch, steps),
        in_specs=[
            pl.BlockSpec((big, q.shape[1]), lambda b, s: (qrow(b, s), 0)),
            pl.BlockSpec((blk, kv.shape[1]), small(-1)),
            pl.BlockSpec((big, kv.shape[1]), lambda b, s: (b * ns + jnp.minimum(s, ns - 1), 0)),
            pl.BlockSpec((blk, kv.shape[1]), small(2)),
            pl.BlockSpec((ctx_len, kv.shape[1]), lambda b, s: (n_lat // ctx_len + b, 0)),
            pl.BlockSpec((1,) + bias.shape[1:], case0),
            pl.BlockSpec((1,) + bias.shape[1:], case1),
            pl.BlockSpec(sink_t.shape, lambda b, s: (0, 0)),
        ],
        out_specs=pl.BlockSpec((big, q.shape[1]), lambda b, s: (qrow(b, s), 0)),
        out_shape=jax.ShapeDtypeStruct((n_out, q.shape[1]), BF16),
        scratch_shapes=[pltpu.VMEM((ATT_KV_HEADS * ATT_SUB, ATT_GROUP * blk, 3 * blk + ctx_len), F32),
                        pltpu.VMEM((ATT_KV_HEADS * ATT_SUB, ATT_GROUP * blk, 3 * blk + ctx_len), BF16),
                        pltpu.VMEM((ATT_KV_HEADS * ATT_SUB, ATT_GROUP * blk, 128), F32)],
        compiler_params=_params(2, VMEM_LIMIT),
        name="window_attention",
    )(q, kv, kv, kv, kv, bias, bias, sink_t)


def _route(logits):
    m = jnp.max(logits, axis=0, keepdims=True)
    e = jnp.exp(logits - m)
    p = e / jnp.sum(e, axis=0, keepdims=True)
    rows = [p[i:i + 1, :] for i in range(N_EXPERTS)]
    best = None
    gsel = None
    for g in range(N_GROUPS):
        a, b, c, d = rows[4 * g:4 * g + 4]
        m1, n1 = jnp.maximum(a, b), jnp.minimum(a, b)
        m2, n2 = jnp.maximum(c, d), jnp.minimum(c, d)
        score = jnp.maximum(m1, m2) + jnp.maximum(jnp.minimum(m1, m2), jnp.maximum(n1, n2))
        if g == 0:
            best, gsel = score, jnp.zeros(score.shape, I32)
        else:
            upd = score > best
            gsel = jnp.where(upd, g, gsel)
            best = jnp.where(upd, score, best)
    v = [jnp.where(gsel == 0, rows[j], jnp.where(gsel == 1, rows[4 + j], jnp.where(gsel == 2, rows[8 + j], rows[12 + j])))
         for j in range(EXPERTS_PER_GROUP)]
    b1, i1 = v[0], jnp.zeros(gsel.shape, I32)
    for j in range(1, 4):
        upd = v[j] > b1
        i1 = jnp.where(upd, j, i1)
        b1 = jnp.where(upd, v[j], b1)
    b2, i2 = jnp.full(b1.shape, -1.0, F32), jnp.zeros(gsel.shape, I32)
    for j in range(4):
        upd = jnp.logical_and(i1 != j, v[j] > b2)
        i2 = jnp.where(upd, j, i2)
        b2 = jnp.where(upd, v[j], b2)
    lo = jnp.minimum(i1, i2)
    hi = jnp.maximum(i1, i2)
    pair = jnp.where(lo == 0, hi - 1, jnp.where(lo == 1, 6 - hi, 5))
    return gsel * N_PAIRS + pair


def _post_kernel(*refs, lat_tiles, first_trash):
    if lat_tiles is None:
        mix_ref, refs = refs[0], refs[1:]
        mix = mix_ref[...]
    else:
        (ml_ref, mc_ref), refs = refs[:2], refs[2:]
        mix = jnp.where(pl.program_id(0) < lat_tiles, ml_ref[...], mc_ref[...])
    (x_ref, mod_ref, g2_ref, wo_ref, wr_ref, br_ref, tri_ref, low_ref, hs_in_ref,
     xo_ref, pos_ref, tab_ref, cnt_ref, hs_ref,
     carry_ref, cur_ref, nfree_ref, hbuf_ref, posv_ref, poss_ref, sem_s, sem_p) = refs
    del hs_in_ref
    i = pl.program_id(0)
    slot = i % 2
    tm, d = x_ref.shape

    @pl.when(i == 0)
    def _():
        carry_ref[...] = jnp.zeros(carry_ref.shape, F32)
        cur_ref[...] = jnp.zeros(cur_ref.shape, F32)
        nfree_ref[...] = jnp.zeros(nfree_ref.shape, F32)
        tab_ref[...] = jnp.zeros(tab_ref.shape, F32)
        hbuf_ref[1] = jnp.zeros(hbuf_ref.shape[1:], F32)
        posv_ref[1] = first_trash + lax.broadcasted_iota(I32, (1, tm), 1)
        pltpu.make_async_copy(posv_ref.at[1], poss_ref.at[1], sem_p.at[1]).start()

    pltpu.make_async_copy(posv_ref.at[1 - slot], poss_ref.at[1 - slot], sem_p.at[1 - slot]).wait()

    def scatter(s, lo, hi):
        for r in range(lo, hi):
            pltpu.make_async_copy(hbuf_ref.at[s, r], hs_ref.at[poss_ref[s, 0, r]], sem_s).start(priority=r % 2)

    def scatter_wait(s):
        pltpu.make_async_copy(hbuf_ref.at[s], hs_ref.at[pl.ds(0, tm)], sem_s).wait()

    nchunk = 4
    wc = d // nchunk
    xs = []
    for c in range(nchunk):
        scatter(1 - slot, tm * c // nchunk, tm * (c + 1) // nchunk)
        cols = slice(wc * c, wc * (c + 1))
        y = jnp.dot(mix, wo_ref[:, cols], preferred_element_type=F32)
        xc = x_ref[:, cols] + mod_ref[0, :, 2 * d + wc * c:2 * d + wc * (c + 1)] * y
        xo_ref[:, cols] = xc
        xs.append(xc)
    x = jnp.concatenate(xs, axis=1)
    h2 = _rms_mod(x, g2_ref[...], mod_ref[0, :, 4 * d:5 * d], mod_ref[0, :, 3 * d:4 * d])
    hbuf_ref[slot] = h2.reshape(hbuf_ref.shape[1:])
    hi = h2.astype(BF16)
    lo = (h2 - hi.astype(F32)).astype(BF16)
    logits = (lax.dot_general(wr_ref[...], hi, _NT, preferred_element_type=F32)
              + lax.dot_general(wr_ref[...], lo, _NT, preferred_element_type=F32)
              + br_ref[:, 0:1])
    bucket = _route(logits)
    ids = lax.broadcasted_iota(I32, (BUCKET_ROWS, tm), 0)
    onehot = (ids == bucket).astype(F32)
    before = jnp.dot(onehot.astype(BF16), tri_ref[...], preferred_element_type=F32)
    inv = 1.0 / MOE_TILE
    c0 = carry_ref[:, 0:1]
    c1 = c0 + jnp.sum(onehot, axis=1, keepdims=True)
    a0 = jnp.floor((c0 + (MOE_TILE - 1)) * inv)
    new = jnp.floor((c1 + (MOE_TILE - 1)) * inv) - a0
    nfree = nfree_ref[0:1, 0:1]
    new_b = jnp.broadcast_to(new, (BUCKET_ROWS, 128)).astype(BF16)
    id1 = nfree + jnp.dot(low_ref[...], new_b, preferred_element_type=F32)[:, 0:1]
    id2 = id1 + 1.0
    cur = cur_ref[:, 0:1]
    rank = before + c0
    k = jnp.floor(rank * inv)
    chunk = jnp.where(k < a0, cur, jnp.where(k == a0, id1, id2))
    pos = jnp.sum(onehot * (chunk * MOE_TILE + (rank - k * MOE_TILE)), axis=0, keepdims=True).astype(I32)
    pos_ref[...] = pos
    posv_ref[slot] = pos
    pltpu.make_async_copy(posv_ref.at[slot], poss_ref.at[slot], sem_p.at[slot]).start()
    carry_ref[...] = jnp.broadcast_to(c1, carry_ref.shape)
    cnt_ref[...] = jnp.broadcast_to(c1, cnt_ref.shape).astype(I32)
    cur_ref[...] = jnp.broadcast_to(jnp.where(new == 0.0, cur, jnp.where(new == 1.0, id1, id2)), cur_ref.shape)
    nfree_ref[...] = jnp.broadcast_to(nfree + jnp.sum(new, axis=0, keepdims=True), nfree_ref.shape)
    kcol = lax.broadcasted_iota(I32, tab_ref.shape, 1).astype(F32)
    tab = tab_ref[...]
    tab = jnp.where(jnp.logical_and(kcol == a0, new >= 1.0), id1, tab)
    tab_ref[...] = jnp.where(jnp.logical_and(kcol == a0 + 1.0, new >= 2.0), id2, tab)
    scatter_wait(1 - slot)

    @pl.when(i == pl.num_programs(0) - 1)
    def _():
        pltpu.make_async_copy(posv_ref.at[slot], poss_ref.at[slot], sem_p.at[slot]).wait()
        scatter(slot, 0, tm)
        scatter_wait(slot)


def _post(mix, xa, mod_l, g2, wo, wr_bf, br_t, tri, low, hs, n_act, n_lat, seq, n_batch):
    d = xa.shape[1]
    tm = ROW_TILE
    steps = n_act // tm
    mrow = _mod_row_map(tm, n_lat, seq, n_batch)
    full = lambda a: pl.BlockSpec(a.shape, lambda i, _nd=a.ndim: (0,) * _nd)
    if len(mix) == 1:
        lat_tiles = None
        mix_specs = [pl.BlockSpec((tm, d), lambda i: (i, 0))]
    else:
        lat_tiles = n_lat // tm
        mix_specs = [pl.BlockSpec((tm, d), lambda i: (jnp.minimum(i, lat_tiles - 1), 0)),
                     pl.BlockSpec((tm, d), lambda i: (jnp.maximum(i - lat_tiles, 0), 0))]
    n_in = len(mix_specs)
    return pl.pallas_call(
        functools.partial(_post_kernel, lat_tiles=lat_tiles, first_trash=hs.shape[0] - tm),
        grid=(steps,),
        in_specs=mix_specs + [
            pl.BlockSpec((tm, d), lambda i: (i, 0)),
            pl.BlockSpec((1, 1, 6 * d), lambda i: (mrow(i), 0, 0)),
            full(g2), full(wo), full(wr_bf), full(br_t), full(tri), full(low),
            pl.BlockSpec(memory_space=pl.ANY),
        ],
        out_specs=[
            pl.BlockSpec((tm, d), lambda i: (i, 0)),
            pl.BlockSpec((1, tm), lambda i: (0, i)),
            pl.BlockSpec((BUCKET_ROWS, 256), lambda i: (0, 0)),
            pl.BlockSpec((BUCKET_ROWS, 128), lambda i: (0, 0)),
            pl.BlockSpec(memory_space=pl.ANY),
        ],
        out_shape=[
            jax.ShapeDtypeStruct((n_act, d), F32),
            jax.ShapeDtypeStruct((1, n_act), I32),
            jax.ShapeDtypeStruct((BUCKET_ROWS, 256), F32),
            jax.ShapeDtypeStruct((BUCKET_ROWS, 128), I32),
            jax.ShapeDtypeStruct(hs.shape, F32),
        ],
        scratch_shapes=[pltpu.VMEM((BUCKET_ROWS, 128), F32), pltpu.VMEM((BUCKET_ROWS, 128), F32), pltpu.VMEM((8, 128), F32),
                        pltpu.VMEM((2, tm) + hs.shape[1:], F32), pltpu.VMEM((2, 1, tm), I32), pltpu.SMEM((2, 1, tm), I32),
                        pltpu.SemaphoreType.DMA(()), pltpu.SemaphoreType.DMA((2,))],
        input_output_aliases={n_in + 8: 4},
        compiler_params=_params(1, VMEM_LIMIT),
        name="post_mixer",
    )(*mix, xa, mod_l, g2, wo, wr_bf, br_t, tri, low, hs)


def _moe_kernel(blk_ref, ea_ref, eb_ref, nu_ref, h_ref, wga_ref, wua_ref, wda_ref, wgb_ref, wub_ref, wdb_ref,
                wrt_ref, brt_ref, o_ref):
    t = pl.program_id(0)

    @pl.when(t >= nu_ref[0])
    def _():
        o_ref[...] = jnp.zeros(o_ref.shape, F32)

    @pl.when(t < nu_ref[0])
    def _():
        ea = ea_ref[t]
        eb = eb_ref[t]
        tm = h_ref.shape[0]
        h = h_ref[...].reshape(tm, h_ref.shape[1] * h_ref.shape[2])
        hb = h.astype(BF16)
        wdiff = wrt_ref[pl.ds(ea, 1), :] - wrt_ref[pl.ds(eb, 1), :]
        bdiff = brt_ref[pl.ds(ea, 1), 0:1] - brt_ref[pl.ds(eb, 1), 0:1]
        wa = jax.nn.sigmoid(jnp.sum(h * wdiff, axis=-1, keepdims=True) + bdiff)

        slots = ((wga_ref, wua_ref, wda_ref), (wgb_ref, wub_ref, wdb_ref))
        f = wga_ref.shape[2]
        chains = [(e, c) for e in range(2) for c in range(f // 512)]
        gu = {}
        for e, c in chains:
            cols = slice(512 * c, 512 * (c + 1))
            gu[e, c] = (jnp.dot(hb, slots[e][0][0, :, cols], preferred_element_type=F32),
                        jnp.dot(hb, slots[e][1][0, :, cols], preferred_element_type=F32))
        act = {k: (_silu(g) * u).astype(BF16) for k, (g, u) in gu.items()}
        outs = [None, None]
        for e, c in chains:
            part = jnp.dot(act[e, c], slots[e][2][0, 512 * c:512 * (c + 1), :], preferred_element_type=F32)
            outs[e] = part if outs[e] is None else outs[e] + part
        y = wa * outs[0] + (1.0 - wa) * outs[1]
        o_ref[...] = y.reshape(o_ref.shape)


def _moe(blk, ea, eb, nu, hs, n_tiles, wg, wu, wd, wr_t, br_t):
    row = hs.shape[1:]
    tm = MOE_TILE
    d, f = wg.shape[1:]
    wa_spec = lambda shape: pl.BlockSpec(shape, lambda t, blk_, ea_, eb_, nu_: (ea_[t], 0, 0))
    wb_spec = lambda shape: pl.BlockSpec(shape, lambda t, blk_, ea_, eb_, nu_: (eb_[t], 0, 0))
    grid_spec = pltpu.PrefetchScalarGridSpec(
        num_scalar_prefetch=4,
        grid=(n_tiles,),
        in_specs=[
            pl.BlockSpec((tm,) + row, lambda t, blk_, ea_, eb_, nu_: (blk_[t], 0, 0)),
            wa_spec((1, d, f)), wa_spec((1, d, f)), wa_spec((1, f, d)),
            wb_spec((1, d, f)), wb_spec((1, d, f)), wb_spec((1, f, d)),
            pl.BlockSpec(wr_t.shape, lambda t, blk_, ea_, eb_, nu_: (0, 0)),
            pl.BlockSpec(br_t.shape, lambda t, blk_, ea_, eb_, nu_: (0, 0)),
        ],
        out_specs=pl.BlockSpec((tm,) + row, lambda t, blk_, ea_, eb_, nu_: (blk_[t], 0, 0)),
    )
    return pl.pallas_call(
        _moe_kernel,
        grid_spec=grid_spec,
        out_shape=jax.ShapeDtypeStruct((n_tiles * tm,) + row, F32),
        compiler_params=_params(1, VMEM_LIMIT),
        name="experts",
    )(blk, ea, eb, nu, hs, wg, wu, wd, wg, wu, wd, wr_t, br_t)


def _combine_kernel(pos_ref, posn_ref, x_ref, mod_ref, fg_ref, ys_ref, o_ref, gbuf_ref, sem):
    d = x_ref.shape[1]
    rows, issue_next, finish = _gather_rows(pos_ref, posn_ref, ys_ref, gbuf_ref, sem)
    issue_next(0, 1)
    x = x_ref[...] + mod_ref[0, :, 5 * d:6 * d] * rows
    o_ref[...] = x * lax.rsqrt(jnp.mean(x * x, axis=-1, keepdims=True) + EPS) * fg_ref[...]
    finish()


def _combine(pos3, xa, mod_l, fg, ys, n_lat, seq, n_batch):
    n, d = xa.shape
    tm = ROW_TILE
    steps = n // tm
    mrow = _mod_row_map(tm, n_lat, seq, n_batch)
    return pl.pallas_call(
        _combine_kernel,
        grid=(steps,),
        in_specs=[
            pl.BlockSpec((1, 1, tm), lambda i: (i, 0, 0), memory_space=pltpu.SMEM),
            pl.BlockSpec((1, 1, tm), lambda i: (jnp.minimum(i + 1, steps - 1), 0, 0), memory_space=pltpu.SMEM),
            pl.BlockSpec((tm, d), lambda i: (i, 0)),
            pl.BlockSpec((1, 1, 6 * d), lambda i: (mrow(i), 0, 0)),
            pl.BlockSpec((1, d), lambda i: (0, 0)),
            pl.BlockSpec(memory_space=pl.ANY),
        ],
        out_specs=pl.BlockSpec((tm, d), lambda i: (i, 0)),
        out_shape=jax.ShapeDtypeStruct((n, d), F32),
        scratch_shapes=[pltpu.VMEM((2, tm) + ys.shape[1:], F32), pltpu.SemaphoreType.DMA((2,))],
        compiler_params=_params(1, VMEM_LIMIT),
        name="combine",
    )(pos3, pos3, xa, mod_l, fg, ys)


def _tile_plan(tab, counts, n_tiles, tile):
    cnt = counts[:N_BUCKETS, 0]
    n_chunks = (cnt + tile - 1) // tile
    ends = jnp.cumsum(n_chunks)
    starts = ends - n_chunks
    n_used = ends[-1]
    t = jnp.arange(n_tiles, dtype=I32)
    t_eff = jnp.minimum(t, jnp.maximum(n_used - 1, 0))
    tb = jnp.minimum(jnp.sum((ends[None, :] <= t_eff[:, None]).astype(I32), axis=1), N_BUCKETS - 1)
    owned = tab.astype(I32)[tb, t_eff - starts[tb]]
    blk = jnp.where(t < n_used, owned, t)
    pairs = np.array(PAIR_SLOTS, dtype=np.int32)
    ea = (tb // N_PAIRS) * EXPERTS_PER_GROUP + jnp.asarray(pairs[:, 0])[tb % N_PAIRS]
    eb = (tb // N_PAIRS) * EXPERTS_PER_GROUP + jnp.asarray(pairs[:, 1])[tb % N_PAIRS]
    return blk.astype(I32), ea.astype(I32), eb.astype(I32), n_used.astype(I32).reshape(1)


def kernel(x, c, ctx, c_ctx, w_mod, b_mod, norm_g, w_in_even, w_out_even, ret_log_decay, ret_gn_g, sg_ln_g, sg_ln_b, sg_w, sg_b, w_qkv_odd, w_o_odd, attn_sink, w_router, b_router, w_e_gate, w_e_up, w_e_down, final_g):
    n_batch, seq, d = x.shape
    ctx_len = ctx.shape[1]
    depth = w_mod.shape[0]
    n_lat = n_batch * seq
    n_all = n_lat + n_batch * ctx_len
    tm = ROW_TILE
    assert seq % tm == 0 and (n_batch * ctx_len) % tm == 0

    mod_rows = -(-(n_batch + 1) // 16) * 16
    c_rows = jnp.zeros((mod_rows, d), F32).at[:n_batch].set(c).at[n_batch].set(c_ctx)
    mod = _modulation(c_rows, w_mod, b_mod).reshape(depth, mod_rows, 1, 6 * d)

    wr_t = w_router.T.astype(F32)
    wr_bf = wr_t.astype(BF16)
    br_t = jnp.broadcast_to(b_router.astype(F32)[:, None], (N_EXPERTS, 128))
    tri = jnp.asarray(np.triu(np.ones((tm, tm), np.float32), k=1), BF16)
    low = jnp.asarray(np.tril(np.ones((BUCKET_ROWS, BUCKET_ROWS), np.float32), k=-1), BF16)
    cos_t, sin_t = _rope_tables(seq, tm)
    bias = _attn_bias()

    n_exp, d_exp = w_e_gate.shape[1], w_e_gate.shape[3]
    assert d_exp == d
    w_views = [w.reshape(depth, n_exp * d, d) for w in (w_e_gate, w_e_up, w_e_down)]

    n_tiles = n_all // MOE_TILE + N_BUCKETS
    p_rows = n_tiles * MOE_TILE
    hs = jnp.zeros((p_rows + tm, ROW_SUB, d // ROW_SUB), F32)

    source = (x.reshape(n_lat, d), ctx.reshape(n_batch * ctx_len, d))
    for l in range(depth):
        i = l // 2
        last = l == depth - 1
        mod_l = mod[l]
        g1 = norm_g[l, 0].reshape(1, d)
        g2 = norm_g[l, 1].reshape(1, d)
        if l % 2 == 0:
            xa, p, wg, wu, wd = _in_stage(l, source, mod_l, g1, w_in_even[i].astype(BF16), None, w_views, n_lat, seq, n_batch)
            tabs = _retention_tables(ret_log_decay[i], RET_BLOCK)
            sg_bias = jnp.repeat(sg_b[i].astype(F32).T, SG_CHUNK, axis=1)
            mix = _even_mix(p, tabs, ret_gn_g[i].reshape(1, -1), sg_ln_g[i].reshape(1, -1), sg_ln_b[i].reshape(1, -1),
                            sg_w[i].astype(BF16), sg_bias, n_batch, seq, ctx_len)
            wo = w_out_even[i].astype(BF16)
        else:
            xa, q, kv, wg, wu, wd = _in_stage(l, source, mod_l, g1, w_qkv_odd[i].astype(BF16), (cos_t, sin_t), w_views,
                                              n_lat, seq, n_batch)
            sink_t = jnp.broadcast_to(attn_sink[i].astype(F32)[:, None], (ATT_HEADS, 128))
            mix = (_attention(q, kv, bias, sink_t, n_batch, seq, ctx_len, need_ctx=not last),)
            wo = w_o_odd[i].astype(BF16)
        n_act = n_lat if last else n_all
        xm, pos, tab, counts, hs = _post(mix, xa, mod_l, g2, wo, wr_bf, br_t, tri, low, hs, n_act, n_lat, seq, n_batch)
        blk, ea, eb, n_used = _tile_plan(tab, counts, n_tiles, MOE_TILE)
        pos3 = pos.reshape(n_act // tm, 1, tm)
        ys = _moe(blk, ea, eb, n_used, hs, n_tiles, *(w.reshape(n_exp, d, d) for w in (wg, wu, wd)), wr_t, br_t)
        source = (pos3, xm, mod_l, ys)
    out = _combine(pos3, xm, mod_l, final_g.reshape(1, d), ys, n_lat, seq, n_batch)
    return out.reshape(n_batch, seq, d)
```

```python
import functools

import jax
import jax.numpy as jnp
import numpy as np
from jax import lax
from jax.experimental import pallas as pl
from jax.experimental.pallas import tpu as pltpu

F32 = jnp.float32
BF16 = jnp.bfloat16
I32 = jnp.int32

EPS = 1e-6
NEG = -1e30
LOG2E = 1.4426950408889634
GRID_W = 64
ROPE_BASE = 10000.0
RET_HEADS = 4
RET_QK_DIM = 64
RET_V_DIM = 128
SG_GROUPS = 4
SG_CHUNK = 128
RET_BLOCK = 256
ATT_HEADS = 8
ATT_KV_HEADS = 2
ATT_GROUP = ATT_HEADS // ATT_KV_HEADS
HEAD_DIM = 128
ATT_BLOCK = 128
WINDOW = 128
ATT_SUB = 4
ATT_SLAB = 32
N_EXPERTS = 16
N_GROUPS = 4
EXPERTS_PER_GROUP = 4
N_PAIRS = 6
N_BUCKETS = N_GROUPS * N_PAIRS
PAIR_SLOTS = ((0, 1), (0, 2), (0, 3), (1, 3), (1, 2), (3, 2))
BUCKET_ROWS = 32

ROW_TILE = 512
MOE_TILE = 256
ROW_SUB = 8
VMEM_LIMIT = 56 * 1024 * 1024

_NT = (((1,), (1,)), ((), ()))
_TN = (((0,), (0,)), ((), ()))


def _params(n_axes, vmem=None):
    return pltpu.CompilerParams(dimension_semantics=("arbitrary",) * n_axes, vmem_limit_bytes=vmem)


def _silu(v):
    return v * jax.nn.sigmoid(v)


def _rms_mod(x, g, scale, shift):
    y = x * lax.rsqrt(jnp.mean(x * x, axis=-1, keepdims=True) + EPS) * g
    return y * (1.0 + scale) + shift


def _mod_kernel(c_ref, w_ref, b_ref, o_ref):
    a = _silu(c_ref[...]).astype(BF16)
    o_ref[0] = jnp.dot(a, w_ref[0].astype(BF16), preferred_element_type=F32) + b_ref[0]


def _modulation(c_rows, w_mod, b_mod):
    depth, d, six_d = w_mod.shape
    mr = c_rows.shape[0]
    tn = 1536
    return pl.pallas_call(
        _mod_kernel,
        grid=(depth, six_d // tn),
        in_specs=[
            pl.BlockSpec((mr, d), lambda l, j: (0, 0)),
            pl.BlockSpec((1, d, tn), lambda l, j: (l, 0, j)),
            pl.BlockSpec((1, 1, tn), lambda l, j: (l, 0, j)),
        ],
        out_specs=pl.BlockSpec((1, mr, tn), lambda l, j: (l, 0, j)),
        out_shape=jax.ShapeDtypeStruct((depth, mr, six_d), F32),
        compiler_params=_params(2, VMEM_LIMIT),
        name="modulation",
    )(c_rows, w_mod, b_mod.reshape(depth, 1, six_d))


def _cast_specs(w, layer, steps):
    w_rows, w_cols = w.shape[1:]
    cast_steps = 1
    while 2 * cast_steps <= steps and w_rows % (2 * cast_steps) == 0:
        cast_steps *= 2
    cast_rows = w_rows // cast_steps
    row = lambda i, *_: jnp.minimum(i, cast_steps - 1)
    return (pl.BlockSpec((1, cast_rows, w_cols), lambda i, *_: (layer, row(i), 0)),
            pl.BlockSpec((cast_rows, w_cols), lambda i, *_: (row(i), 0)),
            jax.ShapeDtypeStruct((w_rows, w_cols), BF16))


def _mod_row_map(tm, n_lat, seq, n_batch):
    def index(i):
        row0 = i * tm
        return jnp.where(row0 < n_lat, row0 // seq, n_batch)
    return index


def _gather_rows(pos_ref, posn_ref, ys_ref, gbuf_ref, sem):
    i = pl.program_id(0)
    slot = i % 2
    tm = gbuf_ref.shape[1]

    def issue(p_ref, s, lo, hi):
        for r in range(lo, hi):
            pltpu.make_async_copy(ys_ref.at[p_ref[0, 0, r]], gbuf_ref.at[s, r], sem.at[s]).start(priority=r % 2)

    def wait(s):
        pltpu.make_async_copy(ys_ref.at[pl.ds(0, tm)], gbuf_ref.at[s], sem.at[s]).wait()

    @pl.when(i == 0)
    def _():
        issue(pos_ref, 0, 0, tm)

    wait(slot)
    rows = gbuf_ref[slot].reshape(tm, gbuf_ref.shape[2] * gbuf_ref.shape[3])

    def issue_next(k, parts):
        issue(posn_ref, 1 - slot, tm * k // parts, tm * (k + 1) // parts)

    def finish():
        @pl.when(i == pl.num_programs(0) - 1)
        def _():
            wait(1 - slot)

    return rows, issue_next, finish


def _project_even(h, w_ref, o_ref, between, cast):
    for c in range(5):
        between(c, 5)
        cast(c)
        p = jnp.dot(h, w_ref[:, 512 * c:512 * (c + 1)], preferred_element_type=F32)
        if c == 0:
            col = lax.broadcasted_iota(I32, p.shape, 1)
            p = jnp.where(col >= 256, p * (RET_QK_DIM ** -0.5), p)
        elif c >= 3:
            p = jax.nn.gelu(p)
        o_ref[:, 512 * c:512 * (c + 1)] = p.astype(BF16)


def _project_odd(h, w_ref, cos_ref, sin_ref, q_ref, kv_ref, between, cast):
    tm = h.shape[0]
    cos = cos_ref[...]
    sin = sin_ref[...]
    lane = lax.broadcasted_iota(I32, (tm, HEAD_DIM), 1)
    first = (lane % 64) < 32

    def rope(p):
        partner = jnp.where(first, pltpu.roll(p, 96, 1), pltpu.roll(p, 32, 1))
        return p * cos + partner * sin

    scale = HEAD_DIM ** -0.5 * LOG2E
    for c in range(3):
        if c < 2:
            between(c, 2)
        cast(c)
        p = jnp.dot(h, w_ref[:, 512 * c:512 * (c + 1)], preferred_element_type=F32)
        for j in range(4):
            pj = p[:, 128 * j:128 * (j + 1)]
            if c < 2:
                q_ref[:, 512 * c + 128 * j:512 * c + 128 * (j + 1)] = (rope(pj) * scale).astype(BF16)
            elif j < 2:
                kv_ref[:, 128 * j:128 * (j + 1)] = rope(pj).astype(BF16)
            else:
                kv_ref[:, 128 * j:128 * (j + 1)] = pj.astype(BF16)


def _in_kernel(*refs, first, odd, lat_tiles, casts):
    if first:
        (xl_ref, xc_ref), refs = refs[:2], refs[2:]
        x = jnp.where(pl.program_id(0) < lat_tiles, xl_ref[...], xc_ref[...])
        between, finish = (lambda k, parts: None), (lambda: None)
    else:
        (pos_ref, posn_ref, x_ref, modp_ref, ys_ref), refs = refs[:5], refs[5:]
        gbuf_ref, sem = refs[-2:]
        refs = refs[:-2]
        d = x_ref.shape[1]
        rows, between, finish = _gather_rows(pos_ref, posn_ref, ys_ref, gbuf_ref, sem)
        x = x_ref[...] + modp_ref[0, :, 5 * d:6 * d] * rows
    mod_ref, g_ref, w_ref = refs[:3]
    d = x.shape[1]
    h = _rms_mod(x, g_ref[...], mod_ref[0, :, d:2 * d], mod_ref[0, :, 0:d]).astype(BF16)
    n_out = 3 if odd else 2
    if casts:
        cast_in, tail, cast_out = refs[-n_out - 6:-n_out - 3], refs[-n_out - 3:-3], refs[-3:]
    else:
        cast_in, tail, cast_out = (), refs[-n_out:], ()

    def cast(k):
        if k < len(cast_in):
            cast_out[k][...] = cast_in[k][0].astype(BF16)

    if odd:
        xo_ref, q_ref, kv_ref = tail
        cos_ref, sin_ref = refs[3:5]
        xo_ref[...] = x
        _project_odd(h, w_ref, cos_ref, sin_ref, q_ref, kv_ref, between, cast)
    else:
        xo_ref, o_ref = tail
        xo_ref[...] = x
        _project_even(h, w_ref, o_ref, between, cast)
    finish()


def _in_stage(layer, source, mod_l, g, w, rope, w_exp, n_lat, seq, n_batch):
    first = len(source) == 2
    odd = rope is not None
    tm = ROW_TILE
    if first:
        xl, xc = source
        d = xl.shape[1]
        n = xl.shape[0] + xc.shape[0]
        lat_tiles = xl.shape[0] // tm
        src_specs = [pl.BlockSpec((tm, d), lambda i: (jnp.minimum(i, lat_tiles - 1), 0)),
                     pl.BlockSpec((tm, d), lambda i: (jnp.maximum(i - lat_tiles, 0), 0))]
        scratch = []
    else:
        pos3, xm, mod_prev, ys = source
        n, d = xm.shape
        lat_tiles = n_lat // tm
        steps = n // tm
        mrow_p = _mod_row_map(tm, n_lat, seq, n_batch)
        src_specs = [pl.BlockSpec((1, 1, tm), lambda i: (i, 0, 0), memory_space=pltpu.SMEM),
                     pl.BlockSpec((1, 1, tm), lambda i: (jnp.minimum(i + 1, steps - 1), 0, 0), memory_space=pltpu.SMEM),
                     pl.BlockSpec((tm, d), lambda i: (i, 0)),
                     pl.BlockSpec((1, 1, 6 * d), lambda i: (mrow_p(i), 0, 0)),
                     pl.BlockSpec(memory_space=pl.ANY)]
        source = (pos3, pos3, xm, mod_prev, ys)
        scratch = [pltpu.VMEM((2, tm) + ys.shape[1:], F32), pltpu.SemaphoreType.DMA((2,))]
    mrow = _mod_row_map(tm, n_lat, seq, n_batch)
    specs = src_specs + [pl.BlockSpec((1, 1, 6 * d), lambda i: (mrow(i), 0, 0)),
                         pl.BlockSpec((1, d), lambda i: (0, 0)),
                         pl.BlockSpec(w.shape, lambda i: (0, 0))]
    args = tuple(source) + (mod_l, g, w)
    out_specs = [pl.BlockSpec((tm, d), lambda i: (i, 0))]
    out_shape = [jax.ShapeDtypeStruct((n, d), F32)]
    if odd:
        tiles_per_seq = seq // tm

        def rope_row(i):
            return jnp.where(i < lat_tiles, i % tiles_per_seq, tiles_per_seq)

        specs += [pl.BlockSpec((tm, HEAD_DIM), lambda i: (rope_row(i), 0))] * 2
        args += tuple(rope)
        widths = (ATT_HEADS * HEAD_DIM, 2 * ATT_KV_HEADS * HEAD_DIM)
    else:
        widths = (w.shape[1],)
    out_specs += [pl.BlockSpec((tm, wd), lambda i: (i, 0)) for wd in widths]
    out_shape += [jax.ShapeDtypeStruct((n, wd), BF16) for wd in widths]
    if w_exp is not None:
        w_spec, wo_spec, wo_shape = _cast_specs(w_exp[0], layer, n // tm)
        specs += [w_spec] * 3
        args += tuple(w_exp)
        out_specs += [wo_spec] * 3
        out_shape += [wo_shape] * 3
    return pl.pallas_call(
        functools.partial(_in_kernel, first=first, odd=odd, lat_tiles=lat_tiles, casts=w_exp is not None),
        grid=(n // tm,),
        in_specs=specs,
        out_specs=out_specs,
        out_shape=out_shape,
        scratch_shapes=scratch,
        compiler_params=_params(1, VMEM_LIMIT),
        name="in_odd" if odd else "in_even",
    )(*args)


def _rope_tables(seq, tm):
    t = np.arange(seq)
    quarter = HEAD_DIM // 4
    inv_freq = ROPE_BASE ** (-np.arange(quarter, dtype=np.float64) / quarter)
    ar = (t // GRID_W)[:, None] * inv_freq[None, :]
    ac = (t % GRID_W)[:, None] * inv_freq[None, :]
    cos = np.concatenate([np.cos(ar), np.cos(ar), np.cos(ac), np.cos(ac)], axis=1)
    sin = np.concatenate([-np.sin(ar), np.sin(ar), -np.sin(ac), np.sin(ac)], axis=1)
    cos = np.concatenate([cos, np.ones((tm, HEAD_DIM))], axis=0)
    sin = np.concatenate([sin, np.zeros((tm, HEAD_DIM))], axis=0)
    return jnp.asarray(cos, F32), jnp.asarray(sin, F32)


def _ret_increments(qk_ref, v_ref, rows, zf_ref, zb_ref):
    upper_rows = lax.broadcasted_iota(I32, (128, 128), 0) < RET_QK_DIM
    out = []
    for grp in range(2):
        k2 = qk_ref[rows, 256 + 128 * grp:256 + 128 * (grp + 1)].astype(F32)
        v2 = v_ref[rows, 256 * grp:256 * (grp + 1)]
        halves = []
        for z_ref in (zf_ref, zb_ref):
            kz = (k2 * z_ref[grp]).astype(BF16)
            full = lax.dot_general(kz, v2, _TN, preferred_element_type=F32)
            halves.append(jnp.where(upper_rows, full[:, 0:128], full[:, 128:256]))
        out.append(jnp.concatenate(halves, axis=1))
    return out


def _mix_outputs(qk_ref, v_ref, gate_ref, u_ref, s_ref, rows, states, dec_ref, xi_ref, gn_ref, lng_ref, lnb_ref,
                 ws_ref, sgb_ref, o_ref):
    lane = lax.broadcasted_iota(I32, (1, 128), 1)
    head_mask = [(lane // RET_QK_DIM == hh).astype(BF16) for hh in range(2)]
    for grp in range(2):
        q2 = qk_ref[rows, 128 * grp:128 * (grp + 1)]
        k2 = qk_ref[rows, 256 + 128 * grp:256 + 128 * (grp + 1)]
        for hh in range(2):
            hd = 2 * grp + hh
            cols = slice(128 * hd, 128 * (hd + 1))
            qm = q2 * head_mask[hh]
            sc = lax.dot_general(qm, k2, _NT, preferred_element_type=F32) * dec_ref[hd]
            o = jnp.dot(sc.astype(BF16), v_ref[rows, cols], preferred_element_type=F32)
            if states is not None:
                cross = jnp.dot(qm, states[grp], preferred_element_type=F32) * xi_ref[hd]
                o = o + cross[:, 0:128] + cross[:, 128:256]
            mu = jnp.mean(o, axis=-1, keepdims=True)
            oc = o - mu
            var = jnp.mean(oc * oc, axis=-1, keepdims=True)
            y = oc * lax.rsqrt(var + EPS) * gn_ref[:, cols]
            y = y * _silu(gate_ref[rows, cols].astype(F32))
            o_ref[rows, cols] = y.astype(BF16)
    for sub in range((rows.stop - rows.start) // SG_CHUNK):
        srows = slice(rows.start + SG_CHUNK * sub, rows.start + SG_CHUNK * (sub + 1))
        s = s_ref[srows, :].astype(F32)
        mu = jnp.mean(s, axis=-1, keepdims=True)
        sc_ = s - mu
        var = jnp.mean(sc_ * sc_, axis=-1, keepdims=True)
        sn = (sc_ * lax.rsqrt(var + EPS) * lng_ref[...] + lnb_ref[...]).astype(BF16)
        for g in range(SG_GROUPS):
            cols = slice(128 * g, 128 * (g + 1))
            mixed = jnp.dot(ws_ref[g], sn[:, cols], preferred_element_type=F32) + sgb_ref[:, cols]
            o_ref[srows, 512 + 128 * g:512 + 128 * (g + 1)] = (u_ref[srows, cols].astype(F32) * mixed).astype(BF16)


def _even_lat_kernel(bqk_ref, bv_ref, cqk_ref, cv_ref, qk_ref, v_ref, gate_ref, u_ref, s_ref, dec_ref, xi_ref,
                     zf_ref, zb_ref, gdec_ref, gn_ref, lng_ref, lnb_ref, ws_ref, sgb_ref, o_ref, ds_ref, sp_ref,
                     *, inc_steps):
    j = pl.program_id(1)
    cb = RET_BLOCK
    n_slots = ds_ref.shape[0]
    inc_chunks = bqk_ref.shape[0] // cb
    out_chunks = qk_ref.shape[0] // cb

    @pl.when(j == 0)
    def _():
        inc = _ret_increments(cqk_ref, cv_ref, slice(0, cb), zf_ref, zb_ref)
        for grp in range(2):
            ds_ref[0, grp] = inc[grp]

    @pl.when(j < inc_steps)
    def _():
        for c in range(inc_chunks):
            inc = _ret_increments(bqk_ref, bv_ref, slice(cb * c, cb * (c + 1)), zf_ref, zb_ref)
            for grp in range(2):
                ds_ref[1 + inc_chunks * j + c, grp] = inc[grp]

    @pl.when(j == inc_steps)
    def _scan():
        fwd = ds_ref[0, :, :, 0:128]
        for n in range(1, n_slots):
            sp_ref[n, :, :, 0:128] = fwd
            fwd = fwd * gdec_ref[:, :, 0:128] + ds_ref[n, :, :, 0:128]
        bwd = ds_ref[0, :, :, 128:256]
        for n in range(n_slots - 1, 0, -1):
            sp_ref[n, :, :, 128:256] = bwd
            bwd = bwd * gdec_ref[:, :, 128:256] + ds_ref[n, :, :, 128:256]

    @pl.when(j >= inc_steps)
    def _():
        for c in range(out_chunks):
            n = 1 + out_chunks * (j - inc_steps) + c
            states = [sp_ref[n, grp].astype(BF16) for grp in range(2)]
            _mix_outputs(qk_ref, v_ref, gate_ref, u_ref, s_ref, slice(cb * c, cb * (c + 1)), states, dec_ref, xi_ref,
                         gn_ref, lng_ref, lnb_ref, ws_ref, sgb_ref, o_ref)


def _even_ctx_kernel(qk_ref, v_ref, gate_ref, u_ref, s_ref, dec_ref, xi_ref, gn_ref, lng_ref, lnb_ref, ws_ref, sgb_ref,
                     o_ref):
    _mix_outputs(qk_ref, v_ref, gate_ref, u_ref, s_ref, slice(0, RET_BLOCK), None, dec_ref, xi_ref,
                 gn_ref, lng_ref, lnb_ref, ws_ref, sgb_ref, o_ref)


def _retention_tables(log_decay, cb):
    lg = -jnp.exp(log_decay.astype(F32))
    pos = jnp.arange(cb, dtype=F32)
    diff = pos[:, None] - pos[None, :]
    lower = jnp.where(diff >= 0, jnp.exp(lg[0][:, None, None] * jnp.maximum(diff, 0.0)[None]), 0.0)
    upper = jnp.where(diff <= 0, jnp.exp(lg[1][:, None, None] * jnp.maximum(-diff, 0.0)[None]), 0.0)
    dec = lower + upper
    xi_f = jnp.exp(lg[0][:, None] * (pos[None, :] + 1.0))
    xi_b = jnp.exp(lg[1][:, None] * (cb - pos[None, :]))
    xi = jnp.concatenate([jnp.broadcast_to(xi_f[:, :, None], (RET_HEADS, cb, 128)),
                          jnp.broadcast_to(xi_b[:, :, None], (RET_HEADS, cb, 128))], axis=2)
    zeta_f = jnp.exp(lg[0][:, None] * (cb - 1.0 - pos[None, :]))
    zeta_b = jnp.exp(lg[1][:, None] * pos[None, :])

    def lanes(z):
        return jnp.repeat(z.reshape(2, 2, cb), RET_QK_DIM, axis=1).transpose(0, 2, 1)

    gstep = jnp.exp(lg * cb)

    def rows(gv):
        return jnp.broadcast_to(jnp.repeat(gv.reshape(2, 2), RET_QK_DIM, axis=1)[:, :, None], (2, 128, 128))

    gdec = jnp.concatenate([rows(gstep[0]), rows(gstep[1])], axis=2)
    return dec, xi, lanes(zeta_f), lanes(zeta_b), gdec


def _even_mix(p, tabs, gn_g, ln_g, ln_b, w_s, sg_bias, n_batch, seq, ctx_len):
    cb = RET_BLOCK
    big, blk = 4 * cb, 2 * cb
    assert ctx_len == cb and seq % big == 0
    n_lat = n_batch * seq
    inc_steps, out_steps = seq // big, seq // blk
    dec, xi, zf, zb, gdec = tabs
    consts = (dec, xi, zf, zb, gdec, gn_g, ln_g, ln_b, w_s, sg_bias)
    full2 = lambda a: pl.BlockSpec(a.shape, lambda b, j, _nd=a.ndim: (0,) * _nd)
    inc_row = lambda b, j: b * inc_steps + jnp.minimum(j, inc_steps - 1)
    out_row = lambda b, j: b * out_steps + jnp.maximum(j - inc_steps, 0)
    ctx_row = lambda b, j: n_lat // cb + b
    lat = pl.pallas_call(
        functools.partial(_even_lat_kernel, inc_steps=inc_steps),
        grid=(n_batch, inc_steps + out_steps),
        in_specs=[pl.BlockSpec((big, 512), lambda b, j: (inc_row(b, j), 0)),
                  pl.BlockSpec((big, 512), lambda b, j: (inc_row(b, j), 1)),
                  pl.BlockSpec((cb, 512), lambda b, j: (ctx_row(b, j), 0)),
                  pl.BlockSpec((cb, 512), lambda b, j: (ctx_row(b, j), 1))]
                 + [pl.BlockSpec((blk, 512), lambda b, j, _c=c: (out_row(b, j), _c)) for c in range(5)]
                 + [full2(a) for a in consts],
        out_specs=pl.BlockSpec((blk, 1024), lambda b, j: (out_row(b, j), 0)),
        out_shape=jax.ShapeDtypeStruct((n_lat, 1024), BF16),
        scratch_shapes=[pltpu.VMEM((seq // cb + 1, 2, 128, 256), F32), pltpu.VMEM((seq // cb + 1, 2, 128, 256), F32)],
        compiler_params=_params(2, VMEM_LIMIT),
        name="even_mix",
    )(p, p, p, p, p, p, p, p, p, *consts)
    ctx_consts = (dec, xi, gn_g, ln_g, ln_b, w_s, sg_bias)
    full1 = lambda a: pl.BlockSpec(a.shape, lambda b, _nd=a.ndim: (0,) * _nd)
    ctx = pl.pallas_call(
        _even_ctx_kernel,
        grid=(n_batch,),
        in_specs=[pl.BlockSpec((cb, 512), lambda b, _c=c: (n_lat // cb + b, _c)) for c in range(5)]
                 + [full1(a) for a in ctx_consts],
        out_specs=pl.BlockSpec((cb, 1024), lambda b: (b, 0)),
        out_shape=jax.ShapeDtypeStruct((n_batch * ctx_len, 1024), BF16),
        compiler_params=_params(1, VMEM_LIMIT),
        name="even_mix_ctx",
    )(p, p, p, p, p, *ctx_consts)
    return lat, ctx


def _attn_kernel(q_ref, kvp_ref, kvm_ref, kvn_ref, kvx_ref, bias_f_ref, bias_m_ref, bias_l_ref, sink_ref, o_ref,
                 s_scr, p_scr, r_scr, *, n_sub):
    blk = ATT_BLOCK
    nloc = 3 * blk
    rows_all = ATT_GROUP * blk
    bias_refs = [bias_f_ref] + [bias_m_ref] * (n_sub - 2) + [bias_l_ref]
    chains = [(kvh, sub) for kvh in range(ATT_KV_HEADS) for sub in range(n_sub)]
    v_all, sinks = {}, {}
    for ci, (kvh, sub) in enumerate(chains):
        kcols = slice(HEAD_DIM * kvh, HEAD_DIM * (kvh + 1))
        vcols = slice(HEAD_DIM * (ATT_KV_HEADS + kvh), HEAD_DIM * (ATT_KV_HEADS + kvh + 1))
        k_parts = [kvp_ref[:, kcols]] + [kvm_ref[blk * j:blk * (j + 1), kcols] for j in range(n_sub)] + [kvn_ref[:, kcols]]
        v_parts = [kvp_ref[:, vcols]] + [kvm_ref[blk * j:blk * (j + 1), vcols] for j in range(n_sub)] + [kvn_ref[:, vcols]]
        heads = [ATT_GROUP * kvh + j for j in range(ATT_GROUP)]
        rows = slice(blk * sub, blk * (sub + 1))
        k_all = jnp.concatenate(k_parts[sub:sub + 3] + [kvx_ref[:, kcols]], axis=0)
        v_all[ci] = jnp.concatenate(v_parts[sub:sub + 3] + [kvx_ref[:, vcols]], axis=0)
        q4 = jnp.concatenate([q_ref[rows, HEAD_DIM * hd:HEAD_DIM * (hd + 1)] for hd in heads], axis=0)
        s_scr[ci] = lax.dot_general(q4, k_all, _NT, preferred_element_type=F32)
        sinks[ci] = LOG2E * jnp.concatenate(
            [jnp.broadcast_to(sink_ref[hd:hd + 1, 0:1], (blk, 1)) for hd in heads], axis=0)
    nchunk_loc = nloc // 128
    nchunk = s_scr.shape[2] // 128

    def shifted(ci, sub, rs, k):
        t = s_scr[ci, rs, 128 * k:128 * (k + 1)]
        return t + bias_refs[sub][0, rs, 128 * k:128 * (k + 1)] if k < nchunk_loc else t

    mrow = {}
    for ci, (kvh, sub) in enumerate(chains):
        for r0 in range(0, rows_all, ATT_SLAB):
            rs = slice(r0, r0 + ATT_SLAB)
            part = shifted(ci, sub, rs, 0)
            for k in range(1, nchunk):
                part = jnp.maximum(part, shifted(ci, sub, rs, k))
            r_scr[ci, rs, :] = part
    for ci in range(len(chains)):
        m = jnp.maximum(jnp.max(r_scr[ci], axis=-1, keepdims=True), sinks[ci])
        mrow[ci] = m
        r_scr[ci] = jnp.broadcast_to(m, r_scr.shape[1:])
    for ci, (kvh, sub) in enumerate(chains):
        for r0 in range(0, rows_all, ATT_SLAB):
            rs = slice(r0, r0 + ATT_SLAB)
            mb = r_scr[ci, rs, :]
            part = None
            for k in range(nchunk):
                p = jnp.exp2(shifted(ci, sub, rs, k) - mb)
                part = p if part is None else part + p
                p_scr[ci, rs, 128 * k:128 * (k + 1)] = p.astype(BF16)
            r_scr[ci, rs, :] = part
    inv = {}
    for ci in range(len(chains)):
        inv[ci] = 1.0 / (jnp.sum(r_scr[ci], axis=-1, keepdims=True) + jnp.exp2(sinks[ci] - mrow[ci]))
    for ci, (kvh, sub) in enumerate(chains):
        rows = slice(blk * sub, blk * (sub + 1))
        o = (jnp.dot(p_scr[ci, :, 0:nloc], v_all[ci][0:nloc], preferred_element_type=F32)
             + jnp.dot(p_scr[ci, :, nloc:], v_all[ci][nloc:], preferred_element_type=F32)) * inv[ci]
        for j in range(ATT_GROUP):
            hd = ATT_GROUP * kvh + j
            o_ref[rows, HEAD_DIM * hd:HEAD_DIM * (hd + 1)] = o[blk * j:blk * (j + 1)].astype(BF16)


def _attn_bias():
    i = np.arange(ATT_BLOCK)[:, None]
    j = np.arange(3 * ATT_BLOCK)[None, :] - ATT_BLOCK
    band = np.abs(i - j) <= WINDOW
    cases = [band & (j >= 0), band, band & (j < ATT_BLOCK), np.zeros_like(band)]
    bias = np.stack([np.where(c, 0.0, NEG) for c in cases]).astype(np.float32)
    return jnp.asarray(np.tile(bias, (1, ATT_GROUP, 1)))


def _attention(q, kv, bias, sink_t, n_batch, seq, ctx_len, need_ctx):
    blk = ATT_BLOCK
    n_lat = n_batch * seq
    nb = seq // blk
    width = 3 * blk + ctx_len

    def call(n_sub, steps, n_rows, qrow, small, mid, cases, name):
        big = n_sub * blk
        chains = ATT_KV_HEADS * n_sub
        return pl.pallas_call(
            functools.partial(_attn_kernel, n_sub=n_sub),
            grid=(n_batch, steps),
            in_specs=[
                pl.BlockSpec((big, q.shape[1]), lambda b, s: (qrow(b, s), 0)),
                pl.BlockSpec((blk, kv.shape[1]), small(-1)),
                pl.BlockSpec((big, kv.shape[1]), mid),
                pl.BlockSpec((blk, kv.shape[1]), small(n_sub)),
                pl.BlockSpec((ctx_len, kv.shape[1]), lambda b, s: (n_lat // ctx_len + b, 0)),
            ] + [pl.BlockSpec((1,) + bias.shape[1:], c) for c in cases] + [
                pl.BlockSpec(sink_t.shape, lambda b, s: (0, 0)),
            ],
            out_specs=pl.BlockSpec((big, q.shape[1]), lambda b, s: (qrow(b, s) - (n_lat // big if name else 0), 0)),
            out_shape=jax.ShapeDtypeStruct((n_rows, q.shape[1]), BF16),
            scratch_shapes=[pltpu.VMEM((chains, ATT_GROUP * blk, width), F32),
                            pltpu.VMEM((chains, ATT_GROUP * blk, width), BF16),
                            pltpu.VMEM((chains, ATT_GROUP * blk, 128), F32)],
            compiler_params=_params(2, VMEM_LIMIT),
            name="window_attention" + name,
        )(q, kv, kv, kv, kv, bias, bias, bias, sink_t)

    big = ATT_SUB * blk
    assert seq % big == 0
    ns = seq // big

    def small(off):
        return lambda b, s: (b * nb + jnp.clip(ATT_SUB * s + off, 0, nb - 1), 0)

    cases = (lambda b, s: (jnp.where(s == 0, 0, 1), 0, 0),
             lambda b, s: (1, 0, 0),
             lambda b, s: (jnp.where(s == ns - 1, 2, 1), 0, 0))
    lat = call(ATT_SUB, ns, n_lat, lambda b, s: b * ns + s, small, lambda b, s: (b * ns + s, 0), cases, "")
    if not need_ctx:
        return (lat,)
    n_cb = ctx_len // blk
    masked = (lambda b, s: (3, 0, 0),) * 3
    first = lambda off: (lambda b, s: (b * nb, 0))
    ctx = call(n_cb, 1, n_batch * ctx_len, lambda b, s: n_lat // ctx_len + b, first,
               lambda b, s: (b * (seq // ctx_len), 0), masked, "_ctx")
    return lat, ctx


def _route(logits):
    m = jnp.max(logits, axis=0, keepdims=True)
    e = jnp.exp(logits - m)
    p = e / jnp.sum(e, axis=0, keepdims=True)
    rows = [p[i:i + 1, :] for i in range(N_EXPERTS)]
    best = None
    gsel = None
    for g in range(N_GROUPS):
        a, b, c, d = rows[4 * g:4 * g + 4]
        m1, n1 = jnp.maximum(a, b), jnp.minimum(a, b)
        m2, n2 = jnp.maximum(c, d), jnp.minimum(c, d)
        score = jnp.maximum(m1, m2) + jnp.maximum(jnp.minimum(m1, m2), jnp.maximum(n1, n2))
        if g == 0:
            best, gsel = score, jnp.zeros(score.shape, I32)
        else:
            upd = score > best
            gsel = jnp.where(upd, g, gsel)
            best = jnp.where(upd, score, best)
    v = [jnp.where(gsel == 0, rows[j], jnp.where(gsel == 1, rows[4 + j], jnp.where(gsel == 2, rows[8 + j], rows[12 + j])))
         for j in range(EXPERTS_PER_GROUP)]
    b1, i1 = v[0], jnp.zeros(gsel.shape, I32)
    for j in range(1, 4):
        upd = v[j] > b1
        i1 = jnp.where(upd, j, i1)
        b1 = jnp.where(upd, v[j], b1)
    b2, i2 = jnp.full(b1.shape, -1.0, F32), jnp.zeros(gsel.shape, I32)
    for j in range(4):
        upd = jnp.logical_and(i1 != j, v[j] > b2)
        i2 = jnp.where(upd, j, i2)
        b2 = jnp.where(upd, v[j], b2)
    lo = jnp.minimum(i1, i2)
    hi = jnp.maximum(i1, i2)
    pair = jnp.where(lo == 0, hi - 1, jnp.where(lo == 1, 6 - hi, 5))
    return gsel * N_PAIRS + pair


def _post_kernel(*refs, lat_tiles, first_trash):
    if lat_tiles is None:
        mix_ref, refs = refs[0], refs[1:]
        mix = mix_ref[...]
    else:
        (ml_ref, mc_ref), refs = refs[:2], refs[2:]
        mix = jnp.where(pl.program_id(0) < lat_tiles, ml_ref[...], mc_ref[...])
    (x_ref, mod_ref, g2_ref, wo_ref, wr_ref, br_ref, tri_ref, low_ref, hs_in_ref,
     xo_ref, pos_ref, tab_ref, cnt_ref, hs_ref,
     carry_ref, cur_ref, nfree_ref, hbuf_ref, posv_ref, poss_ref, sem_s, sem_p) = refs
    del hs_in_ref
    i = pl.program_id(0)
    slot = i % 2
    tm, d = x_ref.shape

    @pl.when(i == 0)
    def _():
        carry_ref[...] = jnp.zeros(carry_ref.shape, F32)
        cur_ref[...] = jnp.zeros(cur_ref.shape, F32)
        nfree_ref[...] = jnp.zeros(nfree_ref.shape, F32)
        tab_ref[...] = jnp.zeros(tab_ref.shape, F32)
        hbuf_ref[1] = jnp.zeros(hbuf_ref.shape[1:], F32)
        posv_ref[1] = first_trash + lax.broadcasted_iota(I32, (1, tm), 1)
        pltpu.make_async_copy(posv_ref.at[1], poss_ref.at[1], sem_p.at[1]).start()

    pltpu.make_async_copy(posv_ref.at[1 - slot], poss_ref.at[1 - slot], sem_p.at[1 - slot]).wait()

    def scatter(s, lo, hi):
        for r in range(lo, hi):
            pltpu.make_async_copy(hbuf_ref.at[s, r], hs_ref.at[poss_ref[s, 0, r]], sem_s).start(priority=r % 2)

    def scatter_wait(s):
        pltpu.make_async_copy(hbuf_ref.at[s], hs_ref.at[pl.ds(0, tm)], sem_s).wait()

    nchunk = 4
    wc = d // nchunk
    xs = []
    for c in range(nchunk):
        scatter(1 - slot, tm * c // nchunk, tm * (c + 1) // nchunk)
        cols = slice(wc * c, wc * (c + 1))
        y = jnp.dot(mix, wo_ref[:, cols], preferred_element_type=F32)
        xc = x_ref[:, cols] + mod_ref[0, :, 2 * d + wc * c:2 * d + wc * (c + 1)] * y
        xo_ref[:, cols] = xc
        xs.append(xc)
    x = jnp.concatenate(xs, axis=1)
    h2 = _rms_mod(x, g2_ref[...], mod_ref[0, :, 4 * d:5 * d], mod_ref[0, :, 3 * d:4 * d])
    hbuf_ref[slot] = h2.reshape(hbuf_ref.shape[1:])
    hi = h2.astype(BF16)
    lo = (h2 - hi.astype(F32)).astype(BF16)
    logits = (lax.dot_general(wr_ref[...], hi, _NT, preferred_element_type=F32)
              + lax.dot_general(wr_ref[...], lo, _NT, preferred_element_type=F32)
              + br_ref[:, 0:1])
    bucket = _route(logits)
    ids = lax.broadcasted_iota(I32, (BUCKET_ROWS, tm), 0)
    onehot = (ids == bucket).astype(F32)
    before = jnp.dot(onehot.astype(BF16), tri_ref[...], preferred_element_type=F32)
    inv = 1.0 / MOE_TILE
    c0 = carry_ref[:, 0:1]
    c1 = c0 + jnp.sum(onehot, axis=1, keepdims=True)
    a0 = jnp.floor((c0 + (MOE_TILE - 1)) * inv)
    new = jnp.floor((c1 + (MOE_TILE - 1)) * inv) - a0
    nfree = nfree_ref[0:1, 0:1]
    new_b = jnp.broadcast_to(new, (BUCKET_ROWS, 128)).astype(BF16)
    id1 = nfree + jnp.dot(low_ref[...], new_b, preferred_element_type=F32)[:, 0:1]
    id2 = id1 + 1.0
    cur = cur_ref[:, 0:1]
    rank = before + c0
    k = jnp.floor(rank * inv)
    chunk = jnp.where(k < a0, cur, jnp.where(k == a0, id1, id2))
    pos = jnp.sum(onehot * (chunk * MOE_TILE + (rank - k * MOE_TILE)), axis=0, keepdims=True).astype(I32)
    pos_ref[...] = pos
    posv_ref[slot] = pos
    pltpu.make_async_copy(posv_ref.at[slot], poss_ref.at[slot], sem_p.at[slot]).start()
    carry_ref[...] = jnp.broadcast_to(c1, carry_ref.shape)
    cnt_ref[...] = jnp.broadcast_to(c1, cnt_ref.shape).astype(I32)
    cur_ref[...] = jnp.broadcast_to(jnp.where(new == 0.0, cur, jnp.where(new == 1.0, id1, id2)), cur_ref.shape)
    nfree_ref[...] = jnp.broadcast_to(nfree + jnp.sum(new, axis=0, keepdims=True), nfree_ref.shape)
    kcol = lax.broadcasted_iota(I32, tab_ref.shape, 1).astype(F32)
    tab = tab_ref[...]
    tab = jnp.where(jnp.logical_and(kcol == a0, new >= 1.0), id1, tab)
    tab_ref[...] = jnp.where(jnp.logical_and(kcol == a0 + 1.0, new >= 2.0), id2, tab)
    scatter_wait(1 - slot)

    @pl.when(i == pl.num_programs(0) - 1)
    def _():
        pltpu.make_async_copy(posv_ref.at[slot], poss_ref.at[slot], sem_p.at[slot]).wait()
        scatter(slot, 0, tm)
        scatter_wait(slot)


def _post(mix, xa, mod_l, g2, wo, wr_bf, br_t, tri, low, hs, n_act, n_lat, seq, n_batch):
    d = xa.shape[1]
    tm = ROW_TILE
    steps = n_act // tm
    mrow = _mod_row_map(tm, n_lat, seq, n_batch)
    full = lambda a: pl.BlockSpec(a.shape, lambda i, _nd=a.ndim: (0,) * _nd)
    if len(mix) == 1:
        lat_tiles = None
        mix_specs = [pl.BlockSpec((tm, d), lambda i: (i, 0))]
    else:
        lat_tiles = n_lat // tm
        mix_specs = [pl.BlockSpec((tm, d), lambda i: (jnp.minimum(i, lat_tiles - 1), 0)),
                     pl.BlockSpec((tm, d), lambda i: (jnp.maximum(i - lat_tiles, 0), 0))]
    n_in = len(mix_specs)
    return pl.pallas_call(
        functools.partial(_post_kernel, lat_tiles=lat_tiles, first_trash=hs.shape[0] - tm),
        grid=(steps,),
        in_specs=mix_specs + [
            pl.BlockSpec((tm, d), lambda i: (i, 0)),
            pl.BlockSpec((1, 1, 6 * d), lambda i: (mrow(i), 0, 0)),
            full(g2), full(wo), full(wr_bf), full(br_t), full(tri), full(low),
            pl.BlockSpec(memory_space=pl.ANY),
        ],
        out_specs=[
            pl.BlockSpec((tm, d), lambda i: (i, 0)),
            pl.BlockSpec((1, tm), lambda i: (0, i)),
            pl.BlockSpec((BUCKET_ROWS, 256), lambda i: (0, 0)),
            pl.BlockSpec((BUCKET_ROWS, 128), lambda i: (0, 0)),
            pl.BlockSpec(memory_space=pl.ANY),
        ],
        out_shape=[
            jax.ShapeDtypeStruct((n_act, d), F32),
            jax.ShapeDtypeStruct((1, n_act), I32),
            jax.ShapeDtypeStruct((BUCKET_ROWS, 256), F32),
            jax.ShapeDtypeStruct((BUCKET_ROWS, 128), I32),
            jax.ShapeDtypeStruct(hs.shape, F32),
        ],
        scratch_shapes=[pltpu.VMEM((BUCKET_ROWS, 128), F32), pltpu.VMEM((BUCKET_ROWS, 128), F32), pltpu.VMEM((8, 128), F32),
                        pltpu.VMEM((2, tm) + hs.shape[1:], F32), pltpu.VMEM((2, 1, tm), I32), pltpu.SMEM((2, 1, tm), I32),
                        pltpu.SemaphoreType.DMA(()), pltpu.SemaphoreType.DMA((2,))],
        input_output_aliases={n_in + 8: 4},
        compiler_params=_params(1, VMEM_LIMIT),
        name="post_mixer",
    )(*mix, xa, mod_l, g2, wo, wr_bf, br_t, tri, low, hs)


def _moe_kernel(blk_ref, ea_ref, eb_ref, nu_ref, h_ref, wga_ref, wua_ref, wda_ref, wgb_ref, wub_ref, wdb_ref,
                wrt_ref, brt_ref, *refs):
    o_ref = refs[len(refs) // 2]
    n_cast = len(refs) // 2
    for k in range(n_cast):
        refs[n_cast + 1 + k][...] = refs[k][0].astype(BF16)
    t = pl.program_id(0)

    @pl.when(t >= nu_ref[0])
    def _():
        o_ref[...] = jnp.zeros(o_ref.shape, F32)

    @pl.when(t < nu_ref[0])
    def _():
        ea = ea_ref[t]
        eb = eb_ref[t]
        tm = h_ref.shape[0]
        h = h_ref[...].reshape(tm, h_ref.shape[1] * h_ref.shape[2])
        hb = h.astype(BF16)
        wdiff = wrt_ref[pl.ds(ea, 1), :] - wrt_ref[pl.ds(eb, 1), :]
        bdiff = brt_ref[pl.ds(ea, 1), 0:1] - brt_ref[pl.ds(eb, 1), 0:1]
        wa = jax.nn.sigmoid(jnp.sum(h * wdiff, axis=-1, keepdims=True) + bdiff)

        slots = ((wga_ref, wua_ref, wda_ref), (wgb_ref, wub_ref, wdb_ref))
        f = wga_ref.shape[2]
        chains = [(e, c) for e in range(2) for c in range(f // 512)]
        gu = {}
        for e, c in chains:
            cols = slice(512 * c, 512 * (c + 1))
            gu[e, c] = (jnp.dot(hb, slots[e][0][0, :, cols], preferred_element_type=F32),
                        jnp.dot(hb, slots[e][1][0, :, cols], preferred_element_type=F32))
        act = {k: (_silu(g) * u).astype(BF16) for k, (g, u) in gu.items()}
        outs = [None, None]
        for e, c in chains:
            part = jnp.dot(act[e, c], slots[e][2][0, 512 * c:512 * (c + 1), :], preferred_element_type=F32)
            outs[e] = part if outs[e] is None else outs[e] + part
        y = wa * outs[0] + (1.0 - wa) * outs[1]
        o_ref[...] = y.reshape(o_ref.shape)


def _moe(blk, ea, eb, nu, hs, n_tiles, wg, wu, wd, wr_t, br_t, w_next, next_layer):
    row = hs.shape[1:]
    tm = MOE_TILE
    d, f = wg.shape[1:]
    n_cast = 0 if w_next is None else len(w_next)
    if n_cast:
        w_spec, wo_spec, wo_shape = _cast_specs(w_next[0], next_layer, n_tiles)
    else:
        w_spec = wo_spec = wo_shape = None
    wa_spec = lambda shape: pl.BlockSpec(shape, lambda t, blk_, ea_, eb_, nu_: (ea_[t], 0, 0))
    wb_spec = lambda shape: pl.BlockSpec(shape, lambda t, blk_, ea_, eb_, nu_: (eb_[t], 0, 0))
    grid_spec = pltpu.PrefetchScalarGridSpec(
        num_scalar_prefetch=4,
        grid=(n_tiles,),
        in_specs=[
            pl.BlockSpec((tm,) + row, lambda t, blk_, ea_, eb_, nu_: (blk_[t], 0, 0)),
            wa_spec((1, d, f)), wa_spec((1, d, f)), wa_spec((1, f, d)),
            wb_spec((1, d, f)), wb_spec((1, d, f)), wb_spec((1, f, d)),
            pl.BlockSpec(wr_t.shape, lambda t, blk_, ea_, eb_, nu_: (0, 0)),
            pl.BlockSpec(br_t.shape, lambda t, blk_, ea_, eb_, nu_: (0, 0)),
        ] + [w_spec] * n_cast,
        out_specs=[pl.BlockSpec((tm,) + row, lambda t, blk_, ea_, eb_, nu_: (blk_[t], 0, 0))] + [wo_spec] * n_cast,
    )
    return pl.pallas_call(
        _moe_kernel,
        grid_spec=grid_spec,
        out_shape=[jax.ShapeDtypeStruct((n_tiles * tm,) + row, F32)] + [wo_shape] * n_cast,
        compiler_params=_params(1, VMEM_LIMIT),
        name="experts",
    )(blk, ea, eb, nu, hs, wg, wu, wd, wg, wu, wd, wr_t, br_t, *(w_next or ()))


def _combine_kernel(pos_ref, posn_ref, x_ref, mod_ref, fg_ref, ys_ref, o_ref, gbuf_ref, sem):
    d = x_ref.shape[1]
    rows, issue_next, finish = _gather_rows(pos_ref, posn_ref, ys_ref, gbuf_ref, sem)
    issue_next(0, 1)
    x = x_ref[...] + mod_ref[0, :, 5 * d:6 * d] * rows
    o_ref[...] = x * lax.rsqrt(jnp.mean(x * x, axis=-1, keepdims=True) + EPS) * fg_ref[...]
    finish()


def _combine(pos3, xa, mod_l, fg, ys, n_lat, seq, n_batch):
    n, d = xa.shape
    tm = ROW_TILE
    steps = n // tm
    mrow = _mod_row_map(tm, n_lat, seq, n_batch)
    return pl.pallas_call(
        _combine_kernel,
        grid=(steps,),
        in_specs=[
            pl.BlockSpec((1, 1, tm), lambda i: (i, 0, 0), memory_space=pltpu.SMEM),
            pl.BlockSpec((1, 1, tm), lambda i: (jnp.minimum(i + 1, steps - 1), 0, 0), memory_space=pltpu.SMEM),
            pl.BlockSpec((tm, d), lambda i: (i, 0)),
            pl.BlockSpec((1, 1, 6 * d), lambda i: (mrow(i), 0, 0)),
            pl.BlockSpec((1, d), lambda i: (0, 0)),
            pl.BlockSpec(memory_space=pl.ANY),
        ],
        out_specs=pl.BlockSpec((tm, d), lambda i: (i, 0)),
        out_shape=jax.ShapeDtypeStruct((n, d), F32),
        scratch_shapes=[pltpu.VMEM((2, tm) + ys.shape[1:], F32), pltpu.SemaphoreType.DMA((2,))],
        compiler_params=_params(1, VMEM_LIMIT),
        name="combine",
    )(pos3, pos3, xa, mod_l, fg, ys)


def _tile_plan(tab, counts, n_tiles, tile):
    cnt = counts[:N_BUCKETS, 0]
    n_chunks = (cnt + tile - 1) // tile
    ends = jnp.cumsum(n_chunks)
    starts = ends - n_chunks
    n_used = ends[-1]
    t = jnp.arange(n_tiles, dtype=I32)
    t_eff = jnp.minimum(t, jnp.maximum(n_used - 1, 0))
    tb = jnp.minimum(jnp.sum((ends[None, :] <= t_eff[:, None]).astype(I32), axis=1), N_BUCKETS - 1)
    owned = tab.astype(I32)[tb, t_eff - starts[tb]]
    blk = jnp.where(t < n_used, owned, t)
    pairs = np.array(PAIR_SLOTS, dtype=np.int32)
    ea = (tb // N_PAIRS) * EXPERTS_PER_GROUP + jnp.asarray(pairs[:, 0])[tb % N_PAIRS]
    eb = (tb // N_PAIRS) * EXPERTS_PER_GROUP + jnp.asarray(pairs[:, 1])[tb % N_PAIRS]
    return blk.astype(I32), ea.astype(I32), eb.astype(I32), n_used.astype(I32).reshape(1)


def kernel(x, c, ctx, c_ctx, w_mod, b_mod, norm_g, w_in_even, w_out_even, ret_log_decay, ret_gn_g, sg_ln_g, sg_ln_b, sg_w, sg_b, w_qkv_odd, w_o_odd, attn_sink, w_router, b_router, w_e_gate, w_e_up, w_e_down, final_g):
    n_batch, seq, d = x.shape
    ctx_len = ctx.shape[1]
    depth = w_mod.shape[0]
    n_lat = n_batch * seq
    n_all = n_lat + n_batch * ctx_len
    tm = ROW_TILE
    assert seq % tm == 0 and (n_batch * ctx_len) % tm == 0

    mod_rows = -(-(n_batch + 1) // 16) * 16
    c_rows = jnp.zeros((mod_rows, d), F32).at[:n_batch].set(c).at[n_batch].set(c_ctx)
    mod = _modulation(c_rows, w_mod, b_mod).reshape(depth, mod_rows, 1, 6 * d)

    wr_t = w_router.T.astype(F32)
    wr_bf = wr_t.astype(BF16)
    br_t = jnp.broadcast_to(b_router.astype(F32)[:, None], (N_EXPERTS, 128))
    tri = jnp.asarray(np.triu(np.ones((tm, tm), np.float32), k=1), BF16)
    low = jnp.asarray(np.tril(np.ones((BUCKET_ROWS, BUCKET_ROWS), np.float32), k=-1), BF16)
    cos_t, sin_t = _rope_tables(seq, tm)
    bias = _attn_bias()

    n_exp, d_exp = w_e_gate.shape[1], w_e_gate.shape[3]
    assert d_exp == d
    w_views = [w.reshape(depth, n_exp * d, d) for w in (w_e_gate, w_e_up, w_e_down)]

    n_tiles = n_all // MOE_TILE + N_BUCKETS
    p_rows = n_tiles * MOE_TILE
    hs = jnp.zeros((p_rows + tm, ROW_SUB, d // ROW_SUB), F32)

    source = (x.reshape(n_lat, d), ctx.reshape(n_batch * ctx_len, d))
    for l in range(depth):
        i = l // 2
        last = l == depth - 1
        mod_l = mod[l]
        g1 = norm_g[l, 0].reshape(1, d)
        g2 = norm_g[l, 1].reshape(1, d)
        if l % 2 == 0:
            res = _in_stage(l, source, mod_l, g1, w_in_even[i].astype(BF16), None, w_views if l == 0 else None,
                            n_lat, seq, n_batch)
            xa, p = res[:2]
            tabs = _retention_tables(ret_log_decay[i], RET_BLOCK)
            sg_bias = jnp.repeat(sg_b[i].astype(F32).T, SG_CHUNK, axis=1)
            mix = _even_mix(p, tabs, ret_gn_g[i].reshape(1, -1), sg_ln_g[i].reshape(1, -1), sg_ln_b[i].reshape(1, -1),
                            sg_w[i].astype(BF16), sg_bias, n_batch, seq, ctx_len)
            wo = w_out_even[i].astype(BF16)
        else:
            res = _in_stage(l, source, mod_l, g1, w_qkv_odd[i].astype(BF16), (cos_t, sin_t), w_views if l == 0 else None,
                            n_lat, seq, n_batch)
            xa, q, kv = res[:3]
            sink_t = jnp.broadcast_to(attn_sink[i].astype(F32)[:, None], (ATT_HEADS, 128))
            mix = _attention(q, kv, bias, sink_t, n_batch, seq, ctx_len, need_ctx=not last)
            wo = w_o_odd[i].astype(BF16)
        n_act = n_lat if last else n_all
        xm, pos, tab, counts, hs = _post(mix, xa, mod_l, g2, wo, wr_bf, br_t, tri, low, hs, n_act, n_lat, seq, n_batch)
        blk, ea, eb, n_used = _tile_plan(tab, counts, n_tiles, MOE_TILE)
        pos3 = pos.reshape(n_act // tm, 1, tm)
        if l == 0:
            w_bf = res[-3:]
        ys, *w_cast = _moe(blk, ea, eb, n_used, hs, n_tiles, *(w.reshape(n_exp, d, d) for w in w_bf), wr_t, br_t,
                           None if last else w_views, l + 1)
        w_bf = w_cast
        source = (pos3, xm, mod_l, ys)
    out = _combine(pos3, xm, mod_l, final_g.reshape(1, d), ys, n_lat, seq, n_batch)
    return out.reshape(n_batch, seq, d)
```

```python
import functools

import jax
import jax.numpy as jnp
import numpy as np
from jax import lax
from jax.experimental import pallas as pl
from jax.experimental.pallas import tpu as pltpu

F32 = jnp.float32
BF16 = jnp.bfloat16
I32 = jnp.int32

EPS = 1e-6
NEG = -1e30
LOG2E = 1.4426950408889634
GRID_W = 64
ROPE_BASE = 10000.0
RET_HEADS = 4
RET_QK_DIM = 64
RET_V_DIM = 128
SG_GROUPS = 4
SG_CHUNK = 128
RET_BLOCK = 256
ATT_HEADS = 8
ATT_KV_HEADS = 2
ATT_GROUP = ATT_HEADS // ATT_KV_HEADS
HEAD_DIM = 128
ATT_BLOCK = 128
WINDOW = 128
ATT_SUB = 4
ATT_SLAB = 32
N_EXPERTS = 16
N_GROUPS = 4
EXPERTS_PER_GROUP = 4
N_PAIRS = 6
N_BUCKETS = N_GROUPS * N_PAIRS
PAIR_SLOTS = ((0, 1), (0, 2), (0, 3), (1, 3), (1, 2), (3, 2))
BUCKET_ROWS = 32

ROW_TILE = 512
POST_TILE = 1024
MOE_TILE = 256
ROW_SUB = 8
VMEM_LIMIT = 56 * 1024 * 1024

_NT = (((1,), (1,)), ((), ()))
_TN = (((0,), (0,)), ((), ()))


def _params(n_axes, vmem=None):
    return pltpu.CompilerParams(dimension_semantics=("arbitrary",) * n_axes, vmem_limit_bytes=vmem)


def _silu(v):
    return v * jax.nn.sigmoid(v)


def _rms_mod(x, g, scale, shift):
    y = x * lax.rsqrt(jnp.mean(x * x, axis=-1, keepdims=True) + EPS) * g
    return y * (1.0 + scale) + shift


def _mod_kernel(c_ref, w_ref, b_ref, o_ref):
    a = _silu(c_ref[...]).astype(BF16)
    o_ref[0] = jnp.dot(a, w_ref[0].astype(BF16), preferred_element_type=F32) + b_ref[0]


def _modulation(c_rows, w_mod, b_mod):
    depth, d, six_d = w_mod.shape
    mr = c_rows.shape[0]
    tn = 1536
    return pl.pallas_call(
        _mod_kernel,
        grid=(depth, six_d // tn),
        in_specs=[
            pl.BlockSpec((mr, d), lambda l, j: (0, 0)),
            pl.BlockSpec((1, d, tn), lambda l, j: (l, 0, j)),
            pl.BlockSpec((1, 1, tn), lambda l, j: (l, 0, j)),
        ],
        out_specs=pl.BlockSpec((1, mr, tn), lambda l, j: (l, 0, j)),
        out_shape=jax.ShapeDtypeStruct((depth, mr, six_d), F32),
        compiler_params=_params(2, VMEM_LIMIT),
        name="modulation",
    )(c_rows, w_mod, b_mod.reshape(depth, 1, six_d))


def _cast_specs(w, layer, steps):
    w_rows, w_cols = w.shape[1:]
    cast_steps = 1
    while 2 * cast_steps <= steps and w_rows % (2 * cast_steps) == 0:
        cast_steps *= 2
    cast_rows = w_rows // cast_steps
    row = lambda i, *_: jnp.minimum(i, cast_steps - 1)
    return (pl.BlockSpec((1, cast_rows, w_cols), lambda i, *_: (layer, row(i), 0)),
            pl.BlockSpec((cast_rows, w_cols), lambda i, *_: (row(i), 0)),
            jax.ShapeDtypeStruct((w_rows, w_cols), BF16))


def _mod_row_map(tm, n_lat, seq, n_batch):
    def index(i):
        row0 = i * tm
        return jnp.where(row0 < n_lat, row0 // seq, n_batch)
    return index


def _gather_rows(pos_ref, posn_ref, ys_ref, gbuf_ref, sem):
    i = pl.program_id(0)
    slot = i % 2
    tm = gbuf_ref.shape[1]

    def issue(p_ref, s, lo, hi):
        for r in range(lo, hi):
            pltpu.make_async_copy(ys_ref.at[p_ref[0, 0, r]], gbuf_ref.at[s, r], sem.at[s]).start(priority=r % 2)

    def wait(s):
        pltpu.make_async_copy(ys_ref.at[pl.ds(0, tm)], gbuf_ref.at[s], sem.at[s]).wait()

    @pl.when(i == 0)
    def _():
        issue(pos_ref, 0, 0, tm)

    wait(slot)
    rows = gbuf_ref[slot].reshape(tm, gbuf_ref.shape[2] * gbuf_ref.shape[3])

    def issue_next(k, parts):
        issue(posn_ref, 1 - slot, tm * k // parts, tm * (k + 1) // parts)

    def finish():
        @pl.when(i == pl.num_programs(0) - 1)
        def _():
            wait(1 - slot)

    return rows, issue_next, finish


def _project_even(h, w_ref, o_ref, between, cast):
    for c in range(5):
        between(c, 5)
        cast(c)
        p = jnp.dot(h, w_ref[:, 512 * c:512 * (c + 1)], preferred_element_type=F32)
        if c == 0:
            col = lax.broadcasted_iota(I32, p.shape, 1)
            p = jnp.where(col >= 256, p * (RET_QK_DIM ** -0.5), p)
        elif c >= 3:
            p = jax.nn.gelu(p)
        o_ref[:, 512 * c:512 * (c + 1)] = p.astype(BF16)


def _project_odd(h, w_ref, cos_ref, sin_ref, q_ref, kv_ref, between, cast):
    tm = h.shape[0]
    cos = cos_ref[...]
    sin = sin_ref[...]
    lane = lax.broadcasted_iota(I32, (tm, HEAD_DIM), 1)
    first = (lane % 64) < 32

    def rope(p):
        partner = jnp.where(first, pltpu.roll(p, 96, 1), pltpu.roll(p, 32, 1))
        return p * cos + partner * sin

    scale = HEAD_DIM ** -0.5 * LOG2E
    for c in range(3):
        if c < 2:
            between(c, 2)
        cast(c)
        p = jnp.dot(h, w_ref[:, 512 * c:512 * (c + 1)], preferred_element_type=F32)
        for j in range(4):
            pj = p[:, 128 * j:128 * (j + 1)]
            if c < 2:
                q_ref[:, 512 * c + 128 * j:512 * c + 128 * (j + 1)] = (rope(pj) * scale).astype(BF16)
            elif j < 2:
                kv_ref[:, 128 * j:128 * (j + 1)] = rope(pj).astype(BF16)
            else:
                kv_ref[:, 128 * j:128 * (j + 1)] = pj.astype(BF16)


def _in_kernel(*refs, first, odd, lat_tiles, casts):
    if first:
        (xl_ref, xc_ref), refs = refs[:2], refs[2:]
        x = jnp.where(pl.program_id(0) < lat_tiles, xl_ref[...], xc_ref[...])
        between, finish = (lambda k, parts: None), (lambda: None)
    else:
        (pos_ref, posn_ref, x_ref, modp_ref, ys_ref), refs = refs[:5], refs[5:]
        gbuf_ref, sem = refs[-2:]
        refs = refs[:-2]
        d = x_ref.shape[1]
        rows, between, finish = _gather_rows(pos_ref, posn_ref, ys_ref, gbuf_ref, sem)
        x = x_ref[...] + modp_ref[0, :, 5 * d:6 * d] * rows
    mod_ref, g_ref, w_ref = refs[:3]
    d = x.shape[1]
    h = _rms_mod(x, g_ref[...], mod_ref[0, :, d:2 * d], mod_ref[0, :, 0:d]).astype(BF16)
    n_out = 3 if odd else 2
    if casts:
        cast_in, tail, cast_out = refs[-n_out - 6:-n_out - 3], refs[-n_out - 3:-3], refs[-3:]
    else:
        cast_in, tail, cast_out = (), refs[-n_out:], ()

    def cast(k):
        if k < len(cast_in):
            cast_out[k][...] = cast_in[k][0].astype(BF16)

    if odd:
        xo_ref, q_ref, kv_ref = tail
        cos_ref, sin_ref = refs[3:5]
        xo_ref[...] = x
        _project_odd(h, w_ref, cos_ref, sin_ref, q_ref, kv_ref, between, cast)
    else:
        xo_ref, o_ref = tail
        xo_ref[...] = x
        _project_even(h, w_ref, o_ref, between, cast)
    finish()


def _in_stage(layer, source, mod_l, g, w, rope, w_exp, n_lat, seq, n_batch):
    first = len(source) == 2
    odd = rope is not None
    tm = ROW_TILE
    if first:
        xl, xc = source
        d = xl.shape[1]
        n = xl.shape[0] + xc.shape[0]
        lat_tiles = xl.shape[0] // tm
        src_specs = [pl.BlockSpec((tm, d), lambda i: (jnp.minimum(i, lat_tiles - 1), 0)),
                     pl.BlockSpec((tm, d), lambda i: (jnp.maximum(i - lat_tiles, 0), 0))]
        scratch = []
    else:
        pos3, xm, mod_prev, ys = source
        n, d = xm.shape
        lat_tiles = n_lat // tm
        steps = n // tm
        mrow_p = _mod_row_map(tm, n_lat, seq, n_batch)
        src_specs = [pl.BlockSpec((1, 1, tm), lambda i: (i, 0, 0), memory_space=pltpu.SMEM),
                     pl.BlockSpec((1, 1, tm), lambda i: (jnp.minimum(i + 1, steps - 1), 0, 0), memory_space=pltpu.SMEM),
                     pl.BlockSpec((tm, d), lambda i: (i, 0)),
                     pl.BlockSpec((1, 1, 6 * d), lambda i: (mrow_p(i), 0, 0)),
                     pl.BlockSpec(memory_space=pl.ANY)]
        source = (pos3, pos3, xm, mod_prev, ys)
        scratch = [pltpu.VMEM((2, tm) + ys.shape[1:], F32), pltpu.SemaphoreType.DMA((2,))]
    mrow = _mod_row_map(tm, n_lat, seq, n_batch)
    specs = src_specs + [pl.BlockSpec((1, 1, 6 * d), lambda i: (mrow(i), 0, 0)),
                         pl.BlockSpec((1, d), lambda i: (0, 0)),
                         pl.BlockSpec(w.shape, lambda i: (0, 0))]
    args = tuple(source) + (mod_l, g, w)
    out_specs = [pl.BlockSpec((tm, d), lambda i: (i, 0))]
    out_shape = [jax.ShapeDtypeStruct((n, d), F32)]
    if odd:
        tiles_per_seq = seq // tm

        def rope_row(i):
            return jnp.where(i < lat_tiles, i % tiles_per_seq, tiles_per_seq)

        specs += [pl.BlockSpec((tm, HEAD_DIM), lambda i: (rope_row(i), 0))] * 2
        args += tuple(rope)
        widths = (ATT_HEADS * HEAD_DIM, 2 * ATT_KV_HEADS * HEAD_DIM)
    else:
        widths = (w.shape[1],)
    out_specs += [pl.BlockSpec((tm, wd), lambda i: (i, 0)) for wd in widths]
    out_shape += [jax.ShapeDtypeStruct((n, wd), BF16) for wd in widths]
    if w_exp is not None:
        w_spec, wo_spec, wo_shape = _cast_specs(w_exp[0], layer, n // tm)
        specs += [w_spec] * 3
        args += tuple(w_exp)
        out_specs += [wo_spec] * 3
        out_shape += [wo_shape] * 3
    return pl.pallas_call(
        functools.partial(_in_kernel, first=first, odd=odd, lat_tiles=lat_tiles, casts=w_exp is not None),
        grid=(n // tm,),
        in_specs=specs,
        out_specs=out_specs,
        out_shape=out_shape,
        scratch_shapes=scratch,
        compiler_params=_params(1, VMEM_LIMIT),
        name="in_odd" if odd else "in_even",
    )(*args)


def _rope_tables(seq, tm):
    t = np.arange(seq)
    quarter = HEAD_DIM // 4
    inv_freq = ROPE_BASE ** (-np.arange(quarter, dtype=np.float64) / quarter)
    ar = (t // GRID_W)[:, None] * inv_freq[None, :]
    ac = (t % GRID_W)[:, None] * inv_freq[None, :]
    cos = np.concatenate([np.cos(ar), np.cos(ar), np.cos(ac), np.cos(ac)], axis=1)
    sin = np.concatenate([-np.sin(ar), np.sin(ar), -np.sin(ac), np.sin(ac)], axis=1)
    cos = np.concatenate([cos, np.ones((tm, HEAD_DIM))], axis=0)
    sin = np.concatenate([sin, np.zeros((tm, HEAD_DIM))], axis=0)
    return jnp.asarray(cos, F32), jnp.asarray(sin, F32)


def _ret_increments(qk_ref, v_ref, rows, zf_ref, zb_ref):
    upper_rows = lax.broadcasted_iota(I32, (128, 128), 0) < RET_QK_DIM
    out = []
    for grp in range(2):
        k2 = qk_ref[rows, 256 + 128 * grp:256 + 128 * (grp + 1)].astype(F32)
        v2 = v_ref[rows, 256 * grp:256 * (grp + 1)]
        halves = []
        for z_ref in (zf_ref, zb_ref):
            kz = (k2 * z_ref[grp]).astype(BF16)
            full = lax.dot_general(kz, v2, _TN, preferred_element_type=F32)
            halves.append(jnp.where(upper_rows, full[:, 0:128], full[:, 128:256]))
        out.append(jnp.concatenate(halves, axis=1))
    return out


def _mix_outputs(qk_ref, v_ref, gate_ref, u_ref, s_ref, rows, states, dec_ref, xi_ref, gn_ref, lng_ref, lnb_ref,
                 ws_ref, sgb_ref, o_ref):
    lane = lax.broadcasted_iota(I32, (1, 128), 1)
    head_mask = [(lane // RET_QK_DIM == hh).astype(BF16) for hh in range(2)]
    for grp in range(2):
        q2 = qk_ref[rows, 128 * grp:128 * (grp + 1)]
        k2 = qk_ref[rows, 256 + 128 * grp:256 + 128 * (grp + 1)]
        for hh in range(2):
            hd = 2 * grp + hh
            cols = slice(128 * hd, 128 * (hd + 1))
            qm = q2 * head_mask[hh]
            sc = lax.dot_general(qm, k2, _NT, preferred_element_type=F32) * dec_ref[hd]
            o = jnp.dot(sc.astype(BF16), v_ref[rows, cols], preferred_element_type=F32)
            if states is not None:
                cross = jnp.dot(qm, states[grp], preferred_element_type=F32) * xi_ref[hd]
                o = o + cross[:, 0:128] + cross[:, 128:256]
            mu = jnp.mean(o, axis=-1, keepdims=True)
            oc = o - mu
            var = jnp.mean(oc * oc, axis=-1, keepdims=True)
            y = oc * lax.rsqrt(var + EPS) * gn_ref[:, cols]
            y = y * _silu(gate_ref[rows, cols].astype(F32))
            o_ref[rows, cols] = y.astype(BF16)
    for sub in range((rows.stop - rows.start) // SG_CHUNK):
        srows = slice(rows.start + SG_CHUNK * sub, rows.start + SG_CHUNK * (sub + 1))
        s = s_ref[srows, :].astype(F32)
        mu = jnp.mean(s, axis=-1, keepdims=True)
        sc_ = s - mu
        var = jnp.mean(sc_ * sc_, axis=-1, keepdims=True)
        sn = (sc_ * lax.rsqrt(var + EPS) * lng_ref[...] + lnb_ref[...]).astype(BF16)
        for g in range(SG_GROUPS):
            cols = slice(128 * g, 128 * (g + 1))
            mixed = jnp.dot(ws_ref[g], sn[:, cols], preferred_element_type=F32) + sgb_ref[:, cols]
            o_ref[srows, 512 + 128 * g:512 + 128 * (g + 1)] = (u_ref[srows, cols].astype(F32) * mixed).astype(BF16)


def _even_lat_kernel(bqk_ref, bv_ref, cqk_ref, cv_ref, qk_ref, v_ref, gate_ref, u_ref, s_ref, dec_ref, xi_ref,
                     zf_ref, zb_ref, gdec_ref, gn_ref, lng_ref, lnb_ref, ws_ref, sgb_ref, o_ref, ds_ref, sp_ref,
                     *, inc_steps):
    j = pl.program_id(1)
    cb = RET_BLOCK
    n_slots = ds_ref.shape[0]
    inc_chunks = bqk_ref.shape[0] // cb
    out_chunks = qk_ref.shape[0] // cb

    @pl.when(j == 0)
    def _():
        inc = _ret_increments(cqk_ref, cv_ref, slice(0, cb), zf_ref, zb_ref)
        for grp in range(2):
            ds_ref[0, grp] = inc[grp]

    @pl.when(j < inc_steps)
    def _():
        for c in range(inc_chunks):
            inc = _ret_increments(bqk_ref, bv_ref, slice(cb * c, cb * (c + 1)), zf_ref, zb_ref)
            for grp in range(2):
                ds_ref[1 + inc_chunks * j + c, grp] = inc[grp]

    @pl.when(j == inc_steps)
    def _scan():
        fwd = ds_ref[0, :, :, 0:128]
        for n in range(1, n_slots):
            sp_ref[n, :, :, 0:128] = fwd
            fwd = fwd * gdec_ref[:, :, 0:128] + ds_ref[n, :, :, 0:128]
        bwd = ds_ref[0, :, :, 128:256]
        for n in range(n_slots - 1, 0, -1):
            sp_ref[n, :, :, 128:256] = bwd
            bwd = bwd * gdec_ref[:, :, 128:256] + ds_ref[n, :, :, 128:256]

    @pl.when(j >= inc_steps)
    def _():
        for c in range(out_chunks):
            n = 1 + out_chunks * (j - inc_steps) + c
            states = [sp_ref[n, grp].astype(BF16) for grp in range(2)]
            _mix_outputs(qk_ref, v_ref, gate_ref, u_ref, s_ref, slice(cb * c, cb * (c + 1)), states, dec_ref, xi_ref,
                         gn_ref, lng_ref, lnb_ref, ws_ref, sgb_ref, o_ref)


def _even_ctx_kernel(qk_ref, v_ref, gate_ref, u_ref, s_ref, dec_ref, xi_ref, gn_ref, lng_ref, lnb_ref, ws_ref, sgb_ref,
                     o_ref):
    _mix_outputs(qk_ref, v_ref, gate_ref, u_ref, s_ref, slice(0, RET_BLOCK), None, dec_ref, xi_ref,
                 gn_ref, lng_ref, lnb_ref, ws_ref, sgb_ref, o_ref)


def _retention_tables(log_decay, cb):
    lg = -jnp.exp(log_decay.astype(F32))
    pos = jnp.arange(cb, dtype=F32)
    diff = pos[:, None] - pos[None, :]
    lower = jnp.where(diff >= 0, jnp.exp(lg[0][:, None, None] * jnp.maximum(diff, 0.0)[None]), 0.0)
    upper = jnp.where(diff <= 0, jnp.exp(lg[1][:, None, None] * jnp.maximum(-diff, 0.0)[None]), 0.0)
    dec = lower + upper
    xi_f = jnp.exp(lg[0][:, None] * (pos[None, :] + 1.0))
    xi_b = jnp.exp(lg[1][:, None] * (cb - pos[None, :]))
    xi = jnp.concatenate([jnp.broadcast_to(xi_f[:, :, None], (RET_HEADS, cb, 128)),
                          jnp.broadcast_to(xi_b[:, :, None], (RET_HEADS, cb, 128))], axis=2)
    zeta_f = jnp.exp(lg[0][:, None] * (cb - 1.0 - pos[None, :]))
    zeta_b = jnp.exp(lg[1][:, None] * pos[None, :])

    def lanes(z):
        return jnp.repeat(z.reshape(2, 2, cb), RET_QK_DIM, axis=1).transpose(0, 2, 1)

    gstep = jnp.exp(lg * cb)

    def rows(gv):
        return jnp.broadcast_to(jnp.repeat(gv.reshape(2, 2), RET_QK_DIM, axis=1)[:, :, None], (2, 128, 128))

    gdec = jnp.concatenate([rows(gstep[0]), rows(gstep[1])], axis=2)
    return dec, xi, lanes(zeta_f), lanes(zeta_b), gdec


def _even_mix(p, tabs, gn_g, ln_g, ln_b, w_s, sg_bias, n_batch, seq, ctx_len):
    cb = RET_BLOCK
    big, blk = 4 * cb, 2 * cb
    assert ctx_len == cb and seq % big == 0
    n_lat = n_batch * seq
    inc_steps, out_steps = seq // big, seq // blk
    dec, xi, zf, zb, gdec = tabs
    consts = (dec, xi, zf, zb, gdec, gn_g, ln_g, ln_b, w_s, sg_bias)
    full2 = lambda a: pl.BlockSpec(a.shape, lambda b, j, _nd=a.ndim: (0,) * _nd)
    inc_row = lambda b, j: b * inc_steps + jnp.minimum(j, inc_steps - 1)
    out_row = lambda b, j: b * out_steps + jnp.maximum(j - inc_steps, 0)
    ctx_row = lambda b, j: n_lat // cb + b
    lat = pl.pallas_call(
        functools.partial(_even_lat_kernel, inc_steps=inc_steps),
        grid=(n_batch, inc_steps + out_steps),
        in_specs=[pl.BlockSpec((big, 512), lambda b, j: (inc_row(b, j), 0)),
                  pl.BlockSpec((big, 512), lambda b, j: (inc_row(b, j), 1)),
                  pl.BlockSpec((cb, 512), lambda b, j: (ctx_row(b, j), 0)),
                  pl.BlockSpec((cb, 512), lambda b, j: (ctx_row(b, j), 1))]
                 + [pl.BlockSpec((blk, 512), lambda b, j, _c=c: (out_row(b, j), _c)) for c in range(5)]
                 + [full2(a) for a in consts],
        out_specs=pl.BlockSpec((blk, 1024), lambda b, j: (out_row(b, j), 0)),
        out_shape=jax.ShapeDtypeStruct((n_lat, 1024), BF16),
        scratch_shapes=[pltpu.VMEM((seq // cb + 1, 2, 128, 256), F32), pltpu.VMEM((seq // cb + 1, 2, 128, 256), F32)],
        compiler_params=_params(2, VMEM_LIMIT),
        name="even_mix",
    )(p, p, p, p, p, p, p, p, p, *consts)
    ctx_consts = (dec, xi, gn_g, ln_g, ln_b, w_s, sg_bias)
    full1 = lambda a: pl.BlockSpec(a.shape, lambda b, _nd=a.ndim: (0,) * _nd)
    ctx = pl.pallas_call(
        _even_ctx_kernel,
        grid=(n_batch,),
        in_specs=[pl.BlockSpec((cb, 512), lambda b, _c=c: (n_lat // cb + b, _c)) for c in range(5)]
                 + [full1(a) for a in ctx_consts],
        out_specs=pl.BlockSpec((cb, 1024), lambda b: (b, 0)),
        out_shape=jax.ShapeDtypeStruct((n_batch * ctx_len, 1024), BF16),
        compiler_params=_params(1, VMEM_LIMIT),
        name="even_mix_ctx",
    )(p, p, p, p, p, *ctx_consts)
    return lat, ctx


def _attn_kernel(q_ref, kvp_ref, kvm_ref, kvn_ref, kvx_ref, bias_f_ref, bias_m_ref, bias_l_ref, sink_ref, o_ref,
                 s_scr, p_scr, r_scr, *, n_sub):
    blk = ATT_BLOCK
    nloc = 3 * blk
    rows_all = ATT_GROUP * blk
    bias_refs = [bias_f_ref] + [bias_m_ref] * (n_sub - 2) + [bias_l_ref]
    chains = [(kvh, sub) for kvh in range(ATT_KV_HEADS) for sub in range(n_sub)]
    v_all, sinks = {}, {}
    for ci, (kvh, sub) in enumerate(chains):
        kcols = slice(HEAD_DIM * kvh, HEAD_DIM * (kvh + 1))
        vcols = slice(HEAD_DIM * (ATT_KV_HEADS + kvh), HEAD_DIM * (ATT_KV_HEADS + kvh + 1))
        k_parts = [kvp_ref[:, kcols]] + [kvm_ref[blk * j:blk * (j + 1), kcols] for j in range(n_sub)] + [kvn_ref[:, kcols]]
        v_parts = [kvp_ref[:, vcols]] + [kvm_ref[blk * j:blk * (j + 1), vcols] for j in range(n_sub)] + [kvn_ref[:, vcols]]
        heads = [ATT_GROUP * kvh + j for j in range(ATT_GROUP)]
        rows = slice(blk * sub, blk * (sub + 1))
        k_all = jnp.concatenate(k_parts[sub:sub + 3] + [kvx_ref[:, kcols]], axis=0)
        v_all[ci] = jnp.concatenate(v_parts[sub:sub + 3] + [kvx_ref[:, vcols]], axis=0)
        q4 = jnp.concatenate([q_ref[rows, HEAD_DIM * hd:HEAD_DIM * (hd + 1)] for hd in heads], axis=0)
        s_scr[ci] = lax.dot_general(q4, k_all, _NT, preferred_element_type=F32)
        sinks[ci] = LOG2E * jnp.concatenate(
            [jnp.broadcast_to(sink_ref[hd:hd + 1, 0:1], (blk, 1)) for hd in heads], axis=0)
    nchunk_loc = nloc // 128
    nchunk = s_scr.shape[2] // 128

    def shifted(ci, sub, rs, k):
        t = s_scr[ci, rs, 128 * k:128 * (k + 1)]
        return t + bias_refs[sub][0, rs, 128 * k:128 * (k + 1)] if k < nchunk_loc else t

    mrow = {}
    for ci, (kvh, sub) in enumerate(chains):
        for r0 in range(0, rows_all, ATT_SLAB):
            rs = slice(r0, r0 + ATT_SLAB)
            part = shifted(ci, sub, rs, 0)
            for k in range(1, nchunk):
                part = jnp.maximum(part, shifted(ci, sub, rs, k))
            r_scr[ci, rs, :] = part
    for ci in range(len(chains)):
        m = jnp.maximum(jnp.max(r_scr[ci], axis=-1, keepdims=True), sinks[ci])
        mrow[ci] = m
        r_scr[ci] = jnp.broadcast_to(m, r_scr.shape[1:])
    for ci, (kvh, sub) in enumerate(chains):
        for r0 in range(0, rows_all, ATT_SLAB):
            rs = slice(r0, r0 + ATT_SLAB)
            mb = r_scr[ci, rs, :]
            part = None
            for k in range(nchunk):
                p = jnp.exp2(shifted(ci, sub, rs, k) - mb)
                part = p if part is None else part + p
                p_scr[ci, rs, 128 * k:128 * (k + 1)] = p.astype(BF16)
            r_scr[ci, rs, :] = part
    inv = {}
    for ci in range(len(chains)):
        inv[ci] = 1.0 / (jnp.sum(r_scr[ci], axis=-1, keepdims=True) + jnp.exp2(sinks[ci] - mrow[ci]))
    for ci, (kvh, sub) in enumerate(chains):
        rows = slice(blk * sub, blk * (sub + 1))
        o = (jnp.dot(p_scr[ci, :, 0:nloc], v_all[ci][0:nloc], preferred_element_type=F32)
             + jnp.dot(p_scr[ci, :, nloc:], v_all[ci][nloc:], preferred_element_type=F32)) * inv[ci]
        for j in range(ATT_GROUP):
            hd = ATT_GROUP * kvh + j
            o_ref[rows, HEAD_DIM * hd:HEAD_DIM * (hd + 1)] = o[blk * j:blk * (j + 1)].astype(BF16)


def _attn_bias():
    i = np.arange(ATT_BLOCK)[:, None]
    j = np.arange(3 * ATT_BLOCK)[None, :] - ATT_BLOCK
    band = np.abs(i - j) <= WINDOW
    cases = [band & (j >= 0), band, band & (j < ATT_BLOCK), np.zeros_like(band)]
    bias = np.stack([np.where(c, 0.0, NEG) for c in cases]).astype(np.float32)
    return jnp.asarray(np.tile(bias, (1, ATT_GROUP, 1)))


def _attention(q, kv, bias, sink_t, n_batch, seq, ctx_len, need_ctx):
    blk = ATT_BLOCK
    n_lat = n_batch * seq
    nb = seq // blk
    width = 3 * blk + ctx_len

    def call(n_sub, steps, n_rows, qrow, small, mid, cases, name):
        big = n_sub * blk
        chains = ATT_KV_HEADS * n_sub
        return pl.pallas_call(
            functools.partial(_attn_kernel, n_sub=n_sub),
            grid=(n_batch, steps),
            in_specs=[
                pl.BlockSpec((big, q.shape[1]), lambda b, s: (qrow(b, s), 0)),
                pl.BlockSpec((blk, kv.shape[1]), small(-1)),
                pl.BlockSpec((big, kv.shape[1]), mid),
                pl.BlockSpec((blk, kv.shape[1]), small(n_sub)),
                pl.BlockSpec((ctx_len, kv.shape[1]), lambda b, s: (n_lat // ctx_len + b, 0)),
            ] + [pl.BlockSpec((1,) + bias.shape[1:], c) for c in cases] + [
                pl.BlockSpec(sink_t.shape, lambda b, s: (0, 0)),
            ],
            out_specs=pl.BlockSpec((big, q.shape[1]), lambda b, s: (qrow(b, s) - (n_lat // big if name else 0), 0)),
            out_shape=jax.ShapeDtypeStruct((n_rows, q.shape[1]), BF16),
            scratch_shapes=[pltpu.VMEM((chains, ATT_GROUP * blk, width), F32),
                            pltpu.VMEM((chains, ATT_GROUP * blk, width), BF16),
                            pltpu.VMEM((chains, ATT_GROUP * blk, 128), F32)],
            compiler_params=_params(2, VMEM_LIMIT),
            name="window_attention" + name,
        )(q, kv, kv, kv, kv, bias, bias, bias, sink_t)

    big = ATT_SUB * blk
    assert seq % big == 0
    ns = seq // big

    def small(off):
        return lambda b, s: (b * nb + jnp.clip(ATT_SUB * s + off, 0, nb - 1), 0)

    cases = (lambda b, s: (jnp.where(s == 0, 0, 1), 0, 0),
             lambda b, s: (1, 0, 0),
             lambda b, s: (jnp.where(s == ns - 1, 2, 1), 0, 0))
    lat = call(ATT_SUB, ns, n_lat, lambda b, s: b * ns + s, small, lambda b, s: (b * ns + s, 0), cases, "")
    if not need_ctx:
        return (lat,)
    n_cb = ctx_len // blk
    masked = (lambda b, s: (3, 0, 0),) * 3
    first = lambda off: (lambda b, s: (b * nb, 0))
    ctx = call(n_cb, 1, n_batch * ctx_len, lambda b, s: n_lat // ctx_len + b, first,
               lambda b, s: (b * (seq // ctx_len), 0), masked, "_ctx")
    return lat, ctx


def _route(logits):
    m = jnp.max(logits, axis=0, keepdims=True)
    e = jnp.exp(logits - m)
    p = e / jnp.sum(e, axis=0, keepdims=True)
    rows = [p[i:i + 1, :] for i in range(N_EXPERTS)]
    best = None
    gsel = None
    for g in range(N_GROUPS):
        a, b, c, d = rows[4 * g:4 * g + 4]
        m1, n1 = jnp.maximum(a, b), jnp.minimum(a, b)
        m2, n2 = jnp.maximum(c, d), jnp.minimum(c, d)
        score = jnp.maximum(m1, m2) + jnp.maximum(jnp.minimum(m1, m2), jnp.maximum(n1, n2))
        if g == 0:
            best, gsel = score, jnp.zeros(score.shape, I32)
        else:
            upd = score > best
            gsel = jnp.where(upd, g, gsel)
            best = jnp.where(upd, score, best)
    v = [jnp.where(gsel == 0, rows[j], jnp.where(gsel == 1, rows[4 + j], jnp.where(gsel == 2, rows[8 + j], rows[12 + j])))
         for j in range(EXPERTS_PER_GROUP)]
    b1, i1 = v[0], jnp.zeros(gsel.shape, I32)
    for j in range(1, 4):
        upd = v[j] > b1
        i1 = jnp.where(upd, j, i1)
        b1 = jnp.where(upd, v[j], b1)
    b2, i2 = jnp.full(b1.shape, -1.0, F32), jnp.zeros(gsel.shape, I32)
    for j in range(4):
        upd = jnp.logical_and(i1 != j, v[j] > b2)
        i2 = jnp.where(upd, j, i2)
        b2 = jnp.where(upd, v[j], b2)
    lo = jnp.minimum(i1, i2)
    hi = jnp.maximum(i1, i2)
    pair = jnp.where(lo == 0, hi - 1, jnp.where(lo == 1, 6 - hi, 5))
    return gsel * N_PAIRS + pair


def _post_kernel(*refs, lat_tiles, first_trash):
    if lat_tiles is None:
        mix_ref, refs = refs[0], refs[1:]
        mix = mix_ref[...]
    else:
        (ml_ref, mc_ref), refs = refs[:2], refs[2:]
        mix = jnp.where(pl.program_id(0) < lat_tiles, ml_ref[...], mc_ref[...])
    (x_ref, mod_ref, g2_ref, wo_ref, wr_ref, br_ref, tri_ref, low_ref, hs_in_ref,
     xo_ref, pos_ref, tab_ref, cnt_ref, hs_ref,
     carry_ref, cur_ref, nfree_ref, hbuf_ref, posv_ref, poss_ref, sem_s, sem_p) = refs
    del hs_in_ref
    i = pl.program_id(0)
    slot = i % 2
    tm, d = x_ref.shape

    @pl.when(i == 0)
    def _():
        carry_ref[...] = jnp.zeros(carry_ref.shape, F32)
        cur_ref[...] = jnp.zeros(cur_ref.shape, F32)
        nfree_ref[...] = jnp.zeros(nfree_ref.shape, F32)
        tab_ref[...] = jnp.zeros(tab_ref.shape, F32)
        hbuf_ref[1] = jnp.zeros(hbuf_ref.shape[1:], F32)
        posv_ref[1] = first_trash + lax.broadcasted_iota(I32, (1, tm), 1)
        pltpu.make_async_copy(posv_ref.at[1], poss_ref.at[1], sem_p.at[1]).start()

    pltpu.make_async_copy(posv_ref.at[1 - slot], poss_ref.at[1 - slot], sem_p.at[1 - slot]).wait()

    def scatter(s, lo, hi):
        for r in range(lo, hi):
            pltpu.make_async_copy(hbuf_ref.at[s, r], hs_ref.at[poss_ref[s, 0, r]], sem_s).start(priority=r % 2)

    def scatter_wait(s):
        pltpu.make_async_copy(hbuf_ref.at[s], hs_ref.at[pl.ds(0, tm)], sem_s).wait()

    nchunk = 4
    wc = d // nchunk
    xs = []
    for c in range(nchunk):
        scatter(1 - slot, tm * c // nchunk, tm * (c + 1) // nchunk)
        cols = slice(wc * c, wc * (c + 1))
        y = jnp.dot(mix, wo_ref[:, cols], preferred_element_type=F32)
        xc = x_ref[:, cols] + mod_ref[0, :, 2 * d + wc * c:2 * d + wc * (c + 1)] * y
        xo_ref[:, cols] = xc
        xs.append(xc)
    x = jnp.concatenate(xs, axis=1)
    h2 = _rms_mod(x, g2_ref[...], mod_ref[0, :, 4 * d:5 * d], mod_ref[0, :, 3 * d:4 * d])
    hbuf_ref[slot] = h2.reshape(hbuf_ref.shape[1:])
    hi = h2.astype(BF16)
    lo = (h2 - hi.astype(F32)).astype(BF16)
    logits = (lax.dot_general(wr_ref[...], hi, _NT, preferred_element_type=F32)
              + lax.dot_general(wr_ref[...], lo, _NT, preferred_element_type=F32)
              + br_ref[:, 0:1])
    bucket = _route(logits)
    ids = lax.broadcasted_iota(I32, (BUCKET_ROWS, tm), 0)
    onehot = (ids == bucket).astype(F32)
    before = jnp.dot(onehot.astype(BF16), tri_ref[...], preferred_element_type=F32)
    inv = 1.0 / MOE_TILE
    c0 = carry_ref[:, 0:1]
    c1 = c0 + jnp.sum(onehot, axis=1, keepdims=True)
    a0 = jnp.floor((c0 + (MOE_TILE - 1)) * inv)
    new = jnp.floor((c1 + (MOE_TILE - 1)) * inv) - a0
    nfree = nfree_ref[0:1, 0:1]
    new_b = jnp.broadcast_to(new, (BUCKET_ROWS, 128)).astype(BF16)
    id1 = nfree + jnp.dot(low_ref[...], new_b, preferred_element_type=F32)[:, 0:1]
    cur = cur_ref[:, 0:1]
    rank = before + c0
    k = jnp.floor(rank * inv)
    chunk = jnp.where(k < a0, cur, id1 + (k - a0))
    pos = jnp.sum(onehot * (chunk * MOE_TILE + (rank - k * MOE_TILE)), axis=0, keepdims=True).astype(I32)
    pos_ref[...] = pos
    posv_ref[slot] = pos
    pltpu.make_async_copy(posv_ref.at[slot], poss_ref.at[slot], sem_p.at[slot]).start()
    carry_ref[...] = jnp.broadcast_to(c1, carry_ref.shape)
    cnt_ref[...] = jnp.broadcast_to(c1, cnt_ref.shape).astype(I32)
    cur_ref[...] = jnp.broadcast_to(jnp.where(new == 0.0, cur, id1 + new - 1.0), cur_ref.shape)
    nfree_ref[...] = jnp.broadcast_to(nfree + jnp.sum(new, axis=0, keepdims=True), nfree_ref.shape)
    kcol = lax.broadcasted_iota(I32, tab_ref.shape, 1).astype(F32)
    tab = tab_ref[...]
    fresh = jnp.logical_and(kcol >= a0, kcol < a0 + new)
    tab_ref[...] = jnp.where(fresh, id1 + (kcol - a0), tab)
    scatter_wait(1 - slot)

    @pl.when(i == pl.num_programs(0) - 1)
    def _():
        pltpu.make_async_copy(posv_ref.at[slot], poss_ref.at[slot], sem_p.at[slot]).wait()
        scatter(slot, 0, tm)
        scatter_wait(slot)


def _post(mix, xa, mod_l, g2, wo, wr_bf, br_t, tri, low, hs, n_act, n_lat, seq, n_batch):
    d = xa.shape[1]
    tm = POST_TILE
    steps = n_act // tm
    mrow = _mod_row_map(tm, n_lat, seq, n_batch)
    full = lambda a: pl.BlockSpec(a.shape, lambda i, _nd=a.ndim: (0,) * _nd)
    if len(mix) == 1:
        lat_tiles = None
        mix_specs = [pl.BlockSpec((tm, d), lambda i: (i, 0))]
    else:
        lat_tiles = n_lat // tm
        mix_specs = [pl.BlockSpec((tm, d), lambda i: (jnp.minimum(i, lat_tiles - 1), 0)),
                     pl.BlockSpec((tm, d), lambda i: (jnp.maximum(i - lat_tiles, 0), 0))]
    n_in = len(mix_specs)
    return pl.pallas_call(
        functools.partial(_post_kernel, lat_tiles=lat_tiles, first_trash=hs.shape[0] - tm),
        grid=(steps,),
        in_specs=mix_specs + [
            pl.BlockSpec((tm, d), lambda i: (i, 0)),
            pl.BlockSpec((1, 1, 6 * d), lambda i: (mrow(i), 0, 0)),
            full(g2), full(wo), full(wr_bf), full(br_t), full(tri), full(low),
            pl.BlockSpec(memory_space=pl.ANY),
        ],
        out_specs=[
            pl.BlockSpec((tm, d), lambda i: (i, 0)),
            pl.BlockSpec((1, tm), lambda i: (0, i)),
            pl.BlockSpec((BUCKET_ROWS, 256), lambda i: (0, 0)),
            pl.BlockSpec((BUCKET_ROWS, 128), lambda i: (0, 0)),
            pl.BlockSpec(memory_space=pl.ANY),
        ],
        out_shape=[
            jax.ShapeDtypeStruct((n_act, d), F32),
            jax.ShapeDtypeStruct((1, n_act), I32),
            jax.ShapeDtypeStruct((BUCKET_ROWS, 256), F32),
            jax.ShapeDtypeStruct((BUCKET_ROWS, 128), I32),
            jax.ShapeDtypeStruct(hs.shape, F32),
        ],
        scratch_shapes=[pltpu.VMEM((BUCKET_ROWS, 128), F32), pltpu.VMEM((BUCKET_ROWS, 128), F32), pltpu.VMEM((8, 128), F32),
                        pltpu.VMEM((2, tm) + hs.shape[1:], F32), pltpu.VMEM((2, 1, tm), I32), pltpu.SMEM((2, 1, tm), I32),
                        pltpu.SemaphoreType.DMA(()), pltpu.SemaphoreType.DMA((2,))],
        input_output_aliases={n_in + 8: 4},
        compiler_params=_params(1, VMEM_LIMIT),
        name="post_mixer",
    )(*mix, xa, mod_l, g2, wo, wr_bf, br_t, tri, low, hs)


def _moe_kernel(blk_ref, ea_ref, eb_ref, nu_ref, h_ref, wga_ref, wua_ref, wda_ref, wgb_ref, wub_ref, wdb_ref,
                wrt_ref, brt_ref, *refs):
    o_ref = refs[len(refs) // 2]
    n_cast = len(refs) // 2
    for k in range(n_cast):
        refs[n_cast + 1 + k][...] = refs[k][0].astype(BF16)
    t = pl.program_id(0)

    @pl.when(t >= nu_ref[0])
    def _():
        o_ref[...] = jnp.zeros(o_ref.shape, F32)

    @pl.when(t < nu_ref[0])
    def _():
        ea = ea_ref[t]
        eb = eb_ref[t]
        tm = h_ref.shape[0]
        h = h_ref[...].reshape(tm, h_ref.shape[1] * h_ref.shape[2])
        hb = h.astype(BF16)
        wdiff = wrt_ref[pl.ds(ea, 1), :] - wrt_ref[pl.ds(eb, 1), :]
        bdiff = brt_ref[pl.ds(ea, 1), 0:1] - brt_ref[pl.ds(eb, 1), 0:1]
        wa = jax.nn.sigmoid(jnp.sum(h * wdiff, axis=-1, keepdims=True) + bdiff)

        slots = ((wga_ref, wua_ref, wda_ref), (wgb_ref, wub_ref, wdb_ref))
        f = wga_ref.shape[2]
        chains = [(e, c) for e in range(2) for c in range(f // 512)]
        gu = {}
        for e, c in chains:
            cols = slice(512 * c, 512 * (c + 1))
            gu[e, c] = (jnp.dot(hb, slots[e][0][0, :, cols], preferred_element_type=F32),
                        jnp.dot(hb, slots[e][1][0, :, cols], preferred_element_type=F32))
        act = {k: (_silu(g) * u).astype(BF16) for k, (g, u) in gu.items()}
        outs = [None, None]
        for e, c in chains:
            part = jnp.dot(act[e, c], slots[e][2][0, 512 * c:512 * (c + 1), :], preferred_element_type=F32)
            outs[e] = part if outs[e] is None else outs[e] + part
        y = wa * outs[0] + (1.0 - wa) * outs[1]
        o_ref[...] = y.reshape(o_ref.shape)


def _moe(blk, ea, eb, nu, hs, n_tiles, wg, wu, wd, wr_t, br_t, w_next, next_layer):
    row = hs.shape[1:]
    tm = MOE_TILE
    d, f = wg.shape[1:]
    n_cast = 0 if w_next is None else len(w_next)
    if n_cast:
        w_spec, wo_spec, wo_shape = _cast_specs(w_next[0], next_layer, n_tiles)
    else:
        w_spec = wo_spec = wo_shape = None
    wa_spec = lambda shape: pl.BlockSpec(shape, lambda t, blk_, ea_, eb_, nu_: (ea_[t], 0, 0))
    wb_spec = lambda shape: pl.BlockSpec(shape, lambda t, blk_, ea_, eb_, nu_: (eb_[t], 0, 0))
    grid_spec = pltpu.PrefetchScalarGridSpec(
        num_scalar_prefetch=4,
        grid=(n_tiles,),
        in_specs=[
            pl.BlockSpec((tm,) + row, lambda t, blk_, ea_, eb_, nu_: (blk_[t], 0, 0)),
            wa_spec((1, d, f)), wa_spec((1, d, f)), wa_spec((1, f, d)),
            wb_spec((1, d, f)), wb_spec((1, d, f)), wb_spec((1, f, d)),
            pl.BlockSpec(wr_t.shape, lambda t, blk_, ea_, eb_, nu_: (0, 0)),
            pl.BlockSpec(br_t.shape, lambda t, blk_, ea_, eb_, nu_: (0, 0)),
        ] + [w_spec] * n_cast,
        out_specs=[pl.BlockSpec((tm,) + row, lambda t, blk_, ea_, eb_, nu_: (blk_[t], 0, 0))] + [wo_spec] * n_cast,
    )
    return pl.pallas_call(
        _moe_kernel,
        grid_spec=grid_spec,
        out_shape=[jax.ShapeDtypeStruct((n_tiles * tm,) + row, F32)] + [wo_shape] * n_cast,
        compiler_params=_params(1, VMEM_LIMIT),
        name="experts",
    )(blk, ea, eb, nu, hs, wg, wu, wd, wg, wu, wd, wr_t, br_t, *(w_next or ()))


def _combine_kernel(pos_ref, posn_ref, x_ref, mod_ref, fg_ref, ys_ref, o_ref, gbuf_ref, sem):
    d = x_ref.shape[1]
    rows, issue_next, finish = _gather_rows(pos_ref, posn_ref, ys_ref, gbuf_ref, sem)
    issue_next(0, 1)
    x = x_ref[...] + mod_ref[0, :, 5 * d:6 * d] * rows
    o_ref[...] = x * lax.rsqrt(jnp.mean(x * x, axis=-1, keepdims=True) + EPS) * fg_ref[...]
    finish()


def _combine(pos3, xa, mod_l, fg, ys, n_lat, seq, n_batch):
    n, d = xa.shape
    tm = ROW_TILE
    steps = n // tm
    mrow = _mod_row_map(tm, n_lat, seq, n_batch)
    return pl.pallas_call(
        _combine_kernel,
        grid=(steps,),
        in_specs=[
            pl.BlockSpec((1, 1, tm), lambda i: (i, 0, 0), memory_space=pltpu.SMEM),
            pl.BlockSpec((1, 1, tm), lambda i: (jnp.minimum(i + 1, steps - 1), 0, 0), memory_space=pltpu.SMEM),
            pl.BlockSpec((tm, d), lambda i: (i, 0)),
            pl.BlockSpec((1, 1, 6 * d), lambda i: (mrow(i), 0, 0)),
            pl.BlockSpec((1, d), lambda i: (0, 0)),
            pl.BlockSpec(memory_space=pl.ANY),
        ],
        out_specs=pl.BlockSpec((tm, d), lambda i: (i, 0)),
        out_shape=jax.ShapeDtypeStruct((n, d), F32),
        scratch_shapes=[pltpu.VMEM((2, tm) + ys.shape[1:], F32), pltpu.SemaphoreType.DMA((2,))],
        compiler_params=_params(1, VMEM_LIMIT),
        name="combine",
    )(pos3, pos3, xa, mod_l, fg, ys)


def _tile_plan(tab, counts, n_tiles, tile):
    cnt = counts[:N_BUCKETS, 0]
    n_chunks = (cnt + tile - 1) // tile
    ends = jnp.cumsum(n_chunks)
    starts = ends - n_chunks
    n_used = ends[-1]
    t = jnp.arange(n_tiles, dtype=I32)
    t_eff = jnp.minimum(t, jnp.maximum(n_used - 1, 0))
    tb = jnp.minimum(jnp.sum((ends[None, :] <= t_eff[:, None]).astype(I32), axis=1), N_BUCKETS - 1)
    owned = tab.astype(I32)[tb, t_eff - starts[tb]]
    blk = jnp.where(t < n_used, owned, t)
    pairs = np.array(PAIR_SLOTS, dtype=np.int32)
    ea = (tb // N_PAIRS) * EXPERTS_PER_GROUP + jnp.asarray(pairs[:, 0])[tb % N_PAIRS]
    eb = (tb // N_PAIRS) * EXPERTS_PER_GROUP + jnp.asarray(pairs[:, 1])[tb % N_PAIRS]
    return blk.astype(I32), ea.astype(I32), eb.astype(I32), n_used.astype(I32).reshape(1)


def kernel(x, c, ctx, c_ctx, w_mod, b_mod, norm_g, w_in_even, w_out_even, ret_log_decay, ret_gn_g, sg_ln_g, sg_ln_b, sg_w, sg_b, w_qkv_odd, w_o_odd, attn_sink, w_router, b_router, w_e_gate, w_e_up, w_e_down, final_g):
    n_batch, seq, d = x.shape
    ctx_len = ctx.shape[1]
    depth = w_mod.shape[0]
    n_lat = n_batch * seq
    n_all = n_lat + n_batch * ctx_len
    tm = ROW_TILE
    assert seq % POST_TILE == 0 and (n_batch * ctx_len) % POST_TILE == 0

    mod_rows = -(-(n_batch + 1) // 16) * 16
    c_rows = jnp.zeros((mod_rows, d), F32).at[:n_batch].set(c).at[n_batch].set(c_ctx)
    mod = _modulation(c_rows, w_mod, b_mod).reshape(depth, mod_rows, 1, 6 * d)

    wr_t = w_router.T.astype(F32)
    wr_bf = wr_t.astype(BF16)
    br_t = jnp.broadcast_to(b_router.astype(F32)[:, None], (N_EXPERTS, 128))
    tri = jnp.asarray(np.triu(np.ones((POST_TILE, POST_TILE), np.float32), k=1), BF16)
    low = jnp.asarray(np.tril(np.ones((BUCKET_ROWS, BUCKET_ROWS), np.float32), k=-1), BF16)
    cos_t, sin_t = _rope_tables(seq, tm)
    bias = _attn_bias()

    n_exp, d_exp = w_e_gate.shape[1], w_e_gate.shape[3]
    assert d_exp == d
    w_views = [w.reshape(depth, n_exp * d, d) for w in (w_e_gate, w_e_up, w_e_down)]

    n_tiles = n_all // MOE_TILE + N_BUCKETS
    p_rows = n_tiles * MOE_TILE
    hs = jnp.zeros((p_rows + POST_TILE, ROW_SUB, d // ROW_SUB), F32)

    source = (x.reshape(n_lat, d), ctx.reshape(n_batch * ctx_len, d))
    for l in range(depth):
        i = l // 2
        last = l == depth - 1
        mod_l = mod[l]
        g1 = norm_g[l, 0].reshape(1, d)
        g2 = norm_g[l, 1].reshape(1, d)
        if l % 2 == 0:
            res = _in_stage(l, source, mod_l, g1, w_in_even[i].astype(BF16), None, w_views if l == 0 else None,
                            n_lat, seq, n_batch)
            xa, p = res[:2]
            tabs = _retention_tables(ret_log_decay[i], RET_BLOCK)
            sg_bias = jnp.repeat(sg_b[i].astype(F32).T, SG_CHUNK, axis=1)
            mix = _even_mix(p, tabs, ret_gn_g[i].reshape(1, -1), sg_ln_g[i].reshape(1, -1), sg_ln_b[i].reshape(1, -1),
                            sg_w[i].astype(BF16), sg_bias, n_batch, seq, ctx_len)
            wo = w_out_even[i].astype(BF16)
        else:
            res = _in_stage(l, source, mod_l, g1, w_qkv_odd[i].astype(BF16), (cos_t, sin_t), w_views if l == 0 else None,
                            n_lat, seq, n_batch)
            xa, q, kv = res[:3]
            sink_t = jnp.broadcast_to(attn_sink[i].astype(F32)[:, None], (ATT_HEADS, 128))
            mix = _attention(q, kv, bias, sink_t, n_batch, seq, ctx_len, need_ctx=not last)
            wo = w_o_odd[i].astype(BF16)
        n_act = n_lat if last else n_all
        xm, pos, tab, counts, hs = _post(mix, xa, mod_l, g2, wo, wr_bf, br_t, tri, low, hs, n_act, n_lat, seq, n_batch)
        blk, ea, eb, n_used = _tile_plan(tab, counts, n_tiles, MOE_TILE)
        pos3 = pos.reshape(n_act // tm, 1, tm)
        if l == 0:
            w_bf = res[-3:]
        ys, *w_cast = _moe(blk, ea, eb, n_used, hs, n_tiles, *(w.reshape(n_exp, d, d) for w in w_bf), wr_t, br_t,
                           None if last else w_views, l + 1)
        w_bf = w_cast
        source = (pos3, xm, mod_l, ys)
    out = _combine(pos3, xm, mod_l, final_g.reshape(1, d), ys, n_lat, seq, n_batch)
    return out.reshape(n_batch, seq, d)
```

```python
import functools

import jax
import jax.numpy as jnp
import numpy as np
from jax import lax
from jax.experimental import pallas as pl
from jax.experimental.pallas import tpu as pltpu

F32 = jnp.float32
BF16 = jnp.bfloat16
I32 = jnp.int32

EPS = 1e-6
NEG = -1e30
LOG2E = 1.4426950408889634
GRID_W = 64
ROPE_BASE = 10000.0
RET_HEADS = 4
RET_QK_DIM = 64
SG_GROUPS = 4
SG_CHUNK = 128
RET_BLOCK = 256
ATT_HEADS = 8
ATT_KV_HEADS = 2
ATT_GROUP = ATT_HEADS // ATT_KV_HEADS
HEAD_DIM = 128
ATT_BLOCK = 128
WINDOW = 128
ATT_SUB = 4
ATT_SLAB = 32
N_EXPERTS = 16
N_GROUPS = 4
EXPERTS_PER_GROUP = 4
N_PAIRS = 6
N_BUCKETS = N_GROUPS * N_PAIRS
PAIR_SLOTS = ((0, 1), (0, 2), (0, 3), (1, 3), (1, 2), (3, 2))
BUCKET_ROWS = 32

ROW_TILE = 512
POST_TILE = 1024
MOE_TILE = 512
ROW_SUB = 8
VMEM_LIMIT = 56 * 1024 * 1024

_NT = (((1,), (1,)), ((), ()))
_TN = (((0,), (0,)), ((), ()))


def _params(n_axes, vmem=None):
    return pltpu.CompilerParams(dimension_semantics=("arbitrary",) * n_axes, vmem_limit_bytes=vmem)


def _silu(v):
    return v * jax.nn.sigmoid(v)


def _rms_mod(x, g, scale, shift):
    y = x * lax.rsqrt(jnp.mean(x * x, axis=-1, keepdims=True) + EPS) * g
    return y * (1.0 + scale) + shift


def _mod_kernel(c_ref, w_ref, b_ref, o_ref):
    a = _silu(c_ref[...]).astype(BF16)
    o_ref[0] = jnp.dot(a, w_ref[0].astype(BF16), preferred_element_type=F32) + b_ref[0]


def _modulation(c_rows, w_mod, b_mod):
    depth, d, six_d = w_mod.shape
    mr = c_rows.shape[0]
    tn = 1536
    return pl.pallas_call(
        _mod_kernel,
        grid=(depth, six_d // tn),
        in_specs=[
            pl.BlockSpec((mr, d), lambda l, j: (0, 0)),
            pl.BlockSpec((1, d, tn), lambda l, j: (l, 0, j)),
            pl.BlockSpec((1, 1, tn), lambda l, j: (l, 0, j)),
        ],
        out_specs=pl.BlockSpec((1, mr, tn), lambda l, j: (l, 0, j)),
        out_shape=jax.ShapeDtypeStruct((depth, mr, six_d), F32),
        compiler_params=_params(2, VMEM_LIMIT),
        name="modulation",
    )(c_rows, w_mod, b_mod.reshape(depth, 1, six_d))


def _cast_specs(w, layer, steps):
    w_rows, w_cols = w.shape[1:]
    cast_steps = 1
    while 2 * cast_steps <= steps and w_rows % (2 * cast_steps) == 0:
        cast_steps *= 2
    cast_rows = w_rows // cast_steps
    row = lambda i, *_: jnp.minimum(i, cast_steps - 1)
    return (pl.BlockSpec((1, cast_rows, w_cols), lambda i, *_: (layer, row(i), 0)),
            pl.BlockSpec((cast_rows, w_cols), lambda i, *_: (row(i), 0)),
            jax.ShapeDtypeStruct((w_rows, w_cols), BF16))


def _mod_row_map(tm, n_lat, seq, n_batch):
    def index(i):
        row0 = i * tm
        return jnp.where(row0 < n_lat, row0 // seq, n_batch)
    return index


def _gather_rows(pos_ref, posn_ref, ys_ref, gbuf_ref, sem):
    i = pl.program_id(0)
    slot = i % 2
    tm = gbuf_ref.shape[1]

    def issue(p_ref, s, lo, hi):
        for r in range(lo, hi):
            pltpu.make_async_copy(ys_ref.at[p_ref[0, 0, r]], gbuf_ref.at[s, r], sem.at[s]).start(priority=r % 2)

    def wait(s):
        pltpu.make_async_copy(ys_ref.at[pl.ds(0, tm)], gbuf_ref.at[s], sem.at[s]).wait()

    @pl.when(i == 0)
    def _():
        issue(pos_ref, 0, 0, tm)

    wait(slot)
    rows = gbuf_ref[slot].reshape(tm, gbuf_ref.shape[2] * gbuf_ref.shape[3])

    def issue_next(k, parts):
        issue(posn_ref, 1 - slot, tm * k // parts, tm * (k + 1) // parts)

    def finish():
        @pl.when(i == pl.num_programs(0) - 1)
        def _():
            wait(1 - slot)

    return rows, issue_next, finish


def _project_even(h, w_ref, o_ref, between, cast):
    for c in range(5):
        between(c, 5)
        cast(c)
        p = jnp.dot(h, w_ref[:, 512 * c:512 * (c + 1)], preferred_element_type=F32)
        if c == 0:
            col = lax.broadcasted_iota(I32, p.shape, 1)
            p = jnp.where(col >= 256, p * (RET_QK_DIM ** -0.5), p)
        elif c >= 3:
            p = jax.nn.gelu(p)
        o_ref[:, 512 * c:512 * (c + 1)] = p.astype(BF16)


def _project_odd(h, w_ref, cos_ref, sin_ref, q_ref, kv_ref, between, cast):
    tm = h.shape[0]
    cos = cos_ref[...]
    sin = sin_ref[...]
    lane = lax.broadcasted_iota(I32, (tm, HEAD_DIM), 1)
    first = (lane % 64) < 32

    def rope(p):
        partner = jnp.where(first, pltpu.roll(p, 96, 1), pltpu.roll(p, 32, 1))
        return p * cos + partner * sin

    scale = HEAD_DIM ** -0.5 * LOG2E
    for c in range(3):
        if c < 2:
            between(c, 2)
        cast(c)
        p = jnp.dot(h, w_ref[:, 512 * c:512 * (c + 1)], preferred_element_type=F32)
        for j in range(4):
            pj = p[:, 128 * j:128 * (j + 1)]
            if c < 2:
                q_ref[:, 512 * c + 128 * j:512 * c + 128 * (j + 1)] = (rope(pj) * scale).astype(BF16)
            elif j < 2:
                kv_ref[:, 128 * j:128 * (j + 1)] = rope(pj).astype(BF16)
            else:
                kv_ref[:, 128 * j:128 * (j + 1)] = pj.astype(BF16)


def _in_kernel(*refs, first, odd, lat_tiles, casts):
    if first:
        (xl_ref, xc_ref), refs = refs[:2], refs[2:]
        x = jnp.where(pl.program_id(0) < lat_tiles, xl_ref[...], xc_ref[...])
        between, finish = (lambda k, parts: None), (lambda: None)
    else:
        (pos_ref, posn_ref, x_ref, modp_ref, ys_ref), refs = refs[:5], refs[5:]
        gbuf_ref, sem = refs[-2:]
        refs = refs[:-2]
        d = x_ref.shape[1]
        rows, between, finish = _gather_rows(pos_ref, posn_ref, ys_ref, gbuf_ref, sem)
        x = x_ref[...] + modp_ref[0, :, 5 * d:6 * d] * rows
    mod_ref, g_ref, w_ref = refs[:3]
    d = x.shape[1]
    h = _rms_mod(x, g_ref[...], mod_ref[0, :, d:2 * d], mod_ref[0, :, 0:d]).astype(BF16)
    n_out = 3 if odd else 2
    if casts:
        cast_in, tail, cast_out = refs[-n_out - 6:-n_out - 3], refs[-n_out - 3:-3], refs[-3:]
    else:
        cast_in, tail, cast_out = (), refs[-n_out:], ()

    def cast(k):
        if k < len(cast_in):
            cast_out[k][...] = cast_in[k][0].astype(BF16)

    if odd:
        xo_ref, q_ref, kv_ref = tail
        cos_ref, sin_ref = refs[3:5]
        xo_ref[...] = x
        _project_odd(h, w_ref, cos_ref, sin_ref, q_ref, kv_ref, between, cast)
    else:
        xo_ref, o_ref = tail
        xo_ref[...] = x
        _project_even(h, w_ref, o_ref, between, cast)
    finish()


def _in_stage(layer, source, mod_l, g, w, rope, w_exp, n_lat, seq, n_batch):
    first = len(source) == 2
    odd = rope is not None
    tm = ROW_TILE
    if first:
        xl, xc = source
        d = xl.shape[1]
        n = xl.shape[0] + xc.shape[0]
        lat_tiles = xl.shape[0] // tm
        src_specs = [pl.BlockSpec((tm, d), lambda i: (jnp.minimum(i, lat_tiles - 1), 0)),
                     pl.BlockSpec((tm, d), lambda i: (jnp.maximum(i - lat_tiles, 0), 0))]
        scratch = []
    else:
        pos3, xm, mod_prev, ys = source
        n, d = xm.shape
        lat_tiles = n_lat // tm
        steps = n // tm
        mrow_p = _mod_row_map(tm, n_lat, seq, n_batch)
        src_specs = [pl.BlockSpec((1, 1, tm), lambda i: (i, 0, 0), memory_space=pltpu.SMEM),
                     pl.BlockSpec((1, 1, tm), lambda i: (jnp.minimum(i + 1, steps - 1), 0, 0), memory_space=pltpu.SMEM),
                     pl.BlockSpec((tm, d), lambda i: (i, 0)),
                     pl.BlockSpec((1, 1, 6 * d), lambda i: (mrow_p(i), 0, 0)),
                     pl.BlockSpec(memory_space=pl.ANY)]
        source = (pos3, pos3, xm, mod_prev, ys)
        scratch = [pltpu.VMEM((2, tm) + ys.shape[1:], F32), pltpu.SemaphoreType.DMA((2,))]
    mrow = _mod_row_map(tm, n_lat, seq, n_batch)
    specs = src_specs + [pl.BlockSpec((1, 1, 6 * d), lambda i: (mrow(i), 0, 0)),
                         pl.BlockSpec((1, d), lambda i: (0, 0)),
                         pl.BlockSpec(w.shape, lambda i: (0, 0))]
    args = tuple(source) + (mod_l, g, w)
    out_specs = [pl.BlockSpec((tm, d), lambda i: (i, 0))]
    out_shape = [jax.ShapeDtypeStruct((n, d), F32)]
    if odd:
        tiles_per_seq = seq // tm

        def rope_row(i):
            return jnp.where(i < lat_tiles, i % tiles_per_seq, tiles_per_seq)

        specs += [pl.BlockSpec((tm, HEAD_DIM), lambda i: (rope_row(i), 0))] * 2
        args += tuple(rope)
        widths = (ATT_HEADS * HEAD_DIM, 2 * ATT_KV_HEADS * HEAD_DIM)
    else:
        widths = (w.shape[1],)
    out_specs += [pl.BlockSpec((tm, wd), lambda i: (i, 0)) for wd in widths]
    out_shape += [jax.ShapeDtypeStruct((n, wd), BF16) for wd in widths]
    if w_exp is not None:
        w_spec, wo_spec, wo_shape = _cast_specs(w_exp[0], layer, n // tm)
        specs += [w_spec] * 3
        args += tuple(w_exp)
        out_specs += [wo_spec] * 3
        out_shape += [wo_shape] * 3
    return pl.pallas_call(
        functools.partial(_in_kernel, first=first, odd=odd, lat_tiles=lat_tiles, casts=w_exp is not None),
        grid=(n // tm,),
        in_specs=specs,
        out_specs=out_specs,
        out_shape=out_shape,
        scratch_shapes=scratch,
        compiler_params=_params(1, VMEM_LIMIT),
        name="in_odd" if odd else "in_even",
    )(*args)


def _rope_tables(seq, tm):
    t = np.arange(seq)
    quarter = HEAD_DIM // 4
    inv_freq = ROPE_BASE ** (-np.arange(quarter, dtype=np.float64) / quarter)
    ar = (t // GRID_W)[:, None] * inv_freq[None, :]
    ac = (t % GRID_W)[:, None] * inv_freq[None, :]
    cos = np.concatenate([np.cos(ar), np.cos(ar), np.cos(ac), np.cos(ac)], axis=1)
    sin = np.concatenate([-np.sin(ar), np.sin(ar), -np.sin(ac), np.sin(ac)], axis=1)
    cos = np.concatenate([cos, np.ones((tm, HEAD_DIM))], axis=0)
    sin = np.concatenate([sin, np.zeros((tm, HEAD_DIM))], axis=0)
    return jnp.asarray(cos, F32), jnp.asarray(sin, F32)


def _ret_increments(qk_ref, v_ref, rows, zf_ref, zb_ref):
    upper_rows = lax.broadcasted_iota(I32, (128, 128), 0) < RET_QK_DIM
    out = []
    for grp in range(2):
        k2 = qk_ref[rows, 256 + 128 * grp:256 + 128 * (grp + 1)].astype(F32)
        v2 = v_ref[rows, 256 * grp:256 * (grp + 1)]
        halves = []
        for z_ref in (zf_ref, zb_ref):
            kz = (k2 * z_ref[grp]).astype(BF16)
            full = lax.dot_general(kz, v2, _TN, preferred_element_type=F32)
            halves.append(jnp.where(upper_rows, full[:, 0:128], full[:, 128:256]))
        out.append(jnp.concatenate(halves, axis=1))
    return out


def _mix_outputs(qk_ref, v_ref, gate_ref, u_ref, s_ref, rows, states, dec_ref, xi_ref, gn_ref, lng_ref, lnb_ref,
                 ws_ref, sgb_ref, o_ref):
    lane = lax.broadcasted_iota(I32, (1, 128), 1)
    head_mask = [(lane // RET_QK_DIM == hh).astype(BF16) for hh in range(2)]
    for grp in range(2):
        q2 = qk_ref[rows, 128 * grp:128 * (grp + 1)]
        k2 = qk_ref[rows, 256 + 128 * grp:256 + 128 * (grp + 1)]
        for hh in range(2):
            hd = 2 * grp + hh
            cols = slice(128 * hd, 128 * (hd + 1))
            qm = q2 * head_mask[hh]
            sc = lax.dot_general(qm, k2, _NT, preferred_element_type=F32) * dec_ref[hd]
            o = jnp.dot(sc.astype(BF16), v_ref[rows, cols], preferred_element_type=F32)
            if states is not None:
                cross = jnp.dot(qm, states[grp], preferred_element_type=F32) * xi_ref[hd]
                o = o + cross[:, 0:128] + cross[:, 128:256]
            mu = jnp.mean(o, axis=-1, keepdims=True)
            oc = o - mu
            var = jnp.mean(oc * oc, axis=-1, keepdims=True)
            y = oc * lax.rsqrt(var + EPS) * gn_ref[:, cols]
            y = y * _silu(gate_ref[rows, cols].astype(F32))
            o_ref[rows, cols] = y.astype(BF16)
    for sub in range((rows.stop - rows.start) // SG_CHUNK):
        srows = slice(rows.start + SG_CHUNK * sub, rows.start + SG_CHUNK * (sub + 1))
        s = s_ref[srows, :].astype(F32)
        mu = jnp.mean(s, axis=-1, keepdims=True)
        sc_ = s - mu
        var = jnp.mean(sc_ * sc_, axis=-1, keepdims=True)
        sn = (sc_ * lax.rsqrt(var + EPS) * lng_ref[...] + lnb_ref[...]).astype(BF16)
        for g in range(SG_GROUPS):
            cols = slice(128 * g, 128 * (g + 1))
            mixed = jnp.dot(ws_ref[g], sn[:, cols], preferred_element_type=F32) + sgb_ref[:, cols]
            o_ref[srows, 512 + 128 * g:512 + 128 * (g + 1)] = (u_ref[srows, cols].astype(F32) * mixed).astype(BF16)


def _even_lat_kernel(bqk_ref, bv_ref, cqk_ref, cv_ref, qk_ref, v_ref, gate_ref, u_ref, s_ref, dec_ref, xi_ref,
                     zf_ref, zb_ref, gdec_ref, gn_ref, lng_ref, lnb_ref, ws_ref, sgb_ref, o_ref, ds_ref, sp_ref,
                     *, inc_steps):
    j = pl.program_id(1)
    cb = RET_BLOCK
    n_slots = ds_ref.shape[0]
    inc_chunks = bqk_ref.shape[0] // cb
    out_chunks = qk_ref.shape[0] // cb

    @pl.when(j == 0)
    def _():
        inc = _ret_increments(cqk_ref, cv_ref, slice(0, cb), zf_ref, zb_ref)
        for grp in range(2):
            ds_ref[0, grp] = inc[grp]

    @pl.when(j < inc_steps)
    def _():
        for c in range(inc_chunks):
            inc = _ret_increments(bqk_ref, bv_ref, slice(cb * c, cb * (c + 1)), zf_ref, zb_ref)
            for grp in range(2):
                ds_ref[1 + inc_chunks * j + c, grp] = inc[grp]

    @pl.when(j == inc_steps)
    def _scan():
        fwd = ds_ref[0, :, :, 0:128]
        for n in range(1, n_slots):
            sp_ref[n, :, :, 0:128] = fwd
            fwd = fwd * gdec_ref[:, :, 0:128] + ds_ref[n, :, :, 0:128]
        bwd = ds_ref[0, :, :, 128:256]
        for n in range(n_slots - 1, 0, -1):
            sp_ref[n, :, :, 128:256] = bwd
            bwd = bwd * gdec_ref[:, :, 128:256] + ds_ref[n, :, :, 128:256]

    @pl.when(j >= inc_steps)
    def _():
        for c in range(out_chunks):
            n = 1 + out_chunks * (j - inc_steps) + c
            states = [sp_ref[n, grp].astype(BF16) for grp in range(2)]
            _mix_outputs(qk_ref, v_ref, gate_ref, u_ref, s_ref, slice(cb * c, cb * (c + 1)), states, dec_ref, xi_ref,
                         gn_ref, lng_ref, lnb_ref, ws_ref, sgb_ref, o_ref)


def _even_ctx_kernel(qk_ref, v_ref, gate_ref, u_ref, s_ref, dec_ref, xi_ref, gn_ref, lng_ref, lnb_ref, ws_ref, sgb_ref,
                     o_ref):
    _mix_outputs(qk_ref, v_ref, gate_ref, u_ref, s_ref, slice(0, RET_BLOCK), None, dec_ref, xi_ref,
                 gn_ref, lng_ref, lnb_ref, ws_ref, sgb_ref, o_ref)


def _retention_tables(log_decay, cb):
    lg = -jnp.exp(log_decay.astype(F32))
    pos = jnp.arange(cb, dtype=F32)
    diff = pos[:, None] - pos[None, :]
    lower = jnp.where(diff >= 0, jnp.exp(lg[0][:, None, None] * jnp.maximum(diff, 0.0)[None]), 0.0)
    upper = jnp.where(diff <= 0, jnp.exp(lg[1][:, None, None] * jnp.maximum(-diff, 0.0)[None]), 0.0)
    dec = lower + upper
    xi_f = jnp.exp(lg[0][:, None] * (pos[None, :] + 1.0))
    xi_b = jnp.exp(lg[1][:, None] * (cb - pos[None, :]))
    xi = jnp.concatenate([jnp.broadcast_to(xi_f[:, :, None], (RET_HEADS, cb, 128)),
                          jnp.broadcast_to(xi_b[:, :, None], (RET_HEADS, cb, 128))], axis=2)
    zeta_f = jnp.exp(lg[0][:, None] * (cb - 1.0 - pos[None, :]))
    zeta_b = jnp.exp(lg[1][:, None] * pos[None, :])

    def lanes(z):
        return jnp.repeat(z.reshape(2, 2, cb), RET_QK_DIM, axis=1).transpose(0, 2, 1)

    gstep = jnp.exp(lg * cb)

    def rows(gv):
        return jnp.broadcast_to(jnp.repeat(gv.reshape(2, 2), RET_QK_DIM, axis=1)[:, :, None], (2, 128, 128))

    gdec = jnp.concatenate([rows(gstep[0]), rows(gstep[1])], axis=2)
    return dec, xi, lanes(zeta_f), lanes(zeta_b), gdec


def _even_mix(p, tabs, gn_g, ln_g, ln_b, w_s, sg_bias, n_batch, seq, ctx_len):
    cb = RET_BLOCK
    big, blk = 4 * cb, 2 * cb
    assert ctx_len == cb and seq % big == 0
    n_lat = n_batch * seq
    inc_steps, out_steps = seq // big, seq // blk
    dec, xi, zf, zb, gdec = tabs
    consts = (dec, xi, zf, zb, gdec, gn_g, ln_g, ln_b, w_s, sg_bias)
    full2 = lambda a: pl.BlockSpec(a.shape, lambda b, j, _nd=a.ndim: (0,) * _nd)
    inc_row = lambda b, j: b * inc_steps + jnp.minimum(j, inc_steps - 1)
    out_row = lambda b, j: b * out_steps + jnp.maximum(j - inc_steps, 0)
    ctx_row = lambda b, j: n_lat // cb + b
    lat = pl.pallas_call(
        functools.partial(_even_lat_kernel, inc_steps=inc_steps),
        grid=(n_batch, inc_steps + out_steps),
        in_specs=[pl.BlockSpec((big, 512), lambda b, j: (inc_row(b, j), 0)),
                  pl.BlockSpec((big, 512), lambda b, j: (inc_row(b, j), 1)),
                  pl.BlockSpec((cb, 512), lambda b, j: (ctx_row(b, j), 0)),
                  pl.BlockSpec((cb, 512), lambda b, j: (ctx_row(b, j), 1))]
                 + [pl.BlockSpec((blk, 512), lambda b, j, _c=c: (out_row(b, j), _c)) for c in range(5)]
                 + [full2(a) for a in consts],
        out_specs=pl.BlockSpec((blk, 1024), lambda b, j: (out_row(b, j), 0)),
        out_shape=jax.ShapeDtypeStruct((n_lat, 1024), BF16),
        scratch_shapes=[pltpu.VMEM((seq // cb + 1, 2, 128, 256), F32), pltpu.VMEM((seq // cb + 1, 2, 128, 256), F32)],
        compiler_params=_params(2, VMEM_LIMIT),
        name="even_mix",
    )(p, p, p, p, p, p, p, p, p, *consts)
    ctx_consts = (dec, xi, gn_g, ln_g, ln_b, w_s, sg_bias)
    full1 = lambda a: pl.BlockSpec(a.shape, lambda b, _nd=a.ndim: (0,) * _nd)
    ctx = pl.pallas_call(
        _even_ctx_kernel,
        grid=(n_batch,),
        in_specs=[pl.BlockSpec((cb, 512), lambda b, _c=c: (n_lat // cb + b, _c)) for c in range(5)]
                 + [full1(a) for a in ctx_consts],
        out_specs=pl.BlockSpec((cb, 1024), lambda b: (b, 0)),
        out_shape=jax.ShapeDtypeStruct((n_batch * ctx_len, 1024), BF16),
        compiler_params=_params(1, VMEM_LIMIT),
        name="even_mix_ctx",
    )(p, p, p, p, p, *ctx_consts)
    return lat, ctx


def _attn_kernel(q_ref, kvp_ref, kvm_ref, kvn_ref, kvx_ref, bias_f_ref, bias_m_ref, bias_l_ref, sink_ref, o_ref,
                 s_scr, p_scr, r_scr, *, n_sub):
    blk = ATT_BLOCK
    nloc = 3 * blk
    rows_all = ATT_GROUP * blk
    bias_refs = [bias_f_ref] + [bias_m_ref] * (n_sub - 2) + [bias_l_ref]
    chains = [(kvh, sub) for kvh in range(ATT_KV_HEADS) for sub in range(n_sub)]
    v_all, sinks = {}, {}
    for ci, (kvh, sub) in enumerate(chains):
        kcols = slice(HEAD_DIM * kvh, HEAD_DIM * (kvh + 1))
        vcols = slice(HEAD_DIM * (ATT_KV_HEADS + kvh), HEAD_DIM * (ATT_KV_HEADS + kvh + 1))
        k_parts = [kvp_ref[:, kcols]] + [kvm_ref[blk * j:blk * (j + 1), kcols] for j in range(n_sub)] + [kvn_ref[:, kcols]]
        v_parts = [kvp_ref[:, vcols]] + [kvm_ref[blk * j:blk * (j + 1), vcols] for j in range(n_sub)] + [kvn_ref[:, vcols]]
        heads = [ATT_GROUP * kvh + j for j in range(ATT_GROUP)]
        rows = slice(blk * sub, blk * (sub + 1))
        k_all = jnp.concatenate(k_parts[sub:sub + 3] + [kvx_ref[:, kcols]], axis=0)
        v_all[ci] = jnp.concatenate(v_parts[sub:sub + 3] + [kvx_ref[:, vcols]], axis=0)
        q4 = jnp.concatenate([q_ref[rows, HEAD_DIM * hd:HEAD_DIM * (hd + 1)] for hd in heads], axis=0)
        s_scr[ci] = lax.dot_general(q4, k_all, _NT, preferred_element_type=F32)
        sinks[ci] = LOG2E * jnp.concatenate(
            [jnp.broadcast_to(sink_ref[hd:hd + 1, 0:1], (blk, 1)) for hd in heads], axis=0)
    nchunk_loc = nloc // 128
    nchunk = s_scr.shape[2] // 128

    def shifted(ci, sub, rs, k):
        t = s_scr[ci, rs, 128 * k:128 * (k + 1)]
        return t + bias_refs[sub][0, rs, 128 * k:128 * (k + 1)] if k < nchunk_loc else t

    mrow = {}
    for ci, (kvh, sub) in enumerate(chains):
        for r0 in range(0, rows_all, ATT_SLAB):
            rs = slice(r0, r0 + ATT_SLAB)
            part = shifted(ci, sub, rs, 0)
            for k in range(1, nchunk):
                part = jnp.maximum(part, shifted(ci, sub, rs, k))
            r_scr[ci, rs, :] = part
    for ci in range(len(chains)):
        m = jnp.maximum(jnp.max(r_scr[ci], axis=-1, keepdims=True), sinks[ci])
        mrow[ci] = m
        r_scr[ci] = jnp.broadcast_to(m, r_scr.shape[1:])
    for ci, (kvh, sub) in enumerate(chains):
        for r0 in range(0, rows_all, ATT_SLAB):
            rs = slice(r0, r0 + ATT_SLAB)
            mb = r_scr[ci, rs, :]
            part = None
            for k in range(nchunk):
                p = jnp.exp2(shifted(ci, sub, rs, k) - mb)
                part = p if part is None else part + p
                p_scr[ci, rs, 128 * k:128 * (k + 1)] = p.astype(BF16)
            r_scr[ci, rs, :] = part
    inv = {}
    for ci in range(len(chains)):
        inv[ci] = 1.0 / (jnp.sum(r_scr[ci], axis=-1, keepdims=True) + jnp.exp2(sinks[ci] - mrow[ci]))
    for ci, (kvh, sub) in enumerate(chains):
        rows = slice(blk * sub, blk * (sub + 1))
        o = (jnp.dot(p_scr[ci, :, 0:nloc], v_all[ci][0:nloc], preferred_element_type=F32)
             + jnp.dot(p_scr[ci, :, nloc:], v_all[ci][nloc:], preferred_element_type=F32)) * inv[ci]
        for j in range(ATT_GROUP):
            hd = ATT_GROUP * kvh + j
            o_ref[rows, HEAD_DIM * hd:HEAD_DIM * (hd + 1)] = o[blk * j:blk * (j + 1)].astype(BF16)


def _attn_bias():
    i = np.arange(ATT_BLOCK)[:, None]
    j = np.arange(3 * ATT_BLOCK)[None, :] - ATT_BLOCK
    band = np.abs(i - j) <= WINDOW
    cases = [band & (j >= 0), band, band & (j < ATT_BLOCK), np.zeros_like(band)]
    bias = np.stack([np.where(c, 0.0, NEG) for c in cases]).astype(np.float32)
    return jnp.asarray(np.tile(bias, (1, ATT_GROUP, 1)))


def _attention(q, kv, bias, sink_t, n_batch, seq, ctx_len, need_ctx):
    blk = ATT_BLOCK
    n_lat = n_batch * seq
    nb = seq // blk
    width = 3 * blk + ctx_len

    def call(n_sub, steps, n_rows, qrow, small, mid, cases, name):
        big = n_sub * blk
        chains = ATT_KV_HEADS * n_sub
        return pl.pallas_call(
            functools.partial(_attn_kernel, n_sub=n_sub),
            grid=(n_batch, steps),
            in_specs=[
                pl.BlockSpec((big, q.shape[1]), lambda b, s: (qrow(b, s), 0)),
                pl.BlockSpec((blk, kv.shape[1]), small(-1)),
                pl.BlockSpec((big, kv.shape[1]), mid),
                pl.BlockSpec((blk, kv.shape[1]), small(n_sub)),
                pl.BlockSpec((ctx_len, kv.shape[1]), lambda b, s: (n_lat // ctx_len + b, 0)),
            ] + [pl.BlockSpec((1,) + bias.shape[1:], c) for c in cases] + [
                pl.BlockSpec(sink_t.shape, lambda b, s: (0, 0)),
            ],
            out_specs=pl.BlockSpec((big, q.shape[1]), lambda b, s: (qrow(b, s) - (n_lat // big if name else 0), 0)),
            out_shape=jax.ShapeDtypeStruct((n_rows, q.shape[1]), BF16),
            scratch_shapes=[pltpu.VMEM((chains, ATT_GROUP * blk, width), F32),
                            pltpu.VMEM((chains, ATT_GROUP * blk, width), BF16),
                            pltpu.VMEM((chains, ATT_GROUP * blk, 128), F32)],
            compiler_params=_params(2, VMEM_LIMIT),
            name="window_attention" + name,
        )(q, kv, kv, kv, kv, bias, bias, bias, sink_t)

    big = ATT_SUB * blk
    assert seq % big == 0
    ns = seq // big

    def small(off):
        return lambda b, s: (b * nb + jnp.clip(ATT_SUB * s + off, 0, nb - 1), 0)

    cases = (lambda b, s: (jnp.where(s == 0, 0, 1), 0, 0),
             lambda b, s: (1, 0, 0),
             lambda b, s: (jnp.where(s == ns - 1, 2, 1), 0, 0))
    lat = call(ATT_SUB, ns, n_lat, lambda b, s: b * ns + s, small, lambda b, s: (b * ns + s, 0), cases, "")
    if not need_ctx:
        return (lat,)
    n_cb = ctx_len // blk
    masked = (lambda b, s: (3, 0, 0),) * 3
    first = lambda off: (lambda b, s: (b * nb, 0))
    ctx = call(n_cb, 1, n_batch * ctx_len, lambda b, s: n_lat // ctx_len + b, first,
               lambda b, s: (b * (seq // ctx_len), 0), masked, "_ctx")
    return lat, ctx


def _route(logits):
    m = jnp.max(logits, axis=0, keepdims=True)
    e = jnp.exp(logits - m)
    p = e / jnp.sum(e, axis=0, keepdims=True)
    rows = [p[i:i + 1, :] for i in range(N_EXPERTS)]
    best = None
    gsel = None
    for g in range(N_GROUPS):
        a, b, c, d = rows[4 * g:4 * g + 4]
        m1, n1 = jnp.maximum(a, b), jnp.minimum(a, b)
        m2, n2 = jnp.maximum(c, d), jnp.minimum(c, d)
        score = jnp.maximum(m1, m2) + jnp.maximum(jnp.minimum(m1, m2), jnp.maximum(n1, n2))
        if g == 0:
            best, gsel = score, jnp.zeros(score.shape, I32)
        else:
            upd = score > best
            gsel = jnp.where(upd, g, gsel)
            best = jnp.where(upd, score, best)
    v = [jnp.where(gsel == 0, rows[j], jnp.where(gsel == 1, rows[4 + j], jnp.where(gsel == 2, rows[8 + j], rows[12 + j])))
         for j in range(EXPERTS_PER_GROUP)]
    b1, i1 = v[0], jnp.zeros(gsel.shape, I32)
    for j in range(1, 4):
        upd = v[j] > b1
        i1 = jnp.where(upd, j, i1)
        b1 = jnp.where(upd, v[j], b1)
    b2, i2 = jnp.full(b1.shape, -1.0, F32), jnp.zeros(gsel.shape, I32)
    for j in range(4):
        upd = jnp.logical_and(i1 != j, v[j] > b2)
        i2 = jnp.where(upd, j, i2)
        b2 = jnp.where(upd, v[j], b2)
    lo = jnp.minimum(i1, i2)
    hi = jnp.maximum(i1, i2)
    pair = jnp.where(lo == 0, hi - 1, jnp.where(lo == 1, 6 - hi, 5))
    return gsel * N_PAIRS + pair


def _post_kernel(*refs, lat_tiles, first_trash):
    if lat_tiles is None:
        mix_ref, refs = refs[0], refs[1:]
        mix = mix_ref[...]
    else:
        (ml_ref, mc_ref), refs = refs[:2], refs[2:]
        mix = jnp.where(pl.program_id(0) < lat_tiles, ml_ref[...], mc_ref[...])
    (x_ref, mod_ref, g2_ref, wo_ref, wr_ref, br_ref, tri_ref, low_ref, hs_in_ref,
     xo_ref, pos_ref, tab_ref, cnt_ref, hs_ref,
     carry_ref, cur_ref, nfree_ref, hbuf_ref, posv_ref, poss_ref, sem_s, sem_p) = refs
    del hs_in_ref
    i = pl.program_id(0)
    slot = i % 2
    tm, d = x_ref.shape

    @pl.when(i == 0)
    def _():
        carry_ref[...] = jnp.zeros(carry_ref.shape, F32)
        cur_ref[...] = jnp.zeros(cur_ref.shape, F32)
        nfree_ref[...] = jnp.zeros(nfree_ref.shape, F32)
        tab_ref[...] = jnp.zeros(tab_ref.shape, F32)
        hbuf_ref[1] = jnp.zeros(hbuf_ref.shape[1:], F32)
        posv_ref[1] = first_trash + lax.broadcasted_iota(I32, (1, tm), 1)
        pltpu.make_async_copy(posv_ref.at[1], poss_ref.at[1], sem_p.at[1]).start()

    pltpu.make_async_copy(posv_ref.at[1 - slot], poss_ref.at[1 - slot], sem_p.at[1 - slot]).wait()

    def scatter(s, lo, hi):
        for r in range(lo, hi):
            pltpu.make_async_copy(hbuf_ref.at[s, r], hs_ref.at[poss_ref[s, 0, r]], sem_s).start(priority=r % 2)

    def scatter_wait(s):
        pltpu.make_async_copy(hbuf_ref.at[s], hs_ref.at[pl.ds(0, tm)], sem_s).wait()

    nchunk = 4
    wc = d // nchunk
    xs = []
    for c in range(nchunk):
        scatter(1 - slot, tm * c // nchunk, tm * (c + 1) // nchunk)
        cols = slice(wc * c, wc * (c + 1))
        y = jnp.dot(mix, wo_ref[:, cols], preferred_element_type=F32)
        xc = x_ref[:, cols] + mod_ref[0, :, 2 * d + wc * c:2 * d + wc * (c + 1)] * y
        xo_ref[:, cols] = xc
        xs.append(xc)
    x = jnp.concatenate(xs, axis=1)
    h2 = _rms_mod(x, g2_ref[...], mod_ref[0, :, 4 * d:5 * d], mod_ref[0, :, 3 * d:4 * d])
    hbuf_ref[slot] = h2.reshape(hbuf_ref.shape[1:])
    hi = h2.astype(BF16)
    lo = (h2 - hi.astype(F32)).astype(BF16)
    logits = (lax.dot_general(wr_ref[...], hi, _NT, preferred_element_type=F32)
              + lax.dot_general(wr_ref[...], lo, _NT, preferred_element_type=F32)
              + br_ref[:, 0:1])
    bucket = _route(logits)
    ids = lax.broadcasted_iota(I32, (BUCKET_ROWS, tm), 0)
    onehot = (ids == bucket).astype(F32)
    before = jnp.dot(onehot.astype(BF16), tri_ref[...], preferred_element_type=F32)
    inv = 1.0 / MOE_TILE
    c0 = carry_ref[:, 0:1]
    c1 = c0 + jnp.sum(onehot, axis=1, keepdims=True)
    a0 = jnp.floor((c0 + (MOE_TILE - 1)) * inv)
    new = jnp.floor((c1 + (MOE_TILE - 1)) * inv) - a0
    nfree = nfree_ref[0:1, 0:1]
    new_b = jnp.broadcast_to(new, (BUCKET_ROWS, 128)).astype(BF16)
    id1 = nfree + jnp.dot(low_ref[...], new_b, preferred_element_type=F32)[:, 0:1]
    cur = cur_ref[:, 0:1]
    rank = before + c0
    k = jnp.floor(rank * inv)
    chunk = jnp.where(k < a0, cur, id1 + (k - a0))
    pos = jnp.sum(onehot * (chunk * MOE_TILE + (rank - k * MOE_TILE)), axis=0, keepdims=True).astype(I32)
    pos_ref[...] = pos
    posv_ref[slot] = pos
    pltpu.make_async_copy(posv_ref.at[slot], poss_ref.at[slot], sem_p.at[slot]).start()
    carry_ref[...] = jnp.broadcast_to(c1, carry_ref.shape)
    cnt_ref[...] = jnp.broadcast_to(c1, cnt_ref.shape).astype(I32)
    cur_ref[...] = jnp.broadcast_to(jnp.where(new == 0.0, cur, id1 + new - 1.0), cur_ref.shape)
    nfree_ref[...] = jnp.broadcast_to(nfree + jnp.sum(new, axis=0, keepdims=True), nfree_ref.shape)
    kcol = lax.broadcasted_iota(I32, tab_ref.shape, 1).astype(F32)
    tab = tab_ref[...]
    fresh = jnp.logical_and(kcol >= a0, kcol < a0 + new)
    tab_ref[...] = jnp.where(fresh, id1 + (kcol - a0), tab)
    scatter_wait(1 - slot)

    @pl.when(i == pl.num_programs(0) - 1)
    def _():
        pltpu.make_async_copy(posv_ref.at[slot], poss_ref.at[slot], sem_p.at[slot]).wait()
        scatter(slot, 0, tm)
        scatter_wait(slot)


def _post(mix, xa, mod_l, g2, wo, wr_bf, br_t, tri, low, hs, n_act, n_lat, seq, n_batch):
    d = xa.shape[1]
    tm = POST_TILE
    steps = n_act // tm
    mrow = _mod_row_map(tm, n_lat, seq, n_batch)
    full = lambda a: pl.BlockSpec(a.shape, lambda i, _nd=a.ndim: (0,) * _nd)
    if len(mix) == 1:
        lat_tiles = None
        mix_specs = [pl.BlockSpec((tm, d), lambda i: (i, 0))]
    else:
        lat_tiles = n_lat // tm
        mix_specs = [pl.BlockSpec((tm, d), lambda i: (jnp.minimum(i, lat_tiles - 1), 0)),
                     pl.BlockSpec((tm, d), lambda i: (jnp.maximum(i - lat_tiles, 0), 0))]
    n_in = len(mix_specs)
    return pl.pallas_call(
        functools.partial(_post_kernel, lat_tiles=lat_tiles, first_trash=hs.shape[0] - tm),
        grid=(steps,),
        in_specs=mix_specs + [
            pl.BlockSpec((tm, d), lambda i: (i, 0)),
            pl.BlockSpec((1, 1, 6 * d), lambda i: (mrow(i), 0, 0)),
            full(g2), full(wo), full(wr_bf), full(br_t), full(tri), full(low),
            pl.BlockSpec(memory_space=pl.ANY),
        ],
        out_specs=[
            pl.BlockSpec((tm, d), lambda i: (i, 0)),
            pl.BlockSpec((1, tm), lambda i: (0, i)),
            pl.BlockSpec((BUCKET_ROWS, 256), lambda i: (0, 0)),
            pl.BlockSpec((BUCKET_ROWS, 128), lambda i: (0, 0)),
            pl.BlockSpec(memory_space=pl.ANY),
        ],
        out_shape=[
            jax.ShapeDtypeStruct((n_act, d), F32),
            jax.ShapeDtypeStruct((1, n_act), I32),
            jax.ShapeDtypeStruct((BUCKET_ROWS, 256), F32),
            jax.ShapeDtypeStruct((BUCKET_ROWS, 128), I32),
            jax.ShapeDtypeStruct(hs.shape, F32),
        ],
        scratch_shapes=[pltpu.VMEM((BUCKET_ROWS, 128), F32), pltpu.VMEM((BUCKET_ROWS, 128), F32), pltpu.VMEM((8, 128), F32),
                        pltpu.VMEM((2, tm) + hs.shape[1:], F32), pltpu.VMEM((2, 1, tm), I32), pltpu.SMEM((2, 1, tm), I32),
                        pltpu.SemaphoreType.DMA(()), pltpu.SemaphoreType.DMA((2,))],
        input_output_aliases={n_in + 8: 4},
        compiler_params=_params(1, VMEM_LIMIT),
        name="post_mixer",
    )(*mix, xa, mod_l, g2, wo, wr_bf, br_t, tri, low, hs)


def _moe_kernel(blk_ref, two_ref, ea_ref, eb_ref, nu_ref, h_ref, wga_ref, wua_ref, wda_ref, wgb_ref, wub_ref, wdb_ref,
                wrt_ref, brt_ref, *refs):
    o_ref = refs[len(refs) // 2]
    n_cast = len(refs) // 2
    for k in range(n_cast):
        refs[n_cast + 1 + k][...] = refs[k][0].astype(BF16)
    t = pl.program_id(0)
    active = t < nu_ref[0]
    both = two_ref[t] == 1
    half = h_ref.shape[0] // 2

    def compute(n_halves):
        ea = ea_ref[t]
        eb = eb_ref[t]
        wdiff = wrt_ref[pl.ds(ea, 1), :] - wrt_ref[pl.ds(eb, 1), :]
        bdiff = brt_ref[pl.ds(ea, 1), 0:1] - brt_ref[pl.ds(eb, 1), 0:1]
        hbs, was = [], []
        for i in range(n_halves):
            rows = h_ref[half * i:half * (i + 1)]
            h = rows.reshape(half, h_ref.shape[1] * h_ref.shape[2])
            hbs.append(h.astype(BF16))
            was.append(jax.nn.sigmoid(jnp.sum(h * wdiff, axis=-1, keepdims=True) + bdiff))
        slots = ((wga_ref, wua_ref, wda_ref), (wgb_ref, wub_ref, wdb_ref))
        f = wga_ref.shape[2]
        chains = [(i, e, c) for i in range(n_halves) for e in range(2) for c in range(f // 512)]
        gu = {}
        for i, e, c in chains:
            cols = slice(512 * c, 512 * (c + 1))
            gu[i, e, c] = (jnp.dot(hbs[i], slots[e][0][0, :, cols], preferred_element_type=F32),
                           jnp.dot(hbs[i], slots[e][1][0, :, cols], preferred_element_type=F32))
        act = {k: (_silu(g) * u).astype(BF16) for k, (g, u) in gu.items()}
        outs = {}
        for i, e, c in chains:
            part = jnp.dot(act[i, e, c], slots[e][2][0, 512 * c:512 * (c + 1), :], preferred_element_type=F32)
            outs[i, e] = part if (i, e) not in outs else outs[i, e] + part
        for i in range(n_halves):
            y = was[i] * outs[i, 0] + (1.0 - was[i]) * outs[i, 1]
            o_ref[half * i:half * (i + 1)] = y.reshape((half,) + o_ref.shape[1:])
        if n_halves == 1:
            o_ref[half:2 * half] = jnp.zeros((half,) + o_ref.shape[1:], F32)

    @pl.when(jnp.logical_and(active, both))
    def _():
        compute(2)

    @pl.when(jnp.logical_and(active, jnp.logical_not(both)))
    def _():
        compute(1)

    @pl.when(jnp.logical_not(active))
    def _():
        o_ref[...] = jnp.zeros(o_ref.shape, F32)


def _moe(blk, two, ea, eb, nu, hs, n_tiles, wg, wu, wd, wr_t, br_t, w_next, next_layer):
    row = hs.shape[1:]
    tm = MOE_TILE
    d, f = wg.shape[1:]
    n_cast = 0 if w_next is None else len(w_next)
    if n_cast:
        w_spec, wo_spec, wo_shape = _cast_specs(w_next[0], next_layer, n_tiles)
    else:
        w_spec = wo_spec = wo_shape = None
    wa_spec = lambda shape: pl.BlockSpec(shape, lambda t, blk_, two_, ea_, eb_, nu_: (ea_[t], 0, 0))
    wb_spec = lambda shape: pl.BlockSpec(shape, lambda t, blk_, two_, ea_, eb_, nu_: (eb_[t], 0, 0))
    chunk = pl.BlockSpec((tm,) + row, lambda t, blk_, two_, ea_, eb_, nu_: (blk_[t], 0, 0))
    grid_spec = pltpu.PrefetchScalarGridSpec(
        num_scalar_prefetch=5,
        grid=(n_tiles,),
        in_specs=[
            chunk,
            wa_spec((1, d, f)), wa_spec((1, d, f)), wa_spec((1, f, d)),
            wb_spec((1, d, f)), wb_spec((1, d, f)), wb_spec((1, f, d)),
            pl.BlockSpec(wr_t.shape, lambda t, *_: (0, 0)),
            pl.BlockSpec(br_t.shape, lambda t, *_: (0, 0)),
        ] + [w_spec] * n_cast,
        out_specs=[chunk] + [wo_spec] * n_cast,
    )
    return pl.pallas_call(
        _moe_kernel,
        grid_spec=grid_spec,
        out_shape=[jax.ShapeDtypeStruct((n_tiles * tm,) + row, F32)] + [wo_shape] * n_cast,
        compiler_params=_params(1, VMEM_LIMIT),
        name="experts",
    )(blk, two, ea, eb, nu, hs, wg, wu, wd, wg, wu, wd, wr_t, br_t, *(w_next or ()))


def _combine_kernel(pos_ref, posn_ref, x_ref, mod_ref, fg_ref, ys_ref, o_ref, gbuf_ref, sem):
    d = x_ref.shape[1]
    rows, issue_next, finish = _gather_rows(pos_ref, posn_ref, ys_ref, gbuf_ref, sem)
    issue_next(0, 1)
    x = x_ref[...] + mod_ref[0, :, 5 * d:6 * d] * rows
    o_ref[...] = x * lax.rsqrt(jnp.mean(x * x, axis=-1, keepdims=True) + EPS) * fg_ref[...]
    finish()


def _combine(pos3, xa, mod_l, fg, ys, n_lat, seq, n_batch):
    n, d = xa.shape
    tm = ROW_TILE
    steps = n // tm
    mrow = _mod_row_map(tm, n_lat, seq, n_batch)
    return pl.pallas_call(
        _combine_kernel,
        grid=(steps,),
        in_specs=[
            pl.BlockSpec((1, 1, tm), lambda i: (i, 0, 0), memory_space=pltpu.SMEM),
            pl.BlockSpec((1, 1, tm), lambda i: (jnp.minimum(i + 1, steps - 1), 0, 0), memory_space=pltpu.SMEM),
            pl.BlockSpec((tm, d), lambda i: (i, 0)),
            pl.BlockSpec((1, 1, 6 * d), lambda i: (mrow(i), 0, 0)),
            pl.BlockSpec((1, d), lambda i: (0, 0)),
            pl.BlockSpec(memory_space=pl.ANY),
        ],
        out_specs=pl.BlockSpec((tm, d), lambda i: (i, 0)),
        out_shape=jax.ShapeDtypeStruct((n, d), F32),
        scratch_shapes=[pltpu.VMEM((2, tm) + ys.shape[1:], F32), pltpu.SemaphoreType.DMA((2,))],
        compiler_params=_params(1, VMEM_LIMIT),
        name="combine",
    )(pos3, pos3, xa, mod_l, fg, ys)


def _tile_plan(tab, counts, n_tiles, tile):
    cnt = counts[:N_BUCKETS, 0]
    n_chunks = (cnt + tile - 1) // tile
    ends = jnp.cumsum(n_chunks)
    starts = ends - n_chunks
    n_used = ends[-1]
    t = jnp.arange(n_tiles, dtype=I32)
    t_eff = jnp.minimum(t, jnp.maximum(n_used - 1, 0))
    tb = jnp.minimum(jnp.sum((ends[None, :] <= t_eff[:, None]).astype(I32), axis=1), N_BUCKETS - 1)
    k = t_eff - starts[tb]
    owned = tab.astype(I32)[tb, k]
    blk = jnp.where(t < n_used, owned, t)
    two = (cnt[tb] - k * tile > tile // 2).astype(I32)
    pairs = np.array(PAIR_SLOTS, dtype=np.int32)
    ea = (tb // N_PAIRS) * EXPERTS_PER_GROUP + jnp.asarray(pairs[:, 0])[tb % N_PAIRS]
    eb = (tb // N_PAIRS) * EXPERTS_PER_GROUP + jnp.asarray(pairs[:, 1])[tb % N_PAIRS]
    return blk.astype(I32), two, ea.astype(I32), eb.astype(I32), n_used.astype(I32).reshape(1)


def kernel(x, c, ctx, c_ctx, w_mod, b_mod, norm_g, w_in_even, w_out_even, ret_log_decay, ret_gn_g, sg_ln_g, sg_ln_b, sg_w, sg_b, w_qkv_odd, w_o_odd, attn_sink, w_router, b_router, w_e_gate, w_e_up, w_e_down, final_g):
    n_batch, seq, d = x.shape
    ctx_len = ctx.shape[1]
    depth = w_mod.shape[0]
    n_lat = n_batch * seq
    n_all = n_lat + n_batch * ctx_len
    tm = ROW_TILE
    assert seq % POST_TILE == 0 and (n_batch * ctx_len) % POST_TILE == 0

    mod_rows = -(-(n_batch + 1) // 16) * 16
    c_rows = jnp.zeros((mod_rows, d), F32).at[:n_batch].set(c).at[n_batch].set(c_ctx)
    mod = _modulation(c_rows, w_mod, b_mod).reshape(depth, mod_rows, 1, 6 * d)

    wr_t = w_router.T.astype(F32)
    wr_bf = wr_t.astype(BF16)
    br_t = jnp.broadcast_to(b_router.astype(F32)[:, None], (N_EXPERTS, 128))
    tri = jnp.asarray(np.triu(np.ones((POST_TILE, POST_TILE), np.float32), k=1), BF16)
    low = jnp.asarray(np.tril(np.ones((BUCKET_ROWS, BUCKET_ROWS), np.float32), k=-1), BF16)
    cos_t, sin_t = _rope_tables(seq, tm)
    bias = _attn_bias()

    n_exp, d_exp = w_e_gate.shape[1], w_e_gate.shape[3]
    assert d_exp == d
    w_views = [w.reshape(depth, n_exp * d, d) for w in (w_e_gate, w_e_up, w_e_down)]

    n_tiles = n_all // MOE_TILE + N_BUCKETS
    p_rows = n_tiles * MOE_TILE
    hs = jnp.zeros((p_rows + POST_TILE, ROW_SUB, d // ROW_SUB), F32)

    source = (x.reshape(n_lat, d), ctx.reshape(n_batch * ctx_len, d))
    for l in range(depth):
        i = l // 2
        last = l == depth - 1
        mod_l = mod[l]
        g1 = norm_g[l, 0].reshape(1, d)
        g2 = norm_g[l, 1].reshape(1, d)
        if l % 2 == 0:
            res = _in_stage(l, source, mod_l, g1, w_in_even[i].astype(BF16), None, w_views if l == 0 else None,
                            n_lat, seq, n_batch)
            xa, p = res[:2]
            tabs = _retention_tables(ret_log_decay[i], RET_BLOCK)
            sg_bias = jnp.repeat(sg_b[i].astype(F32).T, SG_CHUNK, axis=1)
            mix = _even_mix(p, tabs, ret_gn_g[i].reshape(1, -1), sg_ln_g[i].reshape(1, -1), sg_ln_b[i].reshape(1, -1),
                            sg_w[i].astype(BF16), sg_bias, n_batch, seq, ctx_len)
            wo = w_out_even[i].astype(BF16)
        else:
            res = _in_stage(l, source, mod_l, g1, w_qkv_odd[i].astype(BF16), (cos_t, sin_t), w_views if l == 0 else None,
                            n_lat, seq, n_batch)
            xa, q, kv = res[:3]
            sink_t = jnp.broadcast_to(attn_sink[i].astype(F32)[:, None], (ATT_HEADS, 128))
            mix = _attention(q, kv, bias, sink_t, n_batch, seq, ctx_len, need_ctx=not last)
            wo = w_o_odd[i].astype(BF16)
        n_act = n_lat if last else n_all
        xm, pos, tab, counts, hs = _post(mix, xa, mod_l, g2, wo, wr_bf, br_t, tri, low, hs, n_act, n_lat, seq, n_batch)
        plan = _tile_plan(tab, counts, n_tiles, MOE_TILE)
        pos3 = pos.reshape(n_act // tm, 1, tm)
        if l == 0:
            w_bf = res[-3:]
        ys, *w_cast = _moe(*plan, hs, n_tiles, *(w.reshape(n_exp, d, d) for w in w_bf), wr_t, br_t,
                           None if last else w_views, l + 1)
        w_bf = w_cast
        source = (pos3, xm, mod_l, ys)
    out = _combine(pos3, xm, mod_l, final_g.reshape(1, d), ys, n_lat, seq, n_batch)
    return out.reshape(n_batch, seq, d)
```

```python
import functools

import jax
import jax.numpy as jnp
import numpy as np
from jax import lax
from jax.experimental import pallas as pl
from jax.experimental.pallas import tpu as pltpu

F32 = jnp.float32
BF16 = jnp.bfloat16
I32 = jnp.int32

EPS = 1e-6
NEG = -1e30
LOG2E = 1.4426950408889634
GRID_W = 64
ROPE_BASE = 10000.0
RET_HEADS = 4
RET_QK_DIM = 64
SG_GROUPS = 4
SG_CHUNK = 128
RET_BLOCK = 256
ATT_HEADS = 8
ATT_KV_HEADS = 2
ATT_GROUP = ATT_HEADS // ATT_KV_HEADS
HEAD_DIM = 128
ATT_BLOCK = 128
WINDOW = 128
ATT_SUB = 4
ATT_SLAB = 32
N_EXPERTS = 16
N_GROUPS = 4
EXPERTS_PER_GROUP = 4
N_PAIRS = 6
N_BUCKETS = N_GROUPS * N_PAIRS
PAIR_SLOTS = ((0, 1), (0, 2), (0, 3), (1, 3), (1, 2), (3, 2))
BUCKET_ROWS = 32

ROW_TILE = 512
POST_TILE = 1024
MOE_TILE = 512
ROW_SUB = 8
VMEM_LIMIT = 56 * 1024 * 1024

_NT = (((1,), (1,)), ((), ()))
_TN = (((0,), (0,)), ((), ()))


def _params(n_axes, vmem=None):
    return pltpu.CompilerParams(dimension_semantics=("arbitrary",) * n_axes, vmem_limit_bytes=vmem)


def _silu(v):
    return v * jax.nn.sigmoid(v)


def _rms_mod(x, g, scale, shift):
    y = x * lax.rsqrt(jnp.mean(x * x, axis=-1, keepdims=True) + EPS) * g
    return y * (1.0 + scale) + shift


def _mod_kernel(c_ref, w_ref, b_ref, o_ref):
    a = _silu(c_ref[...]).astype(BF16)
    o_ref[0] = jnp.dot(a, w_ref[0].astype(BF16), preferred_element_type=F32) + b_ref[0]


def _modulation(c_rows, w_mod, b_mod):
    depth, d, six_d = w_mod.shape
    mr = c_rows.shape[0]
    tn = 1536
    return pl.pallas_call(
        _mod_kernel,
        grid=(depth, six_d // tn),
        in_specs=[
            pl.BlockSpec((mr, d), lambda l, j: (0, 0)),
            pl.BlockSpec((1, d, tn), lambda l, j: (l, 0, j)),
            pl.BlockSpec((1, 1, tn), lambda l, j: (l, 0, j)),
        ],
        out_specs=pl.BlockSpec((1, mr, tn), lambda l, j: (l, 0, j)),
        out_shape=jax.ShapeDtypeStruct((depth, mr, six_d), F32),
        compiler_params=_params(2, VMEM_LIMIT),
        name="modulation",
    )(c_rows, w_mod, b_mod.reshape(depth, 1, six_d))


def _cast_specs(w, layer, steps):
    w_rows, w_cols = w.shape[1:]
    cast_steps = 1
    while 2 * cast_steps <= steps and w_rows % (2 * cast_steps) == 0:
        cast_steps *= 2
    cast_rows = w_rows // cast_steps
    row = lambda i, *_: jnp.minimum(i, cast_steps - 1)
    return (pl.BlockSpec((1, cast_rows, w_cols), lambda i, *_: (layer, row(i), 0)),
            pl.BlockSpec((cast_rows, w_cols), lambda i, *_: (row(i), 0)),
            jax.ShapeDtypeStruct((w_rows, w_cols), BF16))


def _mod_row_map(tm, n_lat, seq, n_batch):
    def index(i):
        row0 = i * tm
        return jnp.where(row0 < n_lat, row0 // seq, n_batch)
    return index


def _gather_rows(pos_ref, posn_ref, ys_ref, gbuf_ref, sem):
    i = pl.program_id(0)
    slot = i % 2
    tm = gbuf_ref.shape[1]

    def issue(p_ref, s, lo, hi):
        for r in range(lo, hi):
            pltpu.make_async_copy(ys_ref.at[p_ref[0, 0, r]], gbuf_ref.at[s, r], sem.at[s]).start(priority=r % 2)

    def wait(s):
        pltpu.make_async_copy(ys_ref.at[pl.ds(0, tm)], gbuf_ref.at[s], sem.at[s]).wait()

    @pl.when(i == 0)
    def _():
        issue(pos_ref, 0, 0, tm)

    wait(slot)
    rows = gbuf_ref[slot].reshape(tm, gbuf_ref.shape[2] * gbuf_ref.shape[3])

    def issue_next(k, parts):
        issue(posn_ref, 1 - slot, tm * k // parts, tm * (k + 1) // parts)

    def finish():
        @pl.when(i == pl.num_programs(0) - 1)
        def _():
            wait(1 - slot)

    return rows, issue_next, finish


def _project_even(h, w_ref, o_ref, between, cast):
    for c in range(5):
        between(c, 5)
        cast(c)
        p = jnp.dot(h, w_ref[:, 512 * c:512 * (c + 1)], preferred_element_type=F32)
        if c == 0:
            col = lax.broadcasted_iota(I32, p.shape, 1)
            p = jnp.where(col >= 256, p * (RET_QK_DIM ** -0.5), p)
        elif c >= 3:
            p = jax.nn.gelu(p)
        o_ref[:, 512 * c:512 * (c + 1)] = p.astype(BF16)


def _project_odd(h, w_ref, cos_ref, sin_ref, q_ref, kv_ref, between, cast):
    tm = h.shape[0]
    cos = cos_ref[...]
    sin = sin_ref[...]
    lane = lax.broadcasted_iota(I32, (tm, HEAD_DIM), 1)
    first = (lane % 64) < 32

    def rope(p):
        partner = jnp.where(first, pltpu.roll(p, 96, 1), pltpu.roll(p, 32, 1))
        return p * cos + partner * sin

    scale = HEAD_DIM ** -0.5 * LOG2E
    for c in range(3):
        if c < 2:
            between(c, 2)
        cast(c)
        p = jnp.dot(h, w_ref[:, 512 * c:512 * (c + 1)], preferred_element_type=F32)
        for j in range(4):
            pj = p[:, 128 * j:128 * (j + 1)]
            if c < 2:
                q_ref[:, 512 * c + 128 * j:512 * c + 128 * (j + 1)] = (rope(pj) * scale).astype(BF16)
            elif j < 2:
                kv_ref[:, 128 * j:128 * (j + 1)] = rope(pj).astype(BF16)
            else:
                kv_ref[:, 128 * j:128 * (j + 1)] = pj.astype(BF16)


def _in_kernel(*refs, first, odd, lat_tiles, casts):
    if first:
        (xl_ref, xc_ref), refs = refs[:2], refs[2:]
        x = jnp.where(pl.program_id(0) < lat_tiles, xl_ref[...], xc_ref[...])
        between, finish = (lambda k, parts: None), (lambda: None)
    else:
        (pos_ref, posn_ref, x_ref, modp_ref, ys_ref), refs = refs[:5], refs[5:]
        gbuf_ref, sem = refs[-2:]
        refs = refs[:-2]
        d = x_ref.shape[1]
        rows, between, finish = _gather_rows(pos_ref, posn_ref, ys_ref, gbuf_ref, sem)
        x = x_ref[...] + modp_ref[0, :, 5 * d:6 * d] * rows
    mod_ref, g_ref, w_ref = refs[:3]
    d = x.shape[1]
    h = _rms_mod(x, g_ref[...], mod_ref[0, :, d:2 * d], mod_ref[0, :, 0:d]).astype(BF16)
    n_out = 3 if odd else 2
    if casts:
        cast_in, tail, cast_out = refs[-n_out - 6:-n_out - 3], refs[-n_out - 3:-3], refs[-3:]
    else:
        cast_in, tail, cast_out = (), refs[-n_out:], ()

    def cast(k):
        if k < len(cast_in):
            cast_out[k][...] = cast_in[k][0].astype(BF16)

    if odd:
        xo_ref, q_ref, kv_ref = tail
        cos_ref, sin_ref = refs[3:5]
        xo_ref[...] = x
        _project_odd(h, w_ref, cos_ref, sin_ref, q_ref, kv_ref, between, cast)
    else:
        xo_ref, o_ref = tail
        xo_ref[...] = x
        _project_even(h, w_ref, o_ref, between, cast)
    finish()


def _in_stage(layer, source, mod_l, g, w, rope, w_exp, n_lat, seq, n_batch):
    first = len(source) == 2
    odd = rope is not None
    tm = ROW_TILE
    if first:
        xl, xc = source
        d = xl.shape[1]
        n = xl.shape[0] + xc.shape[0]
        lat_tiles = xl.shape[0] // tm
        src_specs = [pl.BlockSpec((tm, d), lambda i: (jnp.minimum(i, lat_tiles - 1), 0)),
                     pl.BlockSpec((tm, d), lambda i: (jnp.maximum(i - lat_tiles, 0), 0))]
        scratch = []
    else:
        pos3, xm, mod_prev, ys = source
        n, d = xm.shape
        lat_tiles = n_lat // tm
        steps = n // tm
        mrow_p = _mod_row_map(tm, n_lat, seq, n_batch)
        src_specs = [pl.BlockSpec((1, 1, tm), lambda i: (i, 0, 0), memory_space=pltpu.SMEM),
                     pl.BlockSpec((1, 1, tm), lambda i: (jnp.minimum(i + 1, steps - 1), 0, 0), memory_space=pltpu.SMEM),
                     pl.BlockSpec((tm, d), lambda i: (i, 0)),
                     pl.BlockSpec((1, 1, 6 * d), lambda i: (mrow_p(i), 0, 0)),
                     pl.BlockSpec(memory_space=pl.ANY)]
        source = (pos3, pos3, xm, mod_prev, ys)
        scratch = [pltpu.VMEM((2, tm) + ys.shape[1:], F32), pltpu.SemaphoreType.DMA((2,))]
    mrow = _mod_row_map(tm, n_lat, seq, n_batch)
    specs = src_specs + [pl.BlockSpec((1, 1, 6 * d), lambda i: (mrow(i), 0, 0)),
                         pl.BlockSpec((1, d), lambda i: (0, 0)),
                         pl.BlockSpec(w.shape, lambda i: (0, 0))]
    args = tuple(source) + (mod_l, g, w)
    out_specs = [pl.BlockSpec((tm, d), lambda i: (i, 0))]
    out_shape = [jax.ShapeDtypeStruct((n, d), F32)]
    if odd:
        tiles_per_seq = seq // tm

        def rope_row(i):
            return jnp.where(i < lat_tiles, i % tiles_per_seq, tiles_per_seq)

        specs += [pl.BlockSpec((tm, HEAD_DIM), lambda i: (rope_row(i), 0))] * 2
        args += tuple(rope)
        widths = (ATT_HEADS * HEAD_DIM, 2 * ATT_KV_HEADS * HEAD_DIM)
    else:
        widths = (w.shape[1],)
    out_specs += [pl.BlockSpec((tm, wd), lambda i: (i, 0)) for wd in widths]
    out_shape += [jax.ShapeDtypeStruct((n, wd), BF16) for wd in widths]
    if w_exp is not None:
        w_spec, wo_spec, wo_shape = _cast_specs(w_exp[0], layer, n // tm)
        specs += [w_spec] * 3
        args += tuple(w_exp)
        out_specs += [wo_spec] * 3
        out_shape += [wo_shape] * 3
    return pl.pallas_call(
        functools.partial(_in_kernel, first=first, odd=odd, lat_tiles=lat_tiles, casts=w_exp is not None),
        grid=(n // tm,),
        in_specs=specs,
        out_specs=out_specs,
        out_shape=out_shape,
        scratch_shapes=scratch,
        compiler_params=_params(1, VMEM_LIMIT),
        name="in_odd" if odd else "in_even",
    )(*args)


def _rope_tables(seq, tm):
    t = np.arange(seq)
    quarter = HEAD_DIM // 4
    inv_freq = ROPE_BASE ** (-np.arange(quarter, dtype=np.float64) / quarter)
    ar = (t // GRID_W)[:, None] * inv_freq[None, :]
    ac = (t % GRID_W)[:, None] * inv_freq[None, :]
    cos = np.concatenate([np.cos(ar), np.cos(ar), np.cos(ac), np.cos(ac)], axis=1)
    sin = np.concatenate([-np.sin(ar), np.sin(ar), -np.sin(ac), np.sin(ac)], axis=1)
    cos = np.concatenate([cos, np.ones((tm, HEAD_DIM))], axis=0)
    sin = np.concatenate([sin, np.zeros((tm, HEAD_DIM))], axis=0)
    return jnp.asarray(cos, F32), jnp.asarray(sin, F32)


def _ret_increments(qk_ref, v_ref, rows, zf_ref, zb_ref):
    upper_rows = lax.broadcasted_iota(I32, (128, 128), 0) < RET_QK_DIM
    out = []
    for grp in range(2):
        k2 = qk_ref[rows, 256 + 128 * grp:256 + 128 * (grp + 1)].astype(F32)
        v2 = v_ref[rows, 256 * grp:256 * (grp + 1)]
        halves = []
        for z_ref in (zf_ref, zb_ref):
            kz = (k2 * z_ref[grp]).astype(BF16)
            full = lax.dot_general(kz, v2, _TN, preferred_element_type=F32)
            halves.append(jnp.where(upper_rows, full[:, 0:128], full[:, 128:256]))
        out.append(jnp.concatenate(halves, axis=1))
    return out


def _mix_outputs(qk_ref, v_ref, gate_ref, u_ref, s_ref, rows, states, dec_ref, xi_ref, gn_ref, lng_ref, lnb_ref,
                 ws_ref, sgb_ref, o_ref):
    lane = lax.broadcasted_iota(I32, (1, 128), 1)
    head_mask = [(lane // RET_QK_DIM == hh).astype(BF16) for hh in range(2)]
    for grp in range(2):
        q2 = qk_ref[rows, 128 * grp:128 * (grp + 1)]
        k2 = qk_ref[rows, 256 + 128 * grp:256 + 128 * (grp + 1)]
        for hh in range(2):
            hd = 2 * grp + hh
            cols = slice(128 * hd, 128 * (hd + 1))
            qm = q2 * head_mask[hh]
            sc = lax.dot_general(qm, k2, _NT, preferred_element_type=F32) * dec_ref[hd]
            o = jnp.dot(sc.astype(BF16), v_ref[rows, cols], preferred_element_type=F32)
            if states is not None:
                cross = jnp.dot(qm, states[grp], preferred_element_type=F32) * xi_ref[hd]
                o = o + cross[:, 0:128] + cross[:, 128:256]
            mu = jnp.mean(o, axis=-1, keepdims=True)
            oc = o - mu
            var = jnp.mean(oc * oc, axis=-1, keepdims=True)
            y = oc * lax.rsqrt(var + EPS) * gn_ref[:, cols]
            y = y * _silu(gate_ref[rows, cols].astype(F32))
            o_ref[rows, cols] = y.astype(BF16)
    for sub in range((rows.stop - rows.start) // SG_CHUNK):
        srows = slice(rows.start + SG_CHUNK * sub, rows.start + SG_CHUNK * (sub + 1))
        s = s_ref[srows, :].astype(F32)
        mu = jnp.mean(s, axis=-1, keepdims=True)
        sc_ = s - mu
        var = jnp.mean(sc_ * sc_, axis=-1, keepdims=True)
        sn = (sc_ * lax.rsqrt(var + EPS) * lng_ref[...] + lnb_ref[...]).astype(BF16)
        for g in range(SG_GROUPS):
            cols = slice(128 * g, 128 * (g + 1))
            mixed = jnp.dot(ws_ref[g], sn[:, cols], preferred_element_type=F32) + sgb_ref[:, cols]
            o_ref[srows, 512 + 128 * g:512 + 128 * (g + 1)] = (u_ref[srows, cols].astype(F32) * mixed).astype(BF16)


def _even_lat_kernel(bqk_ref, bv_ref, cqk_ref, cv_ref, qk_ref, v_ref, gate_ref, u_ref, s_ref, dec_ref, xi_ref,
                     zf_ref, zb_ref, gdec_ref, gn_ref, lng_ref, lnb_ref, ws_ref, sgb_ref, o_ref, ds_ref, sp_ref,
                     *, inc_steps):
    j = pl.program_id(1)
    cb = RET_BLOCK
    n_slots = ds_ref.shape[0]
    inc_chunks = bqk_ref.shape[0] // cb
    out_chunks = qk_ref.shape[0] // cb

    @pl.when(j == 0)
    def _():
        inc = _ret_increments(cqk_ref, cv_ref, slice(0, cb), zf_ref, zb_ref)
        for grp in range(2):
            ds_ref[0, grp] = inc[grp]

    @pl.when(j < inc_steps)
    def _():
        for c in range(inc_chunks):
            inc = _ret_increments(bqk_ref, bv_ref, slice(cb * c, cb * (c + 1)), zf_ref, zb_ref)
            for grp in range(2):
                ds_ref[1 + inc_chunks * j + c, grp] = inc[grp]

    @pl.when(j == inc_steps)
    def _scan():
        fwd = ds_ref[0, :, :, 0:128]
        for n in range(1, n_slots):
            sp_ref[n, :, :, 0:128] = fwd
            fwd = fwd * gdec_ref[:, :, 0:128] + ds_ref[n, :, :, 0:128]
        bwd = ds_ref[0, :, :, 128:256]
        for n in range(n_slots - 1, 0, -1):
            sp_ref[n, :, :, 128:256] = bwd
            bwd = bwd * gdec_ref[:, :, 128:256] + ds_ref[n, :, :, 128:256]

    @pl.when(j >= inc_steps)
    def _():
        for c in range(out_chunks):
            n = 1 + out_chunks * (j - inc_steps) + c
            states = [sp_ref[n, grp].astype(BF16) for grp in range(2)]
            _mix_outputs(qk_ref, v_ref, gate_ref, u_ref, s_ref, slice(cb * c, cb * (c + 1)), states, dec_ref, xi_ref,
                         gn_ref, lng_ref, lnb_ref, ws_ref, sgb_ref, o_ref)


def _even_ctx_kernel(qk_ref, v_ref, gate_ref, u_ref, s_ref, dec_ref, xi_ref, gn_ref, lng_ref, lnb_ref, ws_ref, sgb_ref,
                     o_ref):
    _mix_outputs(qk_ref, v_ref, gate_ref, u_ref, s_ref, slice(0, RET_BLOCK), None, dec_ref, xi_ref,
                 gn_ref, lng_ref, lnb_ref, ws_ref, sgb_ref, o_ref)


def _retention_tables(log_decay, cb):
    lg = -jnp.exp(log_decay.astype(F32))
    pos = jnp.arange(cb, dtype=F32)
    diff = pos[:, None] - pos[None, :]
    lower = jnp.where(diff >= 0, jnp.exp(lg[0][:, None, None] * jnp.maximum(diff, 0.0)[None]), 0.0)
    upper = jnp.where(diff <= 0, jnp.exp(lg[1][:, None, None] * jnp.maximum(-diff, 0.0)[None]), 0.0)
    dec = lower + upper
    xi_f = jnp.exp(lg[0][:, None] * (pos[None, :] + 1.0))
    xi_b = jnp.exp(lg[1][:, None] * (cb - pos[None, :]))
    xi = jnp.concatenate([jnp.broadcast_to(xi_f[:, :, None], (RET_HEADS, cb, 128)),
                          jnp.broadcast_to(xi_b[:, :, None], (RET_HEADS, cb, 128))], axis=2)
    zeta_f = jnp.exp(lg[0][:, None] * (cb - 1.0 - pos[None, :]))
    zeta_b = jnp.exp(lg[1][:, None] * pos[None, :])

    def lanes(z):
        return jnp.repeat(z.reshape(2, 2, cb), RET_QK_DIM, axis=1).transpose(0, 2, 1)

    gstep = jnp.exp(lg * cb)

    def rows(gv):
        return jnp.broadcast_to(jnp.repeat(gv.reshape(2, 2), RET_QK_DIM, axis=1)[:, :, None], (2, 128, 128))

    gdec = jnp.concatenate([rows(gstep[0]), rows(gstep[1])], axis=2)
    return dec, xi, lanes(zeta_f), lanes(zeta_b), gdec


def _even_mix(p, tabs, gn_g, ln_g, ln_b, w_s, sg_bias, n_batch, seq, ctx_len):
    cb = RET_BLOCK
    big, blk = 4 * cb, 2 * cb
    assert ctx_len == cb and seq % big == 0
    n_lat = n_batch * seq
    inc_steps, out_steps = seq // big, seq // blk
    dec, xi, zf, zb, gdec = tabs
    consts = (dec, xi, zf, zb, gdec, gn_g, ln_g, ln_b, w_s, sg_bias)
    full2 = lambda a: pl.BlockSpec(a.shape, lambda b, j, _nd=a.ndim: (0,) * _nd)
    inc_row = lambda b, j: b * inc_steps + jnp.minimum(j, inc_steps - 1)
    out_row = lambda b, j: b * out_steps + jnp.maximum(j - inc_steps, 0)
    ctx_row = lambda b, j: n_lat // cb + b
    lat = pl.pallas_call(
        functools.partial(_even_lat_kernel, inc_steps=inc_steps),
        grid=(n_batch, inc_steps + out_steps),
        in_specs=[pl.BlockSpec((big, 512), lambda b, j: (inc_row(b, j), 0)),
                  pl.BlockSpec((big, 512), lambda b, j: (inc_row(b, j), 1)),
                  pl.BlockSpec((cb, 512), lambda b, j: (ctx_row(b, j), 0)),
                  pl.BlockSpec((cb, 512), lambda b, j: (ctx_row(b, j), 1))]
                 + [pl.BlockSpec((blk, 512), lambda b, j, _c=c: (out_row(b, j), _c)) for c in range(5)]
                 + [full2(a) for a in consts],
        out_specs=pl.BlockSpec((blk, 1024), lambda b, j: (out_row(b, j), 0)),
        out_shape=jax.ShapeDtypeStruct((n_lat, 1024), BF16),
        scratch_shapes=[pltpu.VMEM((seq // cb + 1, 2, 128, 256), F32), pltpu.VMEM((seq // cb + 1, 2, 128, 256), F32)],
        compiler_params=_params(2, VMEM_LIMIT),
        name="even_mix",
    )(p, p, p, p, p, p, p, p, p, *consts)
    ctx_consts = (dec, xi, gn_g, ln_g, ln_b, w_s, sg_bias)
    full1 = lambda a: pl.BlockSpec(a.shape, lambda b, _nd=a.ndim: (0,) * _nd)
    ctx = pl.pallas_call(
        _even_ctx_kernel,
        grid=(n_batch,),
        in_specs=[pl.BlockSpec((cb, 512), lambda b, _c=c: (n_lat // cb + b, _c)) for c in range(5)]
                 + [full1(a) for a in ctx_consts],
        out_specs=pl.BlockSpec((cb, 1024), lambda b: (b, 0)),
        out_shape=jax.ShapeDtypeStruct((n_batch * ctx_len, 1024), BF16),
        compiler_params=_params(1, VMEM_LIMIT),
        name="even_mix_ctx",
    )(p, p, p, p, p, *ctx_consts)
    return lat, ctx


def _attn_kernel(q_ref, kvp_ref, kvm_ref, kvn_ref, kvx_ref, bias_f_ref, bias_m_ref, bias_l_ref, sink_ref, o_ref,
                 s_scr, p_scr, r_scr, *, n_sub):
    blk = ATT_BLOCK
    nloc = 3 * blk
    rows_all = ATT_GROUP * blk
    bias_refs = [bias_f_ref] + [bias_m_ref] * (n_sub - 2) + [bias_l_ref]
    chains = [(kvh, sub) for kvh in range(ATT_KV_HEADS) for sub in range(n_sub)]
    v_all, sinks = {}, {}
    for ci, (kvh, sub) in enumerate(chains):
        kcols = slice(HEAD_DIM * kvh, HEAD_DIM * (kvh + 1))
        vcols = slice(HEAD_DIM * (ATT_KV_HEADS + kvh), HEAD_DIM * (ATT_KV_HEADS + kvh + 1))
        k_parts = [kvp_ref[:, kcols]] + [kvm_ref[blk * j:blk * (j + 1), kcols] for j in range(n_sub)] + [kvn_ref[:, kcols]]
        v_parts = [kvp_ref[:, vcols]] + [kvm_ref[blk * j:blk * (j + 1), vcols] for j in range(n_sub)] + [kvn_ref[:, vcols]]
        heads = [ATT_GROUP * kvh + j for j in range(ATT_GROUP)]
        rows = slice(blk * sub, blk * (sub + 1))
        k_all = jnp.concatenate(k_parts[sub:sub + 3] + [kvx_ref[:, kcols]], axis=0)
        v_all[ci] = jnp.concatenate(v_parts[sub:sub + 3] + [kvx_ref[:, vcols]], axis=0)
        q4 = jnp.concatenate([q_ref[rows, HEAD_DIM * hd:HEAD_DIM * (hd + 1)] for hd in heads], axis=0)
        s_scr[ci] = lax.dot_general(q4, k_all, _NT, preferred_element_type=F32)
        sinks[ci] = LOG2E * jnp.concatenate(
            [jnp.broadcast_to(sink_ref[hd:hd + 1, 0:1], (blk, 1)) for hd in heads], axis=0)
    nchunk_loc = nloc // 128
    nchunk = s_scr.shape[2] // 128

    def shifted(ci, sub, rs, k):
        t = s_scr[ci, rs, 128 * k:128 * (k + 1)]
        return t + bias_refs[sub][0, rs, 128 * k:128 * (k + 1)] if k < nchunk_loc else t

    mrow = {}
    for ci, (kvh, sub) in enumerate(chains):
        for r0 in range(0, rows_all, ATT_SLAB):
            rs = slice(r0, r0 + ATT_SLAB)
            part = shifted(ci, sub, rs, 0)
            for k in range(1, nchunk):
                part = jnp.maximum(part, shifted(ci, sub, rs, k))
            r_scr[ci, rs, :] = part
    for ci in range(len(chains)):
        m = jnp.maximum(jnp.max(r_scr[ci], axis=-1, keepdims=True), sinks[ci])
        mrow[ci] = m
        r_scr[ci] = jnp.broadcast_to(m, r_scr.shape[1:])
    for ci, (kvh, sub) in enumerate(chains):
        for r0 in range(0, rows_all, ATT_SLAB):
            rs = slice(r0, r0 + ATT_SLAB)
            mb = r_scr[ci, rs, :]
            part = None
            for k in range(nchunk):
                p = jnp.exp2(shifted(ci, sub, rs, k) - mb)
                part = p if part is None else part + p
                p_scr[ci, rs, 128 * k:128 * (k + 1)] = p.astype(BF16)
            r_scr[ci, rs, :] = part
    inv = {}
    for ci in range(len(chains)):
        inv[ci] = 1.0 / (jnp.sum(r_scr[ci], axis=-1, keepdims=True) + jnp.exp2(sinks[ci] - mrow[ci]))
    for ci, (kvh, sub) in enumerate(chains):
        rows = slice(blk * sub, blk * (sub + 1))
        o = (jnp.dot(p_scr[ci, :, 0:nloc], v_all[ci][0:nloc], preferred_element_type=F32)
             + jnp.dot(p_scr[ci, :, nloc:], v_all[ci][nloc:], preferred_element_type=F32)) * inv[ci]
        for j in range(ATT_GROUP):
            hd = ATT_GROUP * kvh + j
            o_ref[rows, HEAD_DIM * hd:HEAD_DIM * (hd + 1)] = o[blk * j:blk * (j + 1)].astype(BF16)


def _attn_bias():
    i = np.arange(ATT_BLOCK)[:, None]
    j = np.arange(3 * ATT_BLOCK)[None, :] - ATT_BLOCK
    band = np.abs(i - j) <= WINDOW
    cases = [band & (j >= 0), band, band & (j < ATT_BLOCK), np.zeros_like(band)]
    bias = np.stack([np.where(c, 0.0, NEG) for c in cases]).astype(np.float32)
    return jnp.asarray(np.tile(bias, (1, ATT_GROUP, 1)))


def _attention(q, kv, bias, sink_t, n_batch, seq, ctx_len, need_ctx):
    blk = ATT_BLOCK
    n_lat = n_batch * seq
    nb = seq // blk
    width = 3 * blk + ctx_len

    def call(n_sub, steps, n_rows, qrow, small, mid, cases, name):
        big = n_sub * blk
        chains = ATT_KV_HEADS * n_sub
        return pl.pallas_call(
            functools.partial(_attn_kernel, n_sub=n_sub),
            grid=(n_batch, steps),
            in_specs=[
                pl.BlockSpec((big, q.shape[1]), lambda b, s: (qrow(b, s), 0)),
                pl.BlockSpec((blk, kv.shape[1]), small(-1)),
                pl.BlockSpec((big, kv.shape[1]), mid),
                pl.BlockSpec((blk, kv.shape[1]), small(n_sub)),
                pl.BlockSpec((ctx_len, kv.shape[1]), lambda b, s: (n_lat // ctx_len + b, 0)),
            ] + [pl.BlockSpec((1,) + bias.shape[1:], c) for c in cases] + [
                pl.BlockSpec(sink_t.shape, lambda b, s: (0, 0)),
            ],
            out_specs=pl.BlockSpec((big, q.shape[1]), lambda b, s: (qrow(b, s) - (n_lat // big if name else 0), 0)),
            out_shape=jax.ShapeDtypeStruct((n_rows, q.shape[1]), BF16),
            scratch_shapes=[pltpu.VMEM((chains, ATT_GROUP * blk, width), F32),
                            pltpu.VMEM((chains, ATT_GROUP * blk, width), BF16),
                            pltpu.VMEM((chains, ATT_GROUP * blk, 128), F32)],
            compiler_params=_params(2, VMEM_LIMIT),
            name="window_attention" + name,
        )(q, kv, kv, kv, kv, bias, bias, bias, sink_t)

    big = ATT_SUB * blk
    assert seq % big == 0
    ns = seq // big

    def small(off):
        return lambda b, s: (b * nb + jnp.clip(ATT_SUB * s + off, 0, nb - 1), 0)

    cases = (lambda b, s: (jnp.where(s == 0, 0, 1), 0, 0),
             lambda b, s: (1, 0, 0),
             lambda b, s: (jnp.where(s == ns - 1, 2, 1), 0, 0))
    lat = call(ATT_SUB, ns, n_lat, lambda b, s: b * ns + s, small, lambda b, s: (b * ns + s, 0), cases, "")
    if not need_ctx:
        return (lat,)
    n_cb = ctx_len // blk
    masked = (lambda b, s: (3, 0, 0),) * 3
    first = lambda off: (lambda b, s: (b * nb, 0))
    ctx = call(n_cb, 1, n_batch * ctx_len, lambda b, s: n_lat // ctx_len + b, first,
               lambda b, s: (b * (seq // ctx_len), 0), masked, "_ctx")
    return lat, ctx


def _route(logits):
    m = jnp.max(logits, axis=0, keepdims=True)
    e = jnp.exp(logits - m)
    p = e / jnp.sum(e, axis=0, keepdims=True)
    rows = [p[i:i + 1, :] for i in range(N_EXPERTS)]
    best = None
    gsel = None
    for g in range(N_GROUPS):
        a, b, c, d = rows[4 * g:4 * g + 4]
        m1, n1 = jnp.maximum(a, b), jnp.minimum(a, b)
        m2, n2 = jnp.maximum(c, d), jnp.minimum(c, d)
        score = jnp.maximum(m1, m2) + jnp.maximum(jnp.minimum(m1, m2), jnp.maximum(n1, n2))
        if g == 0:
            best, gsel = score, jnp.zeros(score.shape, I32)
        else:
            upd = score > best
            gsel = jnp.where(upd, g, gsel)
            best = jnp.where(upd, score, best)
    v = [jnp.where(gsel == 0, rows[j], jnp.where(gsel == 1, rows[4 + j], jnp.where(gsel == 2, rows[8 + j], rows[12 + j])))
         for j in range(EXPERTS_PER_GROUP)]
    b1, i1 = v[0], jnp.zeros(gsel.shape, I32)
    for j in range(1, 4):
        upd = v[j] > b1
        i1 = jnp.where(upd, j, i1)
        b1 = jnp.where(upd, v[j], b1)
    b2, i2 = jnp.full(b1.shape, -1.0, F32), jnp.zeros(gsel.shape, I32)
    for j in range(4):
        upd = jnp.logical_and(i1 != j, v[j] > b2)
        i2 = jnp.where(upd, j, i2)
        b2 = jnp.where(upd, v[j], b2)
    lo = jnp.minimum(i1, i2)
    hi = jnp.maximum(i1, i2)
    pair = jnp.where(lo == 0, hi - 1, jnp.where(lo == 1, 6 - hi, 5))
    return gsel * N_PAIRS + pair


def _post_kernel(*refs, lat_tiles, first_trash):
    if lat_tiles is None:
        mix_ref, refs = refs[0], refs[1:]
        mix = mix_ref[...]
    else:
        (ml_ref, mc_ref), refs = refs[:2], refs[2:]
        mix = jnp.where(pl.program_id(0) < lat_tiles, ml_ref[...], mc_ref[...])
    (x_ref, mod_ref, g2_ref, wo_ref, wr_ref, br_ref, tri_ref, low_ref, hs_in_ref,
     xo_ref, pos_ref, tab_ref, cnt_ref, hs_ref,
     carry_ref, cur_ref, nfree_ref, hbuf_ref, posv_ref, poss_ref, sem_s, sem_p) = refs
    del hs_in_ref
    i = pl.program_id(0)
    slot = i % 2
    tm, d = x_ref.shape

    @pl.when(i == 0)
    def _():
        carry_ref[...] = jnp.zeros(carry_ref.shape, F32)
        cur_ref[...] = jnp.zeros(cur_ref.shape, F32)
        nfree_ref[...] = jnp.zeros(nfree_ref.shape, F32)
        tab_ref[...] = jnp.zeros(tab_ref.shape, F32)
        hbuf_ref[1] = jnp.zeros(hbuf_ref.shape[1:], F32)
        posv_ref[1] = first_trash + lax.broadcasted_iota(I32, (1, tm), 1)
        pltpu.make_async_copy(posv_ref.at[1], poss_ref.at[1], sem_p.at[1]).start()

    pltpu.make_async_copy(posv_ref.at[1 - slot], poss_ref.at[1 - slot], sem_p.at[1 - slot]).wait()

    def scatter(s, lo, hi):
        for r in range(lo, hi):
            pltpu.make_async_copy(hbuf_ref.at[s, r], hs_ref.at[poss_ref[s, 0, r]], sem_s).start(priority=r % 2)

    def scatter_wait(s):
        pltpu.make_async_copy(hbuf_ref.at[s], hs_ref.at[pl.ds(0, tm)], sem_s).wait()

    nchunk = 4
    wc = d // nchunk
    xs = []
    for c in range(nchunk):
        scatter(1 - slot, tm * c // nchunk, tm * (c + 1) // nchunk)
        cols = slice(wc * c, wc * (c + 1))
        y = jnp.dot(mix, wo_ref[:, cols], preferred_element_type=F32)
        xc = x_ref[:, cols] + mod_ref[0, :, 2 * d + wc * c:2 * d + wc * (c + 1)] * y
        xo_ref[:, cols] = xc
        xs.append(xc)
    x = jnp.concatenate(xs, axis=1)
    h2 = _rms_mod(x, g2_ref[...], mod_ref[0, :, 4 * d:5 * d], mod_ref[0, :, 3 * d:4 * d])
    hbuf_ref[slot] = h2.reshape(hbuf_ref.shape[1:])
    hi = h2.astype(BF16)
    lo = (h2 - hi.astype(F32)).astype(BF16)
    logits = (lax.dot_general(wr_ref[...], hi, _NT, preferred_element_type=F32)
              + lax.dot_general(wr_ref[...], lo, _NT, preferred_element_type=F32)
              + br_ref[:, 0:1])
    bucket = _route(logits)
    ids = lax.broadcasted_iota(I32, (BUCKET_ROWS, tm), 0)
    onehot = (ids == bucket).astype(F32)
    before = jnp.dot(onehot.astype(BF16), tri_ref[...], preferred_element_type=F32)
    inv = 1.0 / MOE_TILE
    c0 = carry_ref[:, 0:1]
    c1 = c0 + jnp.sum(onehot, axis=1, keepdims=True)
    a0 = jnp.floor((c0 + (MOE_TILE - 1)) * inv)
    new = jnp.floor((c1 + (MOE_TILE - 1)) * inv) - a0
    nfree = nfree_ref[0:1, 0:1]
    new_b = jnp.broadcast_to(new, (BUCKET_ROWS, 128)).astype(BF16)
    id1 = nfree + jnp.dot(low_ref[...], new_b, preferred_element_type=F32)[:, 0:1]
    cur = cur_ref[:, 0:1]
    rank = before + c0
    k = jnp.floor(rank * inv)
    chunk = jnp.where(k < a0, cur, id1 + (k - a0))
    pos = jnp.sum(onehot * (chunk * MOE_TILE + (rank - k * MOE_TILE)), axis=0, keepdims=True).astype(I32)
    pos_ref[...] = pos
    posv_ref[slot] = pos
    pltpu.make_async_copy(posv_ref.at[slot], poss_ref.at[slot], sem_p.at[slot]).start()
    carry_ref[...] = jnp.broadcast_to(c1, carry_ref.shape)
    cnt_ref[...] = jnp.broadcast_to(c1, cnt_ref.shape).astype(I32)
    cur_ref[...] = jnp.broadcast_to(jnp.where(new == 0.0, cur, id1 + new - 1.0), cur_ref.shape)
    nfree_ref[...] = jnp.broadcast_to(nfree + jnp.sum(new, axis=0, keepdims=True), nfree_ref.shape)
    kcol = lax.broadcasted_iota(I32, tab_ref.shape, 1).astype(F32)
    tab = tab_ref[...]
    fresh = jnp.logical_and(kcol >= a0, kcol < a0 + new)
    tab_ref[...] = jnp.where(fresh, id1 + (kcol - a0), tab)
    scatter_wait(1 - slot)

    @pl.when(i == pl.num_programs(0) - 1)
    def _():
        pltpu.make_async_copy(posv_ref.at[slot], poss_ref.at[slot], sem_p.at[slot]).wait()
        scatter(slot, 0, tm)
        scatter_wait(slot)


def _post(mix, xa, mod_l, g2, wo, wr_bf, br_t, tri, low, hs, n_act, n_lat, seq, n_batch):
    d = xa.shape[1]
    tm = POST_TILE
    steps = n_act // tm
    mrow = _mod_row_map(tm, n_lat, seq, n_batch)
    full = lambda a: pl.BlockSpec(a.shape, lambda i, _nd=a.ndim: (0,) * _nd)
    if len(mix) == 1:
        lat_tiles = None
        mix_specs = [pl.BlockSpec((tm, d), lambda i: (i, 0))]
    else:
        lat_tiles = n_lat // tm
        mix_specs = [pl.BlockSpec((tm, d), lambda i: (jnp.minimum(i, lat_tiles - 1), 0)),
                     pl.BlockSpec((tm, d), lambda i: (jnp.maximum(i - lat_tiles, 0), 0))]
    n_in = len(mix_specs)
    return pl.pallas_call(
        functools.partial(_post_kernel, lat_tiles=lat_tiles, first_trash=hs.shape[0] - tm),
        grid=(steps,),
        in_specs=mix_specs + [
            pl.BlockSpec((tm, d), lambda i: (i, 0)),
            pl.BlockSpec((1, 1, 6 * d), lambda i: (mrow(i), 0, 0)),
            full(g2), full(wo), full(wr_bf), full(br_t), full(tri), full(low),
            pl.BlockSpec(memory_space=pl.ANY),
        ],
        out_specs=[
            pl.BlockSpec((tm, d), lambda i: (i, 0)),
            pl.BlockSpec((1, tm), lambda i: (0, i)),
            pl.BlockSpec((BUCKET_ROWS, 256), lambda i: (0, 0)),
            pl.BlockSpec((BUCKET_ROWS, 128), lambda i: (0, 0)),
            pl.BlockSpec(memory_space=pl.ANY),
        ],
        out_shape=[
            jax.ShapeDtypeStruct((n_act, d), F32),
            jax.ShapeDtypeStruct((1, n_act), I32),
            jax.ShapeDtypeStruct((BUCKET_ROWS, 256), F32),
            jax.ShapeDtypeStruct((BUCKET_ROWS, 128), I32),
            jax.ShapeDtypeStruct(hs.shape, F32),
        ],
        scratch_shapes=[pltpu.VMEM((BUCKET_ROWS, 128), F32), pltpu.VMEM((BUCKET_ROWS, 128), F32), pltpu.VMEM((8, 128), F32),
                        pltpu.VMEM((2, tm) + hs.shape[1:], F32), pltpu.VMEM((2, 1, tm), I32), pltpu.SMEM((2, 1, tm), I32),
                        pltpu.SemaphoreType.DMA(()), pltpu.SemaphoreType.DMA((2,))],
        input_output_aliases={n_in + 8: 4},
        compiler_params=_params(1, VMEM_LIMIT),
        name="post_mixer",
    )(*mix, xa, mod_l, g2, wo, wr_bf, br_t, tri, low, hs)


def _moe_kernel(blk_ref, two_ref, ea_ref, eb_ref, nu_ref, h_ref, wga_ref, wua_ref, wda_ref, wgb_ref, wub_ref, wdb_ref,
                wrt_ref, brt_ref, *refs):
    o_ref = refs[len(refs) // 2]
    n_cast = len(refs) // 2
    for k in range(n_cast):
        refs[n_cast + 1 + k][...] = refs[k][0].astype(BF16)
    t = pl.program_id(0)
    active = t < nu_ref[0]
    both = two_ref[t] == 1
    half = h_ref.shape[0] // 2

    def compute(n_halves):
        ea = ea_ref[t]
        eb = eb_ref[t]
        wdiff = wrt_ref[pl.ds(ea, 1), :] - wrt_ref[pl.ds(eb, 1), :]
        bdiff = brt_ref[pl.ds(ea, 1), 0:1] - brt_ref[pl.ds(eb, 1), 0:1]
        hbs, was = [], []
        for i in range(n_halves):
            rows = h_ref[half * i:half * (i + 1)]
            h = rows.reshape(half, h_ref.shape[1] * h_ref.shape[2])
            hbs.append(h.astype(BF16))
            was.append(jax.nn.sigmoid(jnp.sum(h * wdiff, axis=-1, keepdims=True) + bdiff))
        slots = ((wga_ref, wua_ref, wda_ref), (wgb_ref, wub_ref, wdb_ref))
        f = wga_ref.shape[2]
        chains = [(i, e, c) for i in range(n_halves) for e in range(2) for c in range(f // 512)]
        gu = {}
        for i, e, c in chains:
            cols = slice(512 * c, 512 * (c + 1))
            gu[i, e, c] = (jnp.dot(hbs[i], slots[e][0][0, :, cols], preferred_element_type=F32),
                           jnp.dot(hbs[i], slots[e][1][0, :, cols], preferred_element_type=F32))
        act = {k: (_silu(g) * u).astype(BF16) for k, (g, u) in gu.items()}
        outs = {}
        for i, e, c in chains:
            part = jnp.dot(act[i, e, c], slots[e][2][0, 512 * c:512 * (c + 1), :], preferred_element_type=F32)
            outs[i, e] = part if (i, e) not in outs else outs[i, e] + part
        for i in range(n_halves):
            y = was[i] * outs[i, 0] + (1.0 - was[i]) * outs[i, 1]
            o_ref[half * i:half * (i + 1)] = y.reshape((half,) + o_ref.shape[1:])
        if n_halves == 1:
            o_ref[half:2 * half] = jnp.zeros((half,) + o_ref.shape[1:], F32)

    @pl.when(jnp.logical_and(active, both))
    def _():
        compute(2)

    @pl.when(jnp.logical_and(active, jnp.logical_not(both)))
    def _():
        compute(1)

    @pl.when(jnp.logical_not(active))
    def _():
        o_ref[...] = jnp.zeros(o_ref.shape, F32)


def _moe(blk, two, ea, eb, nu, hs, n_tiles, wg, wu, wd, wr_t, br_t, w_next, next_layer):
    row = hs.shape[1:]
    tm = MOE_TILE
    d, f = wg.shape[1:]
    n_cast = 0 if w_next is None else len(w_next)
    if n_cast:
        w_spec, wo_spec, wo_shape = _cast_specs(w_next[0], next_layer, n_tiles)
    else:
        w_spec = wo_spec = wo_shape = None
    wa_spec = lambda shape: pl.BlockSpec(shape, lambda t, blk_, two_, ea_, eb_, nu_: (ea_[t], 0, 0))
    wb_spec = lambda shape: pl.BlockSpec(shape, lambda t, blk_, two_, ea_, eb_, nu_: (eb_[t], 0, 0))
    chunk = pl.BlockSpec((tm,) + row, lambda t, blk_, two_, ea_, eb_, nu_: (blk_[t], 0, 0))
    grid_spec = pltpu.PrefetchScalarGridSpec(
        num_scalar_prefetch=5,
        grid=(n_tiles,),
        in_specs=[
            chunk,
            wa_spec((1, d, f)), wa_spec((1, d, f)), wa_spec((1, f, d)),
            wb_spec((1, d, f)), wb_spec((1, d, f)), wb_spec((1, f, d)),
            pl.BlockSpec(wr_t.shape, lambda t, *_: (0, 0)),
            pl.BlockSpec(br_t.shape, lambda t, *_: (0, 0)),
        ] + [w_spec] * n_cast,
        out_specs=[chunk] + [wo_spec] * n_cast,
    )
    return pl.pallas_call(
        _moe_kernel,
        grid_spec=grid_spec,
        out_shape=[jax.ShapeDtypeStruct((n_tiles * tm,) + row, F32)] + [wo_shape] * n_cast,
        compiler_params=_params(1, VMEM_LIMIT),
        name="experts",
    )(blk, two, ea, eb, nu, hs, wg, wu, wd, wg, wu, wd, wr_t, br_t, *(w_next or ()))


def _combine_kernel(pos_ref, posn_ref, x_ref, mod_ref, fg_ref, ys_ref, o_ref, gbuf_ref, sem):
    d = x_ref.shape[1]
    rows, issue_next, finish = _gather_rows(pos_ref, posn_ref, ys_ref, gbuf_ref, sem)
    issue_next(0, 1)
    x = x_ref[...] + mod_ref[0, :, 5 * d:6 * d] * rows
    o_ref[...] = x * lax.rsqrt(jnp.mean(x * x, axis=-1, keepdims=True) + EPS) * fg_ref[...]
    finish()


def _combine(pos3, xa, mod_l, fg, ys, n_lat, seq, n_batch):
    n, d = xa.shape
    tm = ROW_TILE
    steps = n // tm
    mrow = _mod_row_map(tm, n_lat, seq, n_batch)
    return pl.pallas_call(
        _combine_kernel,
        grid=(steps,),
        in_specs=[
            pl.BlockSpec((1, 1, tm), lambda i: (i, 0, 0), memory_space=pltpu.SMEM),
            pl.BlockSpec((1, 1, tm), lambda i: (jnp.minimum(i + 1, steps - 1), 0, 0), memory_space=pltpu.SMEM),
            pl.BlockSpec((tm, d), lambda i: (i, 0)),
            pl.BlockSpec((1, 1, 6 * d), lambda i: (mrow(i), 0, 0)),
            pl.BlockSpec((1, d), lambda i: (0, 0)),
            pl.BlockSpec(memory_space=pl.ANY),
        ],
        out_specs=pl.BlockSpec((tm, d), lambda i: (i, 0)),
        out_shape=jax.ShapeDtypeStruct((n, d), F32),
        scratch_shapes=[pltpu.VMEM((2, tm) + ys.shape[1:], F32), pltpu.SemaphoreType.DMA((2,))],
        compiler_params=_params(1, VMEM_LIMIT),
        name="combine",
    )(pos3, pos3, xa, mod_l, fg, ys)


def _tile_plan(tab, counts, n_tiles, tile):
    cnt = counts[:N_BUCKETS, 0]
    n_chunks = (cnt + tile - 1) // tile
    ends = jnp.cumsum(n_chunks)
    starts = ends - n_chunks
    n_used = ends[-1]
    t = jnp.arange(n_tiles, dtype=I32)
    t_eff = jnp.minimum(t, jnp.maximum(n_used - 1, 0))
    tb = jnp.minimum(jnp.sum((ends[None, :] <= t_eff[:, None]).astype(I32), axis=1), N_BUCKETS - 1)
    k = n_chunks[tb] - 1 - (t_eff - starts[tb])
    owned = tab.astype(I32)[tb, k]
    blk = jnp.where(t < n_used, owned, t)
    two = (cnt[tb] - k * tile > tile // 2).astype(I32)
    pairs = np.array(PAIR_SLOTS, dtype=np.int32)
    ea = (tb // N_PAIRS) * EXPERTS_PER_GROUP + jnp.asarray(pairs[:, 0])[tb % N_PAIRS]
    eb = (tb // N_PAIRS) * EXPERTS_PER_GROUP + jnp.asarray(pairs[:, 1])[tb % N_PAIRS]
    return blk.astype(I32), two, ea.astype(I32), eb.astype(I32), n_used.astype(I32).reshape(1)


def kernel(x, c, ctx, c_ctx, w_mod, b_mod, norm_g, w_in_even, w_out_even, ret_log_decay, ret_gn_g, sg_ln_g, sg_ln_b, sg_w, sg_b, w_qkv_odd, w_o_odd, attn_sink, w_router, b_router, w_e_gate, w_e_up, w_e_down, final_g):
    n_batch, seq, d = x.shape
    ctx_len = ctx.shape[1]
    depth = w_mod.shape[0]
    n_lat = n_batch * seq
    n_all = n_lat + n_batch * ctx_len
    tm = ROW_TILE
    assert seq % POST_TILE == 0 and (n_batch * ctx_len) % POST_TILE == 0

    mod_rows = -(-(n_batch + 1) // 16) * 16
    c_rows = jnp.zeros((mod_rows, d), F32).at[:n_batch].set(c).at[n_batch].set(c_ctx)
    mod = _modulation(c_rows, w_mod, b_mod).reshape(depth, mod_rows, 1, 6 * d)

    wr_t = w_router.T.astype(F32)
    wr_bf = wr_t.astype(BF16)
    br_t = jnp.broadcast_to(b_router.astype(F32)[:, None], (N_EXPERTS, 128))
    tri = jnp.asarray(np.triu(np.ones((POST_TILE, POST_TILE), np.float32), k=1), BF16)
    low = jnp.asarray(np.tril(np.ones((BUCKET_ROWS, BUCKET_ROWS), np.float32), k=-1), BF16)
    cos_t, sin_t = _rope_tables(seq, tm)
    bias = _attn_bias()

    n_exp, d_exp = w_e_gate.shape[1], w_e_gate.shape[3]
    assert d_exp == d
    w_views = [w.reshape(depth, n_exp * d, d) for w in (w_e_gate, w_e_up, w_e_down)]

    n_tiles = n_all // MOE_TILE + N_BUCKETS
    p_rows = n_tiles * MOE_TILE
    hs = jnp.zeros((p_rows + POST_TILE, ROW_SUB, d // ROW_SUB), F32)

    source = (x.reshape(n_lat, d), ctx.reshape(n_batch * ctx_len, d))
    for l in range(depth):
        i = l // 2
        last = l == depth - 1
        mod_l = mod[l]
        g1 = norm_g[l, 0].reshape(1, d)
        g2 = norm_g[l, 1].reshape(1, d)
        if l % 2 == 0:
            res = _in_stage(l, source, mod_l, g1, w_in_even[i].astype(BF16), None, w_views if l == 0 else None,
                            n_lat, seq, n_batch)
            xa, p = res[:2]
            tabs = _retention_tables(ret_log_decay[i], RET_BLOCK)
            sg_bias = jnp.repeat(sg_b[i].astype(F32).T, SG_CHUNK, axis=1)
            mix = _even_mix(p, tabs, ret_gn_g[i].reshape(1, -1), sg_ln_g[i].reshape(1, -1), sg_ln_b[i].reshape(1, -1),
                            sg_w[i].astype(BF16), sg_bias, n_batch, seq, ctx_len)
            wo = w_out_even[i].astype(BF16)
        else:
            res = _in_stage(l, source, mod_l, g1, w_qkv_odd[i].astype(BF16), (cos_t, sin_t), w_views if l == 0 else None,
                            n_lat, seq, n_batch)
            xa, q, kv = res[:3]
            sink_t = jnp.broadcast_to(attn_sink[i].astype(F32)[:, None], (ATT_HEADS, 128))
            mix = _attention(q, kv, bias, sink_t, n_batch, seq, ctx_len, need_ctx=not last)
            wo = w_o_odd[i].astype(BF16)
        n_act = n_lat if last else n_all
        xm, pos, tab, counts, hs = _post(mix, xa, mod_l, g2, wo, wr_bf, br_t, tri, low, hs, n_act, n_lat, seq, n_batch)
        plan = _tile_plan(tab, counts, n_tiles, MOE_TILE)
        pos3 = pos.reshape(n_act // tm, 1, tm)
        if l == 0:
            w_bf = res[-3:]
        ys, *w_cast = _moe(*plan, hs, n_tiles, *(w.reshape(n_exp, d, d) for w in w_bf), wr_t, br_t,
                           None if last else w_views, l + 1)
        w_bf = w_cast
        source = (pos3, xm, mod_l, ys)
    out = _combine(pos3, xm, mod_l, final_g.reshape(1, d), ys, n_lat, seq, n_batch)
    return out.reshape(n_batch, seq, d)
```

```python
import functools

import jax
import jax.numpy as jnp
import numpy as np
from jax import lax
from jax.experimental import pallas as pl
from jax.experimental.pallas import tpu as pltpu

F32 = jnp.float32
BF16 = jnp.bfloat16
I32 = jnp.int32

EPS = 1e-6
NEG = -1e30
LOG2E = 1.4426950408889634
GRID_W = 64
ROPE_BASE = 10000.0
RET_HEADS = 4
RET_QK_DIM = 64
SG_GROUPS = 4
SG_CHUNK = 128
RET_BLOCK = 256
ATT_HEADS = 8
ATT_KV_HEADS = 2
ATT_GROUP = ATT_HEADS // ATT_KV_HEADS
HEAD_DIM = 128
ATT_BLOCK = 128
WINDOW = 128
ATT_SUB = 4
ATT_SLAB = 32
N_EXPERTS = 16
N_GROUPS = 4
EXPERTS_PER_GROUP = 4
N_PAIRS = 6
N_BUCKETS = N_GROUPS * N_PAIRS
PAIR_SLOTS = ((0, 1), (0, 2), (0, 3), (1, 3), (1, 2), (3, 2))
BUCKET_ROWS = 32

ROW_TILE = 512
POST_TILE = 1024
MOE_TILE = 512
MOE_PARTS = 4
ROW_SUB = 8
VMEM_LIMIT = 56 * 1024 * 1024

_NT = (((1,), (1,)), ((), ()))
_TN = (((0,), (0,)), ((), ()))


def _params(n_axes, vmem=None):
    return pltpu.CompilerParams(dimension_semantics=("arbitrary",) * n_axes, vmem_limit_bytes=vmem)


def _silu(v):
    return v * jax.nn.sigmoid(v)


def _rms_mod(x, g, scale, shift):
    y = x * lax.rsqrt(jnp.mean(x * x, axis=-1, keepdims=True) + EPS) * g
    return y * (1.0 + scale) + shift


def _mod_kernel(c_ref, w_ref, b_ref, o_ref):
    a = _silu(c_ref[...]).astype(BF16)
    o_ref[0] = jnp.dot(a, w_ref[0].astype(BF16), preferred_element_type=F32) + b_ref[0]


def _modulation(c_rows, w_mod, b_mod):
    depth, d, six_d = w_mod.shape
    mr = c_rows.shape[0]
    tn = 1536
    return pl.pallas_call(
        _mod_kernel,
        grid=(depth, six_d // tn),
        in_specs=[
            pl.BlockSpec((mr, d), lambda l, j: (0, 0)),
            pl.BlockSpec((1, d, tn), lambda l, j: (l, 0, j)),
            pl.BlockSpec((1, 1, tn), lambda l, j: (l, 0, j)),
        ],
        out_specs=pl.BlockSpec((1, mr, tn), lambda l, j: (l, 0, j)),
        out_shape=jax.ShapeDtypeStruct((depth, mr, six_d), F32),
        compiler_params=_params(2, VMEM_LIMIT),
        name="modulation",
    )(c_rows, w_mod, b_mod.reshape(depth, 1, six_d))


def _cast_specs(w, layer, steps):
    w_rows, w_cols = w.shape[1:]
    cast_steps = 1
    while 2 * cast_steps <= steps and w_rows % (2 * cast_steps) == 0:
        cast_steps *= 2
    cast_rows = w_rows // cast_steps
    row = lambda i, *_: jnp.minimum(i, cast_steps - 1)
    return (pl.BlockSpec((1, cast_rows, w_cols), lambda i, *_: (layer, row(i), 0)),
            pl.BlockSpec((cast_rows, w_cols), lambda i, *_: (row(i), 0)),
            jax.ShapeDtypeStruct((w_rows, w_cols), BF16))


def _mod_row_map(tm, n_lat, seq, n_batch):
    def index(i):
        row0 = i * tm
        return jnp.where(row0 < n_lat, row0 // seq, n_batch)
    return index


def _gather_rows(pos_ref, posn_ref, ys_ref, gbuf_ref, sem):
    i = pl.program_id(0)
    slot = i % 2
    tm = gbuf_ref.shape[1]

    def issue(p_ref, s, lo, hi):
        for r in range(lo, hi):
            pltpu.make_async_copy(ys_ref.at[p_ref[0, 0, r]], gbuf_ref.at[s, r], sem.at[s]).start(priority=r % 2)

    def wait(s):
        pltpu.make_async_copy(ys_ref.at[pl.ds(0, tm)], gbuf_ref.at[s], sem.at[s]).wait()

    @pl.when(i == 0)
    def _():
        issue(pos_ref, 0, 0, tm)

    wait(slot)
    rows = gbuf_ref[slot].reshape(tm, gbuf_ref.shape[2] * gbuf_ref.shape[3])

    def issue_next(k, parts):
        issue(posn_ref, 1 - slot, tm * k // parts, tm * (k + 1) // parts)

    def finish():
        @pl.when(i == pl.num_programs(0) - 1)
        def _():
            wait(1 - slot)

    return rows, issue_next, finish


def _project_even(h, w_ref, o_ref, between, cast):
    for c in range(5):
        between(c, 5)
        cast(c)
        p = jnp.dot(h, w_ref[:, 512 * c:512 * (c + 1)], preferred_element_type=F32)
        if c == 0:
            col = lax.broadcasted_iota(I32, p.shape, 1)
            p = jnp.where(col >= 256, p * (RET_QK_DIM ** -0.5), p)
        elif c >= 3:
            p = jax.nn.gelu(p)
        o_ref[:, 512 * c:512 * (c + 1)] = p.astype(BF16)


def _project_odd(h, w_ref, cos_ref, sin_ref, q_ref, kv_ref, between, cast):
    tm = h.shape[0]
    cos = cos_ref[...]
    sin = sin_ref[...]
    lane = lax.broadcasted_iota(I32, (tm, HEAD_DIM), 1)
    first = (lane % 64) < 32

    def rope(p):
        partner = jnp.where(first, pltpu.roll(p, 96, 1), pltpu.roll(p, 32, 1))
        return p * cos + partner * sin

    scale = HEAD_DIM ** -0.5 * LOG2E
    for c in range(3):
        if c < 2:
            between(c, 2)
        cast(c)
        p = jnp.dot(h, w_ref[:, 512 * c:512 * (c + 1)], preferred_element_type=F32)
        for j in range(4):
            pj = p[:, 128 * j:128 * (j + 1)]
            if c < 2:
                q_ref[:, 512 * c + 128 * j:512 * c + 128 * (j + 1)] = (rope(pj) * scale).astype(BF16)
            elif j < 2:
                kv_ref[:, 128 * j:128 * (j + 1)] = rope(pj).astype(BF16)
            else:
                kv_ref[:, 128 * j:128 * (j + 1)] = pj.astype(BF16)


def _in_kernel(*refs, first, odd, lat_tiles, casts):
    if first:
        (xl_ref, xc_ref), refs = refs[:2], refs[2:]
        x = jnp.where(pl.program_id(0) < lat_tiles, xl_ref[...], xc_ref[...])
        between, finish = (lambda k, parts: None), (lambda: None)
    else:
        (pos_ref, posn_ref, x_ref, modp_ref, ys_ref), refs = refs[:5], refs[5:]
        gbuf_ref, sem = refs[-2:]
        refs = refs[:-2]
        d = x_ref.shape[1]
        rows, between, finish = _gather_rows(pos_ref, posn_ref, ys_ref, gbuf_ref, sem)
        x = x_ref[...] + modp_ref[0, :, 5 * d:6 * d] * rows
    mod_ref, g_ref, w_ref = refs[:3]
    d = x.shape[1]
    h = _rms_mod(x, g_ref[...], mod_ref[0, :, d:2 * d], mod_ref[0, :, 0:d]).astype(BF16)
    n_out = 3 if odd else 2
    if casts:
        cast_in, tail, cast_out = refs[-n_out - 6:-n_out - 3], refs[-n_out - 3:-3], refs[-3:]
    else:
        cast_in, tail, cast_out = (), refs[-n_out:], ()

    def cast(k):
        if k < len(cast_in):
            cast_out[k][...] = cast_in[k][0].astype(BF16)

    if odd:
        xo_ref, q_ref, kv_ref = tail
        cos_ref, sin_ref = refs[3:5]
        xo_ref[...] = x
        _project_odd(h, w_ref, cos_ref, sin_ref, q_ref, kv_ref, between, cast)
    else:
        xo_ref, o_ref = tail
        xo_ref[...] = x
        _project_even(h, w_ref, o_ref, between, cast)
    finish()


def _in_stage(layer, source, mod_l, g, w, rope, w_exp, n_lat, seq, n_batch):
    first = len(source) == 2
    odd = rope is not None
    tm = ROW_TILE
    if first:
        xl, xc = source
        d = xl.shape[1]
        n = xl.shape[0] + xc.shape[0]
        lat_tiles = xl.shape[0] // tm
        src_specs = [pl.BlockSpec((tm, d), lambda i: (jnp.minimum(i, lat_tiles - 1), 0)),
                     pl.BlockSpec((tm, d), lambda i: (jnp.maximum(i - lat_tiles, 0), 0))]
        scratch = []
    else:
        pos3, xm, mod_prev, ys = source
        n, d = xm.shape
        lat_tiles = n_lat // tm
        steps = n // tm
        mrow_p = _mod_row_map(tm, n_lat, seq, n_batch)
        src_specs = [pl.BlockSpec((1, 1, tm), lambda i: (i, 0, 0), memory_space=pltpu.SMEM),
                     pl.BlockSpec((1, 1, tm), lambda i: (jnp.minimum(i + 1, steps - 1), 0, 0), memory_space=pltpu.SMEM),
                     pl.BlockSpec((tm, d), lambda i: (i, 0)),
                     pl.BlockSpec((1, 1, 6 * d), lambda i: (mrow_p(i), 0, 0)),
                     pl.BlockSpec(memory_space=pl.ANY)]
        source = (pos3, pos3, xm, mod_prev, ys)
        scratch = [pltpu.VMEM((2, tm) + ys.shape[1:], F32), pltpu.SemaphoreType.DMA((2,))]
    mrow = _mod_row_map(tm, n_lat, seq, n_batch)
    specs = src_specs + [pl.BlockSpec((1, 1, 6 * d), lambda i: (mrow(i), 0, 0)),
                         pl.BlockSpec((1, d), lambda i: (0, 0)),
                         pl.BlockSpec(w.shape, lambda i: (0, 0))]
    args = tuple(source) + (mod_l, g, w)
    out_specs = [pl.BlockSpec((tm, d), lambda i: (i, 0))]
    out_shape = [jax.ShapeDtypeStruct((n, d), F32)]
    if odd:
        tiles_per_seq = seq // tm

        def rope_row(i):
            return jnp.where(i < lat_tiles, i % tiles_per_seq, tiles_per_seq)

        specs += [pl.BlockSpec((tm, HEAD_DIM), lambda i: (rope_row(i), 0))] * 2
        args += tuple(rope)
        widths = (ATT_HEADS * HEAD_DIM, 2 * ATT_KV_HEADS * HEAD_DIM)
    else:
        widths = (w.shape[1],)
    out_specs += [pl.BlockSpec((tm, wd), lambda i: (i, 0)) for wd in widths]
    out_shape += [jax.ShapeDtypeStruct((n, wd), BF16) for wd in widths]
    if w_exp is not None:
        w_spec, wo_spec, wo_shape = _cast_specs(w_exp[0], layer, n // tm)
        specs += [w_spec] * 3
        args += tuple(w_exp)
        out_specs += [wo_spec] * 3
        out_shape += [wo_shape] * 3
    return pl.pallas_call(
        functools.partial(_in_kernel, first=first, odd=odd, lat_tiles=lat_tiles, casts=w_exp is not None),
        grid=(n // tm,),
        in_specs=specs,
        out_specs=out_specs,
        out_shape=out_shape,
        scratch_shapes=scratch,
        compiler_params=_params(1, VMEM_LIMIT),
        name="in_odd" if odd else "in_even",
    )(*args)


def _rope_tables(seq, tm):
    t = np.arange(seq)
    quarter = HEAD_DIM // 4
    inv_freq = ROPE_BASE ** (-np.arange(quarter, dtype=np.float64) / quarter)
    ar = (t // GRID_W)[:, None] * inv_freq[None, :]
    ac = (t % GRID_W)[:, None] * inv_freq[None, :]
    cos = np.concatenate([np.cos(ar), np.cos(ar), np.cos(ac), np.cos(ac)], axis=1)
    sin = np.concatenate([-np.sin(ar), np.sin(ar), -np.sin(ac), np.sin(ac)], axis=1)
    cos = np.concatenate([cos, np.ones((tm, HEAD_DIM))], axis=0)
    sin = np.concatenate([sin, np.zeros((tm, HEAD_DIM))], axis=0)
    return jnp.asarray(cos, F32), jnp.asarray(sin, F32)


def _ret_increments(qk_ref, v_ref, rows, zf_ref, zb_ref):
    upper_rows = lax.broadcasted_iota(I32, (128, 128), 0) < RET_QK_DIM
    out = []
    for grp in range(2):
        k2 = qk_ref[rows, 256 + 128 * grp:256 + 128 * (grp + 1)].astype(F32)
        v2 = v_ref[rows, 256 * grp:256 * (grp + 1)]
        halves = []
        for z_ref in (zf_ref, zb_ref):
            kz = (k2 * z_ref[grp]).astype(BF16)
            full = lax.dot_general(kz, v2, _TN, preferred_element_type=F32)
            halves.append(jnp.where(upper_rows, full[:, 0:128], full[:, 128:256]))
        out.append(jnp.concatenate(halves, axis=1))
    return out


def _mix_outputs(qk_ref, v_ref, gate_ref, u_ref, s_ref, rows, states, dec_ref, xi_ref, gn_ref, lng_ref, lnb_ref,
                 ws_ref, sgb_ref, o_ref):
    lane = lax.broadcasted_iota(I32, (1, 128), 1)
    head_mask = [(lane // RET_QK_DIM == hh).astype(BF16) for hh in range(2)]
    for grp in range(2):
        q2 = qk_ref[rows, 128 * grp:128 * (grp + 1)]
        k2 = qk_ref[rows, 256 + 128 * grp:256 + 128 * (grp + 1)]
        for hh in range(2):
            hd = 2 * grp + hh
            cols = slice(128 * hd, 128 * (hd + 1))
            qm = q2 * head_mask[hh]
            sc = lax.dot_general(qm, k2, _NT, preferred_element_type=F32) * dec_ref[hd]
            o = jnp.dot(sc.astype(BF16), v_ref[rows, cols], preferred_element_type=F32)
            if states is not None:
                cross = jnp.dot(qm, states[grp], preferred_element_type=F32) * xi_ref[hd]
                o = o + cross[:, 0:128] + cross[:, 128:256]
            mu = jnp.mean(o, axis=-1, keepdims=True)
            oc = o - mu
            var = jnp.mean(oc * oc, axis=-1, keepdims=True)
            y = oc * lax.rsqrt(var + EPS) * gn_ref[:, cols]
            y = y * _silu(gate_ref[rows, cols].astype(F32))
            o_ref[rows, cols] = y.astype(BF16)
    for sub in range((rows.stop - rows.start) // SG_CHUNK):
        srows = slice(rows.start + SG_CHUNK * sub, rows.start + SG_CHUNK * (sub + 1))
        s = s_ref[srows, :].astype(F32)
        mu = jnp.mean(s, axis=-1, keepdims=True)
        sc_ = s - mu
        var = jnp.mean(sc_ * sc_, axis=-1, keepdims=True)
        sn = (sc_ * lax.rsqrt(var + EPS) * lng_ref[...] + lnb_ref[...]).astype(BF16)
        for g in range(SG_GROUPS):
            cols = slice(128 * g, 128 * (g + 1))
            mixed = jnp.dot(ws_ref[g], sn[:, cols], preferred_element_type=F32) + sgb_ref[:, cols]
            o_ref[srows, 512 + 128 * g:512 + 128 * (g + 1)] = (u_ref[srows, cols].astype(F32) * mixed).astype(BF16)


def _even_lat_kernel(bqk_ref, bv_ref, cqk_ref, cv_ref, qk_ref, v_ref, gate_ref, u_ref, s_ref, dec_ref, xi_ref,
                     zf_ref, zb_ref, gdec_ref, gn_ref, lng_ref, lnb_ref, ws_ref, sgb_ref, o_ref, ds_ref, sp_ref,
                     *, inc_steps):
    j = pl.program_id(1)
    cb = RET_BLOCK
    n_slots = ds_ref.shape[0]
    inc_chunks = bqk_ref.shape[0] // cb
    out_chunks = qk_ref.shape[0] // cb

    @pl.when(j == 0)
    def _():
        inc = _ret_increments(cqk_ref, cv_ref, slice(0, cb), zf_ref, zb_ref)
        for grp in range(2):
            ds_ref[0, grp] = inc[grp]

    @pl.when(j < inc_steps)
    def _():
        for c in range(inc_chunks):
            inc = _ret_increments(bqk_ref, bv_ref, slice(cb * c, cb * (c + 1)), zf_ref, zb_ref)
            for grp in range(2):
                ds_ref[1 + inc_chunks * j + c, grp] = inc[grp]

    @pl.when(j == inc_steps)
    def _scan():
        fwd = ds_ref[0, :, :, 0:128]
        for n in range(1, n_slots):
            sp_ref[n, :, :, 0:128] = fwd
            fwd = fwd * gdec_ref[:, :, 0:128] + ds_ref[n, :, :, 0:128]
        bwd = ds_ref[0, :, :, 128:256]
        for n in range(n_slots - 1, 0, -1):
            sp_ref[n, :, :, 128:256] = bwd
            bwd = bwd * gdec_ref[:, :, 128:256] + ds_ref[n, :, :, 128:256]

    @pl.when(j >= inc_steps)
    def _():
        for c in range(out_chunks):
            n = 1 + out_chunks * (j - inc_steps) + c
            states = [sp_ref[n, grp].astype(BF16) for grp in range(2)]
            _mix_outputs(qk_ref, v_ref, gate_ref, u_ref, s_ref, slice(cb * c, cb * (c + 1)), states, dec_ref, xi_ref,
                         gn_ref, lng_ref, lnb_ref, ws_ref, sgb_ref, o_ref)


def _even_ctx_kernel(qk_ref, v_ref, gate_ref, u_ref, s_ref, dec_ref, xi_ref, gn_ref, lng_ref, lnb_ref, ws_ref, sgb_ref,
                     o_ref):
    _mix_outputs(qk_ref, v_ref, gate_ref, u_ref, s_ref, slice(0, RET_BLOCK), None, dec_ref, xi_ref,
                 gn_ref, lng_ref, lnb_ref, ws_ref, sgb_ref, o_ref)


def _retention_tables(log_decay, cb):
    lg = -jnp.exp(log_decay.astype(F32))
    pos = jnp.arange(cb, dtype=F32)
    diff = pos[:, None] - pos[None, :]
    lower = jnp.where(diff >= 0, jnp.exp(lg[0][:, None, None] * jnp.maximum(diff, 0.0)[None]), 0.0)
    upper = jnp.where(diff <= 0, jnp.exp(lg[1][:, None, None] * jnp.maximum(-diff, 0.0)[None]), 0.0)
    dec = lower + upper
    xi_f = jnp.exp(lg[0][:, None] * (pos[None, :] + 1.0))
    xi_b = jnp.exp(lg[1][:, None] * (cb - pos[None, :]))
    xi = jnp.concatenate([jnp.broadcast_to(xi_f[:, :, None], (RET_HEADS, cb, 128)),
                          jnp.broadcast_to(xi_b[:, :, None], (RET_HEADS, cb, 128))], axis=2)
    zeta_f = jnp.exp(lg[0][:, None] * (cb - 1.0 - pos[None, :]))
    zeta_b = jnp.exp(lg[1][:, None] * pos[None, :])

    def lanes(z):
        return jnp.repeat(z.reshape(2, 2, cb), RET_QK_DIM, axis=1).transpose(0, 2, 1)

    gstep = jnp.exp(lg * cb)

    def rows(gv):
        return jnp.broadcast_to(jnp.repeat(gv.reshape(2, 2), RET_QK_DIM, axis=1)[:, :, None], (2, 128, 128))

    gdec = jnp.concatenate([rows(gstep[0]), rows(gstep[1])], axis=2)
    return dec, xi, lanes(zeta_f), lanes(zeta_b), gdec


def _even_mix(p, tabs, gn_g, ln_g, ln_b, w_s, sg_bias, n_batch, seq, ctx_len):
    cb = RET_BLOCK
    big, blk = 4 * cb, 2 * cb
    assert ctx_len == cb and seq % big == 0
    n_lat = n_batch * seq
    inc_steps, out_steps = seq // big, seq // blk
    dec, xi, zf, zb, gdec = tabs
    consts = (dec, xi, zf, zb, gdec, gn_g, ln_g, ln_b, w_s, sg_bias)
    full2 = lambda a: pl.BlockSpec(a.shape, lambda b, j, _nd=a.ndim: (0,) * _nd)
    inc_row = lambda b, j: b * inc_steps + jnp.minimum(j, inc_steps - 1)
    out_row = lambda b, j: b * out_steps + jnp.maximum(j - inc_steps, 0)
    ctx_row = lambda b, j: n_lat // cb + b
    lat = pl.pallas_call(
        functools.partial(_even_lat_kernel, inc_steps=inc_steps),
        grid=(n_batch, inc_steps + out_steps),
        in_specs=[pl.BlockSpec((big, 512), lambda b, j: (inc_row(b, j), 0)),
                  pl.BlockSpec((big, 512), lambda b, j: (inc_row(b, j), 1)),
                  pl.BlockSpec((cb, 512), lambda b, j: (ctx_row(b, j), 0)),
                  pl.BlockSpec((cb, 512), lambda b, j: (ctx_row(b, j), 1))]
                 + [pl.BlockSpec((blk, 512), lambda b, j, _c=c: (out_row(b, j), _c)) for c in range(5)]
                 + [full2(a) for a in consts],
        out_specs=pl.BlockSpec((blk, 1024), lambda b, j: (out_row(b, j), 0)),
        out_shape=jax.ShapeDtypeStruct((n_lat, 1024), BF16),
        scratch_shapes=[pltpu.VMEM((seq // cb + 1, 2, 128, 256), F32), pltpu.VMEM((seq // cb + 1, 2, 128, 256), F32)],
        compiler_params=_params(2, VMEM_LIMIT),
        name="even_mix",
    )(p, p, p, p, p, p, p, p, p, *consts)
    ctx_consts = (dec, xi, gn_g, ln_g, ln_b, w_s, sg_bias)
    full1 = lambda a: pl.BlockSpec(a.shape, lambda b, _nd=a.ndim: (0,) * _nd)
    ctx = pl.pallas_call(
        _even_ctx_kernel,
        grid=(n_batch,),
        in_specs=[pl.BlockSpec((cb, 512), lambda b, _c=c: (n_lat // cb + b, _c)) for c in range(5)]
                 + [full1(a) for a in ctx_consts],
        out_specs=pl.BlockSpec((cb, 1024), lambda b: (b, 0)),
        out_shape=jax.ShapeDtypeStruct((n_batch * ctx_len, 1024), BF16),
        compiler_params=_params(1, VMEM_LIMIT),
        name="even_mix_ctx",
    )(p, p, p, p, p, *ctx_consts)
    return lat, ctx


def _attn_kernel(q_ref, kvp_ref, kvm_ref, kvn_ref, kvx_ref, bias_f_ref, bias_m_ref, bias_l_ref, sink_ref, o_ref,
                 s_scr, p_scr, r_scr, *, n_sub):
    blk = ATT_BLOCK
    nloc = 3 * blk
    rows_all = ATT_GROUP * blk
    bias_refs = [bias_f_ref] + [bias_m_ref] * (n_sub - 2) + [bias_l_ref]
    chains = [(kvh, sub) for kvh in range(ATT_KV_HEADS) for sub in range(n_sub)]
    v_all, sinks = {}, {}
    for ci, (kvh, sub) in enumerate(chains):
        kcols = slice(HEAD_DIM * kvh, HEAD_DIM * (kvh + 1))
        vcols = slice(HEAD_DIM * (ATT_KV_HEADS + kvh), HEAD_DIM * (ATT_KV_HEADS + kvh + 1))
        k_parts = [kvp_ref[:, kcols]] + [kvm_ref[blk * j:blk * (j + 1), kcols] for j in range(n_sub)] + [kvn_ref[:, kcols]]
        v_parts = [kvp_ref[:, vcols]] + [kvm_ref[blk * j:blk * (j + 1), vcols] for j in range(n_sub)] + [kvn_ref[:, vcols]]
        heads = [ATT_GROUP * kvh + j for j in range(ATT_GROUP)]
        rows = slice(blk * sub, blk * (sub + 1))
        k_all = jnp.concatenate(k_parts[sub:sub + 3] + [kvx_ref[:, kcols]], axis=0)
        v_all[ci] = jnp.concatenate(v_parts[sub:sub + 3] + [kvx_ref[:, vcols]], axis=0)
        q4 = jnp.concatenate([q_ref[rows, HEAD_DIM * hd:HEAD_DIM * (hd + 1)] for hd in heads], axis=0)
        s_scr[ci] = lax.dot_general(q4, k_all, _NT, preferred_element_type=F32)
        sinks[ci] = LOG2E * jnp.concatenate(
            [jnp.broadcast_to(sink_ref[hd:hd + 1, 0:1], (blk, 1)) for hd in heads], axis=0)
    nchunk_loc = nloc // 128
    nchunk = s_scr.shape[2] // 128

    def shifted(ci, sub, rs, k):
        t = s_scr[ci, rs, 128 * k:128 * (k + 1)]
        return t + bias_refs[sub][0, rs, 128 * k:128 * (k + 1)] if k < nchunk_loc else t

    mrow = {}
    for ci, (kvh, sub) in enumerate(chains):
        for r0 in range(0, rows_all, ATT_SLAB):
            rs = slice(r0, r0 + ATT_SLAB)
            part = shifted(ci, sub, rs, 0)
            for k in range(1, nchunk):
                part = jnp.maximum(part, shifted(ci, sub, rs, k))
            r_scr[ci, rs, :] = part
    for ci in range(len(chains)):
        m = jnp.maximum(jnp.max(r_scr[ci], axis=-1, keepdims=True), sinks[ci])
        mrow[ci] = m
        r_scr[ci] = jnp.broadcast_to(m, r_scr.shape[1:])
    for ci, (kvh, sub) in enumerate(chains):
        for r0 in range(0, rows_all, ATT_SLAB):
            rs = slice(r0, r0 + ATT_SLAB)
            mb = r_scr[ci, rs, :]
            part = None
            for k in range(nchunk):
                p = jnp.exp2(shifted(ci, sub, rs, k) - mb)
                part = p if part is None else part + p
                p_scr[ci, rs, 128 * k:128 * (k + 1)] = p.astype(BF16)
            r_scr[ci, rs, :] = part
    inv = {}
    for ci in range(len(chains)):
        inv[ci] = 1.0 / (jnp.sum(r_scr[ci], axis=-1, keepdims=True) + jnp.exp2(sinks[ci] - mrow[ci]))
    for ci, (kvh, sub) in enumerate(chains):
        rows = slice(blk * sub, blk * (sub + 1))
        o = (jnp.dot(p_scr[ci, :, 0:nloc], v_all[ci][0:nloc], preferred_element_type=F32)
             + jnp.dot(p_scr[ci, :, nloc:], v_all[ci][nloc:], preferred_element_type=F32)) * inv[ci]
        for j in range(ATT_GROUP):
            hd = ATT_GROUP * kvh + j
            o_ref[rows, HEAD_DIM * hd:HEAD_DIM * (hd + 1)] = o[blk * j:blk * (j + 1)].astype(BF16)


def _attn_bias():
    i = np.arange(ATT_BLOCK)[:, None]
    j = np.arange(3 * ATT_BLOCK)[None, :] - ATT_BLOCK
    band = np.abs(i - j) <= WINDOW
    cases = [band & (j >= 0), band, band & (j < ATT_BLOCK), np.zeros_like(band)]
    bias = np.stack([np.where(c, 0.0, NEG) for c in cases]).astype(np.float32)
    return jnp.asarray(np.tile(bias, (1, ATT_GROUP, 1)))


def _attention(q, kv, bias, sink_t, n_batch, seq, ctx_len, need_ctx):
    blk = ATT_BLOCK
    n_lat = n_batch * seq
    nb = seq // blk
    width = 3 * blk + ctx_len

    def call(n_sub, steps, n_rows, qrow, small, mid, cases, name):
        big = n_sub * blk
        chains = ATT_KV_HEADS * n_sub
        return pl.pallas_call(
            functools.partial(_attn_kernel, n_sub=n_sub),
            grid=(n_batch, steps),
            in_specs=[
                pl.BlockSpec((big, q.shape[1]), lambda b, s: (qrow(b, s), 0)),
                pl.BlockSpec((blk, kv.shape[1]), small(-1)),
                pl.BlockSpec((big, kv.shape[1]), mid),
                pl.BlockSpec((blk, kv.shape[1]), small(n_sub)),
                pl.BlockSpec((ctx_len, kv.shape[1]), lambda b, s: (n_lat // ctx_len + b, 0)),
            ] + [pl.BlockSpec((1,) + bias.shape[1:], c) for c in cases] + [
                pl.BlockSpec(sink_t.shape, lambda b, s: (0, 0)),
            ],
            out_specs=pl.BlockSpec((big, q.shape[1]), lambda b, s: (qrow(b, s) - (n_lat // big if name else 0), 0)),
            out_shape=jax.ShapeDtypeStruct((n_rows, q.shape[1]), BF16),
            scratch_shapes=[pltpu.VMEM((chains, ATT_GROUP * blk, width), F32),
                            pltpu.VMEM((chains, ATT_GROUP * blk, width), BF16),
                            pltpu.VMEM((chains, ATT_GROUP * blk, 128), F32)],
            compiler_params=_params(2, VMEM_LIMIT),
            name="window_attention" + name,
        )(q, kv, kv, kv, kv, bias, bias, bias, sink_t)

    big = ATT_SUB * blk
    assert seq % big == 0
    ns = seq // big

    def small(off):
        return lambda b, s: (b * nb + jnp.clip(ATT_SUB * s + off, 0, nb - 1), 0)

    cases = (lambda b, s: (jnp.where(s == 0, 0, 1), 0, 0),
             lambda b, s: (1, 0, 0),
             lambda b, s: (jnp.where(s == ns - 1, 2, 1), 0, 0))
    lat = call(ATT_SUB, ns, n_lat, lambda b, s: b * ns + s, small, lambda b, s: (b * ns + s, 0), cases, "")
    if not need_ctx:
        return (lat,)
    n_cb = ctx_len // blk
    masked = (lambda b, s: (3, 0, 0),) * 3
    first = lambda off: (lambda b, s: (b * nb, 0))
    ctx = call(n_cb, 1, n_batch * ctx_len, lambda b, s: n_lat // ctx_len + b, first,
               lambda b, s: (b * (seq // ctx_len), 0), masked, "_ctx")
    return lat, ctx


def _route(logits):
    m = jnp.max(logits, axis=0, keepdims=True)
    e = jnp.exp(logits - m)
    p = e / jnp.sum(e, axis=0, keepdims=True)
    rows = [p[i:i + 1, :] for i in range(N_EXPERTS)]
    best = None
    gsel = None
    for g in range(N_GROUPS):
        a, b, c, d = rows[4 * g:4 * g + 4]
        m1, n1 = jnp.maximum(a, b), jnp.minimum(a, b)
        m2, n2 = jnp.maximum(c, d), jnp.minimum(c, d)
        score = jnp.maximum(m1, m2) + jnp.maximum(jnp.minimum(m1, m2), jnp.maximum(n1, n2))
        if g == 0:
            best, gsel = score, jnp.zeros(score.shape, I32)
        else:
            upd = score > best
            gsel = jnp.where(upd, g, gsel)
            best = jnp.where(upd, score, best)
    v = [jnp.where(gsel == 0, rows[j], jnp.where(gsel == 1, rows[4 + j], jnp.where(gsel == 2, rows[8 + j], rows[12 + j])))
         for j in range(EXPERTS_PER_GROUP)]
    b1, i1 = v[0], jnp.zeros(gsel.shape, I32)
    for j in range(1, 4):
        upd = v[j] > b1
        i1 = jnp.where(upd, j, i1)
        b1 = jnp.where(upd, v[j], b1)
    b2, i2 = jnp.full(b1.shape, -1.0, F32), jnp.zeros(gsel.shape, I32)
    for j in range(4):
        upd = jnp.logical_and(i1 != j, v[j] > b2)
        i2 = jnp.where(upd, j, i2)
        b2 = jnp.where(upd, v[j], b2)
    lo = jnp.minimum(i1, i2)
    hi = jnp.maximum(i1, i2)
    pair = jnp.where(lo == 0, hi - 1, jnp.where(lo == 1, 6 - hi, 5))
    return gsel * N_PAIRS + pair


def _post_kernel(*refs, lat_tiles, first_trash):
    if lat_tiles is None:
        mix_ref, refs = refs[0], refs[1:]
        mix = mix_ref[...]
    else:
        (ml_ref, mc_ref), refs = refs[:2], refs[2:]
        mix = jnp.where(pl.program_id(0) < lat_tiles, ml_ref[...], mc_ref[...])
    (x_ref, mod_ref, g2_ref, wo_ref, wr_ref, br_ref, tri_ref, low_ref, hs_in_ref,
     xo_ref, pos_ref, tab_ref, cnt_ref, hs_ref,
     carry_ref, cur_ref, nfree_ref, hbuf_ref, posv_ref, poss_ref, sem_s, sem_p) = refs
    del hs_in_ref
    i = pl.program_id(0)
    slot = i % 2
    tm, d = x_ref.shape

    @pl.when(i == 0)
    def _():
        carry_ref[...] = jnp.zeros(carry_ref.shape, F32)
        cur_ref[...] = jnp.zeros(cur_ref.shape, F32)
        nfree_ref[...] = jnp.zeros(nfree_ref.shape, F32)
        tab_ref[...] = jnp.zeros(tab_ref.shape, F32)
        hbuf_ref[1] = jnp.zeros(hbuf_ref.shape[1:], F32)
        posv_ref[1] = first_trash + lax.broadcasted_iota(I32, (1, tm), 1)
        pltpu.make_async_copy(posv_ref.at[1], poss_ref.at[1], sem_p.at[1]).start()

    pltpu.make_async_copy(posv_ref.at[1 - slot], poss_ref.at[1 - slot], sem_p.at[1 - slot]).wait()

    def scatter(s, lo, hi):
        for r in range(lo, hi):
            pltpu.make_async_copy(hbuf_ref.at[s, r], hs_ref.at[poss_ref[s, 0, r]], sem_s).start(priority=r % 2)

    def scatter_wait(s):
        pltpu.make_async_copy(hbuf_ref.at[s], hs_ref.at[pl.ds(0, tm)], sem_s).wait()

    nchunk = 4
    wc = d // nchunk
    xs = []
    for c in range(nchunk):
        scatter(1 - slot, tm * c // nchunk, tm * (c + 1) // nchunk)
        cols = slice(wc * c, wc * (c + 1))
        y = jnp.dot(mix, wo_ref[:, cols], preferred_element_type=F32)
        xc = x_ref[:, cols] + mod_ref[0, :, 2 * d + wc * c:2 * d + wc * (c + 1)] * y
        xo_ref[:, cols] = xc
        xs.append(xc)
    x = jnp.concatenate(xs, axis=1)
    h2 = _rms_mod(x, g2_ref[...], mod_ref[0, :, 4 * d:5 * d], mod_ref[0, :, 3 * d:4 * d])
    hbuf_ref[slot] = h2.reshape(hbuf_ref.shape[1:])
    hi = h2.astype(BF16)
    lo = (h2 - hi.astype(F32)).astype(BF16)
    logits = (lax.dot_general(wr_ref[...], hi, _NT, preferred_element_type=F32)
              + lax.dot_general(wr_ref[...], lo, _NT, preferred_element_type=F32)
              + br_ref[:, 0:1])
    bucket = _route(logits)
    ids = lax.broadcasted_iota(I32, (BUCKET_ROWS, tm), 0)
    onehot = (ids == bucket).astype(F32)
    before = jnp.dot(onehot.astype(BF16), tri_ref[...], preferred_element_type=F32)
    inv = 1.0 / MOE_TILE
    c0 = carry_ref[:, 0:1]
    c1 = c0 + jnp.sum(onehot, axis=1, keepdims=True)
    a0 = jnp.floor((c0 + (MOE_TILE - 1)) * inv)
    new = jnp.floor((c1 + (MOE_TILE - 1)) * inv) - a0
    nfree = nfree_ref[0:1, 0:1]
    new_b = jnp.broadcast_to(new, (BUCKET_ROWS, 128)).astype(BF16)
    id1 = nfree + jnp.dot(low_ref[...], new_b, preferred_element_type=F32)[:, 0:1]
    cur = cur_ref[:, 0:1]
    rank = before + c0
    k = jnp.floor(rank * inv)
    chunk = jnp.where(k < a0, cur, id1 + (k - a0))
    pos = jnp.sum(onehot * (chunk * MOE_TILE + (rank - k * MOE_TILE)), axis=0, keepdims=True).astype(I32)
    pos_ref[...] = pos
    posv_ref[slot] = pos
    pltpu.make_async_copy(posv_ref.at[slot], poss_ref.at[slot], sem_p.at[slot]).start()
    carry_ref[...] = jnp.broadcast_to(c1, carry_ref.shape)
    cnt_ref[...] = jnp.broadcast_to(c1, cnt_ref.shape).astype(I32)
    cur_ref[...] = jnp.broadcast_to(jnp.where(new == 0.0, cur, id1 + new - 1.0), cur_ref.shape)
    nfree_ref[...] = jnp.broadcast_to(nfree + jnp.sum(new, axis=0, keepdims=True), nfree_ref.shape)
    kcol = lax.broadcasted_iota(I32, tab_ref.shape, 1).astype(F32)
    tab = tab_ref[...]
    fresh = jnp.logical_and(kcol >= a0, kcol < a0 + new)
    tab_ref[...] = jnp.where(fresh, id1 + (kcol - a0), tab)
    scatter_wait(1 - slot)

    @pl.when(i == pl.num_programs(0) - 1)
    def _():
        pltpu.make_async_copy(posv_ref.at[slot], poss_ref.at[slot], sem_p.at[slot]).wait()
        scatter(slot, 0, tm)
        scatter_wait(slot)


def _post(mix, xa, mod_l, g2, wo, wr_bf, br_t, tri, low, hs, n_act, n_lat, seq, n_batch):
    d = xa.shape[1]
    tm = POST_TILE
    steps = n_act // tm
    mrow = _mod_row_map(tm, n_lat, seq, n_batch)
    full = lambda a: pl.BlockSpec(a.shape, lambda i, _nd=a.ndim: (0,) * _nd)
    if len(mix) == 1:
        lat_tiles = None
        mix_specs = [pl.BlockSpec((tm, d), lambda i: (i, 0))]
    else:
        lat_tiles = n_lat // tm
        mix_specs = [pl.BlockSpec((tm, d), lambda i: (jnp.minimum(i, lat_tiles - 1), 0)),
                     pl.BlockSpec((tm, d), lambda i: (jnp.maximum(i - lat_tiles, 0), 0))]
    n_in = len(mix_specs)
    return pl.pallas_call(
        functools.partial(_post_kernel, lat_tiles=lat_tiles, first_trash=hs.shape[0] - tm),
        grid=(steps,),
        in_specs=mix_specs + [
            pl.BlockSpec((tm, d), lambda i: (i, 0)),
            pl.BlockSpec((1, 1, 6 * d), lambda i: (mrow(i), 0, 0)),
            full(g2), full(wo), full(wr_bf), full(br_t), full(tri), full(low),
            pl.BlockSpec(memory_space=pl.ANY),
        ],
        out_specs=[
            pl.BlockSpec((tm, d), lambda i: (i, 0)),
            pl.BlockSpec((1, tm), lambda i: (0, i)),
            pl.BlockSpec((BUCKET_ROWS, 256), lambda i: (0, 0)),
            pl.BlockSpec((BUCKET_ROWS, 128), lambda i: (0, 0)),
            pl.BlockSpec(memory_space=pl.ANY),
        ],
        out_shape=[
            jax.ShapeDtypeStruct((n_act, d), F32),
            jax.ShapeDtypeStruct((1, n_act), I32),
            jax.ShapeDtypeStruct((BUCKET_ROWS, 256), F32),
            jax.ShapeDtypeStruct((BUCKET_ROWS, 128), I32),
            jax.ShapeDtypeStruct(hs.shape, F32),
        ],
        scratch_shapes=[pltpu.VMEM((BUCKET_ROWS, 128), F32), pltpu.VMEM((BUCKET_ROWS, 128), F32), pltpu.VMEM((8, 128), F32),
                        pltpu.VMEM((2, tm) + hs.shape[1:], F32), pltpu.VMEM((2, 1, tm), I32), pltpu.SMEM((2, 1, tm), I32),
                        pltpu.SemaphoreType.DMA(()), pltpu.SemaphoreType.DMA((2,))],
        input_output_aliases={n_in + 8: 4},
        compiler_params=_params(1, VMEM_LIMIT),
        name="post_mixer",
    )(*mix, xa, mod_l, g2, wo, wr_bf, br_t, tri, low, hs)


def _moe_kernel(blk_ref, parts_ref, ea_ref, eb_ref, nu_ref, h_ref, wga_ref, wua_ref, wda_ref, wgb_ref, wub_ref, wdb_ref,
                wrt_ref, brt_ref, *refs):
    o_ref = refs[len(refs) // 2]
    n_cast = len(refs) // 2
    for k in range(n_cast):
        refs[n_cast + 1 + k][...] = refs[k][0].astype(BF16)
    t = pl.program_id(0)
    active = t < nu_ref[0]
    n_parts = parts_ref[t]
    part_rows = h_ref.shape[0] // MOE_PARTS

    def compute(parts):
        pieces = [(2 * part_rows * i, min(2, parts - 2 * i) * part_rows) for i in range((parts + 1) // 2)]
        ea = ea_ref[t]
        eb = eb_ref[t]
        wdiff = wrt_ref[pl.ds(ea, 1), :] - wrt_ref[pl.ds(eb, 1), :]
        bdiff = brt_ref[pl.ds(ea, 1), 0:1] - brt_ref[pl.ds(eb, 1), 0:1]
        hbs, was = [], []
        for lo, n in pieces:
            rows = h_ref[lo:lo + n]
            h = rows.reshape(n, h_ref.shape[1] * h_ref.shape[2])
            hbs.append(h.astype(BF16))
            was.append(jax.nn.sigmoid(jnp.sum(h * wdiff, axis=-1, keepdims=True) + bdiff))
        slots = ((wga_ref, wua_ref, wda_ref), (wgb_ref, wub_ref, wdb_ref))
        f = wga_ref.shape[2]
        chains = [(i, e, c) for i in range(len(pieces)) for e in range(2) for c in range(f // 512)]
        gu = {}
        for i, e, c in chains:
            cols = slice(512 * c, 512 * (c + 1))
            gu[i, e, c] = (jnp.dot(hbs[i], slots[e][0][0, :, cols], preferred_element_type=F32),
                           jnp.dot(hbs[i], slots[e][1][0, :, cols], preferred_element_type=F32))
        act = {k: (_silu(g) * u).astype(BF16) for k, (g, u) in gu.items()}
        outs = {}
        for i, e, c in chains:
            part = jnp.dot(act[i, e, c], slots[e][2][0, 512 * c:512 * (c + 1), :], preferred_element_type=F32)
            outs[i, e] = part if (i, e) not in outs else outs[i, e] + part
        for i, (lo, n) in enumerate(pieces):
            y = was[i] * outs[i, 0] + (1.0 - was[i]) * outs[i, 1]
            o_ref[lo:lo + n] = y.reshape((n,) + o_ref.shape[1:])
        if parts < MOE_PARTS:
            o_ref[parts * part_rows:] = jnp.zeros(((MOE_PARTS - parts) * part_rows,) + o_ref.shape[1:], F32)

    for parts in range(1, MOE_PARTS + 1):
        pl.when(jnp.logical_and(active, n_parts == parts))(functools.partial(compute, parts))

    @pl.when(jnp.logical_not(active))
    def _():
        o_ref[...] = jnp.zeros(o_ref.shape, F32)


def _moe(blk, parts, ea, eb, nu, hs, n_tiles, wg, wu, wd, wr_t, br_t, w_next, next_layer):
    row = hs.shape[1:]
    tm = MOE_TILE
    d, f = wg.shape[1:]
    n_cast = 0 if w_next is None else len(w_next)
    if n_cast:
        w_spec, wo_spec, wo_shape = _cast_specs(w_next[0], next_layer, n_tiles)
    else:
        w_spec = wo_spec = wo_shape = None
    wa_spec = lambda shape: pl.BlockSpec(shape, lambda t, blk_, parts_, ea_, eb_, nu_: (ea_[t], 0, 0))
    wb_spec = lambda shape: pl.BlockSpec(shape, lambda t, blk_, parts_, ea_, eb_, nu_: (eb_[t], 0, 0))
    chunk = pl.BlockSpec((tm,) + row, lambda t, blk_, parts_, ea_, eb_, nu_: (blk_[t], 0, 0))
    grid_spec = pltpu.PrefetchScalarGridSpec(
        num_scalar_prefetch=5,
        grid=(n_tiles,),
        in_specs=[
            chunk,
            wa_spec((1, d, f)), wa_spec((1, d, f)), wa_spec((1, f, d)),
            wb_spec((1, d, f)), wb_spec((1, d, f)), wb_spec((1, f, d)),
            pl.BlockSpec(wr_t.shape, lambda t, *_: (0, 0)),
            pl.BlockSpec(br_t.shape, lambda t, *_: (0, 0)),
        ] + [w_spec] * n_cast,
        out_specs=[chunk] + [wo_spec] * n_cast,
    )
    return pl.pallas_call(
        _moe_kernel,
        grid_spec=grid_spec,
        out_shape=[jax.ShapeDtypeStruct((n_tiles * tm,) + row, F32)] + [wo_shape] * n_cast,
        compiler_params=_params(1, VMEM_LIMIT),
        name="experts",
    )(blk, parts, ea, eb, nu, hs, wg, wu, wd, wg, wu, wd, wr_t, br_t, *(w_next or ()))


def _combine_kernel(pos_ref, posn_ref, x_ref, mod_ref, fg_ref, ys_ref, o_ref, gbuf_ref, sem):
    d = x_ref.shape[1]
    rows, issue_next, finish = _gather_rows(pos_ref, posn_ref, ys_ref, gbuf_ref, sem)
    issue_next(0, 1)
    x = x_ref[...] + mod_ref[0, :, 5 * d:6 * d] * rows
    o_ref[...] = x * lax.rsqrt(jnp.mean(x * x, axis=-1, keepdims=True) + EPS) * fg_ref[...]
    finish()


def _combine(pos3, xa, mod_l, fg, ys, n_lat, seq, n_batch):
    n, d = xa.shape
    tm = ROW_TILE
    steps = n // tm
    mrow = _mod_row_map(tm, n_lat, seq, n_batch)
    return pl.pallas_call(
        _combine_kernel,
        grid=(steps,),
        in_specs=[
            pl.BlockSpec((1, 1, tm), lambda i: (i, 0, 0), memory_space=pltpu.SMEM),
            pl.BlockSpec((1, 1, tm), lambda i: (jnp.minimum(i + 1, steps - 1), 0, 0), memory_space=pltpu.SMEM),
            pl.BlockSpec((tm, d), lambda i: (i, 0)),
            pl.BlockSpec((1, 1, 6 * d), lambda i: (mrow(i), 0, 0)),
            pl.BlockSpec((1, d), lambda i: (0, 0)),
            pl.BlockSpec(memory_space=pl.ANY),
        ],
        out_specs=pl.BlockSpec((tm, d), lambda i: (i, 0)),
        out_shape=jax.ShapeDtypeStruct((n, d), F32),
        scratch_shapes=[pltpu.VMEM((2, tm) + ys.shape[1:], F32), pltpu.SemaphoreType.DMA((2,))],
        compiler_params=_params(1, VMEM_LIMIT),
        name="combine",
    )(pos3, pos3, xa, mod_l, fg, ys)


def _tile_plan(tab, counts, n_tiles, tile):
    cnt = counts[:N_BUCKETS, 0]
    n_chunks = (cnt + tile - 1) // tile
    ends = jnp.cumsum(n_chunks)
    starts = ends - n_chunks
    n_used = ends[-1]
    t = jnp.arange(n_tiles, dtype=I32)
    t_eff = jnp.minimum(t, jnp.maximum(n_used - 1, 0))
    tb = jnp.minimum(jnp.sum((ends[None, :] <= t_eff[:, None]).astype(I32), axis=1), N_BUCKETS - 1)
    k = n_chunks[tb] - 1 - (t_eff - starts[tb])
    owned = tab.astype(I32)[tb, k]
    blk = jnp.where(t < n_used, owned, t)
    part = tile // MOE_PARTS
    parts = jnp.clip((cnt[tb] - k * tile + part - 1) // part, 1, MOE_PARTS).astype(I32)
    pairs = np.array(PAIR_SLOTS, dtype=np.int32)
    ea = (tb // N_PAIRS) * EXPERTS_PER_GROUP + jnp.asarray(pairs[:, 0])[tb % N_PAIRS]
    eb = (tb // N_PAIRS) * EXPERTS_PER_GROUP + jnp.asarray(pairs[:, 1])[tb % N_PAIRS]
    return blk.astype(I32), parts, ea.astype(I32), eb.astype(I32), n_used.astype(I32).reshape(1)


def kernel(x, c, ctx, c_ctx, w_mod, b_mod, norm_g, w_in_even, w_out_even, ret_log_decay, ret_gn_g, sg_ln_g, sg_ln_b, sg_w, sg_b, w_qkv_odd, w_o_odd, attn_sink, w_router, b_router, w_e_gate, w_e_up, w_e_down, final_g):
    n_batch, seq, d = x.shape
    ctx_len = ctx.shape[1]
    depth = w_mod.shape[0]
    n_lat = n_batch * seq
    n_all = n_lat + n_batch * ctx_len
    tm = ROW_TILE
    assert seq % POST_TILE == 0 and (n_batch * ctx_len) % POST_TILE == 0

    mod_rows = -(-(n_batch + 1) // 16) * 16
    c_rows = jnp.zeros((mod_rows, d), F32).at[:n_batch].set(c).at[n_batch].set(c_ctx)
    mod = _modulation(c_rows, w_mod, b_mod).reshape(depth, mod_rows, 1, 6 * d)

    wr_t = w_router.T.astype(F32)
    wr_bf = wr_t.astype(BF16)
    br_t = jnp.broadcast_to(b_router.astype(F32)[:, None], (N_EXPERTS, 128))
    tri = jnp.asarray(np.triu(np.ones((POST_TILE, POST_TILE), np.float32), k=1), BF16)
    low = jnp.asarray(np.tril(np.ones((BUCKET_ROWS, BUCKET_ROWS), np.float32), k=-1), BF16)
    cos_t, sin_t = _rope_tables(seq, tm)
    bias = _attn_bias()

    n_exp, d_exp = w_e_gate.shape[1], w_e_gate.shape[3]
    assert d_exp == d
    w_views = [w.reshape(depth, n_exp * d, d) for w in (w_e_gate, w_e_up, w_e_down)]

    n_tiles = n_all // MOE_TILE + N_BUCKETS
    p_rows = n_tiles * MOE_TILE
    hs = jnp.zeros((p_rows + POST_TILE, ROW_SUB, d // ROW_SUB), F32)

    source = (x.reshape(n_lat, d), ctx.reshape(n_batch * ctx_len, d))
    for l in range(depth):
        i = l // 2
        last = l == depth - 1
        mod_l = mod[l]
        g1 = norm_g[l, 0].reshape(1, d)
        g2 = norm_g[l, 1].reshape(1, d)
        if l % 2 == 0:
            res = _in_stage(l, source, mod_l, g1, w_in_even[i].astype(BF16), None, w_views if l == 0 else None,
                            n_lat, seq, n_batch)
            xa, p = res[:2]
            tabs = _retention_tables(ret_log_decay[i], RET_BLOCK)
            sg_bias = jnp.repeat(sg_b[i].astype(F32).T, SG_CHUNK, axis=1)
            mix = _even_mix(p, tabs, ret_gn_g[i].reshape(1, -1), sg_ln_g[i].reshape(1, -1), sg_ln_b[i].reshape(1, -1),
                            sg_w[i].astype(BF16), sg_bias, n_batch, seq, ctx_len)
            wo = w_out_even[i].astype(BF16)
        else:
            res = _in_stage(l, source, mod_l, g1, w_qkv_odd[i].astype(BF16), (cos_t, sin_t), w_views if l == 0 else None,
                            n_lat, seq, n_batch)
            xa, q, kv = res[:3]
            sink_t = jnp.broadcast_to(attn_sink[i].astype(F32)[:, None], (ATT_HEADS, 128))
            mix = _attention(q, kv, bias, sink_t, n_batch, seq, ctx_len, need_ctx=not last)
            wo = w_o_odd[i].astype(BF16)
        n_act = n_lat if last else n_all
        xm, pos, tab, counts, hs = _post(mix, xa, mod_l, g2, wo, wr_bf, br_t, tri, low, hs, n_act, n_lat, seq, n_batch)
        plan = _tile_plan(tab, counts, n_tiles, MOE_TILE)
        pos3 = pos.reshape(n_act // tm, 1, tm)
        if l == 0:
            w_bf = res[-3:]
        ys, *w_cast = _moe(*plan, hs, n_tiles, *(w.reshape(n_exp, d, d) for w in w_bf), wr_t, br_t,
                           None if last else w_views, l + 1)
        w_bf = w_cast
        source = (pos3, xm, mod_l, ys)
    out = _combine(pos3, xm, mod_l, final_g.reshape(1, d), ys, n_lat, seq, n_batch)
    return out.reshape(n_batch, seq, d)
```
